```python
import math
import jax, jax.numpy as jnp
from jax import lax
import numpy as np

D_MODEL = 1024
BATCH = 8
SEQ = 16384
DEPTH = 4

GRID_W = 64
CTX_LEN = 256
N_BRANCH = 4
MIX_W = D_MODEL // 2
HEAD_DIM = 64
S5_GROUP_CH = 16
S5_GROUPS = MIX_W // S5_GROUP_CH
S5_STATE = 64
S5_DT_MIN = 1e-3
S5_DT_MAX = 1e-1
CONV_W = 3
NA_HEADS = MIX_W // HEAD_DIM
NA_ROWS = 8
NA_COLS = 16
GQA_Q_HEADS = MIX_W // HEAD_DIM
GQA_KV_HEADS = 2
GQA_KV_W = GQA_KV_HEADS * HEAD_DIM
ATTN_BLOCK = 128
WINDOW = 128
ROPE_BASE = 10000.0
ROPE_PAIRS = HEAD_DIM // 4
EPS = 1e-6
NEG_INF = -1e30
BRANCH_NAMES = ("s5", "conv", "na", "gqa")
PROJ_LAYOUT = (
    ("s5_u", MIX_W), ("s5_gate", MIX_W),
    ("conv_v", MIX_W), ("conv_b", MIX_W), ("conv_c", MIX_W), ("conv_gate", MIX_W),
    ("na_q", MIX_W), ("na_k", MIX_W), ("na_v", MIX_W), ("na_gate", MIX_W),
    ("gqa_q", MIX_W), ("gqa_k", GQA_KV_W), ("gqa_v", GQA_KV_W), ("gqa_gate", MIX_W),
    ("merge_s5", D_MODEL), ("merge_conv", D_MODEL), ("merge_na", D_MODEL), ("merge_gqa", D_MODEL),
)
N_IN = sum(size for _, size in PROJ_LAYOUT)
ALL_NAMES = tuple(name for name, _ in PROJ_LAYOUT)
CTX_KV_NAMES = ("s5_u", "na_k", "na_v", "gqa_k", "gqa_v")

kernel_name = "hybrid_s5_conv_natten_swa_dit_trunk"


def _rmsnorm(x, g):
    xf = x.astype(jnp.float32)
    y = xf * lax.rsqrt(jnp.mean(xf * xf, axis=-1, keepdims=True) + EPS)
    return (y * g.astype(jnp.float32)).astype(x.dtype)


def _adaln(cvec, w_ada, b_ada):
    mod = jax.nn.silu(cvec) @ w_ada + b_ada
    shift, scale, gate = jnp.split(mod[:, None, :], 3, axis=-1)
    return shift, scale, gate


def _project(h, w_in, names):
    out = {}
    start = 0
    for name, size in PROJ_LAYOUT:
        if name in names:
            out[name] = h @ w_in[:, start:start + size]
        start += size
    return out


def _heads(z, n_heads):
    b, n, _ = z.shape
    return z.reshape(b, n, n_heads, HEAD_DIM)


def _rope_2d(x):
    n = x.shape[1]
    t = jnp.arange(n, dtype=jnp.int32)
    row = (t // GRID_W).astype(jnp.float32)
    col = (t % GRID_W).astype(jnp.float32)
    inv = ROPE_BASE ** (-jnp.arange(ROPE_PAIRS, dtype=jnp.float32) / ROPE_PAIRS)
    ang = jnp.concatenate([row[:, None] * inv, col[:, None] * inv], axis=-1)[None, :, None, :]
    cos = jnp.cos(ang).astype(x.dtype)
    sin = jnp.sin(ang).astype(x.dtype)
    half = x.shape[-1] // 2
    x1, x2 = x[..., :half], x[..., half:]
    return jnp.concatenate([x1 * cos - x2 * sin, x2 * cos + x1 * sin], axis=-1)


def _cplx_combine(e1, e2):
    a1r, a1i, b1r, b1i = e1
    a2r, a2i, b2r, b2i = e2
    ar = a1r * a2r - a1i * a2i
    ai = a1r * a2i + a1i * a2r
    br = a2r * b1r - a2i * b1i + b2r
    bi = a2r * b1i + a2i * b1r + b2i
    return ar, ai, br, bi


def _s5_discretise(a_re, a_im, log_dt, b_re, b_im):
    f32 = jnp.float32
    a_re = a_re.astype(f32)
    a_im = a_im.astype(f32)
    dt = jnp.exp(log_dt.astype(f32))[:, None]
    mag = jnp.exp(dt * a_re)
    abr = mag * jnp.cos(dt * a_im)
    abi = mag * jnp.sin(dt * a_im)
    den = a_re * a_re + a_im * a_im
    fr = ((abr - 1.0) * a_re + abi * a_im) / den
    fi = (abi * a_re - (abr - 1.0) * a_im) / den
    b_re = b_re.astype(f32)
    b_im = b_im.astype(f32)
    bbr = fr[..., None] * b_re - fi[..., None] * b_im
    bbi = fr[..., None] * b_im + fi[..., None] * b_re
    return abr, abi, bbr, bbi


def _s5_states(ug, p, d, h0, reverse):
    abr, abi, bbr, bbi = _s5_discretise(p["s5_a_re"][d], p["s5_a_im"][d], p["s5_log_dt"][d],
                                        p["s5_b_re"][d], p["s5_b_im"][d])
    abr, abi, bbr, bbi = (t.astype(ug.dtype) for t in (abr, abi, bbr, bbi))
    xr = jnp.einsum("blgh,gph->blgp", ug, bbr)
    xi = jnp.einsum("blgh,gph->blgp", ug, bbi)
    if h0 is not None:
        h0r, h0i = h0
        first = -1 if reverse else 0
        xr = xr.at[:, first].add(abr * h0r - abi * h0i)
        xi = xi.at[:, first].add(abr * h0i + abi * h0r)
    shape = (1, ug.shape[1]) + abr.shape
    _, _, hr, hi = lax.associative_scan(
        _cplx_combine,
        (jnp.broadcast_to(abr, shape), jnp.broadcast_to(abi, shape), xr, xi),
        reverse=reverse, axis=1)
    return hr, hi


def _s5_readout(states, c_re, c_im):
    hr, hi = states
    return jnp.einsum("blgp,ghp->blgh", hr, c_re) - jnp.einsum("blgp,ghp->blgh", hi, c_im)


def _s5_output(ug, st_f, st_b, p):
    y = (_s5_readout(st_f, p["s5_c_re"][0], p["s5_c_im"][0])
         + _s5_readout(st_b, p["s5_c_re"][1], p["s5_c_im"][1])
         + p["s5_d"] * ug)
    y = jax.nn.gelu(y.reshape(ug.shape[0], ug.shape[1], MIX_W))
    return y * jax.nn.sigmoid(y @ p["s5_w_glu"])


def _groups(u):
    b, n, _ = u.shape
    return u.reshape(b, n, S5_GROUPS, S5_GROUP_CH)


def _short_conv(z, p):
    n = z["conv_v"].shape[1]
    zz = z["conv_c"] * z["conv_v"]
    pad = CONV_W // 2
    zp = jnp.pad(zz, ((0, 0), (pad, pad), (0, 0)))
    y = p["conv_b"]
    for tap in range(CONV_W):
        y = y + zp[:, tap:tap + n] * p["conv_w"][tap]
    return z["conv_b"] * y


def _ctx_attention(q, k, v, sink):
    b, n, hq, dh = q.shape
    hkv = k.shape[2]
    g = hq // hkv
    qg = q.reshape(b, n, hkv, g, dh)
    s = jnp.einsum("bqkgd,bskd->bkgqs", qg, k).astype(jnp.float32) * dh ** -0.5
    if sink is not None:
        s_sink = jnp.broadcast_to(sink.astype(jnp.float32).reshape(1, hkv, g, 1, 1), (b, hkv, g, n, 1))
        s = jnp.concatenate([s, s_sink], axis=-1)
    pr = jax.nn.softmax(s, axis=-1)[..., :k.shape[1]].astype(v.dtype)
    return jnp.einsum("bkgqs,bskd->bqkgd", pr, v).reshape(b, n, hq * dh)


def _na_latent(q, k, v, k_ctx, v_ctx, rel_bias):
    b, n, h, dh = q.shape
    rows = n // GRID_W
    kh = min(NA_ROWS, rows)
    nk = kh * GRID_W
    qg = q.reshape(b, rows, GRID_W, h, dh)
    kg = k.reshape(b, rows, GRID_W, h, dh)
    vg = v.reshape(b, rows, GRID_W, h, dh)
    qcol = jnp.arange(GRID_W, dtype=jnp.int32)
    kcol = jnp.tile(qcol, kh)
    krow = jnp.repeat(jnp.arange(kh, dtype=jnp.int32), GRID_W)
    cs = jnp.clip(qcol - NA_COLS // 2, 0, GRID_W - NA_COLS)
    col_ok = (kcol[None, :] >= cs[:, None]) & (kcol[None, :] < cs[:, None] + NA_COLS)
    dj_idx = jnp.clip(kcol[None, :] - qcol[:, None] + NA_COLS - 1, 0, 2 * NA_COLS - 2)
    scale = dh ** -0.5

    def row_block(r):
        rs = jnp.clip(r - kh // 2, 0, rows - kh)
        q_r = lax.dynamic_index_in_dim(qg, r, axis=1, keepdims=False)
        k_r = lax.dynamic_slice_in_dim(kg, rs, kh, axis=1).reshape(b, nk, h, dh)
        v_r = lax.dynamic_slice_in_dim(vg, rs, kh, axis=1).reshape(b, nk, h, dh)
        di_idx = rs + krow - r + NA_ROWS - 1
        bias = rel_bias[:, di_idx[None, :], dj_idx].astype(jnp.float32)
        s_lat = jnp.einsum("bqhd,bkhd->bhqk", q_r, k_r).astype(jnp.float32) * scale + bias
        s_lat = jnp.where(col_ok, s_lat, NEG_INF)
        s_ctx = jnp.einsum("bqhd,bkhd->bhqk", q_r, k_ctx).astype(jnp.float32) * scale
        pr = jax.nn.softmax(jnp.concatenate([s_lat, s_ctx], axis=-1), axis=-1).astype(v.dtype)
        return (jnp.einsum("bhqk,bkhd->bqhd", pr[..., :nk], v_r)
                + jnp.einsum("bhqk,bkhd->bqhd", pr[..., nk:], v_ctx))

    out = lax.map(row_block, jnp.arange(rows, dtype=jnp.int32))
    return out.transpose(1, 0, 2, 3, 4).reshape(b, n, h * dh)


def _gqa_latent(q, k, v, k_ctx, v_ctx, sink):
    b, n, hq, dh = q.shape
    hkv = k.shape[2]
    g = hq // hkv
    nb = n // ATTN_BLOCK
    span = ATTN_BLOCK + 2 * WINDOW
    pad = ((0, 0), (WINDOW, WINDOW), (0, 0), (0, 0))
    kp = jnp.pad(k, pad)
    vp = jnp.pad(v, pad)
    qi = jnp.arange(ATTN_BLOCK, dtype=jnp.int32)
    si = jnp.arange(span, dtype=jnp.int32)
    band = jnp.abs(si[None, :] - WINDOW - qi[:, None]) <= WINDOW
    sink_l = jnp.broadcast_to(sink.astype(jnp.float32).reshape(1, hkv, g, 1, 1), (b, hkv, g, ATTN_BLOCK, 1))
    n_ctx = k_ctx.shape[1]
    scale = dh ** -0.5

    def block(i):
        start = i * ATTN_BLOCK
        q_i = lax.dynamic_slice_in_dim(q, start, ATTN_BLOCK, axis=1).reshape(b, ATTN_BLOCK, hkv, g, dh)
        k_i = lax.dynamic_slice_in_dim(kp, start, span, axis=1)
        v_i = lax.dynamic_slice_in_dim(vp, start, span, axis=1)
        kpos = start - WINDOW + si
        ok = band & ((kpos >= 0) & (kpos < n))[None, :]
        s_lat = jnp.where(ok, jnp.einsum("bqkgd,bskd->bkgqs", q_i, k_i).astype(jnp.float32) * scale, NEG_INF)
        s_ctx = jnp.einsum("bqkgd,bckd->bkgqc", q_i, k_ctx).astype(jnp.float32) * scale
        pr = jax.nn.softmax(jnp.concatenate([s_lat, s_ctx, sink_l], axis=-1), axis=-1).astype(v.dtype)
        o = (jnp.einsum("bkgqs,bskd->bqkgd", pr[..., :span], v_i)
             + jnp.einsum("bkgqc,bckd->bqkgd", pr[..., span:span + n_ctx], v_ctx))
        return o.reshape(b, ATTN_BLOCK, hq * dh)

    out = lax.map(block, jnp.arange(nb, dtype=jnp.int32))
    return out.transpose(1, 0, 2, 3).reshape(b, n, hq * dh)


def _merge(z, ys, p):
    out = None
    for k, (name, y) in enumerate(zip(BRANCH_NAMES, ys)):
        gated = y * jax.nn.silu(z[name + "_gate"])
        contrib = jax.nn.sigmoid(z["merge_" + name]) * (gated @ p["w_br"][k])
        out = contrib if out is None else out + contrib
    return out @ p["w_out"]


def _layer(x, ctx, c, c_ctx, p, update_ctx):
    shift_x, scale_x, gate_x = _adaln(c, p["w_ada"], p["b_ada"])
    shift_c, scale_c, gate_c = _adaln(c_ctx[None, :], p["w_ada"], p["b_ada"])
    hx = _rmsnorm(x, p["norm_g"]) * (1.0 + scale_x) + shift_x
    hc = _rmsnorm(ctx, p["norm_g"]) * (1.0 + scale_c) + shift_c
    zx = _project(hx, p["w_in"], ALL_NAMES)
    zc = _project(hc, p["w_in"], ALL_NAMES if update_ctx else CTX_KV_NAMES)

    uc = _groups(zc["s5_u"])
    stc_f = _s5_states(uc, p, 0, None, reverse=False)
    stc_b = _s5_states(uc, p, 1, None, reverse=True)
    na_kc = _rmsnorm(_heads(zc["na_k"], NA_HEADS), p["na_k_g"])
    na_vc = _heads(zc["na_v"], NA_HEADS)
    gqa_kc = _rmsnorm(_heads(zc["gqa_k"], GQA_KV_HEADS), p["gqa_k_g"])
    gqa_vc = _heads(zc["gqa_v"], GQA_KV_HEADS)

    ux = _groups(zx["s5_u"])
    stx_f = _s5_states(ux, p, 0, (stc_f[0][:, -1], stc_f[1][:, -1]), reverse=False)
    stx_b = _s5_states(ux, p, 1, (stc_b[0][:, 0], stc_b[1][:, 0]), reverse=True)
    y_s5 = _s5_output(ux, stx_f, stx_b, p)
    y_conv = _short_conv(zx, p)
    na_q = _rmsnorm(_heads(zx["na_q"], NA_HEADS), p["na_q_g"])
    na_k = _rmsnorm(_heads(zx["na_k"], NA_HEADS), p["na_k_g"])
    na_v = _heads(zx["na_v"], NA_HEADS)
    y_na = _na_latent(na_q, na_k, na_v, na_kc, na_vc, p["na_rel_bias"])
    gqa_q = _rope_2d(_rmsnorm(_heads(zx["gqa_q"], GQA_Q_HEADS), p["gqa_q_g"]))
    gqa_k = _rope_2d(_rmsnorm(_heads(zx["gqa_k"], GQA_KV_HEADS), p["gqa_k_g"]))
    gqa_v = _heads(zx["gqa_v"], GQA_KV_HEADS)
    y_gqa = _gqa_latent(gqa_q, gqa_k, gqa_v, gqa_kc, gqa_vc, p["gqa_sink"])
    x_new = x + gate_x * _merge(zx, (y_s5, y_conv, y_na, y_gqa), p)

    if update_ctx:
        yc_s5 = _s5_output(uc, stc_f, stc_b, p)
        yc_conv = _short_conv(zc, p)
        yc_na = _ctx_attention(_rmsnorm(_heads(zc["na_q"], NA_HEADS), p["na_q_g"]), na_kc, na_vc, None)
        yc_gqa = _ctx_attention(_rmsnorm(_heads(zc["gqa_q"], GQA_Q_HEADS), p["gqa_q_g"]),
                                gqa_kc, gqa_vc, p["gqa_sink"])
        ctx = ctx + gate_c * _merge(zc, (yc_s5, yc_conv, yc_na, yc_gqa), p)
    return x_new, ctx


def _fwd_setup_inputs(seed: int = 0) -> dict:
    key = jax.random.key(seed)
    ks = jax.random.split(key, 28)
    f32 = jnp.float32

    def nrm(k, shape, scale):
        return scale * jax.random.normal(k, shape, f32)

    g_p = (DEPTH, 2, S5_GROUPS, S5_STATE)
    return {
        "x": nrm(ks[0], (BATCH, SEQ, D_MODEL), 1.0),
        "c": nrm(ks[1], (BATCH, D_MODEL), 1.0),
        "ctx": nrm(ks[2], (BATCH, CTX_LEN, D_MODEL), 1.0),
        "c_ctx": nrm(ks[3], (D_MODEL,), 1.0),
        "norm_g": 1.0 + nrm(ks[4], (DEPTH, D_MODEL), 0.02),
        "w_ada": nrm(ks[5], (DEPTH, D_MODEL, 3 * D_MODEL), 0.5 * D_MODEL ** -0.5),
        "b_ada": nrm(ks[6], (DEPTH, 3 * D_MODEL), 0.02),
        "w_in": nrm(ks[7], (DEPTH, D_MODEL, N_IN), D_MODEL ** -0.5),
        "s5_a_re": -0.5 + nrm(ks[8], g_p, 0.01),
        "s5_a_im": math.pi * jnp.arange(S5_STATE, dtype=f32) + nrm(ks[9], g_p, 0.01),
        "s5_log_dt": jax.random.uniform(ks[10], (DEPTH, 2, S5_GROUPS), f32,
                                        math.log(S5_DT_MIN), math.log(S5_DT_MAX)),
        "s5_b_re": nrm(ks[11], (DEPTH, 2, S5_GROUPS, S5_STATE, S5_GROUP_CH), (2 * S5_GROUP_CH) ** -0.5),
        "s5_b_im": nrm(ks[12], (DEPTH, 2, S5_GROUPS, S5_STATE, S5_GROUP_CH), (2 * S5_GROUP_CH) ** -0.5),
        "s5_c_re": nrm(ks[13], (DEPTH, 2, S5_GROUPS, S5_GROUP_CH, S5_STATE), 0.5),
        "s5_c_im": nrm(ks[14], (DEPTH, 2, S5_GROUPS, S5_GROUP_CH, S5_STATE), 0.5),
        "s5_d": nrm(ks[15], (DEPTH, S5_GROUPS, S5_GROUP_CH), 1.0),
        "s5_w_glu": nrm(ks[16], (DEPTH, MIX_W, MIX_W), MIX_W ** -0.5),
        "conv_w": nrm(ks[17], (DEPTH, CONV_W, MIX_W), CONV_W ** -0.5),
        "conv_b": nrm(ks[18], (DEPTH, MIX_W), 0.02),
        "na_q_g": 1.0 + nrm(ks[19], (DEPTH, HEAD_DIM), 0.02),
        "na_k_g": 1.0 + nrm(ks[20], (DEPTH, HEAD_DIM), 0.02),
        "na_rel_bias": nrm(ks[21], (DEPTH, NA_HEADS, 2 * NA_ROWS - 1, 2 * NA_COLS - 1), 0.1),
        "gqa_q_g": 1.0 + nrm(ks[22], (DEPTH, HEAD_DIM), 0.02),
        "gqa_k_g": 1.0 + nrm(ks[23], (DEPTH, HEAD_DIM), 0.02),
        "gqa_sink": nrm(ks[24], (DEPTH, GQA_Q_HEADS), 0.5),
        "w_br": nrm(ks[25], (DEPTH, N_BRANCH, MIX_W, D_MODEL), MIX_W ** -0.5),
        "w_out": nrm(ks[26], (DEPTH, D_MODEL, D_MODEL), D_MODEL ** -0.5),
    }


def _fwd_reference(x, c, ctx, c_ctx, norm_g, w_ada, b_ada, w_in, s5_a_re, s5_a_im, s5_log_dt,
              s5_b_re, s5_b_im, s5_c_re, s5_c_im, s5_d, s5_w_glu, conv_w, conv_b,
              na_q_g, na_k_g, na_rel_bias, gqa_q_g, gqa_k_g, gqa_sink, w_br, w_out):
    for layer in range(DEPTH):
        p = {
            "norm_g": norm_g[layer], "w_ada": w_ada[layer], "b_ada": b_ada[layer], "w_in": w_in[layer],
            "s5_a_re": s5_a_re[layer], "s5_a_im": s5_a_im[layer], "s5_log_dt": s5_log_dt[layer],
            "s5_b_re": s5_b_re[layer], "s5_b_im": s5_b_im[layer],
            "s5_c_re": s5_c_re[layer], "s5_c_im": s5_c_im[layer],
            "s5_d": s5_d[layer], "s5_w_glu": s5_w_glu[layer],
            "conv_w": conv_w[layer], "conv_b": conv_b[layer],
            "na_q_g": na_q_g[layer], "na_k_g": na_k_g[layer], "na_rel_bias": na_rel_bias[layer],
            "gqa_q_g": gqa_q_g[layer], "gqa_k_g": gqa_k_g[layer], "gqa_sink": gqa_sink[layer],
            "w_br": w_br[layer], "w_out": w_out[layer],
        }
        x, ctx = _layer(x, ctx, c, c_ctx, p, update_ctx=layer < DEPTH - 1)
    return x


import jax as _jax
import jax.numpy as _jnp

TWIN_FORMAT = 'train_step'
FWD_PARAMS = ['x', 'c', 'ctx', 'c_ctx', 'norm_g', 'w_ada', 'b_ada', 'w_in', 's5_a_re', 's5_a_im', 's5_log_dt', 's5_b_re', 's5_b_im', 's5_c_re', 's5_c_im', 's5_d', 's5_w_glu', 'conv_w', 'conv_b', 'na_q_g', 'na_k_g', 'na_rel_bias', 'gqa_q_g', 'gqa_k_g', 'gqa_sink', 'w_br', 'w_out']
TWIN_WEIGHTS = ['c_ctx', 'norm_g', 'w_ada', 'b_ada', 'w_in', 's5_a_re', 's5_a_im', 's5_log_dt', 's5_b_re', 's5_b_im', 's5_c_re', 's5_c_im', 's5_d', 's5_w_glu', 'conv_w', 'conv_b', 'na_q_g', 'na_k_g', 'na_rel_bias', 'gqa_q_g', 'gqa_k_g', 'gqa_sink', 'w_br', 'w_out']
TWIN_DIFF_INPUT = 'x'
TWIN_INPUTS = ['x', 'c', 'ctx', 'c_ctx', 'norm_g', 'w_ada', 'b_ada', 'w_in', 's5_a_re', 's5_a_im', 's5_log_dt', 's5_b_re', 's5_b_im', 's5_c_re', 's5_c_im', 's5_d', 's5_w_glu', 'conv_w', 'conv_b', 'na_q_g', 'na_k_g', 'na_rel_bias', 'gqa_q_g', 'gqa_k_g', 'gqa_sink', 'w_br', 'w_out', 'loss_target', 'm_c_ctx', 'm_norm_g', 'm_w_ada', 'm_b_ada', 'm_w_in', 'm_s5_a_re', 'm_s5_a_im', 'm_s5_log_dt', 'm_s5_b_re', 'm_s5_b_im', 'm_s5_c_re', 'm_s5_c_im', 'm_s5_d', 'm_s5_w_glu', 'm_conv_w', 'm_conv_b', 'm_na_q_g', 'm_na_k_g', 'm_na_rel_bias', 'm_gqa_q_g', 'm_gqa_k_g', 'm_gqa_sink', 'm_w_br', 'm_w_out', 'v_c_ctx', 'v_norm_g', 'v_w_ada', 'v_b_ada', 'v_w_in', 'v_s5_a_re', 'v_s5_a_im', 'v_s5_log_dt', 'v_s5_b_re', 'v_s5_b_im', 'v_s5_c_re', 'v_s5_c_im', 'v_s5_d', 'v_s5_w_glu', 'v_conv_w', 'v_conv_b', 'v_na_q_g', 'v_na_k_g', 'v_na_rel_bias', 'v_gqa_q_g', 'v_gqa_k_g', 'v_gqa_sink', 'v_w_br', 'v_w_out']
TWIN_OUTPUTS = ['loss', 'grad_x', 'grad_c_ctx', 'grad_norm_g', 'grad_w_ada', 'grad_b_ada', 'grad_w_in', 'grad_s5_a_re', 'grad_s5_a_im', 'grad_s5_log_dt', 'grad_s5_b_re', 'grad_s5_b_im', 'grad_s5_c_re', 'grad_s5_c_im', 'grad_s5_d', 'grad_s5_w_glu', 'grad_conv_w', 'grad_conv_b', 'grad_na_q_g', 'grad_na_k_g', 'grad_na_rel_bias', 'grad_gqa_q_g', 'grad_gqa_k_g', 'grad_gqa_sink', 'grad_w_br', 'grad_w_out', 'delta_c_ctx', 'delta_norm_g', 'delta_w_ada', 'delta_b_ada', 'delta_w_in', 'delta_s5_a_re', 'delta_s5_a_im', 'delta_s5_log_dt', 'delta_s5_b_re', 'delta_s5_b_im', 'delta_s5_c_re', 'delta_s5_c_im', 'delta_s5_d', 'delta_s5_w_glu', 'delta_conv_w', 'delta_conv_b', 'delta_na_q_g', 'delta_na_k_g', 'delta_na_rel_bias', 'delta_gqa_q_g', 'delta_gqa_k_g', 'delta_gqa_sink', 'delta_w_br', 'delta_w_out', 'new_m_c_ctx', 'new_m_norm_g', 'new_m_w_ada', 'new_m_b_ada', 'new_m_w_in', 'new_m_s5_a_re', 'new_m_s5_a_im', 'new_m_s5_log_dt', 'new_m_s5_b_re', 'new_m_s5_b_im', 'new_m_s5_c_re', 'new_m_s5_c_im', 'new_m_s5_d', 'new_m_s5_w_glu', 'new_m_conv_w', 'new_m_conv_b', 'new_m_na_q_g', 'new_m_na_k_g', 'new_m_na_rel_bias', 'new_m_gqa_q_g', 'new_m_gqa_k_g', 'new_m_gqa_sink', 'new_m_w_br', 'new_m_w_out', 'new_v_c_ctx', 'new_v_norm_g', 'new_v_w_ada', 'new_v_b_ada', 'new_v_w_in', 'new_v_s5_a_re', 'new_v_s5_a_im', 'new_v_s5_log_dt', 'new_v_s5_b_re', 'new_v_s5_b_im', 'new_v_s5_c_re', 'new_v_s5_c_im', 'new_v_s5_d', 'new_v_s5_w_glu', 'new_v_conv_w', 'new_v_conv_b', 'new_v_na_q_g', 'new_v_na_k_g', 'new_v_na_rel_bias', 'new_v_gqa_q_g', 'new_v_gqa_k_g', 'new_v_gqa_sink', 'new_v_w_br', 'new_v_w_out']
TWIN_LEAF_KINDS = {'loss': 'loss', 'grad_x': 'grad_x', 'grad_c_ctx': 'grad_w', 'grad_norm_g': 'grad_w', 'grad_w_ada': 'grad_w', 'grad_b_ada': 'grad_w', 'grad_w_in': 'grad_w', 'grad_s5_a_re': 'grad_w', 'grad_s5_a_im': 'grad_w', 'grad_s5_log_dt': 'grad_w', 'grad_s5_b_re': 'grad_w', 'grad_s5_b_im': 'grad_w', 'grad_s5_c_re': 'grad_w', 'grad_s5_c_im': 'grad_w', 'grad_s5_d': 'grad_w', 'grad_s5_w_glu': 'grad_w', 'grad_conv_w': 'grad_w', 'grad_conv_b': 'grad_w', 'grad_na_q_g': 'grad_w', 'grad_na_k_g': 'grad_w', 'grad_na_rel_bias': 'grad_w', 'grad_gqa_q_g': 'grad_w', 'grad_gqa_k_g': 'grad_w', 'grad_gqa_sink': 'grad_w', 'grad_w_br': 'grad_w', 'grad_w_out': 'grad_w', 'delta_c_ctx': 'delta_w', 'delta_norm_g': 'delta_w', 'delta_w_ada': 'delta_w', 'delta_b_ada': 'delta_w', 'delta_w_in': 'delta_w', 'delta_s5_a_re': 'delta_w', 'delta_s5_a_im': 'delta_w', 'delta_s5_log_dt': 'delta_w', 'delta_s5_b_re': 'delta_w', 'delta_s5_b_im': 'delta_w', 'delta_s5_c_re': 'delta_w', 'delta_s5_c_im': 'delta_w', 'delta_s5_d': 'delta_w', 'delta_s5_w_glu': 'delta_w', 'delta_conv_w': 'delta_w', 'delta_conv_b': 'delta_w', 'delta_na_q_g': 'delta_w', 'delta_na_k_g': 'delta_w', 'delta_na_rel_bias': 'delta_w', 'delta_gqa_q_g': 'delta_w', 'delta_gqa_k_g': 'delta_w', 'delta_gqa_sink': 'delta_w', 'delta_w_br': 'delta_w', 'delta_w_out': 'delta_w', 'new_m_c_ctx': 'new_m', 'new_m_norm_g': 'new_m', 'new_m_w_ada': 'new_m', 'new_m_b_ada': 'new_m', 'new_m_w_in': 'new_m', 'new_m_s5_a_re': 'new_m', 'new_m_s5_a_im': 'new_m', 'new_m_s5_log_dt': 'new_m', 'new_m_s5_b_re': 'new_m', 'new_m_s5_b_im': 'new_m', 'new_m_s5_c_re': 'new_m', 'new_m_s5_c_im': 'new_m', 'new_m_s5_d': 'new_m', 'new_m_s5_w_glu': 'new_m', 'new_m_conv_w': 'new_m', 'new_m_conv_b': 'new_m', 'new_m_na_q_g': 'new_m', 'new_m_na_k_g': 'new_m', 'new_m_na_rel_bias': 'new_m', 'new_m_gqa_q_g': 'new_m', 'new_m_gqa_k_g': 'new_m', 'new_m_gqa_sink': 'new_m', 'new_m_w_br': 'new_m', 'new_m_w_out': 'new_m', 'new_v_c_ctx': 'new_v', 'new_v_norm_g': 'new_v', 'new_v_w_ada': 'new_v', 'new_v_b_ada': 'new_v', 'new_v_w_in': 'new_v', 'new_v_s5_a_re': 'new_v', 'new_v_s5_a_im': 'new_v', 'new_v_s5_log_dt': 'new_v', 'new_v_s5_b_re': 'new_v', 'new_v_s5_b_im': 'new_v', 'new_v_s5_c_re': 'new_v', 'new_v_s5_c_im': 'new_v', 'new_v_s5_d': 'new_v', 'new_v_s5_w_glu': 'new_v', 'new_v_conv_w': 'new_v', 'new_v_conv_b': 'new_v', 'new_v_na_q_g': 'new_v', 'new_v_na_k_g': 'new_v', 'new_v_na_rel_bias': 'new_v', 'new_v_gqa_q_g': 'new_v', 'new_v_gqa_k_g': 'new_v', 'new_v_gqa_sink': 'new_v', 'new_v_w_br': 'new_v', 'new_v_w_out': 'new_v'}


def _forward(args):
    return _fwd_reference(*[args[k] for k in FWD_PARAMS])


def _output_shape():
    def fwd():
        inp = _fwd_setup_inputs(0)
        return _fwd_reference(*[inp[k] for k in FWD_PARAMS])
    out = _jax.eval_shape(fwd)
    return out.shape, out.dtype

N_MICROBATCH = 1
ADAM_LR = 0.001
ADAM_B1 = 0.9
ADAM_B2 = 0.999
ADAM_EPS = 1e-08
ADAM_WD = 0.01
ADAM_STEP = 10
PER_EXAMPLE_BATCH_AXIS = {'x': 0, 'c': 0, 'ctx': 0, 'loss_target': 0}
SHARED_INPUTS = []
_WEIGHT_DTYPES = {'c_ctx': _jnp.float32, 'norm_g': _jnp.float32, 'w_ada': _jnp.float32, 'b_ada': _jnp.float32, 'w_in': _jnp.float32, 's5_a_re': _jnp.float32, 's5_a_im': _jnp.float32, 's5_log_dt': _jnp.float32, 's5_b_re': _jnp.float32, 's5_b_im': _jnp.float32, 's5_c_re': _jnp.float32, 's5_c_im': _jnp.float32, 's5_d': _jnp.float32, 's5_w_glu': _jnp.float32, 'conv_w': _jnp.float32, 'conv_b': _jnp.float32, 'na_q_g': _jnp.float32, 'na_k_g': _jnp.float32, 'na_rel_bias': _jnp.float32, 'gqa_q_g': _jnp.float32, 'gqa_k_g': _jnp.float32, 'gqa_sink': _jnp.float32, 'w_br': _jnp.float32, 'w_out': _jnp.float32}
MOMENT_SCALE = {'c_ctx': 4.868886e-01, 'norm_g': 1.074779e+01, 'w_ada': 1.933693e+00, 'b_ada': 6.373283e+00, 'w_in': 1.256519e-01, 's5_a_re': 1.422712e-01, 's5_a_im': 4.650558e-02, 's5_log_dt': 5.591900e+00, 's5_b_re': 2.994302e-02, 's5_b_im': 2.947622e-02, 's5_c_re': 9.643834e-03, 's5_c_im': 1.043478e-02, 's5_d': 5.612822e-01, 's5_w_glu': 1.564187e-01, 'conv_w': 2.587654e+00, 'conv_b': 1.570412e-01, 'na_q_g': 1.282927e-01, 'na_k_g': 1.270808e-01, 'na_rel_bias': 2.460707e-03, 'gqa_q_g': 8.839023e-02, 'gqa_k_g': 8.587773e-02, 'gqa_sink': 1.441779e-02, 'w_br': 6.674173e-02, 'w_out': 1.227269e-01}


def _to_microbatches(a, axis):
    t = _jnp.moveaxis(a, axis, 0)
    t = t.reshape((N_MICROBATCH, t.shape[0] // N_MICROBATCH) + t.shape[1:])
    return _jnp.moveaxis(t, 1, axis + 1)


def setup_inputs(seed: int = 0) -> dict:
    inp = _fwd_setup_inputs(seed)
    key = _jax.random.fold_in(_jax.random.key(seed), 7919)
    shape, _ = _output_shape()
    out = dict(inp)
    out["loss_target"] = _jax.random.normal(_jax.random.fold_in(key, 0), shape, _jnp.float32)
    for i, name in enumerate(TWIN_WEIGHTS):
        w = inp[name].astype(_jnp.float32)
        if MOMENT_SCALE is None:
            s = _jnp.sqrt(_jnp.mean(_jnp.square(w)) + 1e-30)
        else:
            s = MOMENT_SCALE[name]
        km, kv = _jax.random.split(_jax.random.fold_in(key, i + 1))
        out[name] = w
        out["m_" + name] = s * _jax.random.normal(km, w.shape, _jnp.float32)
        out["v_" + name] = (s * s) * _jax.random.uniform(kv, w.shape, _jnp.float32, 0.5, 1.5)
    if N_MICROBATCH > 1:
        for name, axis in PER_EXAMPLE_BATCH_AXIS.items():
            out[name] = _to_microbatches(out[name], axis)
    return {'x': out['x'], 'c': out['c'], 'ctx': out['ctx'], 'c_ctx': out['c_ctx'], 'norm_g': out['norm_g'], 'w_ada': out['w_ada'], 'b_ada': out['b_ada'], 'w_in': out['w_in'], 's5_a_re': out['s5_a_re'], 's5_a_im': out['s5_a_im'], 's5_log_dt': out['s5_log_dt'], 's5_b_re': out['s5_b_re'], 's5_b_im': out['s5_b_im'], 's5_c_re': out['s5_c_re'], 's5_c_im': out['s5_c_im'], 's5_d': out['s5_d'], 's5_w_glu': out['s5_w_glu'], 'conv_w': out['conv_w'], 'conv_b': out['conv_b'], 'na_q_g': out['na_q_g'], 'na_k_g': out['na_k_g'], 'na_rel_bias': out['na_rel_bias'], 'gqa_q_g': out['gqa_q_g'], 'gqa_k_g': out['gqa_k_g'], 'gqa_sink': out['gqa_sink'], 'w_br': out['w_br'], 'w_out': out['w_out'], 'loss_target': out['loss_target'], 'm_c_ctx': out['m_c_ctx'], 'm_norm_g': out['m_norm_g'], 'm_w_ada': out['m_w_ada'], 'm_b_ada': out['m_b_ada'], 'm_w_in': out['m_w_in'], 'm_s5_a_re': out['m_s5_a_re'], 'm_s5_a_im': out['m_s5_a_im'], 'm_s5_log_dt': out['m_s5_log_dt'], 'm_s5_b_re': out['m_s5_b_re'], 'm_s5_b_im': out['m_s5_b_im'], 'm_s5_c_re': out['m_s5_c_re'], 'm_s5_c_im': out['m_s5_c_im'], 'm_s5_d': out['m_s5_d'], 'm_s5_w_glu': out['m_s5_w_glu'], 'm_conv_w': out['m_conv_w'], 'm_conv_b': out['m_conv_b'], 'm_na_q_g': out['m_na_q_g'], 'm_na_k_g': out['m_na_k_g'], 'm_na_rel_bias': out['m_na_rel_bias'], 'm_gqa_q_g': out['m_gqa_q_g'], 'm_gqa_k_g': out['m_gqa_k_g'], 'm_gqa_sink': out['m_gqa_sink'], 'm_w_br': out['m_w_br'], 'm_w_out': out['m_w_out'], 'v_c_ctx': out['v_c_ctx'], 'v_norm_g': out['v_norm_g'], 'v_w_ada': out['v_w_ada'], 'v_b_ada': out['v_b_ada'], 'v_w_in': out['v_w_in'], 'v_s5_a_re': out['v_s5_a_re'], 'v_s5_a_im': out['v_s5_a_im'], 'v_s5_log_dt': out['v_s5_log_dt'], 'v_s5_b_re': out['v_s5_b_re'], 'v_s5_b_im': out['v_s5_b_im'], 'v_s5_c_re': out['v_s5_c_re'], 'v_s5_c_im': out['v_s5_c_im'], 'v_s5_d': out['v_s5_d'], 'v_s5_w_glu': out['v_s5_w_glu'], 'v_conv_w': out['v_conv_w'], 'v_conv_b': out['v_conv_b'], 'v_na_q_g': out['v_na_q_g'], 'v_na_k_g': out['v_na_k_g'], 'v_na_rel_bias': out['v_na_rel_bias'], 'v_gqa_q_g': out['v_gqa_q_g'], 'v_gqa_k_g': out['v_gqa_k_g'], 'v_gqa_sink': out['v_gqa_sink'], 'v_w_br': out['v_w_br'], 'v_w_out': out['v_w_out']}


def _loss(weights, diff, rest, loss_target):
    with _jax.named_scope("forward"):
        args = {**rest, TWIN_DIFF_INPUT: diff, **{k: w.astype(_WEIGHT_DTYPES[k]) for k, w in weights.items()}}
        y = _forward(args)
    with _jax.named_scope("loss_head"):
        err = _jnp.square(y.astype(_jnp.float32) - loss_target)
        return 0.5 * _jnp.sum(_jnp.mean(err, axis=-1)) if err.ndim else 0.5 * err


def _adamw(w, g, m, v):
    m = ADAM_B1 * m + (1.0 - ADAM_B1) * g
    v = ADAM_B2 * v + (1.0 - ADAM_B2) * _jnp.square(g)
    m_hat = m / (1.0 - ADAM_B1 ** ADAM_STEP)
    v_hat = v / (1.0 - ADAM_B2 ** ADAM_STEP)
    delta = -ADAM_LR * (m_hat / (_jnp.sqrt(v_hat) + ADAM_EPS) + ADAM_WD * w)
    return delta, m, v


def reference(x, c, ctx, c_ctx, norm_g, w_ada, b_ada, w_in, s5_a_re, s5_a_im, s5_log_dt, s5_b_re, s5_b_im, s5_c_re, s5_c_im, s5_d, s5_w_glu, conv_w, conv_b, na_q_g, na_k_g, na_rel_bias, gqa_q_g, gqa_k_g, gqa_sink, w_br, w_out, loss_target, m_c_ctx, m_norm_g, m_w_ada, m_b_ada, m_w_in, m_s5_a_re, m_s5_a_im, m_s5_log_dt, m_s5_b_re, m_s5_b_im, m_s5_c_re, m_s5_c_im, m_s5_d, m_s5_w_glu, m_conv_w, m_conv_b, m_na_q_g, m_na_k_g, m_na_rel_bias, m_gqa_q_g, m_gqa_k_g, m_gqa_sink, m_w_br, m_w_out, v_c_ctx, v_norm_g, v_w_ada, v_b_ada, v_w_in, v_s5_a_re, v_s5_a_im, v_s5_log_dt, v_s5_b_re, v_s5_b_im, v_s5_c_re, v_s5_c_im, v_s5_d, v_s5_w_glu, v_conv_w, v_conv_b, v_na_q_g, v_na_k_g, v_na_rel_bias, v_gqa_q_g, v_gqa_k_g, v_gqa_sink, v_w_br, v_w_out):
    given = dict(x=x, c=c, ctx=ctx, c_ctx=c_ctx, norm_g=norm_g, w_ada=w_ada, b_ada=b_ada, w_in=w_in, s5_a_re=s5_a_re, s5_a_im=s5_a_im, s5_log_dt=s5_log_dt, s5_b_re=s5_b_re, s5_b_im=s5_b_im, s5_c_re=s5_c_re, s5_c_im=s5_c_im, s5_d=s5_d, s5_w_glu=s5_w_glu, conv_w=conv_w, conv_b=conv_b, na_q_g=na_q_g, na_k_g=na_k_g, na_rel_bias=na_rel_bias, gqa_q_g=gqa_q_g, gqa_k_g=gqa_k_g, gqa_sink=gqa_sink, w_br=w_br, w_out=w_out, loss_target=loss_target, m_c_ctx=m_c_ctx, m_norm_g=m_norm_g, m_w_ada=m_w_ada, m_b_ada=m_b_ada, m_w_in=m_w_in, m_s5_a_re=m_s5_a_re, m_s5_a_im=m_s5_a_im, m_s5_log_dt=m_s5_log_dt, m_s5_b_re=m_s5_b_re, m_s5_b_im=m_s5_b_im, m_s5_c_re=m_s5_c_re, m_s5_c_im=m_s5_c_im, m_s5_d=m_s5_d, m_s5_w_glu=m_s5_w_glu, m_conv_w=m_conv_w, m_conv_b=m_conv_b, m_na_q_g=m_na_q_g, m_na_k_g=m_na_k_g, m_na_rel_bias=m_na_rel_bias, m_gqa_q_g=m_gqa_q_g, m_gqa_k_g=m_gqa_k_g, m_gqa_sink=m_gqa_sink, m_w_br=m_w_br, m_w_out=m_w_out, v_c_ctx=v_c_ctx, v_norm_g=v_norm_g, v_w_ada=v_w_ada, v_b_ada=v_b_ada, v_w_in=v_w_in, v_s5_a_re=v_s5_a_re, v_s5_a_im=v_s5_a_im, v_s5_log_dt=v_s5_log_dt, v_s5_b_re=v_s5_b_re, v_s5_b_im=v_s5_b_im, v_s5_c_re=v_s5_c_re, v_s5_c_im=v_s5_c_im, v_s5_d=v_s5_d, v_s5_w_glu=v_s5_w_glu, v_conv_w=v_conv_w, v_conv_b=v_conv_b, v_na_q_g=v_na_q_g, v_na_k_g=v_na_k_g, v_na_rel_bias=v_na_rel_bias, v_gqa_q_g=v_gqa_q_g, v_gqa_k_g=v_gqa_k_g, v_gqa_sink=v_gqa_sink, v_w_br=v_w_br, v_w_out=v_w_out)
    weights = {n: given[n] for n in TWIN_WEIGHTS}
    shared = {n: given[n] for n in SHARED_INPUTS}
    per_example = {n: given[n] for n in ['x', 'c', 'ctx']}
    grad_fn = _jax.value_and_grad(_loss, argnums=(0, 1))

    def one_microbatch(ex, loss_target):
        ex = dict(ex)
        diff = ex.pop(TWIN_DIFF_INPUT)
        return grad_fn(weights, diff, {**shared, **ex}, loss_target)

    if N_MICROBATCH == 1:
        loss, (grad_w, grad_x) = one_microbatch(per_example, given["loss_target"])
    else:
        def body(carry, xs):
            loss_sum, grad_sum = carry
            l_k, (gw_k, gx_k) = one_microbatch(xs[0], xs[1])
            with _jax.named_scope("update"):
                return (loss_sum + l_k, _jax.tree.map(_jnp.add, grad_sum, gw_k)), gx_k

        init = (_jnp.zeros((), _jnp.float32), _jax.tree.map(_jnp.zeros_like, weights))
        (loss, grad_w), grad_x = _jax.lax.scan(body, init, (per_example, given["loss_target"]))
    with _jax.named_scope("update"):
        delta_w, new_m, new_v = {}, {}, {}
        for n in TWIN_WEIGHTS:
            delta_w[n], new_m[n], new_v[n] = _adamw(weights[n], grad_w[n], given["m_" + n], given["v_" + n])
    return (loss, grad_x, *[grad_w[n] for n in TWIN_WEIGHTS], *[delta_w[n] for n in TWIN_WEIGHTS],
            *[new_m[n] for n in TWIN_WEIGHTS], *[new_v[n] for n in TWIN_WEIGHTS])
```

```python
import functools

import numpy as np
import jax
import jax.numpy as jnp
from jax import lax
from jax.experimental import pallas as pl
from jax.experimental.pallas import tpu as pltpu

F32 = jnp.float32
BF16 = jnp.bfloat16

D_MODEL = 1024
MIX_W = 512
GRID_W = 64
HEAD_DIM = 64
N_HEADS = 8
S5_GROUPS = 32
S5_CH = 16
S5_CHUNK = 32
NA_ROWS = 8
NA_COLS = 16
WINDOW = 128
ROPE_BASE = 10000.0
ROPE_PAIRS = 16
EPS = 1e-6
NEG_INF = -1e30
ATT_BLK = 256
TOK = 256
VMEM_LIMIT = 56 * 1024 * 1024

ADAM_LR, ADAM_B1, ADAM_B2, ADAM_EPS, ADAM_WD, ADAM_STEP = 0.001, 0.9, 0.999, 1e-8, 0.01, 10

OFF = dict(s5_u=0, s5_gate=512, conv_v=1024, conv_b=1536, conv_c=2048, conv_gate=2560,
           na_q=3072, na_k=3584, na_v=4096, na_gate=4608, gqa_q=5120, gqa_k=5632, gqa_v=5760,
           pad=5888, gqa_gate=6144, merge_s5=6656, merge_conv=7680, merge_na=8704, merge_gqa=9728)
N_Z = 10752
GATE_OFFS = (OFF["s5_gate"], OFF["conv_gate"], OFF["na_gate"], OFF["gqa_gate"])
MERGE_OFFS = (OFF["merge_s5"], OFF["merge_conv"], OFF["merge_na"], OFF["merge_gqa"])


def _cparams(sem):
    return pltpu.CompilerParams(dimension_semantics=sem, vmem_limit_bytes=VMEM_LIMIT)


def _dot(a, b, ca, cb):
    return lax.dot_general(a.astype(BF16), b.astype(BF16), (((ca,), (cb,)), ((), ())),
                           preferred_element_type=F32)


def _dot_tn(a, b):
    return _dot(a.astype(F32).T, b, 1, 0)


@jax.custom_vjp
def mm(a, b):
    return _dot(a, b, 1, 0)


@jax.custom_vjp
def mm_nt(a, b):
    return _dot(a, b, 1, 1)


@jax.custom_vjp
def mm_tn(a, b):
    return _dot_tn(a, b)


mm.defvjp(lambda a, b: (mm(a, b), (a, b)), lambda r, g: (mm_nt(g, r[1]), mm_tn(r[0], g)))
mm_nt.defvjp(lambda a, b: (mm_nt(a, b), (a, b)), lambda r, g: (mm(g, r[1]), mm_tn(g, r[0])))
mm_tn.defvjp(lambda a, b: (mm_tn(a, b), (a, b)), lambda r, g: (mm_nt(r[1], g), mm(r[0], g)))


@functools.partial(jax.custom_vjp, nondiff_argnums=(1,))
def lane_roll(x, shift):
    return pltpu.roll(x, shift, 1)


lane_roll.defvjp(lambda x, shift: (lane_roll(x, shift), None),
                 lambda shift, _, g: (lane_roll(g, (g.shape[1] - shift) % g.shape[1]),))


def _silu(x):
    return x * jax.nn.sigmoid(x)


def _dsilu(x):
    s = jax.nn.sigmoid(x)
    return s * (1.0 + x * (1.0 - s))


def _pick(n, prefs):
    for p in prefs:
        if n % p == 0:
            return p
    return n


def _matmul(a, b, *, trans_b=False, out_dtype=F32, tm=None, tn=None, tk=None, name):
    squeeze = a.ndim == 2
    if squeeze:
        a, b = a[None], b[None]
    nb, m, k = a.shape
    n = b.shape[1] if trans_b else b.shape[2]
    tm = tm or _pick(m, (1280, 1024, 640, 512, 256, 128))
    tn = tn or _pick(n, (1536, 1024, 512, 256, 128))
    tk = tk or _pick(k, (1536, 1280, 1024, 768, 640, 512, 256, 128))
    nk = k // tk

    def body(a_ref, b_ref, o_ref, *scr):
        part = _dot(a_ref[...], b_ref[...], 1, 1 if trans_b else 0)
        if nk == 1:
            o_ref[...] = part.astype(out_dtype)
        else:
            acc = scr[0]
            kk = pl.program_id(3)

            @pl.when(kk == 0)
            def _():
                acc[...] = part

            @pl.when(kk > 0)
            def _():
                acc[...] += part

            @pl.when(kk == nk - 1)
            def _():
                o_ref[...] = acc[...].astype(out_dtype)

    if trans_b:
        b_spec = pl.BlockSpec((None, tn, tk), lambda bb, i, j, kk: (bb, j, kk))
    else:
        b_spec = pl.BlockSpec((None, tk, tn), lambda bb, i, j, kk: (bb, kk, j))
    out = pl.pallas_call(
        body, name=name,
        grid=(nb, m // tm, n // tn, nk),
        in_specs=[pl.BlockSpec((None, tm, tk), lambda bb, i, j, kk: (bb, i, kk)), b_spec],
        out_specs=pl.BlockSpec((None, tm, tn), lambda bb, i, j, kk: (bb, i, j)),
        out_shape=jax.ShapeDtypeStruct((nb, m, n), out_dtype),
        scratch_shapes=[] if nk == 1 else [pltpu.VMEM((tm, tn), F32)],
        compiler_params=_cparams(("parallel", "parallel", "parallel", "arbitrary")),
    )(a, b)
    return out[0] if squeeze else out


def _adaln_fn(cc, w, b):
    return mm(_silu(cc), w) + b


def _adaln_fwd(cc, w_ada, b_ada):
    def body(cc_ref, w_ref, b_ref, o_ref):
        o_ref[...] = _adaln_fn(cc_ref[...], w_ref[...], b_ref[...])

    return pl.pallas_call(
        body, name="adaln_fwd", out_shape=jax.ShapeDtypeStruct((8, 3 * D_MODEL), F32),
        compiler_params=pltpu.CompilerParams(vmem_limit_bytes=VMEM_LIMIT),
    )(cc, w_ada, b_ada)


def _adaln_bwd(cc, w_ada, dmod):
    def body(cc_ref, w_ref, g_ref, dcc_ref, dw_ref, db_ref):
        cc_v, g = cc_ref[...], g_ref[...]
        dw_ref[...] = mm_tn(_silu(cc_v), g)
        db_ref[...] = jnp.sum(g, axis=0, keepdims=True)
        dcc_ref[...] = mm_nt(g, w_ref[...]) * _dsilu(cc_v)

    return pl.pallas_call(
        body, name="adaln_bwd",
        out_shape=(jax.ShapeDtypeStruct((8, D_MODEL), F32),
                   jax.ShapeDtypeStruct((D_MODEL, 3 * D_MODEL), F32),
                   jax.ShapeDtypeStruct((1, 3 * D_MODEL), F32)),
        compiler_params=pltpu.CompilerParams(vmem_limit_bytes=VMEM_LIMIT),
    )(cc, w_ada, dmod)


def _seg_spec(which, n_ctx_tiles):
    return pl.BlockSpec((None, None, 1, D_MODEL),
                        lambda i: (jnp.where(i < n_ctx_tiles, 0, 1), which, 0, 0))


def _row_spec(width, col_block=0, tile=TOK):
    return pl.BlockSpec((tile, width), lambda i: (i, col_block))


def _const_spec(shape):
    zeros = (0,) * len(shape)
    return pl.BlockSpec(shape, lambda i: zeros)


def _modnorm_fn(x, g, shift, scale):
    y = x * lax.rsqrt(jnp.mean(x * x, axis=-1, keepdims=True) + EPS)
    return (y * g) * (1.0 + scale) + shift


def _modnorm_fwd(xt, g, mod4, n_ctx):
    t = xt.shape[0]
    nct = n_ctx // TOK

    def body(x_ref, g_ref, sh_ref, sc_ref, o_ref):
        o_ref[...] = _modnorm_fn(x_ref[...], g_ref[...], sh_ref[...], sc_ref[...]).astype(BF16)

    return pl.pallas_call(
        body, name="modnorm_fwd", grid=(t // TOK,),
        in_specs=[_row_spec(D_MODEL), _const_spec((1, D_MODEL)), _seg_spec(0, nct), _seg_spec(1, nct)],
        out_specs=_row_spec(D_MODEL),
        out_shape=jax.ShapeDtypeStruct((t, D_MODEL), BF16),
        compiler_params=_cparams(("parallel",)),
    )(xt, g, mod4, mod4)


def _modnorm_bwd(xt, g, mod4, dh, dres, n_ctx):
    t = xt.shape[0]
    nct = n_ctx // TOK

    def body(x_ref, g_ref, sh_ref, sc_ref, dh_ref, dres_ref, dx_ref, dg_ref, dsh_ref, dsc_ref):
        i = pl.program_id(0)
        _, vjp = jax.vjp(_modnorm_fn, x_ref[...], g_ref[...], sh_ref[...], sc_ref[...])
        dx, dg, dsh, dsc = vjp(dh_ref[...])
        dx_ref[...] = dx + dres_ref[...]

        @pl.when(i == 0)
        def _():
            dg_ref[...] = jnp.zeros_like(dg_ref)

        dg_ref[...] += dg
        first = jnp.logical_or(i == 0, i == nct)

        @pl.when(first)
        def _():
            dsh_ref[...] = dsh
            dsc_ref[...] = dsc

        @pl.when(jnp.logical_not(first))
        def _():
            dsh_ref[...] += dsh
            dsc_ref[...] += dsc

    seg_out = lambda which: pl.BlockSpec((None, None, 1, D_MODEL),
                                         lambda i: (jnp.where(i < nct, 0, 1), which, 0, 0))
    dx, dg, dss, dss2 = pl.pallas_call(
        body, name="modnorm_bwd", grid=(t // TOK,),
        in_specs=[_row_spec(D_MODEL), _const_spec((1, D_MODEL)), _seg_spec(0, nct), _seg_spec(1, nct),
                  _row_spec(D_MODEL), _row_spec(D_MODEL)],
        out_specs=[_row_spec(D_MODEL), _const_spec((1, D_MODEL)), seg_out(0), seg_out(0)],
        out_shape=(jax.ShapeDtypeStruct((t, D_MODEL), F32), jax.ShapeDtypeStruct((1, D_MODEL), F32),
                   jax.ShapeDtypeStruct((2, 1, 1, D_MODEL), F32), jax.ShapeDtypeStruct((2, 1, 1, D_MODEL), F32)),
        compiler_params=_cparams(("arbitrary",)),
    )(xt, g, mod4, mod4, dh, dres)
    return dx, dg, dss, dss2


def _group_mean_sq(x, gs):
    x2 = x * x
    hi = x2.astype(BF16).astype(F32)
    return mm(hi, gs) + mm(x2 - hi, gs)


def _head_norm(x, g, gs):
    return (x * lax.rsqrt(_group_mean_sq(x, gs) + EPS)) * g


def _rope(x, cos, sin_signed):
    lane = lax.broadcasted_iota(jnp.int32, (1, 128), 1)
    first_half = jnp.bitwise_and(lane, 63) < 32
    cols = []
    for c in range(x.shape[1] // 128):
        xb = x[:, 128 * c:128 * (c + 1)]
        partner = jnp.where(first_half, lane_roll(xb, 96), lane_roll(xb, 32))
        cols.append(xb * cos + partner * sin_signed)
    return cols[0] if len(cols) == 1 else jnp.concatenate(cols, axis=1)


def _prep_fn(zq_na, zk_na, zv_na, zq_g, zk_g, zv_g, g_naq, g_nak, g_gq, g_gk, cos, sin_signed, gs512, gs128, expand):
    q_na = _head_norm(zq_na, g_naq, gs512)
    k_na = _head_norm(zk_na, g_nak, gs512)
    q_g = _rope(_head_norm(zq_g, g_gq, gs512), cos, sin_signed)
    k_g = _rope(_head_norm(zk_g, g_gk, gs128), cos, sin_signed)
    return q_na, k_na, zv_na, q_g, mm(k_g, expand), mm(zv_g, expand)


def _prep_consts():
    gid = np.arange(512) // 64
    gs512 = (gid[:, None] == gid[None, :]).astype(np.float32) / 64.0
    expand = np.zeros((128, 512), np.float32)
    for h in range(N_HEADS):
        for j in range(64):
            expand[64 * (h // 4) + j, 64 * h + j] = 1.0
    return jnp.asarray(gs512), jnp.asarray(gs512[:128, :128]), jnp.asarray(expand)


def _prep_in_specs():
    blk = lambda off, w: _row_spec(w, off // w)
    return [blk(OFF["na_q"], 512), blk(OFF["na_k"], 512), blk(OFF["na_v"], 512), blk(OFF["gqa_q"], 512),
            blk(OFF["gqa_k"], 128), blk(OFF["gqa_v"], 128),
            _const_spec((1, 512)), _const_spec((1, 512)), _const_spec((1, 512)), _const_spec((1, 128)),
            _row_spec(128), _row_spec(128),
            _const_spec((512, 512)), _const_spec((128, 128)), _const_spec((128, 512))]


def _prep_fwd(z, gains, rope_tabs):
    t = z.shape[0]
    consts = _prep_consts()

    def body(*refs):
        ins, outs = refs[:15], refs[15:]
        res = _prep_fn(*[r[...] for r in ins])
        for o_ref, v in zip(outs, res):
            o_ref[...] = v.astype(BF16)

    return pl.pallas_call(
        body, name="prep_fwd", grid=(t // TOK,),
        in_specs=_prep_in_specs(),
        out_specs=[_row_spec(512)] * 6,
        out_shape=tuple(jax.ShapeDtypeStruct((t, 512), BF16) for _ in range(6)),
        compiler_params=_cparams(("parallel",)),
    )(z, z, z, z, z, z, *gains, *rope_tabs, *consts)


def _prep_bwd(z, gains, rope_tabs, cots, du_a, du_b):
    t = z.shape[0]
    consts = _prep_consts()

    def body(*refs):
        ins, cot, (dua_ref, dub_ref), outs = refs[:15], refs[15:21], refs[21:23], refs[23:]
        i = pl.program_id(0)
        vals = [r[...] for r in ins]
        _, vjp = jax.vjp(lambda *a: _prep_fn(*a, *vals[10:]), *vals[:10])
        grads = vjp(tuple(c[...] for c in cot))
        for o_ref, v in zip(outs[:6], grads[:6]):
            o_ref[...] = v.astype(BF16)
        outs[6][...] = (dua_ref[...] + dub_ref[...]).astype(BF16)

        @pl.when(i == 0)
        def _():
            for o_ref in outs[7:]:
                o_ref[...] = jnp.zeros_like(o_ref)

        for o_ref, v in zip(outs[7:], grads[6:10]):
            o_ref[...] += v

    return pl.pallas_call(
        body, name="prep_bwd", grid=(t // TOK,),
        in_specs=_prep_in_specs() + [_row_spec(512)] * 8,
        out_specs=[_row_spec(512)] * 4 + [_row_spec(128)] * 2 + [_row_spec(512)]
        + [_const_spec((1, 512))] * 3 + [_const_spec((1, 128))],
        out_shape=tuple([jax.ShapeDtypeStruct((t, 512), BF16)] * 4 + [jax.ShapeDtypeStruct((t, 128), BF16)] * 2
                        + [jax.ShapeDtypeStruct((t, 512), BF16)]
                        + [jax.ShapeDtypeStruct((1, 512), F32)] * 3 + [jax.ShapeDtypeStruct((1, 128), F32)]),
        compiler_params=_cparams(("arbitrary",)),
    )(z, z, z, z, z, z, *gains, *rope_tabs, *consts, *cots, du_a, du_b)


def _s5post_fn(ys, u, d, w_glu):
    y = jax.nn.gelu(ys + d * u)
    return y * jax.nn.sigmoid(mm(y, w_glu))


def _s5post_fwd(ys, z, d, w_glu):
    t = z.shape[0]

    def body(ys_ref, u_ref, d_ref, w_ref, o_ref):
        o_ref[...] = _s5post_fn(ys_ref[...], u_ref[...], d_ref[...], w_ref[...])

    return pl.pallas_call(
        body, name="s5post_fwd", grid=(t // TOK,),
        in_specs=[_row_spec(512), _row_spec(512, OFF["s5_u"] // 512),
                  _const_spec((1, 512)), _const_spec((512, 512))],
        out_specs=_row_spec(512), out_shape=jax.ShapeDtypeStruct((t, 512), F32),
        compiler_params=_cparams(("parallel",)),
    )(ys, z, d, w_glu)


def _s5post_bwd(ys, z, d, w_glu, dy):
    t = z.shape[0]

    def body(ys_ref, u_ref, d_ref, w_ref, dy_ref, dpre_ref, du_ref, dd_ref, dw_ref):
        i = pl.program_id(0)
        _, vjp = jax.vjp(_s5post_fn, ys_ref[...], u_ref[...], d_ref[...], w_ref[...].astype(F32))
        dys, du, dd, dw = vjp(dy_ref[...])
        dpre_ref[...] = dys.astype(BF16)
        du_ref[...] = du

        @pl.when(i == 0)
        def _():
            dd_ref[...] = jnp.zeros_like(dd_ref)
            dw_ref[...] = jnp.zeros_like(dw_ref)

        dd_ref[...] += dd
        dw_ref[...] += dw

    return pl.pallas_call(
        body, name="s5post_bwd", grid=(t // TOK,),
        in_specs=[_row_spec(512), _row_spec(512, OFF["s5_u"] // 512),
                  _const_spec((1, 512)), _const_spec((512, 512)), _row_spec(512)],
        out_specs=[_row_spec(512), _row_spec(512), _const_spec((1, 512)), _const_spec((512, 512))],
        out_shape=(jax.ShapeDtypeStruct((t, 512), BF16), jax.ShapeDtypeStruct((t, 512), F32),
                   jax.ShapeDtypeStruct((1, 512), F32), jax.ShapeDtypeStruct((512, 512), F32)),
        compiler_params=_cparams(("arbitrary",)),
    )(ys, z, d, w_glu, dy)


def _halo_specs(col_block, t):
    last = t // 8 - 1
    prev = pl.BlockSpec((8, 512), lambda i: (jnp.maximum(i * (TOK // 8) - 1, 0), col_block))
    nxt = pl.BlockSpec((8, 512), lambda i: (jnp.minimum((i + 1) * (TOK // 8), last), col_block))
    return [_row_spec(512, col_block), prev, nxt]


def _shifted(cur, prev_row, next_row, tok0, n_ctx, t_total):
    row = lax.broadcasted_iota(jnp.int32, (TOK, 1), 0)
    tpos = row + tok0
    down = jnp.where(row == 0, prev_row, pltpu.roll(cur, 1, 0))
    down = jnp.where(jnp.logical_or(tpos == 0, tpos == n_ctx), 0.0, down)
    up = jnp.where(row == TOK - 1, next_row, pltpu.roll(cur, TOK - 1, 0))
    up = jnp.where(jnp.logical_or(tpos == n_ctx - 1, tpos == t_total - 1), 0.0, up)
    return down, up


def _conv_fwd(z, conv_w, conv_b, n_ctx):
    t = z.shape[0]

    def body(v_ref, vp_ref, vn_ref, c_ref, cp_ref, cn_ref, b_ref, w_ref, cb_ref, o_ref):
        tok0 = pl.program_id(0) * TOK
        zz = v_ref[...] * c_ref[...]
        zz_m1, zz_p1 = _shifted(zz, vp_ref[7:8, :] * cp_ref[7:8, :], vn_ref[0:1, :] * cn_ref[0:1, :], tok0, n_ctx, t)
        s = cb_ref[...] + zz_m1 * w_ref[0:1, :] + zz * w_ref[1:2, :] + zz_p1 * w_ref[2:3, :]
        o_ref[...] = b_ref[...] * s

    return pl.pallas_call(
        body, name="conv_fwd", grid=(t // TOK,),
        in_specs=_halo_specs(OFF["conv_v"] // 512, t) + _halo_specs(OFF["conv_c"] // 512, t)
        + [_row_spec(512, OFF["conv_b"] // 512), _const_spec((8, 512)), _const_spec((1, 512))],
        out_specs=_row_spec(512), out_shape=jax.ShapeDtypeStruct((t, 512), F32),
        compiler_params=_cparams(("parallel",)),
    )(z, z, z, z, z, z, z, conv_w, conv_b)


def _conv_bwd(z, conv_w, conv_b, dy, n_ctx):
    t = z.shape[0]

    def body(v_ref, vp_ref, vn_ref, c_ref, cp_ref, cn_ref, b_ref, bp_ref, bn_ref, dy_ref, dyp_ref, dyn_ref,
             w_ref, cb_ref, dv_ref, db_ref, dc_ref, dw_ref, dcb_ref):
        i = pl.program_id(0)
        tok0 = i * TOK
        v, c, b, dy_v = v_ref[...], c_ref[...], b_ref[...], dy_ref[...]
        w0, w1, w2 = w_ref[0:1, :], w_ref[1:2, :], w_ref[2:3, :]
        zz = v * c
        zz_m1, zz_p1 = _shifted(zz, vp_ref[7:8, :] * cp_ref[7:8, :], vn_ref[0:1, :] * cn_ref[0:1, :], tok0, n_ctx, t)
        s = cb_ref[...] + zz_m1 * w0 + zz * w1 + zz_p1 * w2
        ds = dy_v * b
        ds_m1, ds_p1 = _shifted(ds, dyp_ref[7:8, :] * bp_ref[7:8, :], dyn_ref[0:1, :] * bn_ref[0:1, :], tok0, n_ctx, t)
        dzz = ds_p1 * w0 + ds * w1 + ds_m1 * w2
        db_ref[...] = (dy_v * s).astype(BF16)
        dv_ref[...] = (dzz * c).astype(BF16)
        dc_ref[...] = (dzz * v).astype(BF16)

        @pl.when(i == 0)
        def _():
            dw_ref[...] = jnp.zeros_like(dw_ref)
            dcb_ref[...] = jnp.zeros_like(dcb_ref)

        rsum = lambda a: jnp.sum(a, axis=0, keepdims=True)
        dw_ref[0:1, :] += rsum(ds * zz_m1)
        dw_ref[1:2, :] += rsum(ds * zz)
        dw_ref[2:3, :] += rsum(ds * zz_p1)
        dcb_ref[...] += rsum(ds)

    return pl.pallas_call(
        body, name="conv_bwd", grid=(t // TOK,),
        in_specs=_halo_specs(OFF["conv_v"] // 512, t) + _halo_specs(OFF["conv_c"] // 512, t)
        + _halo_specs(OFF["conv_b"] // 512, t) + _halo_specs(0, t) + [_const_spec((8, 512)), _const_spec((1, 512))],
        out_specs=[_row_spec(512)] * 3 + [_const_spec((8, 512)), _const_spec((1, 512))],
        out_shape=tuple([jax.ShapeDtypeStruct((t, 512), BF16)] * 3
                        + [jax.ShapeDtypeStruct((8, 512), F32), jax.ShapeDtypeStruct((1, 512), F32)]),
        compiler_params=_cparams(("arbitrary",)),
    )(z, z, z, z, z, z, z, z, z, dy, dy, dy, conv_w, conv_b)


def _merge_col_specs(tile):
    specs = []
    for off in MERGE_OFFS:
        specs.append(pl.BlockSpec((tile, 512), functools.partial(lambda i, cb: (i, cb), cb=off // 512)))
        specs.append(pl.BlockSpec((tile, 512), functools.partial(lambda i, cb: (i, cb), cb=off // 512 + 1)))
    return specs


def _merge_fwd(xt, ys, z, mod4, w_br, w_out, n_ctx):
    t = xt.shape[0]
    nct = n_ctx // TOK

    def body(x_ref, *refs):
        y_refs, gt_refs, mg_refs = refs[0:4], refs[4:8], refs[8:16]
        gate_ref, wbr_ref, wout_ref, o_ref = refs[16:20]
        acc_lo = acc_hi = None
        for k in range(4):
            gated = y_refs[k][...] * _silu(gt_refs[k][...])
            proj = mm(gated, wbr_ref[k])
            lo = jax.nn.sigmoid(mg_refs[2 * k][...]) * proj[:, :512]
            hi = jax.nn.sigmoid(mg_refs[2 * k + 1][...]) * proj[:, 512:]
            acc_lo = lo if acc_lo is None else acc_lo + lo
            acc_hi = hi if acc_hi is None else acc_hi + hi
        acc = jnp.concatenate([acc_lo, acc_hi], axis=1)
        o_ref[...] = x_ref[...] + gate_ref[...] * mm(acc, wout_ref[...])

    gate_specs = [pl.BlockSpec((TOK, 512), functools.partial(lambda i, cb: (i, cb), cb=o // 512)) for o in GATE_OFFS]
    return pl.pallas_call(
        body, name="merge_fwd", grid=(t // TOK,),
        in_specs=[_row_spec(D_MODEL)] + [_row_spec(512)] * 4 + gate_specs + _merge_col_specs(TOK)
        + [_seg_spec(2, nct), _const_spec((4, 512, 1024)), _const_spec((1024, 1024))],
        out_specs=_row_spec(D_MODEL), out_shape=jax.ShapeDtypeStruct((t, D_MODEL), F32),
        compiler_params=_cparams(("parallel",)),
    )(xt, *ys, z, z, z, z, z, z, z, z, z, z, z, z, mod4, w_br, w_out)


MERGE_BWD_TILE = 128


def _merge_bwd(g, ys, z, mod4, w_br, w_out, n_ctx):
    t = g.shape[0]
    tile = MERGE_BWD_TILE
    nct = n_ctx // tile
    nsteps = t // tile

    def body(g_ref, *refs):
        y_refs, gt_refs, mg_refs = refs[0:4], refs[4:8], refs[8:16]
        gate_ref, wbr_hbm, wout_hbm = refs[16:19]
        dy_refs, dgt_refs, dmg_refs = refs[19:23], refs[23:27], refs[27:31]
        dgate_ref, dwbr_hbm, dwout_hbm = refs[31:34]
        wbr_v, wout_v, dwbr_acc, dwout_acc = refs[34:38]
        i = pl.program_id(0)

        @pl.when(i == 0)
        def _():
            pltpu.sync_copy(wbr_hbm, wbr_v)
            pltpu.sync_copy(wout_hbm, wout_v)
            dwbr_acc[...] = jnp.zeros_like(dwbr_acc)
            dwout_acc[...] = jnp.zeros_like(dwout_acc)

        g_v, gate = g_ref[...], gate_ref[...]
        gated, proj, sig = [], [], []
        acc = None
        for k in range(4):
            gated.append(y_refs[k][...] * _silu(gt_refs[k][...]))
            proj.append(mm(gated[k], wbr_v[k]))
            sig.append(jax.nn.sigmoid(jnp.concatenate([mg_refs[2 * k][...], mg_refs[2 * k + 1][...]], axis=1)))
            contrib = sig[k] * proj[k]
            acc = contrib if acc is None else acc + contrib
        o = mm(acc, wout_v[...])
        dgate = jnp.sum(g_v * o, axis=0, keepdims=True)
        first = jnp.logical_or(i == 0, i == nct)

        @pl.when(first)
        def _():
            dgate_ref[...] = dgate

        @pl.when(jnp.logical_not(first))
        def _():
            dgate_ref[...] += dgate

        do = g_v * gate
        dwout_acc[...] += mm_tn(acc, do)
        dacc = mm_nt(do, wout_v[...])
        for k in range(4):
            dmg_refs[k][...] = (dacc * proj[k] * sig[k] * (1.0 - sig[k])).astype(BF16)
            dproj = dacc * sig[k]
            dwbr_acc[k] += mm_tn(gated[k], dproj)
            dgated = mm_nt(dproj, wbr_v[k])
            gt = gt_refs[k][...]
            dy_refs[k][...] = dgated * _silu(gt)
            dgt_refs[k][...] = (dgated * y_refs[k][...] * _dsilu(gt)).astype(BF16)

        @pl.when(i == nsteps - 1)
        def _():
            pltpu.sync_copy(dwbr_acc, dwbr_hbm)
            pltpu.sync_copy(dwout_acc, dwout_hbm)

    row = lambda w: _row_spec(w, 0, tile)
    gate_specs = [pl.BlockSpec((tile, 512), functools.partial(lambda i, cb: (i, cb), cb=o // 512)) for o in GATE_OFFS]
    anyspec = pl.BlockSpec(memory_space=pl.ANY)
    seg = pl.BlockSpec((None, None, 1, D_MODEL), lambda i: (jnp.where(i < nct, 0, 1), 2, 0, 0))
    seg_out = pl.BlockSpec((None, None, 1, D_MODEL), lambda i: (jnp.where(i < nct, 0, 1), 0, 0, 0))
    res = pl.pallas_call(
        body, name="merge_bwd", grid=(nsteps,),
        in_specs=[row(D_MODEL)] + [row(512)] * 4 + gate_specs + _merge_col_specs(tile) + [seg, anyspec, anyspec],
        out_specs=[row(512)] * 8 + [row(1024)] * 4 + [seg_out, anyspec, anyspec],
        out_shape=tuple([jax.ShapeDtypeStruct((t, 512), F32)] * 4 + [jax.ShapeDtypeStruct((t, 512), BF16)] * 4
                        + [jax.ShapeDtypeStruct((t, 1024), BF16)] * 4
                        + [jax.ShapeDtypeStruct((2, 1, 1, D_MODEL), F32),
                           jax.ShapeDtypeStruct((4, 512, 1024), F32), jax.ShapeDtypeStruct((1024, 1024), F32)]),
        scratch_shapes=[pltpu.VMEM((4, 512, 1024), BF16), pltpu.VMEM((1024, 1024), BF16),
                        pltpu.VMEM((4, 512, 1024), F32), pltpu.VMEM((1024, 1024), F32)],
        compiler_params=_cparams(("arbitrary",)),
    )(g, *ys, z, z, z, z, z, z, z, z, z, z, z, z, mod4, w_br, w_out)
    return res[0:4], res[4:8], res[8:12], res[12], res[13], res[14]


SCAN_ROWS = 16


def _scan_spec(k):
    return pl.BlockSpec((k, SCAN_ROWS, 128), lambda i: (0, i, 0))


def _scan_chunk(j, k, n_ctx_chunks, is_backward):
    backward = jnp.where(j < n_ctx_chunks, n_ctx_chunks - 1 - j, k - 1 - (j - n_ctx_chunks))
    return jnp.where(is_backward, backward, j)


def _scan_block_is_backward():
    return pl.program_id(0) >= 32 // SCAN_ROWS


def _s5_scan_fwd(s, a1, a2, k, n_ctx_chunks):
    kp = s.shape[0]

    def body(s_ref, a1_ref, a2_ref, hp_ref):
        a1_v, a2_v = a1_ref[...], a2_ref[...]
        hp_ref[...] = jnp.zeros_like(hp_ref)
        is_backward = _scan_block_is_backward()

        def step(j, h):
            c = _scan_chunk(j, k, n_ctx_chunks, is_backward)
            hp_ref[c] = h
            return a1_v * h + a2_v * pltpu.roll(h, 64, 1) + s_ref[c]

        lax.fori_loop(0, k, step, jnp.zeros((SCAN_ROWS, 128), F32))

    vec = pl.BlockSpec((SCAN_ROWS, 128), lambda i: (i, 0))
    return pl.pallas_call(
        body, name="s5_scan_fwd", grid=(64 // SCAN_ROWS,),
        in_specs=[_scan_spec(kp), vec, vec], out_specs=_scan_spec(kp),
        out_shape=jax.ShapeDtypeStruct(s.shape, F32),
        compiler_params=_cparams(("parallel",)),
    )(s, a1, a2)


def _s5_scan_bwd(dhp, hp, a1, a2, k, n_ctx_chunks):
    kp = hp.shape[0]

    def body(dhp_ref, hp_ref, a1_ref, a2_ref, ds_ref, da1_ref, da2_ref):
        a1_v, a2_v = a1_ref[...], a2_ref[...]
        ds_ref[...] = jnp.zeros_like(ds_ref)
        is_backward = _scan_block_is_backward()

        def step(j, carry):
            lam, d1, d2 = carry
            c = _scan_chunk(k - 1 - j, k, n_ctx_chunks, is_backward)
            ds_ref[c] = lam
            h = hp_ref[c]
            d1 = d1 + lam * h
            d2 = d2 + lam * pltpu.roll(h, 64, 1)
            lam = dhp_ref[c] + a1_v * lam + pltpu.roll(a2_v * lam, 64, 1)
            return lam, d1, d2

        zero = jnp.zeros((SCAN_ROWS, 128), F32)
        _, d1, d2 = lax.fori_loop(0, k, step, (zero, zero, zero))
        da1_ref[...] = d1
        da2_ref[...] = d2

    vec = pl.BlockSpec((SCAN_ROWS, 128), lambda i: (i, 0))
    return pl.pallas_call(
        body, name="s5_scan_bwd", grid=(64 // SCAN_ROWS,),
        in_specs=[_scan_spec(kp), _scan_spec(kp), vec, vec], out_specs=[_scan_spec(kp), vec, vec],
        out_shape=(jax.ShapeDtypeStruct(hp.shape, F32), jax.ShapeDtypeStruct((64, 128), F32),
                   jax.ShapeDtypeStruct((64, 128), F32)),
        compiler_params=_cparams(("parallel",)),
    )(dhp, hp, a1, a2)


def _attn_block(q, k3, v3, kc, vc, bias0, bias1, sink, qb, *, mode, n_lat):
    lane = lax.broadcasted_iota(jnp.int32, (1, 128), 1)
    iq = lax.broadcasted_iota(jnp.int32, (ATT_BLK, 3 * ATT_BLK), 0)
    ik = lax.broadcasted_iota(jnp.int32, (ATT_BLK, 3 * ATT_BLK), 1)
    ql = qb - 1
    if mode == "na":
        n_rows = n_lat // GRID_W
        r = 4 * ql + lax.shift_right_logical(iq, 6)
        qcol = jnp.bitwise_and(iq, 63)
        kr = 4 * (ql - 1) + lax.shift_right_logical(ik, 6)
        kcol = jnp.bitwise_and(ik, 63)
        rs = jnp.clip(r - NA_ROWS // 2, 0, n_rows - NA_ROWS)
        cs = jnp.clip(qcol - NA_COLS // 2, 0, GRID_W - NA_COLS)
        valid = (kr >= rs) & (kr < rs + NA_ROWS) & (kcol >= cs) & (kcol < cs + NA_COLS)
    else:
        tq = ATT_BLK * ql + iq
        ts = ATT_BLK * (ql - 1) + ik
        valid = (jnp.abs(tq - ts) <= WINDOW) & (ts >= 0) & (ts < n_lat)
    valid = valid & (qb > 0)
    scale = HEAD_DIM ** -0.5
    outs = []
    for e, bias in enumerate((bias0, bias1)):
        in_head = (lane < 64) if e == 0 else (lane >= 64)
        qe = jnp.where(in_head, q, 0.0)
        s_lat = mm_nt(qe, k3) * scale
        if bias is not None:
            s_lat = s_lat + bias
        s_lat = jnp.where(valid, s_lat, NEG_INF)
        s_ctx = mm_nt(qe, kc) * scale
        mx = jnp.maximum(jnp.max(s_lat, axis=1, keepdims=True), jnp.max(s_ctx, axis=1, keepdims=True))
        if sink is not None:
            srow = lax.broadcasted_iota(jnp.int32, sink.shape, 0)
            sv = jnp.sum(jnp.where(srow == e, sink, 0.0), keepdims=True) * (1.0 / 128.0)
            mx = jnp.maximum(mx, sv)
        mx = lax.stop_gradient(mx)
        e_lat = jnp.exp(s_lat - mx)
        e_ctx = jnp.exp(s_ctx - mx)
        den = jnp.sum(e_lat, axis=1, keepdims=True) + jnp.sum(e_ctx, axis=1, keepdims=True)
        if sink is not None:
            den = den + jnp.exp(sv - mx)
        inv = 1.0 / den
        outs.append(mm(e_lat * inv, v3) + mm(e_ctx * inv, vc))
    return jnp.where(lane < 64, outs[0], outs[1])


def _attn_specs(n_ctx):
    def kwin(s):
        return pl.BlockSpec((ATT_BLK, 128), lambda hp, qb: (jnp.maximum(qb - 1, 0) + s, hp))

    q = pl.BlockSpec((ATT_BLK, 128), lambda hp, qb: (qb, hp))
    ctx = pl.BlockSpec((n_ctx, 128), lambda hp, qb: (0, hp))
    bias = pl.BlockSpec((None, 2, ATT_BLK, 3 * ATT_BLK), lambda hp, qb: (hp, 0, 0, 0))
    sink = pl.BlockSpec((None, 8, 128), lambda hp, qb: (hp, 0, 0))
    return q, [kwin(0), kwin(1), kwin(2)], ctx, bias, sink


def _attn_fwd(q, kpad, vpad, kc, vc, bias, sink, *, mode, n_ctx):
    t = q.shape[0]
    n_lat = t - n_ctx
    qs, kws, ctx, bias_s, sink_s = _attn_specs(n_ctx)
    has_bias, has_sink = bias is not None, sink is not None

    def body(*refs):
        q_ref, k_refs, v_refs, kc_ref, vc_ref = refs[0], refs[1:4], refs[4:7], refs[7], refs[8]
        rest = list(refs[9:])
        b_ref = rest.pop(0) if has_bias else None
        s_ref = rest.pop(0) if has_sink else None
        o_ref = rest[0]
        k3 = jnp.concatenate([r[...] for r in k_refs], axis=0)
        v3 = jnp.concatenate([r[...] for r in v_refs], axis=0)
        o_ref[...] = _attn_block(
            q_ref[...], k3, v3, kc_ref[...], vc_ref[...],
            b_ref[0] if has_bias else None, b_ref[1] if has_bias else None,
            s_ref[...] if has_sink else None, pl.program_id(1), mode=mode, n_lat=n_lat)

    in_specs = [qs] + kws + kws + [ctx, ctx] + ([bias_s] if has_bias else []) + ([sink_s] if has_sink else [])
    args = [q, kpad, kpad, kpad, vpad, vpad, vpad, kc, vc] + ([bias] if has_bias else []) + ([sink] if has_sink else [])
    return pl.pallas_call(
        body, name=mode + "_attn_fwd", grid=(4, t // ATT_BLK),
        in_specs=in_specs, out_specs=qs, out_shape=jax.ShapeDtypeStruct((t, 512), F32),
        compiler_params=_cparams(("parallel", "parallel")),
    )(*args)


def _attn_bwd(q, kpad, vpad, kc, vc, bias, sink, do, *, mode, n_ctx):
    t = q.shape[0]
    n_lat = t - n_ctx
    nqb = t // ATT_BLK
    qs, kws, ctx, bias_s, sink_s = _attn_specs(n_ctx)
    has_bias, has_sink = bias is not None, sink is not None
    n_in = 10 + has_bias + has_sink

    def body(*refs):
        q_ref, k_refs, v_refs, kc_ref, vc_ref = refs[0], refs[1:4], refs[4:7], refs[7], refs[8]
        rest = list(refs[9:n_in])
        b_ref = rest.pop(0) if has_bias else None
        s_ref = rest.pop(0) if has_sink else None
        do_ref = rest[0]
        outs = list(refs[n_in:])
        dq_ref, dkp_ref, dvp_ref, dkc_ref, dvc_ref = outs[:5]
        outs = outs[5:]
        db_ref = outs.pop(0) if has_bias else None
        ds_ref = outs.pop(0) if has_sink else None
        qb = pl.program_id(1)
        up = lambda r: r[...].astype(F32)
        k3 = jnp.concatenate([up(r) for r in k_refs], axis=0)
        v3 = jnp.concatenate([up(r) for r in v_refs], axis=0)
        prim = [up(q_ref), k3, v3, up(kc_ref), up(vc_ref)]
        if has_bias:
            prim += [b_ref[0], b_ref[1]]
        if has_sink:
            prim += [s_ref[...]]

        def fn(*a):
            a = list(a)
            qv, k3v, v3v, kcv, vcv = a[:5]
            a = a[5:]
            b0 = a.pop(0) if has_bias else None
            b1 = a.pop(0) if has_bias else None
            sk = a.pop(0) if has_sink else None
            return _attn_block(qv, k3v, v3v, kcv, vcv, b0, b1, sk, qb, mode=mode, n_lat=n_lat)

        _, vjp = jax.vjp(fn, *prim)
        grads = list(vjp(do_ref[...]))
        dq_ref[...] = grads[0]
        for s in range(3):
            dkp_ref[s] = grads[1][ATT_BLK * s:ATT_BLK * (s + 1), :]
            dvp_ref[s] = grads[2][ATT_BLK * s:ATT_BLK * (s + 1), :]

        @pl.when(qb == 0)
        def _():
            dkc_ref[...] = jnp.zeros_like(dkc_ref)
            dvc_ref[...] = jnp.zeros_like(dvc_ref)
            if has_bias:
                db_ref[...] = jnp.zeros_like(db_ref)
            if has_sink:
                ds_ref[...] = jnp.zeros_like(ds_ref)

        dkc_ref[...] += grads[3]
        dvc_ref[...] += grads[4]
        rest_g = grads[5:]
        if has_bias:
            db_ref[0] += rest_g.pop(0)
            db_ref[1] += rest_g.pop(0)
        if has_sink:
            ds_ref[...] += rest_g.pop(0)

    part = pl.BlockSpec((None, None, 3, ATT_BLK, 128), lambda hp, qb: (hp, qb, 0, 0, 0))
    in_specs = [qs] + kws + kws + [ctx, ctx] + ([bias_s] if has_bias else []) + ([sink_s] if has_sink else []) + [qs]
    args = ([q, kpad, kpad, kpad, vpad, vpad, vpad, kc, vc] + ([bias] if has_bias else [])
            + ([sink] if has_sink else []) + [do])
    out_specs = [qs, part, part, ctx, ctx] + ([bias_s] if has_bias else []) + ([sink_s] if has_sink else [])
    out_shape = [jax.ShapeDtypeStruct((t, 512), F32),
                 jax.ShapeDtypeStruct((4, nqb, 3, ATT_BLK, 128), F32),
                 jax.ShapeDtypeStruct((4, nqb, 3, ATT_BLK, 128), F32),
                 jax.ShapeDtypeStruct((n_ctx, 512), F32), jax.ShapeDtypeStruct((n_ctx, 512), F32)]
    if has_bias:
        out_shape.append(jax.ShapeDtypeStruct((4, 2, ATT_BLK, 3 * ATT_BLK), F32))
    if has_sink:
        out_shape.append(jax.ShapeDtypeStruct((4, 8, 128), F32))
    res = list(pl.pallas_call(
        body, name=mode + "_attn_bwd", grid=(4, nqb),
        in_specs=in_specs, out_specs=out_specs, out_shape=tuple(out_shape),
        compiler_params=_cparams(("parallel", "arbitrary")),
    )(*args))
    dq, dkp, dvp, dkc, dvc = res[:5]
    res = res[5:]
    dbias = res.pop(0) if has_bias else None
    dsink = res.pop(0) if has_sink else None
    return dq, dkp, dvp, dkc, dvc, dbias, dsink


def _window_fold(part, dctx, n_ctx):
    nqb = part.shape[1]
    nkb = nqb - 1

    def body(p0_ref, p1_ref, p2_ref, c_ref, o_ref):
        kb = pl.program_id(1) - 1

        @pl.when(kb < 0)
        def _():
            o_ref[...] = c_ref[...]

        @pl.when(kb >= 0)
        def _():
            acc = p1_ref[...]
            acc = acc + jnp.where(kb + 2 <= nkb, p0_ref[...], 0.0)
            acc = acc + jnp.where(kb >= 1, p2_ref[...], 0.0)
            o_ref[...] = acc

    def pspec(s, dq):
        return pl.BlockSpec((None, None, None, ATT_BLK, 128),
                            lambda hp, b: (hp, jnp.clip(b + dq, 1, nqb - 1), s, 0, 0))

    return pl.pallas_call(
        body, name="window_fold", grid=(4, nqb),
        in_specs=[pspec(0, 1), pspec(1, 0), pspec(2, -1), pl.BlockSpec((n_ctx, 128), lambda hp, b: (0, hp))],
        out_specs=pl.BlockSpec((ATT_BLK, 128), lambda hp, b: (b, hp)),
        out_shape=jax.ShapeDtypeStruct((nqb * ATT_BLK, 512), F32),
        compiler_params=_cparams(("parallel", "parallel")),
    )(part, part, part, dctx)


def _loss_head(xt, target, n_ctx):
    t = xt.shape[0]
    nct = n_ctx // TOK

    def body(x_ref, t_ref, l_ref, d_ref):
        i = pl.program_id(0)

        @pl.when(i == 0)
        def _():
            l_ref[...] = jnp.zeros_like(l_ref)

        @pl.when(i < nct)
        def _():
            d_ref[...] = jnp.zeros_like(d_ref)

        @pl.when(i >= nct)
        def _():
            err = x_ref[...] - t_ref[...]
            d_ref[...] = err * (1.0 / D_MODEL)
            l_ref[...] += jnp.sum(err * err, keepdims=True) * (0.5 / D_MODEL)

    return pl.pallas_call(
        body, name="loss_head", grid=(t // TOK,),
        in_specs=[_row_spec(D_MODEL), pl.BlockSpec((TOK, D_MODEL), lambda i: (jnp.maximum(i - nct, 0), 0))],
        out_specs=[_const_spec((8, 128)), _row_spec(D_MODEL)],
        out_shape=(jax.ShapeDtypeStruct((8, 128), F32), jax.ShapeDtypeStruct((t, D_MODEL), F32)),
        compiler_params=_cparams(("arbitrary",)),
    )(xt, target)


PACK_W = 1024
PACK_TILE = 256


def _sum_chips(recv):
    r = recv.shape[1]

    def body(r_ref, o_ref):
        o_ref[...] = ((r_ref[0] + r_ref[1]) + r_ref[2]) + r_ref[3]

    return pl.pallas_call(
        body, name="sum_chips", grid=(r // PACK_TILE,),
        in_specs=[pl.BlockSpec((4, PACK_TILE, PACK_W), lambda i: (0, i, 0))],
        out_specs=pl.BlockSpec((PACK_TILE, PACK_W), lambda i: (i, 0)),
        out_shape=jax.ShapeDtypeStruct((r, PACK_W), F32),
        compiler_params=_cparams(("parallel",)),
    )(recv)


def _adamw(p_a, p_b, w, m, v):
    r = w.shape[0]
    c1 = 1.0 / (1.0 - ADAM_B1 ** ADAM_STEP)
    c2 = 1.0 / (1.0 - ADAM_B2 ** ADAM_STEP)

    def body(a_ref, b_ref, w_ref, m_ref, v_ref, g_ref, d_ref, nm_ref, nv_ref):
        g = a_ref[...] + b_ref[...]
        nm = ADAM_B1 * m_ref[...] + (1.0 - ADAM_B1) * g
        nv = ADAM_B2 * v_ref[...] + (1.0 - ADAM_B2) * (g * g)
        g_ref[...] = g
        nm_ref[...] = nm
        nv_ref[...] = nv
        d_ref[...] = -ADAM_LR * ((nm * c1) / (jnp.sqrt(nv * c2) + ADAM_EPS) + ADAM_WD * w_ref[...])

    spec = pl.BlockSpec((PACK_TILE, PACK_W), lambda i: (i, 0))
    return pl.pallas_call(
        body, name="adamw", grid=(r // PACK_TILE,),
        in_specs=[spec] * 5, out_specs=[spec] * 4,
        out_shape=tuple(jax.ShapeDtypeStruct((r, PACK_W), F32) for _ in range(4)),
        compiler_params=_cparams(("parallel",)),
    )(p_a, p_b, w, m, v)


MESH = pl.DeviceIdType.MESH
ANY_SPEC = pl.BlockSpec(memory_space=pl.ANY)


def _chip_exchange(src, *, same_to_all, name):
    shape = (4,) + tuple(src.shape[-2:])

    def body(src_ref, out_ref, send_sems, recv_sems, local_sem):
        x, y, c = lax.axis_index("x"), lax.axis_index("y"), lax.axis_index("c")
        me = 2 * x + y
        peers = [(x, 1 - y), (1 - x, y), (1 - x, 1 - y)]

        def slab(chip):
            return src_ref if same_to_all else src_ref.at[chip]

        def copy(k, px, py, src_slab, slot):
            return pltpu.make_async_remote_copy(
                src_ref=src_slab, dst_ref=out_ref.at[slot], send_sem=send_sems.at[k], recv_sem=recv_sems.at[k],
                device_id=(px, py, c), device_id_type=MESH)

        mine = pltpu.make_async_copy(slab(me), out_ref.at[me], local_sem)
        mine.start()
        sends = [copy(k, px, py, slab(2 * px + py), me) for k, (px, py) in enumerate(peers)]
        for cp in sends:
            cp.start()
        for k, (px, py) in enumerate(peers):
            copy(k, px, py, slab(me), 2 * px + py).wait_recv()
        for cp in sends:
            cp.wait_send()
        mine.wait()

    return pl.pallas_call(
        body, name=name, in_specs=[ANY_SPEC], out_specs=ANY_SPEC,
        out_shape=jax.ShapeDtypeStruct(shape, src.dtype),
        scratch_shapes=[pltpu.SemaphoreType.DMA((3,)), pltpu.SemaphoreType.DMA((3,)), pltpu.SemaphoreType.DMA],
        compiler_params=pltpu.CompilerParams(has_side_effects=True),
    )(src)


def _core_swap(src):
    def body(src_ref, out_ref, send_sem, recv_sem):
        x, y, c = lax.axis_index("x"), lax.axis_index("y"), lax.axis_index("c")
        cp = pltpu.make_async_remote_copy(src_ref=src_ref, dst_ref=out_ref, send_sem=send_sem, recv_sem=recv_sem,
                                          device_id=(x, y, 1 - c), device_id_type=MESH)
        cp.start()
        cp.wait()

    return pl.pallas_call(
        body, name="core_swap", in_specs=[ANY_SPEC], out_specs=ANY_SPEC,
        out_shape=jax.ShapeDtypeStruct(src.shape, src.dtype),
        scratch_shapes=[pltpu.SemaphoreType.DMA, pltpu.SemaphoreType.DMA],
    )(src)


def _s5_tables(a_re, a_im, log_dt, b_re, b_im, c_re, c_im):
    ln = S5_CHUNK
    hi = lax.Precision.HIGHEST
    dt = jnp.exp(log_dt)[..., None]
    mag = jnp.exp(dt * a_re)
    abr = mag * jnp.cos(dt * a_im)
    abi = mag * jnp.sin(dt * a_im)
    den = a_re * a_re + a_im * a_im
    fr = ((abr - 1.0) * a_re + abi * a_im) / den
    fi = (abi * a_re - (abr - 1.0) * a_im) / den
    bbr = fr[..., None] * b_re - fi[..., None] * b_im
    bbi = fr[..., None] * b_im + fi[..., None] * b_re
    n = jnp.arange(ln + 1, dtype=F32)[:, None, None, None]
    pm = jnp.exp(n * dt * a_re)
    er = pm * jnp.cos(n * dt * a_im)
    ei = pm * jnp.sin(n * dt * a_im)
    e3 = lambda e, b, c: jnp.einsum("tdgp,dgpa,dgbp->dgabt", e, b, c, precision=hi)
    gt = e3(er[:ln], bbr, c_re) - e3(er[:ln], bbi, c_im) - e3(ei[:ln], bbr, c_im) - e3(ei[:ln], bbi, c_re)
    by_dir = lambda fwd, bwd: jnp.stack([fwd[:, 0], bwd[:, 1]], axis=1)
    erj, eij = by_dir(er[:ln][::-1], er[:ln]), by_dir(ei[:ln][::-1], ei[:ln])
    e2 = lambda e, b: jnp.einsum("jdgp,dgpa->dgajp", e, b, precision=hi)
    w = jnp.concatenate([e2(erj, bbr) - e2(eij, bbi), e2(erj, bbi) + e2(eij, bbr)], axis=-1)
    er1, ei1 = by_dir(er[1:], er[1:][::-1]), by_dir(ei[1:], ei[1:][::-1])
    ev = lambda c, e: jnp.einsum("dgbp,idgp->dgpbi", c, e, precision=hi)
    v = jnp.concatenate([ev(c_re, er1) - ev(c_im, ei1), -(ev(c_re, ei1) + ev(c_im, er1))], axis=2)
    a1 = jnp.concatenate([er[ln], er[ln]], axis=-1)
    a2 = jnp.concatenate([-ei[ln], ei[ln]], axis=-1)
    return (gt.transpose(1, 2, 3, 0, 4).reshape(S5_GROUPS, 256, 2 * ln),
            w.transpose(1, 2, 3, 0, 4).reshape(S5_GROUPS, S5_CH * ln, 256),
            v.transpose(1, 0, 2, 3, 4).reshape(S5_GROUPS, 256, S5_CH * ln),
            a1.reshape(64, 128), a2.reshape(64, 128))


def _lag_onehot():
    ln = S5_CHUNK
    j, i = np.meshgrid(np.arange(ln), np.arange(ln), indexing="ij")
    lag = np.arange(ln)[:, None, None]
    z = np.concatenate([lag == (i - j)[None], lag == (j - i)[None]], axis=0).astype(np.float32)
    return jnp.broadcast_to(jnp.asarray(z.reshape(2 * ln, ln * ln), BF16), (S5_GROUPS, 2 * ln, ln * ln))


def _toeplitz(gt):
    ln = S5_CHUNK
    flat = _matmul(gt, _lag_onehot(), out_dtype=BF16, name="s5_toeplitz")
    return (flat.reshape(S5_GROUPS, S5_CH, S5_CH, ln, ln).transpose(0, 1, 3, 2, 4)
            .reshape(S5_GROUPS, S5_CH * ln, S5_CH * ln))


def _toeplitz_fold(dk):
    ln = S5_CHUNK
    flat = dk.reshape(S5_GROUPS, S5_CH, ln, S5_CH, ln).transpose(0, 1, 3, 2, 4).reshape(S5_GROUPS, 256, ln * ln)
    return _matmul(flat, _lag_onehot(), trans_b=True, name="s5_toeplitz_fold")


def _chunk_rows(t):
    k = t // S5_CHUNK
    return k, -(-k // 128) * 128


def _to_chunks(u):
    k, kp = _chunk_rows(u.shape[0])
    v = u.reshape(k, S5_CHUNK, S5_GROUPS, S5_CH).transpose(2, 0, 3, 1).reshape(S5_GROUPS, k, S5_CH * S5_CHUNK)
    return jnp.pad(v, ((0, 0), (0, kp - k), (0, 0)))


def _from_chunks(y, t):
    k, _ = _chunk_rows(t)
    return y[:, :k].reshape(S5_GROUPS, k, S5_CH, S5_CHUNK).transpose(1, 3, 0, 2).reshape(t, MIX_W)


def _states_to_rows(s):
    kp = s.shape[1]
    return s.reshape(S5_GROUPS, kp, 2, 128).transpose(1, 2, 0, 3).reshape(kp, 64, 128)


def _rows_to_states(h):
    kp = h.shape[0]
    return h.reshape(kp, 2, S5_GROUPS, 128).transpose(2, 0, 1, 3).reshape(S5_GROUPS, kp, 256)


def _na_bias(rel_bias):
    a, m = np.meshgrid(np.arange(4), np.arange(12), indexing="ij")
    di = np.clip(m - a + 3, 0, 2 * NA_ROWS - 2).reshape(-1)
    qc, kc = np.meshgrid(np.arange(GRID_W), np.arange(GRID_W), indexing="ij")
    dj = np.clip(kc - qc + NA_COLS - 1, 0, 2 * NA_COLS - 2).reshape(-1)
    oh_i = jnp.asarray(di[:, None] == np.arange(2 * NA_ROWS - 1)[None, :], F32)
    oh_j = jnp.asarray(dj[:, None] == np.arange(2 * NA_COLS - 1)[None, :], F32)
    hi = lax.Precision.HIGHEST
    cols = jnp.einsum("hij,cj->hic", rel_bias, oh_j, precision=hi)
    full = jnp.einsum("ri,hic->hrc", oh_i, cols, precision=hi)
    full = full.reshape(N_HEADS, 4, 12, GRID_W, GRID_W).transpose(0, 1, 3, 2, 4)
    return full.reshape(4, 2, ATT_BLK, 3 * ATT_BLK)


def _rope_tables(n_ctx, n_lat):
    tok = jnp.arange(n_lat, dtype=jnp.int32)
    row = (tok // GRID_W).astype(F32)
    col = (tok % GRID_W).astype(F32)
    inv = ROPE_BASE ** (-jnp.arange(ROPE_PAIRS, dtype=F32) / ROPE_PAIRS)
    ang = jnp.concatenate([row[:, None] * inv, col[:, None] * inv], axis=-1)
    cos, sin = jnp.cos(ang), jnp.sin(ang)
    cos = jnp.tile(jnp.concatenate([cos, cos], axis=-1), (1, 2))
    sin = jnp.tile(jnp.concatenate([-sin, sin], axis=-1), (1, 2))
    return (jnp.concatenate([jnp.ones((n_ctx, 128), F32), cos], axis=0),
            jnp.concatenate([jnp.zeros((n_ctx, 128), F32), sin], axis=0))


def _pad_blocks(a, n_ctx):
    return jnp.pad(a[n_ctx:], ((ATT_BLK, ATT_BLK), (0, 0)))


def _layer_fwd(xt, cc, w, rope, n_ctx):
    t = xt.shape[0]
    sv = {}
    mod = _adaln_fwd(cc, w["w_ada"], w["b_ada"].reshape(1, -1))
    mod4 = mod[:2].reshape(2, 3, 1, D_MODEL)
    h = _modnorm_fwd(xt, w["norm_g"].reshape(1, -1), mod4, n_ctx)
    z = _matmul(h, w["w_in"], name="proj_fwd")

    s5_args = (w["s5_a_re"], w["s5_a_im"], w["s5_log_dt"], w["s5_b_re"], w["s5_b_im"], w["s5_c_re"], w["s5_c_im"])
    (gt, tw, tv, a1, a2), tab_vjp = jax.vjp(_s5_tables, *s5_args)
    ktoe = _toeplitz(gt)
    tw, tv = tw.astype(BF16), tv.astype(BF16)
    uc = _to_chunks(z[:, :MIX_W].astype(BF16))
    st = _matmul(uc, tw, name="s5_chunk_state")
    hprev = _s5_scan_fwd(_states_to_rows(st), a1, a2, t // S5_CHUNK, n_ctx // S5_CHUNK)
    uh = jnp.concatenate([uc, _rows_to_states(hprev).astype(BF16)], axis=2)
    ysum = _from_chunks(_matmul(uh, jnp.concatenate([ktoe, tv], axis=1), name="s5_chunk_out"), t)
    s5_d = w["s5_d"].reshape(1, MIX_W)
    y_s5 = _s5post_fwd(ysum, z, s5_d, w["s5_w_glu"])

    conv_w = jnp.pad(w["conv_w"], ((0, 5), (0, 0)))
    y_conv = _conv_fwd(z, conv_w, w["conv_b"].reshape(1, -1), n_ctx)

    gains = (jnp.tile(w["na_q_g"], 8)[None], jnp.tile(w["na_k_g"], 8)[None],
             jnp.tile(w["gqa_q_g"], 8)[None], jnp.tile(w["gqa_k_g"], 2)[None])
    q_na, k_na, v_na, q_g, k_g, v_g = _prep_fwd(z, gains, rope)
    bias, bias_vjp = jax.vjp(_na_bias, w["na_rel_bias"])
    sink = jnp.zeros((4, 8, 128), F32).at[:, :2, :].set(
        jnp.broadcast_to(w["gqa_sink"].reshape(4, 2, 1), (4, 2, 128)))
    na_in = (q_na, _pad_blocks(k_na, n_ctx), _pad_blocks(v_na, n_ctx), k_na[:n_ctx], v_na[:n_ctx], bias, None)
    gqa_in = (q_g, _pad_blocks(k_g, n_ctx), _pad_blocks(v_g, n_ctx), k_g[:n_ctx], v_g[:n_ctx], None, sink)
    y_na = _attn_fwd(*na_in, mode="na", n_ctx=n_ctx)
    y_gqa = _attn_fwd(*gqa_in, mode="gqa", n_ctx=n_ctx)
    ys = (y_s5, y_conv, y_na, y_gqa)
    xt_new = _merge_fwd(xt, ys, z, mod4, w["w_br"], w["w_out"], n_ctx)
    sv.update(xt=xt, mod4=mod4, h=h, z=z, tab_vjp=tab_vjp, ktoe=ktoe, tw=tw, tv=tv, a1=a1, a2=a2, uc=uc,
              hprev=hprev, uh=uh, ysum=ysum, s5_d=s5_d, conv_w=conv_w, gains=gains, bias_vjp=bias_vjp,
              na_in=na_in, gqa_in=gqa_in, ys=ys)
    return xt_new, sv


def _layer_bwd(dxt_new, sv, cc, w, rope, n_ctx):
    t = dxt_new.shape[0]
    z, mod4 = sv["z"], sv["mod4"]
    dys, dgt, dmg, dgate, dw_br, dw_out = _merge_bwd(dxt_new, sv["ys"], z, mod4, w["w_br"], w["w_out"], n_ctx)

    dpre, du_skip, dd, dw_glu = _s5post_bwd(sv["ysum"], z, sv["s5_d"], w["s5_w_glu"], dys[0])
    dyc = _to_chunks(dpre)
    dhp = _matmul(dyc, sv["tv"], trans_b=True, name="s5_bwd_state")
    ds, da1, da2 = _s5_scan_bwd(_states_to_rows(dhp), sv["hprev"], sv["a1"], sv["a2"],
                                t // S5_CHUNK, n_ctx // S5_CHUNK)
    ds = _rows_to_states(ds).astype(BF16)
    duc = _matmul(jnp.concatenate([dyc, ds], axis=2), jnp.concatenate([sv["ktoe"], sv["tw"]], axis=2),
                  trans_b=True, name="s5_bwd_u")
    dkv = _matmul(sv["uh"].transpose(0, 2, 1), dyc, name="s5_bwd_kv")
    dtw = _matmul(sv["uc"].transpose(0, 2, 1), ds, name="s5_bwd_w")
    dgt_tab = _toeplitz_fold(dkv[:, :S5_CH * S5_CHUNK])
    s5_grads = sv["tab_vjp"]((dgt_tab, dtw, dkv[:, S5_CH * S5_CHUNK:], da1, da2))
    du_scan = _from_chunks(duc, t)

    dzv, dzb, dzc, dconv_w, dconv_b = _conv_bwd(z, sv["conv_w"], w["conv_b"].reshape(1, -1), dys[1], n_ctx)

    dq_na, dkp, dvp, dkc, dvc, dbias, _ = _attn_bwd(*sv["na_in"], dys[2], mode="na", n_ctx=n_ctx)
    dk_na, dv_na = _window_fold(dkp, dkc, n_ctx), _window_fold(dvp, dvc, n_ctx)
    dq_g, dkp, dvp, dkc, dvc, _, dsink = _attn_bwd(*sv["gqa_in"], dys[3], mode="gqa", n_ctx=n_ctx)
    dk_g, dv_g = _window_fold(dkp, dkc, n_ctx), _window_fold(dvp, dvc, n_ctx)
    pb = _prep_bwd(z, sv["gains"], rope, (dq_na, dk_na, dv_na, dq_g, dk_g, dv_g), du_skip, du_scan)
    dz_naq, dz_nak, dz_nav, dz_gq, dz_gk, dz_gv, dz_u, dg_naq, dg_nak, dg_gq, dg_gk = pb

    dz = jnp.concatenate([dz_u, dgt[0], dzv, dzb, dzc, dgt[1], dz_naq, dz_nak, dz_nav, dgt[2], dz_gq, dz_gk, dz_gv,
                          jnp.zeros((t, OFF["gqa_gate"] - OFF["pad"]), BF16), dgt[3], *dmg], axis=1)
    dh = _matmul(dz, w["w_in"], trans_b=True, name="proj_bwd_x")
    dw_in = _matmul(sv["h"].T, dz, name="proj_bwd_w")
    dxt, dnorm_g, dshift, dscale = _modnorm_bwd(sv["xt"], w["norm_g"].reshape(1, -1), mod4, dh, dxt_new, n_ctx)
    dmod = jnp.concatenate([dshift, dscale, dgate], axis=1).reshape(2, 3 * D_MODEL)
    dcc, dw_ada, db_ada = _adaln_bwd(cc, w["w_ada"], jnp.pad(dmod, ((0, 6), (0, 0))))

    (drel,) = sv["bias_vjp"](dbias)
    grads = dict(
        norm_g=dnorm_g[0], w_ada=dw_ada, b_ada=db_ada[0],
        w_in=jnp.concatenate([dw_in[:, :OFF["pad"]], dw_in[:, OFF["gqa_gate"]:]], axis=1),
        s5_a_re=s5_grads[0], s5_a_im=s5_grads[1], s5_log_dt=s5_grads[2], s5_b_re=s5_grads[3], s5_b_im=s5_grads[4],
        s5_c_re=s5_grads[5], s5_c_im=s5_grads[6], s5_d=dd.reshape(S5_GROUPS, S5_CH), s5_w_glu=dw_glu,
        conv_w=dconv_w[:3], conv_b=dconv_b[0],
        na_q_g=dg_naq.reshape(8, HEAD_DIM).sum(0), na_k_g=dg_nak.reshape(8, HEAD_DIM).sum(0), na_rel_bias=drel,
        gqa_q_g=dg_gq.reshape(8, HEAD_DIM).sum(0), gqa_k_g=dg_gk.reshape(2, HEAD_DIM).sum(0),
        gqa_sink=dsink[:, :2, :].sum(-1).reshape(8), w_br=dw_br, w_out=dw_out)
    return dxt, dcc, grads


SHARDED = ("w_ada", "w_in", "s5_w_glu", "conv_w", "w_br", "w_out")
SHARD_AXIS = dict(w_ada=1, w_in=1, s5_w_glu=0, conv_w=1, w_br=2, w_out=0)
REPLICATED = ("norm_g", "b_ada", "s5_a_re", "s5_a_im", "s5_log_dt", "s5_b_re", "s5_b_im", "s5_c_re", "s5_c_im",
              "s5_d", "conv_b", "na_q_g", "na_k_g", "na_rel_bias", "gqa_q_g", "gqa_k_g", "gqa_sink")
WEIGHTS = ("c_ctx", "norm_g", "w_ada", "b_ada", "w_in", "s5_a_re", "s5_a_im", "s5_log_dt", "s5_b_re", "s5_b_im",
           "s5_c_re", "s5_c_im", "s5_d", "s5_w_glu", "conv_w", "conv_b", "na_q_g", "na_k_g", "na_rel_bias",
           "gqa_q_g", "gqa_k_g", "gqa_sink", "w_br", "w_out")
PACK_ORDER = SHARDED + REPLICATED + ("c_ctx",)


def _pack(pieces, row_multiple, dtype):
    flat = jnp.concatenate([p.reshape(-1).astype(dtype) for p in pieces])
    rows = -(-flat.shape[0] // PACK_W)
    rows = -(-rows // row_multiple) * row_multiple
    return jnp.pad(flat, (0, rows * PACK_W - flat.shape[0])).reshape(rows, PACK_W)


def _unpack(buf, shapes):
    flat = buf.reshape(-1)
    out, pos = [], 0
    for shp in shapes:
        size = int(np.prod(shp))
        out.append(flat[pos:pos + size].reshape(shp))
        pos += size
    return out


def _shard_of(full, name, s):
    ax = SHARD_AXIS[name]
    width = full.shape[ax] // 4
    return lax.slice_in_dim(full, s * width, (s + 1) * width, axis=ax)


def _local_step(x, ctx, target, c_vec, c_ctx, layers):
    depth = len(layers)
    n_ctx, n_lat = ctx.shape[0], x.shape[0]
    cc = jnp.zeros((8, D_MODEL), F32).at[0].set(c_ctx).at[1].set(c_vec)
    rope = _rope_tables(n_ctx, n_lat)
    xt = jnp.concatenate([ctx, x], axis=0)
    saved = []
    for l in range(depth):
        xt, sv = _layer_fwd(xt, cc, layers[l], rope, n_ctx)
        saved.append(sv)
    loss_tile, dxt = _loss_head(xt, target, n_ctx)
    grads = [None] * depth
    dc_ctx = jnp.zeros((D_MODEL,), F32)
    for l in reversed(range(depth)):
        dxt, dcc, grads[l] = _layer_bwd(dxt, saved[l], cc, layers[l], rope, n_ctx)
        dc_ctx = dc_ctx + dcc[0]
    return loss_tile[0, 0], dxt[n_ctx:][None], dc_ctx, grads


def kernel(x, c, ctx, c_ctx, norm_g, w_ada, b_ada, w_in, s5_a_re, s5_a_im, s5_log_dt, s5_b_re, s5_b_im,
           s5_c_re, s5_c_im, s5_d, s5_w_glu, conv_w, conv_b, na_q_g, na_k_g, na_rel_bias, gqa_q_g,
           gqa_k_g, gqa_sink, w_br, w_out, loss_target, m_c_ctx, m_norm_g, m_w_ada, m_b_ada, m_w_in,
           m_s5_a_re, m_s5_a_im, m_s5_log_dt, m_s5_b_re, m_s5_b_im, m_s5_c_re, m_s5_c_im, m_s5_d,
           m_s5_w_glu, m_conv_w, m_conv_b, m_na_q_g, m_na_k_g, m_na_rel_bias, m_gqa_q_g, m_gqa_k_g,
           m_gqa_sink, m_w_br, m_w_out, v_c_ctx, v_norm_g, v_w_ada, v_b_ada, v_w_in, v_s5_a_re,
           v_s5_a_im, v_s5_log_dt, v_s5_b_re, v_s5_b_im, v_s5_c_re, v_s5_c_im, v_s5_d, v_s5_w_glu,
           v_conv_w, v_conv_b, v_na_q_g, v_na_k_g, v_na_rel_bias, v_gqa_q_g, v_gqa_k_g, v_gqa_sink,
           v_w_br, v_w_out):
    a = dict(locals())
    depth = a["norm_g"].shape[0]
    x, ctx, target = a["x"][0], a["ctx"][0], a["loss_target"][0]
    n_ctx, n_lat = ctx.shape[0], x.shape[0]
    assert n_ctx % ATT_BLK == 0 and n_lat % (4 * GRID_W) == 0 and n_lat // GRID_W >= NA_ROWS

    gather_names = ("w_ada", "w_in", "s5_w_glu", "w_br", "w_out")
    layers = []
    for l in range(depth):
        pieces = [a[n][l].astype(BF16) for n in gather_names]
        pieces.append(lax.bitcast_convert_type(a["conv_w"][l], BF16))
        got = _chip_exchange(_pack(pieces, 16, BF16), same_to_all=True, name="gather_weights")
        shapes = [a[n][l].shape for n in gather_names] + [a["conv_w"][l].shape + (2,)]
        per_chip = [_unpack(got[s], shapes) for s in range(4)]
        w = {n: a[n][l] for n in REPLICATED}
        for j, n in enumerate(gather_names):
            w[n] = jnp.concatenate([per_chip[s][j] for s in range(4)], axis=SHARD_AXIS[n])
        w["conv_w"] = jnp.concatenate(
            [lax.bitcast_convert_type(per_chip[s][5], F32) for s in range(4)], axis=SHARD_AXIS["conv_w"])
        w["w_in"] = jnp.concatenate(
            [w["w_in"][:, :OFF["pad"]], jnp.zeros((D_MODEL, OFF["gqa_gate"] - OFF["pad"]), BF16),
             w["w_in"][:, OFF["pad"]:]], axis=1)
        layers.append(w)

    loss_local, grad_x, dc_ctx, grads = _local_step(x, ctx, target, a["c"][0], a["c_ctx"], layers)
    loss = lax.psum(loss_local, ("x", "y", "c"))

    out = {k: {n: [] for n in WEIGHTS} for k in ("grad", "delta", "new_m", "new_v")}
    for l in range(depth):
        def slab(s):
            g = grads[l]
            pieces = [_shard_of(g[n], n, s) for n in SHARDED] + [g[n] for n in REPLICATED]
            pieces.append(dc_ctx if l == 0 else jnp.zeros_like(dc_ctx))
            return _pack(pieces, PACK_TILE, F32)

        send = jnp.stack([slab(s) for s in range(4)])
        recv = _chip_exchange(send, same_to_all=False, name="scatter_grads")
        part = _sum_chips(recv)
        other = _core_swap(part)

        def own(prefix):
            pieces = [a[prefix + n][l] for n in SHARDED + REPLICATED]
            pieces.append(a[prefix + "c_ctx"] if l == 0 else jnp.zeros_like(a["c_ctx"]))
            return _pack(pieces, PACK_TILE, F32)

        res = _adamw(part, other, own(""), own("m_"), own("v_"))
        shapes = [a[n][l].shape for n in SHARDED + REPLICATED] + [a["c_ctx"].shape]
        for key, buf in zip(("grad", "delta", "new_m", "new_v"), res):
            for n, val in zip(PACK_ORDER, _unpack(buf, shapes)):
                if n != "c_ctx" or l == 0:
                    out[key][n].append(val)
    results = [loss, grad_x]
    for key in ("grad", "delta", "new_m", "new_v"):
        for n in WEIGHTS:
            results.append(out[key][n][0] if n == "c_ctx" else jnp.stack(out[key][n]))
    return tuple(results)
```

```python
import functools

import numpy as np
import jax
import jax.numpy as jnp
from jax import lax
from jax.experimental import pallas as pl
from jax.experimental.pallas import tpu as pltpu

F32 = jnp.float32
BF16 = jnp.bfloat16

D_MODEL = 1024
MIX_W = 512
GRID_W = 64
HEAD_DIM = 64
N_HEADS = 8
S5_GROUPS = 32
S5_CH = 16
S5_CHUNK = 32
NA_ROWS = 8
NA_COLS = 16
WINDOW = 128
ROPE_BASE = 10000.0
ROPE_PAIRS = 16
EPS = 1e-6
NEG_INF = -1e30
ATT_BLK = 256
TOK = 256
VMEM_LIMIT = 56 * 1024 * 1024

ADAM_LR, ADAM_B1, ADAM_B2, ADAM_EPS, ADAM_WD, ADAM_STEP = 0.001, 0.9, 0.999, 1e-8, 0.01, 10

OFF = dict(s5_u=0, s5_gate=512, conv_v=1024, conv_b=1536, conv_c=2048, conv_gate=2560,
           na_q=3072, na_k=3584, na_v=4096, na_gate=4608, gqa_q=5120, gqa_k=5632, gqa_v=5760,
           pad=5888, gqa_gate=6144, merge_s5=6656, merge_conv=7680, merge_na=8704, merge_gqa=9728)
N_Z = 10752
GATE_OFFS = (OFF["s5_gate"], OFF["conv_gate"], OFF["na_gate"], OFF["gqa_gate"])
MERGE_OFFS = (OFF["merge_s5"], OFF["merge_conv"], OFF["merge_na"], OFF["merge_gqa"])


def _cparams(sem):
    return pltpu.CompilerParams(dimension_semantics=sem, vmem_limit_bytes=VMEM_LIMIT)


def _dot(a, b, ca, cb):
    return lax.dot_general(a.astype(BF16), b.astype(BF16), (((ca,), (cb,)), ((), ())),
                           preferred_element_type=F32)


def _dot_tn(a, b):
    return _dot(a.astype(F32).T, b, 1, 0)


@jax.custom_vjp
def mm(a, b):
    return _dot(a, b, 1, 0)


@jax.custom_vjp
def mm_nt(a, b):
    return _dot(a, b, 1, 1)


@jax.custom_vjp
def mm_tn(a, b):
    return _dot_tn(a, b)


mm.defvjp(lambda a, b: (mm(a, b), (a, b)), lambda r, g: (mm_nt(g, r[1]), mm_tn(r[0], g)))
mm_nt.defvjp(lambda a, b: (mm_nt(a, b), (a, b)), lambda r, g: (mm(g, r[1]), mm_tn(g, r[0])))
mm_tn.defvjp(lambda a, b: (mm_tn(a, b), (a, b)), lambda r, g: (mm_nt(r[1], g), mm(r[0], g)))


@functools.partial(jax.custom_vjp, nondiff_argnums=(1,))
def lane_roll(x, shift):
    return pltpu.roll(x, shift, 1)


lane_roll.defvjp(lambda x, shift: (lane_roll(x, shift), None),
                 lambda shift, _, g: (lane_roll(g, (g.shape[1] - shift) % g.shape[1]),))


def _silu(x):
    return x * jax.nn.sigmoid(x)


def _dsilu(x):
    s = jax.nn.sigmoid(x)
    return s * (1.0 + x * (1.0 - s))


def _pick(n, prefs):
    for p in prefs:
        if n % p == 0:
            return p
    return n


def _matmul(a, b, *, trans_b=False, out_dtype=F32, tm=None, tn=None, tk=None, name):
    squeeze = a.ndim == 2
    if squeeze:
        a, b = a[None], b[None]
    nb, m, k = a.shape
    n = b.shape[1] if trans_b else b.shape[2]
    tm = tm or _pick(m, (1280, 1024, 640, 512, 256, 128))
    tn = tn or _pick(n, (1536, 1024, 512, 256, 128))
    tk = tk or _pick(k, (1536, 1280, 1024, 768, 640, 512, 256, 128))
    nk = k // tk

    def body(a_ref, b_ref, o_ref, *scr):
        part = _dot(a_ref[...], b_ref[...], 1, 1 if trans_b else 0)
        if nk == 1:
            o_ref[...] = part.astype(out_dtype)
        else:
            acc = scr[0]
            kk = pl.program_id(3)

            @pl.when(kk == 0)
            def _():
                acc[...] = part

            @pl.when(kk > 0)
            def _():
                acc[...] += part

            @pl.when(kk == nk - 1)
            def _():
                o_ref[...] = acc[...].astype(out_dtype)

    if trans_b:
        b_spec = pl.BlockSpec((None, tn, tk), lambda bb, i, j, kk: (bb, j, kk))
    else:
        b_spec = pl.BlockSpec((None, tk, tn), lambda bb, i, j, kk: (bb, kk, j))
    out = pl.pallas_call(
        body, name=name,
        grid=(nb, m // tm, n // tn, nk),
        in_specs=[pl.BlockSpec((None, tm, tk), lambda bb, i, j, kk: (bb, i, kk)), b_spec],
        out_specs=pl.BlockSpec((None, tm, tn), lambda bb, i, j, kk: (bb, i, j)),
        out_shape=jax.ShapeDtypeStruct((nb, m, n), out_dtype),
        scratch_shapes=[] if nk == 1 else [pltpu.VMEM((tm, tn), F32)],
        compiler_params=_cparams(("parallel", "parallel", "parallel", "arbitrary")),
    )(a, b)
    return out[0] if squeeze else out


def _adaln_fn(cc, w, b):
    return mm(_silu(cc), w) + b


def _adaln_fwd(cc, w_ada, b_ada):
    def body(cc_ref, w_ref, b_ref, o_ref):
        o_ref[...] = _adaln_fn(cc_ref[...], w_ref[...], b_ref[...])

    return pl.pallas_call(
        body, name="adaln_fwd", out_shape=jax.ShapeDtypeStruct((8, 3 * D_MODEL), F32),
        compiler_params=pltpu.CompilerParams(vmem_limit_bytes=VMEM_LIMIT),
    )(cc, w_ada, b_ada)


def _adaln_bwd(cc, w_ada, dmod):
    def body(cc_ref, w_ref, g_ref, dcc_ref, dw_ref, db_ref):
        cc_v, g = cc_ref[...], g_ref[...]
        dw_ref[...] = mm_tn(_silu(cc_v), g)
        db_ref[...] = jnp.sum(g, axis=0, keepdims=True)
        dcc_ref[...] = mm_nt(g, w_ref[...]) * _dsilu(cc_v)

    return pl.pallas_call(
        body, name="adaln_bwd",
        out_shape=(jax.ShapeDtypeStruct((8, D_MODEL), F32),
                   jax.ShapeDtypeStruct((D_MODEL, 3 * D_MODEL), F32),
                   jax.ShapeDtypeStruct((1, 3 * D_MODEL), F32)),
        compiler_params=pltpu.CompilerParams(vmem_limit_bytes=VMEM_LIMIT),
    )(cc, w_ada, dmod)


def _seg_spec(which, n_ctx_tiles):
    return pl.BlockSpec((None, None, 1, D_MODEL),
                        lambda i: (jnp.where(i < n_ctx_tiles, 0, 1), which, 0, 0))


def _row_spec(width, col_block=0, tile=TOK):
    return pl.BlockSpec((tile, width), lambda i: (i, col_block))


def _const_spec(shape):
    zeros = (0,) * len(shape)
    return pl.BlockSpec(shape, lambda i: zeros)


def _modnorm_fn(x, g, shift, scale):
    y = x * lax.rsqrt(jnp.mean(x * x, axis=-1, keepdims=True) + EPS)
    return (y * g) * (1.0 + scale) + shift


def _modnorm_fwd(xt, g, mod4, n_ctx):
    t = xt.shape[0]
    nct = n_ctx // TOK

    def body(x_ref, g_ref, sh_ref, sc_ref, o_ref):
        o_ref[...] = _modnorm_fn(x_ref[...], g_ref[...], sh_ref[...], sc_ref[...]).astype(BF16)

    return pl.pallas_call(
        body, name="modnorm_fwd", grid=(t // TOK,),
        in_specs=[_row_spec(D_MODEL), _const_spec((1, D_MODEL)), _seg_spec(0, nct), _seg_spec(1, nct)],
        out_specs=_row_spec(D_MODEL),
        out_shape=jax.ShapeDtypeStruct((t, D_MODEL), BF16),
        compiler_params=_cparams(("parallel",)),
    )(xt, g, mod4, mod4)


def _modnorm_bwd(xt, g, mod4, dh, dres, n_ctx):
    t = xt.shape[0]
    nct = n_ctx // TOK

    def body(x_ref, g_ref, sh_ref, sc_ref, dh_ref, dres_ref, dx_ref, dg_ref, dsh_ref, dsc_ref):
        i = pl.program_id(0)
        _, vjp = jax.vjp(_modnorm_fn, x_ref[...], g_ref[...], sh_ref[...], sc_ref[...])
        dx, dg, dsh, dsc = vjp(dh_ref[...])
        dx_ref[...] = dx + dres_ref[...]

        @pl.when(i == 0)
        def _():
            dg_ref[...] = jnp.zeros_like(dg_ref)

        dg_ref[...] += dg
        first = jnp.logical_or(i == 0, i == nct)

        @pl.when(first)
        def _():
            dsh_ref[...] = dsh
            dsc_ref[...] = dsc

        @pl.when(jnp.logical_not(first))
        def _():
            dsh_ref[...] += dsh
            dsc_ref[...] += dsc

    seg_out = lambda which: pl.BlockSpec((None, None, 1, D_MODEL),
                                         lambda i: (jnp.where(i < nct, 0, 1), which, 0, 0))
    dx, dg, dss, dss2 = pl.pallas_call(
        body, name="modnorm_bwd", grid=(t // TOK,),
        in_specs=[_row_spec(D_MODEL), _const_spec((1, D_MODEL)), _seg_spec(0, nct), _seg_spec(1, nct),
                  _row_spec(D_MODEL), _row_spec(D_MODEL)],
        out_specs=[_row_spec(D_MODEL), _const_spec((1, D_MODEL)), seg_out(0), seg_out(0)],
        out_shape=(jax.ShapeDtypeStruct((t, D_MODEL), F32), jax.ShapeDtypeStruct((1, D_MODEL), F32),
                   jax.ShapeDtypeStruct((2, 1, 1, D_MODEL), F32), jax.ShapeDtypeStruct((2, 1, 1, D_MODEL), F32)),
        compiler_params=_cparams(("arbitrary",)),
    )(xt, g, mod4, mod4, dh, dres)
    return dx, dg, dss, dss2


def _group_mean_sq(x, gs):
    x2 = x * x
    hi = x2.astype(BF16).astype(F32)
    return mm(hi, gs) + mm(x2 - hi, gs)


def _head_norm(x, g, gs):
    return (x * lax.rsqrt(_group_mean_sq(x, gs) + EPS)) * g


def _rope(x, cos, sin_signed):
    lane = lax.broadcasted_iota(jnp.int32, (1, 128), 1)
    first_half = jnp.bitwise_and(lane, 63) < 32
    cols = []
    for c in range(x.shape[1] // 128):
        xb = x[:, 128 * c:128 * (c + 1)]
        partner = jnp.where(first_half, lane_roll(xb, 96), lane_roll(xb, 32))
        cols.append(xb * cos + partner * sin_signed)
    return cols[0] if len(cols) == 1 else jnp.concatenate(cols, axis=1)


def _prep_fn(zq_na, zk_na, zv_na, zq_g, zk_g, zv_g, g_naq, g_nak, g_gq, g_gk, cos, sin_signed, gs512, gs128, expand):
    q_na = _head_norm(zq_na, g_naq, gs512)
    k_na = _head_norm(zk_na, g_nak, gs512)
    q_g = _rope(_head_norm(zq_g, g_gq, gs512), cos, sin_signed)
    k_g = _rope(_head_norm(zk_g, g_gk, gs128), cos, sin_signed)
    return q_na, k_na, zv_na, q_g, mm(k_g, expand), mm(zv_g, expand)


def _prep_consts():
    gid = np.arange(512) // 64
    gs512 = (gid[:, None] == gid[None, :]).astype(np.float32) / 64.0
    expand = np.zeros((128, 512), np.float32)
    for h in range(N_HEADS):
        for j in range(64):
            expand[64 * (h // 4) + j, 64 * h + j] = 1.0
    return jnp.asarray(gs512), jnp.asarray(gs512[:128, :128]), jnp.asarray(expand)


def _prep_in_specs():
    blk = lambda off, w: _row_spec(w, off // w)
    return [blk(OFF["na_q"], 512), blk(OFF["na_k"], 512), blk(OFF["na_v"], 512), blk(OFF["gqa_q"], 512),
            blk(OFF["gqa_k"], 128), blk(OFF["gqa_v"], 128),
            _const_spec((1, 512)), _const_spec((1, 512)), _const_spec((1, 512)), _const_spec((1, 128)),
            _row_spec(128), _row_spec(128),
            _const_spec((512, 512)), _const_spec((128, 128)), _const_spec((128, 512))]


def _prep_fwd(z, gains, rope_tabs):
    t = z.shape[0]
    consts = _prep_consts()

    def body(*refs):
        ins, outs = refs[:15], refs[15:]
        res = _prep_fn(*[r[...] for r in ins])
        for o_ref, v in zip(outs, res):
            o_ref[...] = v.astype(BF16)

    return pl.pallas_call(
        body, name="prep_fwd", grid=(t // TOK,),
        in_specs=_prep_in_specs(),
        out_specs=[_row_spec(512)] * 6,
        out_shape=tuple(jax.ShapeDtypeStruct((t, 512), BF16) for _ in range(6)),
        compiler_params=_cparams(("parallel",)),
    )(z, z, z, z, z, z, *gains, *rope_tabs, *consts)


def _prep_bwd(z, gains, rope_tabs, cots, du_a, du_b):
    t = z.shape[0]
    consts = _prep_consts()

    def body(*refs):
        ins, cot, (dua_ref, dub_ref), outs = refs[:15], refs[15:21], refs[21:23], refs[23:]
        i = pl.program_id(0)
        vals = [r[...] for r in ins]
        _, vjp = jax.vjp(lambda *a: _prep_fn(*a, *vals[10:]), *vals[:10])
        grads = vjp(tuple(c[...] for c in cot))
        for o_ref, v in zip(outs[:6], grads[:6]):
            o_ref[...] = v.astype(BF16)
        outs[6][...] = (dua_ref[...] + dub_ref[...]).astype(BF16)

        @pl.when(i == 0)
        def _():
            for o_ref in outs[7:]:
                o_ref[...] = jnp.zeros_like(o_ref)

        for o_ref, v in zip(outs[7:], grads[6:10]):
            o_ref[...] += v

    return pl.pallas_call(
        body, name="prep_bwd", grid=(t // TOK,),
        in_specs=_prep_in_specs() + [_row_spec(512)] * 8,
        out_specs=[_row_spec(512)] * 4 + [_row_spec(128)] * 2 + [_row_spec(512)]
        + [_const_spec((1, 512))] * 3 + [_const_spec((1, 128))],
        out_shape=tuple([jax.ShapeDtypeStruct((t, 512), BF16)] * 4 + [jax.ShapeDtypeStruct((t, 128), BF16)] * 2
                        + [jax.ShapeDtypeStruct((t, 512), BF16)]
                        + [jax.ShapeDtypeStruct((1, 512), F32)] * 3 + [jax.ShapeDtypeStruct((1, 128), F32)]),
        compiler_params=_cparams(("arbitrary",)),
    )(z, z, z, z, z, z, *gains, *rope_tabs, *consts, *cots, du_a, du_b)


def _s5post_fn(ys, u, d, w_glu):
    y = jax.nn.gelu(ys + d * u)
    return y * jax.nn.sigmoid(mm(y, w_glu))


def _s5post_fwd(ys, z, d, w_glu):
    t = z.shape[0]

    def body(ys_ref, u_ref, d_ref, w_ref, o_ref):
        o_ref[...] = _s5post_fn(ys_ref[...], u_ref[...], d_ref[...], w_ref[...])

    return pl.pallas_call(
        body, name="s5post_fwd", grid=(t // TOK,),
        in_specs=[_row_spec(512), _row_spec(512, OFF["s5_u"] // 512),
                  _const_spec((1, 512)), _const_spec((512, 512))],
        out_specs=_row_spec(512), out_shape=jax.ShapeDtypeStruct((t, 512), F32),
        compiler_params=_cparams(("parallel",)),
    )(ys, z, d, w_glu)


def _s5post_bwd(ys, z, d, w_glu, dy):
    t = z.shape[0]

    def body(ys_ref, u_ref, d_ref, w_ref, dy_ref, dpre_ref, du_ref, dd_ref, dw_ref):
        i = pl.program_id(0)
        _, vjp = jax.vjp(_s5post_fn, ys_ref[...], u_ref[...], d_ref[...], w_ref[...].astype(F32))
        dys, du, dd, dw = vjp(dy_ref[...])
        dpre_ref[...] = dys.astype(BF16)
        du_ref[...] = du

        @pl.when(i == 0)
        def _():
            dd_ref[...] = jnp.zeros_like(dd_ref)
            dw_ref[...] = jnp.zeros_like(dw_ref)

        dd_ref[...] += dd
        dw_ref[...] += dw

    return pl.pallas_call(
        body, name="s5post_bwd", grid=(t // TOK,),
        in_specs=[_row_spec(512), _row_spec(512, OFF["s5_u"] // 512),
                  _const_spec((1, 512)), _const_spec((512, 512)), _row_spec(512)],
        out_specs=[_row_spec(512), _row_spec(512), _const_spec((1, 512)), _const_spec((512, 512))],
        out_shape=(jax.ShapeDtypeStruct((t, 512), BF16), jax.ShapeDtypeStruct((t, 512), F32),
                   jax.ShapeDtypeStruct((1, 512), F32), jax.ShapeDtypeStruct((512, 512), F32)),
        compiler_params=_cparams(("arbitrary",)),
    )(ys, z, d, w_glu, dy)


def _halo_specs(col_block, t):
    last = t // 8 - 1
    prev = pl.BlockSpec((8, 512), lambda i: (jnp.maximum(i * (TOK // 8) - 1, 0), col_block))
    nxt = pl.BlockSpec((8, 512), lambda i: (jnp.minimum((i + 1) * (TOK // 8), last), col_block))
    return [_row_spec(512, col_block), prev, nxt]


def _shifted(cur, prev_row, next_row, tok0, n_ctx, t_total):
    row = lax.broadcasted_iota(jnp.int32, (TOK, 1), 0)
    tpos = row + tok0
    down = jnp.where(row == 0, prev_row, pltpu.roll(cur, 1, 0))
    down = jnp.where(jnp.logical_or(tpos == 0, tpos == n_ctx), 0.0, down)
    up = jnp.where(row == TOK - 1, next_row, pltpu.roll(cur, TOK - 1, 0))
    up = jnp.where(jnp.logical_or(tpos == n_ctx - 1, tpos == t_total - 1), 0.0, up)
    return down, up


def _conv_fwd(z, conv_w, conv_b, n_ctx):
    t = z.shape[0]

    def body(v_ref, vp_ref, vn_ref, c_ref, cp_ref, cn_ref, b_ref, w_ref, cb_ref, o_ref):
        tok0 = pl.program_id(0) * TOK
        zz = v_ref[...] * c_ref[...]
        zz_m1, zz_p1 = _shifted(zz, vp_ref[7:8, :] * cp_ref[7:8, :], vn_ref[0:1, :] * cn_ref[0:1, :], tok0, n_ctx, t)
        s = cb_ref[...] + zz_m1 * w_ref[0:1, :] + zz * w_ref[1:2, :] + zz_p1 * w_ref[2:3, :]
        o_ref[...] = b_ref[...] * s

    return pl.pallas_call(
        body, name="conv_fwd", grid=(t // TOK,),
        in_specs=_halo_specs(OFF["conv_v"] // 512, t) + _halo_specs(OFF["conv_c"] // 512, t)
        + [_row_spec(512, OFF["conv_b"] // 512), _const_spec((8, 512)), _const_spec((1, 512))],
        out_specs=_row_spec(512), out_shape=jax.ShapeDtypeStruct((t, 512), F32),
        compiler_params=_cparams(("parallel",)),
    )(z, z, z, z, z, z, z, conv_w, conv_b)


def _conv_bwd(z, conv_w, conv_b, dy, n_ctx):
    t = z.shape[0]

    def body(v_ref, vp_ref, vn_ref, c_ref, cp_ref, cn_ref, b_ref, bp_ref, bn_ref, dy_ref, dyp_ref, dyn_ref,
             w_ref, cb_ref, dv_ref, db_ref, dc_ref, dw_ref, dcb_ref):
        i = pl.program_id(0)
        tok0 = i * TOK
        v, c, b, dy_v = v_ref[...], c_ref[...], b_ref[...], dy_ref[...]
        w0, w1, w2 = w_ref[0:1, :], w_ref[1:2, :], w_ref[2:3, :]
        zz = v * c
        zz_m1, zz_p1 = _shifted(zz, vp_ref[7:8, :] * cp_ref[7:8, :], vn_ref[0:1, :] * cn_ref[0:1, :], tok0, n_ctx, t)
        s = cb_ref[...] + zz_m1 * w0 + zz * w1 + zz_p1 * w2
        ds = dy_v * b
        ds_m1, ds_p1 = _shifted(ds, dyp_ref[7:8, :] * bp_ref[7:8, :], dyn_ref[0:1, :] * bn_ref[0:1, :], tok0, n_ctx, t)
        dzz = ds_p1 * w0 + ds * w1 + ds_m1 * w2
        db_ref[...] = (dy_v * s).astype(BF16)
        dv_ref[...] = (dzz * c).astype(BF16)
        dc_ref[...] = (dzz * v).astype(BF16)

        @pl.when(i == 0)
        def _():
            dw_ref[...] = jnp.zeros_like(dw_ref)
            dcb_ref[...] = jnp.zeros_like(dcb_ref)

        rsum = lambda a: jnp.sum(a, axis=0, keepdims=True)
        dw_ref[0:1, :] += rsum(ds * zz_m1)
        dw_ref[1:2, :] += rsum(ds * zz)
        dw_ref[2:3, :] += rsum(ds * zz_p1)
        dcb_ref[...] += rsum(ds)

    return pl.pallas_call(
        body, name="conv_bwd", grid=(t // TOK,),
        in_specs=_halo_specs(OFF["conv_v"] // 512, t) + _halo_specs(OFF["conv_c"] // 512, t)
        + _halo_specs(OFF["conv_b"] // 512, t) + _halo_specs(0, t) + [_const_spec((8, 512)), _const_spec((1, 512))],
        out_specs=[_row_spec(512)] * 3 + [_const_spec((8, 512)), _const_spec((1, 512))],
        out_shape=tuple([jax.ShapeDtypeStruct((t, 512), BF16)] * 3
                        + [jax.ShapeDtypeStruct((8, 512), F32), jax.ShapeDtypeStruct((1, 512), F32)]),
        compiler_params=_cparams(("arbitrary",)),
    )(z, z, z, z, z, z, z, z, z, dy, dy, dy, conv_w, conv_b)


def _merge_col_specs(tile):
    specs = []
    for off in MERGE_OFFS:
        specs.append(pl.BlockSpec((tile, 512), functools.partial(lambda i, cb: (i, cb), cb=off // 512)))
        specs.append(pl.BlockSpec((tile, 512), functools.partial(lambda i, cb: (i, cb), cb=off // 512 + 1)))
    return specs


def _merge_fwd(xt, ys, z, mod4, w_br, w_out, n_ctx):
    t = xt.shape[0]
    nct = n_ctx // TOK

    def body(x_ref, *refs):
        y_refs, gt_refs, mg_refs = refs[0:4], refs[4:8], refs[8:16]
        gate_ref, wbr_ref, wout_ref, o_ref = refs[16:20]
        acc_lo = acc_hi = None
        for k in range(4):
            gated = y_refs[k][...] * _silu(gt_refs[k][...])
            proj = mm(gated, wbr_ref[k])
            lo = jax.nn.sigmoid(mg_refs[2 * k][...]) * proj[:, :512]
            hi = jax.nn.sigmoid(mg_refs[2 * k + 1][...]) * proj[:, 512:]
            acc_lo = lo if acc_lo is None else acc_lo + lo
            acc_hi = hi if acc_hi is None else acc_hi + hi
        acc = jnp.concatenate([acc_lo, acc_hi], axis=1)
        o_ref[...] = x_ref[...] + gate_ref[...] * mm(acc, wout_ref[...])

    gate_specs = [pl.BlockSpec((TOK, 512), functools.partial(lambda i, cb: (i, cb), cb=o // 512)) for o in GATE_OFFS]
    return pl.pallas_call(
        body, name="merge_fwd", grid=(t // TOK,),
        in_specs=[_row_spec(D_MODEL)] + [_row_spec(512)] * 4 + gate_specs + _merge_col_specs(TOK)
        + [_seg_spec(2, nct), _const_spec((4, 512, 1024)), _const_spec((1024, 1024))],
        out_specs=_row_spec(D_MODEL), out_shape=jax.ShapeDtypeStruct((t, D_MODEL), F32),
        compiler_params=_cparams(("parallel",)),
    )(xt, *ys, z, z, z, z, z, z, z, z, z, z, z, z, mod4, w_br, w_out)


MERGE_BWD_TILE = 128


def _merge_bwd(g, ys, z, mod4, w_br, w_out, n_ctx):
    t = g.shape[0]
    tile = MERGE_BWD_TILE
    nct = n_ctx // tile
    nsteps = t // tile

    def body(g_ref, *refs):
        y_refs, gt_refs, mg_refs = refs[0:4], refs[4:8], refs[8:16]
        gate_ref, wbr_hbm, wout_hbm = refs[16:19]
        dy_refs, dgt_refs, dmg_refs = refs[19:23], refs[23:27], refs[27:31]
        dgate_ref, dwbr_hbm, dwout_hbm = refs[31:34]
        wbr_v, wout_v, dwbr_acc, dwout_acc = refs[34:38]
        i = pl.program_id(0)

        @pl.when(i == 0)
        def _():
            pltpu.sync_copy(wbr_hbm, wbr_v)
            pltpu.sync_copy(wout_hbm, wout_v)
            dwbr_acc[...] = jnp.zeros_like(dwbr_acc)
            dwout_acc[...] = jnp.zeros_like(dwout_acc)

        g_v, gate = g_ref[...], gate_ref[...]
        gated, proj, sig = [], [], []
        acc = None
        for k in range(4):
            gated.append(y_refs[k][...] * _silu(gt_refs[k][...]))
            proj.append(mm(gated[k], wbr_v[k]))
            sig.append(jax.nn.sigmoid(jnp.concatenate([mg_refs[2 * k][...], mg_refs[2 * k + 1][...]], axis=1)))
            contrib = sig[k] * proj[k]
            acc = contrib if acc is None else acc + contrib
        o = mm(acc, wout_v[...])
        dgate = jnp.sum(g_v * o, axis=0, keepdims=True)
        first = jnp.logical_or(i == 0, i == nct)

        @pl.when(first)
        def _():
            dgate_ref[...] = dgate

        @pl.when(jnp.logical_not(first))
        def _():
            dgate_ref[...] += dgate

        do = g_v * gate
        dwout_acc[...] += mm_tn(acc, do)
        dacc = mm_nt(do, wout_v[...])
        for k in range(4):
            dmg_refs[k][...] = (dacc * proj[k] * sig[k] * (1.0 - sig[k])).astype(BF16)
            dproj = dacc * sig[k]
            dwbr_acc[k] += mm_tn(gated[k], dproj)
            dgated = mm_nt(dproj, wbr_v[k])
            gt = gt_refs[k][...]
            dy_refs[k][...] = dgated * _silu(gt)
            dgt_refs[k][...] = (dgated * y_refs[k][...] * _dsilu(gt)).astype(BF16)

        @pl.when(i == nsteps - 1)
        def _():
            pltpu.sync_copy(dwbr_acc, dwbr_hbm)
            pltpu.sync_copy(dwout_acc, dwout_hbm)

    row = lambda w: _row_spec(w, 0, tile)
    gate_specs = [pl.BlockSpec((tile, 512), functools.partial(lambda i, cb: (i, cb), cb=o // 512)) for o in GATE_OFFS]
    anyspec = pl.BlockSpec(memory_space=pl.ANY)
    seg = pl.BlockSpec((None, None, 1, D_MODEL), lambda i: (jnp.where(i < nct, 0, 1), 2, 0, 0))
    seg_out = pl.BlockSpec((None, None, 1, D_MODEL), lambda i: (jnp.where(i < nct, 0, 1), 0, 0, 0))
    res = pl.pallas_call(
        body, name="merge_bwd", grid=(nsteps,),
        in_specs=[row(D_MODEL)] + [row(512)] * 4 + gate_specs + _merge_col_specs(tile) + [seg, anyspec, anyspec],
        out_specs=[row(512)] * 8 + [row(1024)] * 4 + [seg_out, anyspec, anyspec],
        out_shape=tuple([jax.ShapeDtypeStruct((t, 512), F32)] * 4 + [jax.ShapeDtypeStruct((t, 512), BF16)] * 4
                        + [jax.ShapeDtypeStruct((t, 1024), BF16)] * 4
                        + [jax.ShapeDtypeStruct((2, 1, 1, D_MODEL), F32),
                           jax.ShapeDtypeStruct((4, 512, 1024), F32), jax.ShapeDtypeStruct((1024, 1024), F32)]),
        scratch_shapes=[pltpu.VMEM((4, 512, 1024), BF16), pltpu.VMEM((1024, 1024), BF16),
                        pltpu.VMEM((4, 512, 1024), F32), pltpu.VMEM((1024, 1024), F32)],
        compiler_params=_cparams(("arbitrary",)),
    )(g, *ys, z, z, z, z, z, z, z, z, z, z, z, z, mod4, w_br, w_out)
    return res[0:4], res[4:8], res[8:12], res[12], res[13], res[14]


SCAN_ROWS = 16


def _scan_spec(k):
    return pl.BlockSpec((k, SCAN_ROWS, 128), lambda i: (0, i, 0))


def _scan_chunk(j, k, n_ctx_chunks, is_backward):
    backward = jnp.where(j < n_ctx_chunks, n_ctx_chunks - 1 - j, k - 1 - (j - n_ctx_chunks))
    return jnp.where(is_backward, backward, j)


def _scan_block_is_backward():
    return pl.program_id(0) >= 32 // SCAN_ROWS


def _s5_scan_fwd(s, a1, a2, k, n_ctx_chunks):
    kp = s.shape[0]

    def body(s_ref, a1_ref, a2_ref, hp_ref):
        a1_v, a2_v = a1_ref[...], a2_ref[...]
        hp_ref[...] = jnp.zeros_like(hp_ref)
        is_backward = _scan_block_is_backward()

        def step(j, h):
            c = _scan_chunk(j, k, n_ctx_chunks, is_backward)
            hp_ref[c] = h
            return a1_v * h + a2_v * pltpu.roll(h, 64, 1) + s_ref[c]

        lax.fori_loop(0, k, step, jnp.zeros((SCAN_ROWS, 128), F32))

    vec = pl.BlockSpec((SCAN_ROWS, 128), lambda i: (i, 0))
    return pl.pallas_call(
        body, name="s5_scan_fwd", grid=(64 // SCAN_ROWS,),
        in_specs=[_scan_spec(kp), vec, vec], out_specs=_scan_spec(kp),
        out_shape=jax.ShapeDtypeStruct(s.shape, F32),
        compiler_params=_cparams(("parallel",)),
    )(s, a1, a2)


def _s5_scan_bwd(dhp, hp, a1, a2, k, n_ctx_chunks):
    kp = hp.shape[0]

    def body(dhp_ref, hp_ref, a1_ref, a2_ref, ds_ref, da1_ref, da2_ref):
        a1_v, a2_v = a1_ref[...], a2_ref[...]
        ds_ref[...] = jnp.zeros_like(ds_ref)
        is_backward = _scan_block_is_backward()

        def step(j, carry):
            lam, d1, d2 = carry
            c = _scan_chunk(k - 1 - j, k, n_ctx_chunks, is_backward)
            ds_ref[c] = lam
            h = hp_ref[c]
            d1 = d1 + lam * h
            d2 = d2 + lam * pltpu.roll(h, 64, 1)
            lam = dhp_ref[c] + a1_v * lam + pltpu.roll(a2_v * lam, 64, 1)
            return lam, d1, d2

        zero = jnp.zeros((SCAN_ROWS, 128), F32)
        _, d1, d2 = lax.fori_loop(0, k, step, (zero, zero, zero))
        da1_ref[...] = d1
        da2_ref[...] = d2

    vec = pl.BlockSpec((SCAN_ROWS, 128), lambda i: (i, 0))
    return pl.pallas_call(
        body, name="s5_scan_bwd", grid=(64 // SCAN_ROWS,),
        in_specs=[_scan_spec(kp), _scan_spec(kp), vec, vec], out_specs=[_scan_spec(kp), vec, vec],
        out_shape=(jax.ShapeDtypeStruct(hp.shape, F32), jax.ShapeDtypeStruct((64, 128), F32),
                   jax.ShapeDtypeStruct((64, 128), F32)),
        compiler_params=_cparams(("parallel",)),
    )(dhp, hp, a1, a2)


def _attn_block(q, k3, v3, kc, vc, bias0, bias1, sink, qb, *, mode, n_lat):
    lane = lax.broadcasted_iota(jnp.int32, (1, 128), 1)
    iq = lax.broadcasted_iota(jnp.int32, (ATT_BLK, 3 * ATT_BLK), 0)
    ik = lax.broadcasted_iota(jnp.int32, (ATT_BLK, 3 * ATT_BLK), 1)
    ql = qb - 1
    if mode == "na":
        n_rows = n_lat // GRID_W
        r = 4 * ql + lax.shift_right_logical(iq, 6)
        qcol = jnp.bitwise_and(iq, 63)
        kr = 4 * (ql - 1) + lax.shift_right_logical(ik, 6)
        kcol = jnp.bitwise_and(ik, 63)
        rs = jnp.clip(r - NA_ROWS // 2, 0, n_rows - NA_ROWS)
        cs = jnp.clip(qcol - NA_COLS // 2, 0, GRID_W - NA_COLS)
        valid = (kr >= rs) & (kr < rs + NA_ROWS) & (kcol >= cs) & (kcol < cs + NA_COLS)
    else:
        tq = ATT_BLK * ql + iq
        ts = ATT_BLK * (ql - 1) + ik
        valid = (jnp.abs(tq - ts) <= WINDOW) & (ts >= 0) & (ts < n_lat)
    valid = valid & (qb > 0)
    scale = HEAD_DIM ** -0.5
    outs = []
    for e, bias in enumerate((bias0, bias1)):
        in_head = (lane < 64) if e == 0 else (lane >= 64)
        qe = jnp.where(in_head, q, 0.0)
        s_lat = mm_nt(qe, k3) * scale
        if bias is not None:
            s_lat = s_lat + bias
        s_lat = jnp.where(valid, s_lat, NEG_INF)
        s_ctx = mm_nt(qe, kc) * scale
        mx = jnp.maximum(jnp.max(s_lat, axis=1, keepdims=True), jnp.max(s_ctx, axis=1, keepdims=True))
        if sink is not None:
            srow = lax.broadcasted_iota(jnp.int32, sink.shape, 0)
            sv = jnp.sum(jnp.where(srow == e, sink, 0.0), keepdims=True) * (1.0 / 128.0)
            mx = jnp.maximum(mx, sv)
        mx = lax.stop_gradient(mx)
        e_lat = jnp.exp(s_lat - mx)
        e_ctx = jnp.exp(s_ctx - mx)
        den = jnp.sum(e_lat, axis=1, keepdims=True) + jnp.sum(e_ctx, axis=1, keepdims=True)
        if sink is not None:
            den = den + jnp.exp(sv - mx)
        inv = 1.0 / den
        outs.append(mm(e_lat * inv, v3) + mm(e_ctx * inv, vc))
    return jnp.where(lane < 64, outs[0], outs[1])


def _attn_specs(n_ctx):
    def kwin(s):
        return pl.BlockSpec((ATT_BLK, 128), lambda hp, qb: (jnp.maximum(qb - 1, 0) + s, hp))

    q = pl.BlockSpec((ATT_BLK, 128), lambda hp, qb: (qb, hp))
    ctx = pl.BlockSpec((n_ctx, 128), lambda hp, qb: (0, hp))
    bias = pl.BlockSpec((None, 2, ATT_BLK, 3 * ATT_BLK), lambda hp, qb: (hp, 0, 0, 0))
    sink = pl.BlockSpec((None, 8, 128), lambda hp, qb: (hp, 0, 0))
    return q, [kwin(0), kwin(1), kwin(2)], ctx, bias, sink


def _attn_fwd(q, kpad, vpad, kc, vc, bias, sink, *, mode, n_ctx):
    t = q.shape[0]
    n_lat = t - n_ctx
    qs, kws, ctx, bias_s, sink_s = _attn_specs(n_ctx)
    has_bias, has_sink = bias is not None, sink is not None

    def body(*refs):
        q_ref, k_refs, v_refs, kc_ref, vc_ref = refs[0], refs[1:4], refs[4:7], refs[7], refs[8]
        rest = list(refs[9:])
        b_ref = rest.pop(0) if has_bias else None
        s_ref = rest.pop(0) if has_sink else None
        o_ref = rest[0]
        k3 = jnp.concatenate([r[...] for r in k_refs], axis=0)
        v3 = jnp.concatenate([r[...] for r in v_refs], axis=0)
        o_ref[...] = _attn_block(
            q_ref[...], k3, v3, kc_ref[...], vc_ref[...],
            b_ref[0] if has_bias else None, b_ref[1] if has_bias else None,
            s_ref[...] if has_sink else None, pl.program_id(1), mode=mode, n_lat=n_lat)

    in_specs = [qs] + kws + kws + [ctx, ctx] + ([bias_s] if has_bias else []) + ([sink_s] if has_sink else [])
    args = [q, kpad, kpad, kpad, vpad, vpad, vpad, kc, vc] + ([bias] if has_bias else []) + ([sink] if has_sink else [])
    return pl.pallas_call(
        body, name=mode + "_attn_fwd", grid=(4, t // ATT_BLK),
        in_specs=in_specs, out_specs=qs, out_shape=jax.ShapeDtypeStruct((t, 512), F32),
        compiler_params=_cparams(("parallel", "parallel")),
    )(*args)


def _attn_bwd(q, kpad, vpad, kc, vc, bias, sink, do, *, mode, n_ctx):
    t = q.shape[0]
    n_lat = t - n_ctx
    nqb = t // ATT_BLK
    qs, kws, ctx, bias_s, sink_s = _attn_specs(n_ctx)
    has_bias, has_sink = bias is not None, sink is not None
    n_in = 10 + has_bias + has_sink

    def body(*refs):
        q_ref, k_refs, v_refs, kc_ref, vc_ref = refs[0], refs[1:4], refs[4:7], refs[7], refs[8]
        rest = list(refs[9:n_in])
        b_ref = rest.pop(0) if has_bias else None
        s_ref = rest.pop(0) if has_sink else None
        do_ref = rest[0]
        outs = list(refs[n_in:])
        dq_ref, dkp_ref, dvp_ref, dkc_ref, dvc_ref = outs[:5]
        outs = outs[5:]
        db_ref = outs.pop(0) if has_bias else None
        ds_ref = outs.pop(0) if has_sink else None
        qb = pl.program_id(1)
        up = lambda r: r[...].astype(F32)
        k3 = jnp.concatenate([up(r) for r in k_refs], axis=0)
        v3 = jnp.concatenate([up(r) for r in v_refs], axis=0)
        prim = [up(q_ref), k3, v3, up(kc_ref), up(vc_ref)]
        if has_bias:
            prim += [b_ref[0], b_ref[1]]
        if has_sink:
            prim += [s_ref[...]]

        def fn(*a):
            a = list(a)
            qv, k3v, v3v, kcv, vcv = a[:5]
            a = a[5:]
            b0 = a.pop(0) if has_bias else None
            b1 = a.pop(0) if has_bias else None
            sk = a.pop(0) if has_sink else None
            return _attn_block(qv, k3v, v3v, kcv, vcv, b0, b1, sk, qb, mode=mode, n_lat=n_lat)

        _, vjp = jax.vjp(fn, *prim)
        grads = list(vjp(do_ref[...]))
        dq_ref[...] = grads[0]
        for s in range(3):
            dkp_ref[s] = grads[1][ATT_BLK * s:ATT_BLK * (s + 1), :]
            dvp_ref[s] = grads[2][ATT_BLK * s:ATT_BLK * (s + 1), :]

        @pl.when(qb == 0)
        def _():
            dkc_ref[...] = jnp.zeros_like(dkc_ref)
            dvc_ref[...] = jnp.zeros_like(dvc_ref)
            if has_bias:
                db_ref[...] = jnp.zeros_like(db_ref)
            if has_sink:
                ds_ref[...] = jnp.zeros_like(ds_ref)

        dkc_ref[...] += grads[3]
        dvc_ref[...] += grads[4]
        rest_g = grads[5:]
        if has_bias:
            db_ref[0] += rest_g.pop(0)
            db_ref[1] += rest_g.pop(0)
        if has_sink:
            ds_ref[...] += rest_g.pop(0)

    part = pl.BlockSpec((None, None, 3, ATT_BLK, 128), lambda hp, qb: (hp, qb, 0, 0, 0))
    in_specs = [qs] + kws + kws + [ctx, ctx] + ([bias_s] if has_bias else []) + ([sink_s] if has_sink else []) + [qs]
    args = ([q, kpad, kpad, kpad, vpad, vpad, vpad, kc, vc] + ([bias] if has_bias else [])
            + ([sink] if has_sink else []) + [do])
    out_specs = [qs, part, part, ctx, ctx] + ([bias_s] if has_bias else []) + ([sink_s] if has_sink else [])
    out_shape = [jax.ShapeDtypeStruct((t, 512), F32),
                 jax.ShapeDtypeStruct((4, nqb, 3, ATT_BLK, 128), F32),
                 jax.ShapeDtypeStruct((4, nqb, 3, ATT_BLK, 128), F32),
                 jax.ShapeDtypeStruct((n_ctx, 512), F32), jax.ShapeDtypeStruct((n_ctx, 512), F32)]
    if has_bias:
        out_shape.append(jax.ShapeDtypeStruct((4, 2, ATT_BLK, 3 * ATT_BLK), F32))
    if has_sink:
        out_shape.append(jax.ShapeDtypeStruct((4, 8, 128), F32))
    res = list(pl.pallas_call(
        body, name=mode + "_attn_bwd", grid=(4, nqb),
        in_specs=in_specs, out_specs=out_specs, out_shape=tuple(out_shape),
        compiler_params=_cparams(("parallel", "arbitrary")),
    )(*args))
    dq, dkp, dvp, dkc, dvc = res[:5]
    res = res[5:]
    dbias = res.pop(0) if has_bias else None
    dsink = res.pop(0) if has_sink else None
    return dq, dkp, dvp, dkc, dvc, dbias, dsink


def _window_fold(part, dctx, n_ctx):
    nqb = part.shape[1]
    nkb = nqb - 1

    def body(p0_ref, p1_ref, p2_ref, c_ref, o_ref):
        kb = pl.program_id(1) - 1

        @pl.when(kb < 0)
        def _():
            o_ref[...] = c_ref[...]

        @pl.when(kb >= 0)
        def _():
            acc = p1_ref[...]
            acc = acc + jnp.where(kb + 2 <= nkb, p0_ref[...], 0.0)
            acc = acc + jnp.where(kb >= 1, p2_ref[...], 0.0)
            o_ref[...] = acc

    def pspec(s, dq):
        return pl.BlockSpec((None, None, None, ATT_BLK, 128),
                            lambda hp, b: (hp, jnp.clip(b + dq, 1, nqb - 1), s, 0, 0))

    return pl.pallas_call(
        body, name="window_fold", grid=(4, nqb),
        in_specs=[pspec(0, 1), pspec(1, 0), pspec(2, -1), pl.BlockSpec((n_ctx, 128), lambda hp, b: (0, hp))],
        out_specs=pl.BlockSpec((ATT_BLK, 128), lambda hp, b: (b, hp)),
        out_shape=jax.ShapeDtypeStruct((nqb * ATT_BLK, 512), F32),
        compiler_params=_cparams(("parallel", "parallel")),
    )(part, part, part, dctx)


def _loss_head(xt, target, n_ctx):
    t = xt.shape[0]
    nct = n_ctx // TOK

    def body(x_ref, t_ref, l_ref, d_ref):
        i = pl.program_id(0)

        @pl.when(i == 0)
        def _():
            l_ref[...] = jnp.zeros_like(l_ref)

        @pl.when(i < nct)
        def _():
            d_ref[...] = jnp.zeros_like(d_ref)

        @pl.when(i >= nct)
        def _():
            err = x_ref[...] - t_ref[...]
            d_ref[...] = err * (1.0 / D_MODEL)
            l_ref[...] += jnp.sum(err * err, keepdims=True) * (0.5 / D_MODEL)

    return pl.pallas_call(
        body, name="loss_head", grid=(t // TOK,),
        in_specs=[_row_spec(D_MODEL), pl.BlockSpec((TOK, D_MODEL), lambda i: (jnp.maximum(i - nct, 0), 0))],
        out_specs=[_const_spec((8, 128)), _row_spec(D_MODEL)],
        out_shape=(jax.ShapeDtypeStruct((8, 128), F32), jax.ShapeDtypeStruct((t, D_MODEL), F32)),
        compiler_params=_cparams(("arbitrary",)),
    )(xt, target)


PACK_W = 1024
SUM_STEPS = 8


def _sum_chips(recvs):
    def split(a):
        rows = a.shape[1]
        if rows % (8 * SUM_STEPS):
            return None
        return rows // SUM_STEPS

    def body(*refs):
        n = len(refs) // 2
        for r_ref, o_ref in zip(refs[:n], refs[n:]):
            o_ref[...] = ((r_ref[0] + r_ref[1]) + r_ref[2]) + r_ref[3]

    in_specs, out_specs = [], []
    for a in recvs:
        rb, tail = split(a), a.shape[2:]
        zeros = (0,) * len(tail)
        if rb is None:
            in_specs.append(pl.BlockSpec(a.shape, functools.partial(lambda i, z: (0, 0) + z, z=zeros)))
            out_specs.append(pl.BlockSpec(a.shape[1:], functools.partial(lambda i, z: (0,) + z, z=zeros)))
        else:
            in_specs.append(pl.BlockSpec((4, rb) + tail, functools.partial(lambda i, z: (0, i) + z, z=zeros)))
            out_specs.append(pl.BlockSpec((rb,) + tail, functools.partial(lambda i, z: (i,) + z, z=zeros)))
    return pl.pallas_call(
        body, name="sum_chips", grid=(SUM_STEPS,),
        in_specs=in_specs, out_specs=out_specs,
        out_shape=tuple(jax.ShapeDtypeStruct(a.shape[1:], F32) for a in recvs),
        compiler_params=_cparams(("arbitrary",)),
    )(*recvs)


ADAM_BLOCK_BYTES = 1 << 20


def _adamw(p_a, p_b, w, m, v, name):
    layers, rows, cols = w.shape
    tr = rows
    while tr % 16 == 0 and tr * cols * 4 > ADAM_BLOCK_BYTES:
        tr //= 2
    c1 = 1.0 / (1.0 - ADAM_B1 ** ADAM_STEP)
    c2 = 1.0 / (1.0 - ADAM_B2 ** ADAM_STEP)

    def body(a_ref, b_ref, w_ref, m_ref, v_ref, g_ref, d_ref, nm_ref, nv_ref):
        g = a_ref[...] + b_ref[...]
        nm = ADAM_B1 * m_ref[...] + (1.0 - ADAM_B1) * g
        nv = ADAM_B2 * v_ref[...] + (1.0 - ADAM_B2) * (g * g)
        g_ref[...] = g
        nm_ref[...] = nm
        nv_ref[...] = nv
        d_ref[...] = -ADAM_LR * ((nm * c1) / (jnp.sqrt(nv * c2) + ADAM_EPS) + ADAM_WD * w_ref[...])

    spec = pl.BlockSpec((None, tr, cols), lambda l, i: (l, i, 0))
    return pl.pallas_call(
        body, name=name, grid=(layers, rows // tr),
        in_specs=[spec] * 5, out_specs=[spec] * 4,
        out_shape=tuple(jax.ShapeDtypeStruct(w.shape, F32) for _ in range(4)),
        compiler_params=_cparams(("parallel", "parallel")),
    )(p_a, p_b, w, m, v)


MESH = pl.DeviceIdType.MESH
ANY_SPEC = pl.BlockSpec(memory_space=pl.ANY)


def _chip_exchange(srcs, out_shapes, src_window, dst_window, name):
    n = len(srcs)

    def body(*refs):
        src_refs, out_refs = refs[:n], refs[n:2 * n]
        send_sems, recv_sems, local_sems = refs[2 * n:]
        x, y, c = lax.axis_index("x"), lax.axis_index("y"), lax.axis_index("c")
        me = 2 * x + y
        peers = [(x, 1 - y), (1 - x, y), (1 - x, 1 - y)]

        def copy(k, j, from_chip, to_chip):
            px, py = peers[j]
            return pltpu.make_async_remote_copy(
                src_ref=src_window(k, src_refs[k], to_chip), dst_ref=dst_window(k, out_refs[k], from_chip),
                send_sem=send_sems.at[3 * k + j], recv_sem=recv_sems.at[3 * k + j],
                device_id=(px, py, c), device_id_type=MESH)

        local = [pltpu.make_async_copy(src_window(k, src_refs[k], me), dst_window(k, out_refs[k], me),
                                       local_sems.at[k]) for k in range(n)]
        for cp in local:
            cp.start()
        sends = [copy(k, j, me, 2 * px + py) for k in range(n) for j, (px, py) in enumerate(peers)]
        for cp in sends:
            cp.start()
        for k in range(n):
            for j, (px, py) in enumerate(peers):
                copy(k, j, 2 * px + py, me).wait_recv()
        for cp in sends:
            cp.wait_send()
        for cp in local:
            cp.wait()

    return pl.pallas_call(
        body, name=name, in_specs=[ANY_SPEC] * n, out_specs=[ANY_SPEC] * n,
        out_shape=tuple(out_shapes),
        scratch_shapes=[pltpu.SemaphoreType.DMA((3 * n,)), pltpu.SemaphoreType.DMA((3 * n,)),
                        pltpu.SemaphoreType.DMA((n,))],
    )(*srcs)


def _core_swap(srcs):
    n = len(srcs)

    def body(*refs):
        src_refs, out_refs, send_sems, recv_sems = refs[:n], refs[n:2 * n], refs[2 * n], refs[2 * n + 1]
        x, y, c = lax.axis_index("x"), lax.axis_index("y"), lax.axis_index("c")
        copies = [pltpu.make_async_remote_copy(
            src_ref=src_refs[k], dst_ref=out_refs[k], send_sem=send_sems.at[k], recv_sem=recv_sems.at[k],
            device_id=(x, y, 1 - c), device_id_type=MESH) for k in range(n)]
        for cp in copies:
            cp.start()
        for cp in copies:
            cp.wait()

    return pl.pallas_call(
        body, name="core_swap", in_specs=[ANY_SPEC] * n, out_specs=[ANY_SPEC] * n,
        out_shape=tuple(jax.ShapeDtypeStruct(s.shape, s.dtype) for s in srcs),
        scratch_shapes=[pltpu.SemaphoreType.DMA((n,)), pltpu.SemaphoreType.DMA((n,))],
    )(*srcs)


def _col_window(ref, start, size):
    idx = (slice(None),) * (len(ref.shape) - 1) + (pl.ds(pl.multiple_of(start, 128), size),)
    return ref.at[idx]


def _row_window(ref, start, size):
    idx = (slice(None),) * (len(ref.shape) - 2) + (pl.ds(pl.multiple_of(start, 8), size), slice(None))
    return ref.at[idx]


N_SHARD_IN = 2624
WIN_W = 2944
WIN_START = (0, 2560, 5248, 7808)
WIN_PIECES = (((0, 2624),), ((64, 2688),), ((0, 640), (896, 2880)), ((320, 2944),))


def _gather_weights(w_ada, w_in, w_glu, w_br, w_out, conv_w):
    lay = w_ada.shape[0]
    sizes = (768, None, 128, 256, 256, 128)

    def dst(k, ref, s):
        if k == 1:
            return ref.at[s]
        if k in (2, 4):
            return _row_window(ref, s * sizes[k], sizes[k])
        return _col_window(ref, s * sizes[k], sizes[k])

    shapes = (jax.ShapeDtypeStruct((lay, D_MODEL, 3 * D_MODEL), w_ada.dtype),
              jax.ShapeDtypeStruct((4,) + w_in.shape, w_in.dtype),
              jax.ShapeDtypeStruct((lay, MIX_W, MIX_W), w_glu.dtype),
              jax.ShapeDtypeStruct((lay, 4, MIX_W, D_MODEL), w_br.dtype),
              jax.ShapeDtypeStruct((lay, D_MODEL, D_MODEL), w_out.dtype),
              jax.ShapeDtypeStruct((lay, 8, MIX_W), conv_w.dtype))
    return _chip_exchange((w_ada, w_in, w_glu, w_br, w_out, conv_w), shapes,
                          lambda k, ref, t: ref, dst, "gather_weights")


def _scatter_grads(dw_ada, dw_in, dw_glu, dw_br, dw_out, dconv_w, small):
    def src(k, ref, t):
        if k == 0:
            return _col_window(ref, t * 768, 768)
        if k == 1:
            start = jnp.where(t == 0, WIN_START[0], jnp.where(t == 1, WIN_START[1],
                              jnp.where(t == 2, WIN_START[2], WIN_START[3])))
            return _col_window(ref, start, WIN_W)
        if k == 2:
            return _row_window(ref, t * 128, 128)
        if k == 3:
            return _col_window(ref, t * 256, 256)
        if k == 4:
            return _row_window(ref, t * 256, 256)
        if k == 5:
            return _col_window(ref, t * 128, 128)
        return ref

    pieces = ((D_MODEL, 768), (D_MODEL, WIN_W), (128, MIX_W), (4, MIX_W, 256), (256, D_MODEL), (8, 128), small.shape)
    shapes = tuple(jax.ShapeDtypeStruct((4,) + p, F32) for p in pieces)
    return _chip_exchange((dw_ada, dw_in, dw_glu, dw_br, dw_out, dconv_w, small), shapes,
                          src, lambda k, ref, s: ref.at[s], "scatter_grads")


def _s5_tables(a_re, a_im, log_dt, b_re, b_im, c_re, c_im):
    ln = S5_CHUNK
    hi = lax.Precision.HIGHEST
    dt = jnp.exp(log_dt)[..., None]
    mag = jnp.exp(dt * a_re)
    abr = mag * jnp.cos(dt * a_im)
    abi = mag * jnp.sin(dt * a_im)
    den = a_re * a_re + a_im * a_im
    fr = ((abr - 1.0) * a_re + abi * a_im) / den
    fi = (abi * a_re - (abr - 1.0) * a_im) / den
    bbr = fr[..., None] * b_re - fi[..., None] * b_im
    bbi = fr[..., None] * b_im + fi[..., None] * b_re
    n = jnp.arange(ln + 1, dtype=F32)[:, None, None, None]
    pm = jnp.exp(n * dt * a_re)
    er = pm * jnp.cos(n * dt * a_im)
    ei = pm * jnp.sin(n * dt * a_im)
    e3 = lambda e, b, c: jnp.einsum("tdgp,dgpa,dgbp->dgabt", e, b, c, precision=hi)
    gt = e3(er[:ln], bbr, c_re) - e3(er[:ln], bbi, c_im) - e3(ei[:ln], bbr, c_im) - e3(ei[:ln], bbi, c_re)
    by_dir = lambda fwd, bwd: jnp.stack([fwd[:, 0], bwd[:, 1]], axis=1)
    erj, eij = by_dir(er[:ln][::-1], er[:ln]), by_dir(ei[:ln][::-1], ei[:ln])
    e2 = lambda e, b: jnp.einsum("jdgp,dgpa->dgajp", e, b, precision=hi)
    w = jnp.concatenate([e2(erj, bbr) - e2(eij, bbi), e2(erj, bbi) + e2(eij, bbr)], axis=-1)
    er1, ei1 = by_dir(er[1:], er[1:][::-1]), by_dir(ei[1:], ei[1:][::-1])
    ev = lambda c, e: jnp.einsum("dgbp,idgp->dgpbi", c, e, precision=hi)
    v = jnp.concatenate([ev(c_re, er1) - ev(c_im, ei1), -(ev(c_re, ei1) + ev(c_im, er1))], axis=2)
    a1 = jnp.concatenate([er[ln], er[ln]], axis=-1)
    a2 = jnp.concatenate([-ei[ln], ei[ln]], axis=-1)
    return (gt.transpose(1, 2, 3, 0, 4).reshape(S5_GROUPS, 256, 2 * ln),
            w.transpose(1, 2, 3, 0, 4).reshape(S5_GROUPS, S5_CH * ln, 256),
            v.transpose(1, 0, 2, 3, 4).reshape(S5_GROUPS, 256, S5_CH * ln),
            a1.reshape(64, 128), a2.reshape(64, 128))


def _lag_onehot():
    ln = S5_CHUNK
    j, i = np.meshgrid(np.arange(ln), np.arange(ln), indexing="ij")
    lag = np.arange(ln)[:, None, None]
    z = np.concatenate([lag == (i - j)[None], lag == (j - i)[None]], axis=0).astype(np.float32)
    return jnp.broadcast_to(jnp.asarray(z.reshape(2 * ln, ln * ln), BF16), (S5_GROUPS, 2 * ln, ln * ln))


def _toeplitz(gt):
    ln = S5_CHUNK
    flat = _matmul(gt, _lag_onehot(), out_dtype=BF16, name="s5_toeplitz")
    return (flat.reshape(S5_GROUPS, S5_CH, S5_CH, ln, ln).transpose(0, 1, 3, 2, 4)
            .reshape(S5_GROUPS, S5_CH * ln, S5_CH * ln))


def _toeplitz_fold(dk):
    ln = S5_CHUNK
    flat = dk.reshape(S5_GROUPS, S5_CH, ln, S5_CH, ln).transpose(0, 1, 3, 2, 4).reshape(S5_GROUPS, 256, ln * ln)
    return _matmul(flat, _lag_onehot(), trans_b=True, name="s5_toeplitz_fold")


def _chunk_rows(t):
    k = t // S5_CHUNK
    return k, -(-k // 128) * 128


def _to_chunks(u):
    k, kp = _chunk_rows(u.shape[0])
    v = u.reshape(k, S5_CHUNK, S5_GROUPS, S5_CH).transpose(2, 0, 3, 1).reshape(S5_GROUPS, k, S5_CH * S5_CHUNK)
    return jnp.pad(v, ((0, 0), (0, kp - k), (0, 0)))


def _from_chunks(y, t):
    k, _ = _chunk_rows(t)
    return y[:, :k].reshape(S5_GROUPS, k, S5_CH, S5_CHUNK).transpose(1, 3, 0, 2).reshape(t, MIX_W)


def _states_to_rows(s):
    kp = s.shape[1]
    return s.reshape(S5_GROUPS, kp, 2, 128).transpose(1, 2, 0, 3).reshape(kp, 64, 128)


def _rows_to_states(h):
    kp = h.shape[0]
    return h.reshape(kp, 2, S5_GROUPS, 128).transpose(2, 0, 1, 3).reshape(S5_GROUPS, kp, 256)


def _na_bias(rel_bias):
    a, m = np.meshgrid(np.arange(4), np.arange(12), indexing="ij")
    di = np.clip(m - a + 3, 0, 2 * NA_ROWS - 2).reshape(-1)
    qc, kc = np.meshgrid(np.arange(GRID_W), np.arange(GRID_W), indexing="ij")
    dj = np.clip(kc - qc + NA_COLS - 1, 0, 2 * NA_COLS - 2).reshape(-1)
    oh_i = jnp.asarray(di[:, None] == np.arange(2 * NA_ROWS - 1)[None, :], F32)
    oh_j = jnp.asarray(dj[:, None] == np.arange(2 * NA_COLS - 1)[None, :], F32)
    hi = lax.Precision.HIGHEST
    cols = jnp.einsum("hij,cj->hic", rel_bias, oh_j, precision=hi)
    full = jnp.einsum("ri,hic->hrc", oh_i, cols, precision=hi)
    full = full.reshape(N_HEADS, 4, 12, GRID_W, GRID_W).transpose(0, 1, 3, 2, 4)
    return full.reshape(4, 2, ATT_BLK, 3 * ATT_BLK)


def _rope_tables(n_ctx, n_lat):
    tok = jnp.arange(n_lat, dtype=jnp.int32)
    row = (tok // GRID_W).astype(F32)
    col = (tok % GRID_W).astype(F32)
    inv = ROPE_BASE ** (-jnp.arange(ROPE_PAIRS, dtype=F32) / ROPE_PAIRS)
    ang = jnp.concatenate([row[:, None] * inv, col[:, None] * inv], axis=-1)
    cos, sin = jnp.cos(ang), jnp.sin(ang)
    cos = jnp.tile(jnp.concatenate([cos, cos], axis=-1), (1, 2))
    sin = jnp.tile(jnp.concatenate([-sin, sin], axis=-1), (1, 2))
    return (jnp.concatenate([jnp.ones((n_ctx, 128), F32), cos], axis=0),
            jnp.concatenate([jnp.zeros((n_ctx, 128), F32), sin], axis=0))


def _pad_blocks(a, n_ctx):
    return jnp.pad(a[n_ctx:], ((ATT_BLK, ATT_BLK), (0, 0)))


def _layer_fwd(xt, cc, w, rope, n_ctx):
    t = xt.shape[0]
    sv = {}
    mod = _adaln_fwd(cc, w["w_ada"], w["b_ada"].reshape(1, -1))
    mod4 = mod[:2].reshape(2, 3, 1, D_MODEL)
    h = _modnorm_fwd(xt, w["norm_g"].reshape(1, -1), mod4, n_ctx)
    z = _matmul(h, w["w_in"], name="proj_fwd")

    s5_args = (w["s5_a_re"], w["s5_a_im"], w["s5_log_dt"], w["s5_b_re"], w["s5_b_im"], w["s5_c_re"], w["s5_c_im"])
    (gt, tw, tv, a1, a2), tab_vjp = jax.vjp(_s5_tables, *s5_args)
    ktoe = _toeplitz(gt)
    tw, tv = tw.astype(BF16), tv.astype(BF16)
    uc = _to_chunks(z[:, :MIX_W].astype(BF16))
    st = _matmul(uc, tw, name="s5_chunk_state")
    hprev = _s5_scan_fwd(_states_to_rows(st), a1, a2, t // S5_CHUNK, n_ctx // S5_CHUNK)
    uh = jnp.concatenate([uc, _rows_to_states(hprev).astype(BF16)], axis=2)
    ysum = _from_chunks(_matmul(uh, jnp.concatenate([ktoe, tv], axis=1), name="s5_chunk_out"), t)
    s5_d = w["s5_d"].reshape(1, MIX_W)
    y_s5 = _s5post_fwd(ysum, z, s5_d, w["s5_w_glu"])

    conv_w = w["conv_w"]
    y_conv = _conv_fwd(z, conv_w, w["conv_b"].reshape(1, -1), n_ctx)

    gains = (jnp.tile(w["na_q_g"], 8)[None], jnp.tile(w["na_k_g"], 8)[None],
             jnp.tile(w["gqa_q_g"], 8)[None], jnp.tile(w["gqa_k_g"], 2)[None])
    q_na, k_na, v_na, q_g, k_g, v_g = _prep_fwd(z, gains, rope)
    bias, bias_vjp = jax.vjp(_na_bias, w["na_rel_bias"])
    sink = jnp.zeros((4, 8, 128), F32).at[:, :2, :].set(
        jnp.broadcast_to(w["gqa_sink"].reshape(4, 2, 1), (4, 2, 128)))
    na_in = (q_na, _pad_blocks(k_na, n_ctx), _pad_blocks(v_na, n_ctx), k_na[:n_ctx], v_na[:n_ctx], bias, None)
    gqa_in = (q_g, _pad_blocks(k_g, n_ctx), _pad_blocks(v_g, n_ctx), k_g[:n_ctx], v_g[:n_ctx], None, sink)
    y_na = _attn_fwd(*na_in, mode="na", n_ctx=n_ctx)
    y_gqa = _attn_fwd(*gqa_in, mode="gqa", n_ctx=n_ctx)
    ys = (y_s5, y_conv, y_na, y_gqa)
    xt_new = _merge_fwd(xt, ys, z, mod4, w["w_br"], w["w_out"], n_ctx)
    sv.update(xt=xt, mod4=mod4, h=h, z=z, tab_vjp=tab_vjp, ktoe=ktoe, tw=tw, tv=tv, a1=a1, a2=a2, uc=uc,
              hprev=hprev, uh=uh, ysum=ysum, s5_d=s5_d, conv_w=conv_w, gains=gains, bias_vjp=bias_vjp,
              na_in=na_in, gqa_in=gqa_in, ys=ys)
    return xt_new, sv


def _layer_bwd(dxt_new, sv, cc, w, rope, n_ctx):
    t = dxt_new.shape[0]
    z, mod4 = sv["z"], sv["mod4"]
    dys, dgt, dmg, dgate, dw_br, dw_out = _merge_bwd(dxt_new, sv["ys"], z, mod4, w["w_br"], w["w_out"], n_ctx)

    dpre, du_skip, dd, dw_glu = _s5post_bwd(sv["ysum"], z, sv["s5_d"], w["s5_w_glu"], dys[0])
    dyc = _to_chunks(dpre)
    dhp = _matmul(dyc, sv["tv"], trans_b=True, name="s5_bwd_state")
    ds, da1, da2 = _s5_scan_bwd(_states_to_rows(dhp), sv["hprev"], sv["a1"], sv["a2"],
                                t // S5_CHUNK, n_ctx // S5_CHUNK)
    ds = _rows_to_states(ds).astype(BF16)
    duc = _matmul(jnp.concatenate([dyc, ds], axis=2), jnp.concatenate([sv["ktoe"], sv["tw"]], axis=2),
                  trans_b=True, name="s5_bwd_u")
    dkv = _matmul(sv["uh"].transpose(0, 2, 1), dyc, name="s5_bwd_kv")
    dtw = _matmul(sv["uc"].transpose(0, 2, 1), ds, name="s5_bwd_w")
    dgt_tab = _toeplitz_fold(dkv[:, :S5_CH * S5_CHUNK])
    s5_grads = sv["tab_vjp"]((dgt_tab, dtw, dkv[:, S5_CH * S5_CHUNK:], da1, da2))
    du_scan = _from_chunks(duc, t)

    dzv, dzb, dzc, dconv_w, dconv_b = _conv_bwd(z, sv["conv_w"], w["conv_b"].reshape(1, -1), dys[1], n_ctx)

    dq_na, dkp, dvp, dkc, dvc, dbias, _ = _attn_bwd(*sv["na_in"], dys[2], mode="na", n_ctx=n_ctx)
    dk_na, dv_na = _window_fold(dkp, dkc, n_ctx), _window_fold(dvp, dvc, n_ctx)
    dq_g, dkp, dvp, dkc, dvc, _, dsink = _attn_bwd(*sv["gqa_in"], dys[3], mode="gqa", n_ctx=n_ctx)
    dk_g, dv_g = _window_fold(dkp, dkc, n_ctx), _window_fold(dvp, dvc, n_ctx)
    pb = _prep_bwd(z, sv["gains"], rope, (dq_na, dk_na, dv_na, dq_g, dk_g, dv_g), du_skip, du_scan)
    dz_naq, dz_nak, dz_nav, dz_gq, dz_gk, dz_gv, dz_u, dg_naq, dg_nak, dg_gq, dg_gk = pb

    dz = jnp.concatenate([dz_u, dgt[0], dzv, dzb, dzc, dgt[1], dz_naq, dz_nak, dz_nav, dgt[2], dz_gq, dz_gk, dz_gv,
                          jnp.zeros((t, OFF["gqa_gate"] - OFF["pad"]), BF16), dgt[3], *dmg], axis=1)
    dh = _matmul(dz, w["w_in"], trans_b=True, name="proj_bwd_x")
    dw_in = _matmul(sv["h"].T, dz, name="proj_bwd_w")
    dxt, dnorm_g, dshift, dscale = _modnorm_bwd(sv["xt"], w["norm_g"].reshape(1, -1), mod4, dh, dxt_new, n_ctx)
    dmod = jnp.concatenate([dshift, dscale, dgate], axis=1).reshape(2, 3 * D_MODEL)
    dcc, dw_ada, db_ada = _adaln_bwd(cc, w["w_ada"], jnp.pad(dmod, ((0, 6), (0, 0))))

    (drel,) = sv["bias_vjp"](dbias)
    grads = dict(
        norm_g=dnorm_g[0], w_ada=dw_ada, b_ada=db_ada[0], w_in=dw_in,
        s5_a_re=s5_grads[0], s5_a_im=s5_grads[1], s5_log_dt=s5_grads[2], s5_b_re=s5_grads[3], s5_b_im=s5_grads[4],
        s5_c_re=s5_grads[5], s5_c_im=s5_grads[6], s5_d=dd.reshape(S5_GROUPS, S5_CH), s5_w_glu=dw_glu,
        conv_w=dconv_w, conv_b=dconv_b[0],
        na_q_g=dg_naq.reshape(8, HEAD_DIM).sum(0), na_k_g=dg_nak.reshape(8, HEAD_DIM).sum(0), na_rel_bias=drel,
        gqa_q_g=dg_gq.reshape(8, HEAD_DIM).sum(0), gqa_k_g=dg_gk.reshape(2, HEAD_DIM).sum(0),
        gqa_sink=dsink[:, :2, :].sum(-1).reshape(8), w_br=dw_br, w_out=dw_out)
    return dxt, dcc, grads


SHARDED = ("w_ada", "w_in", "s5_w_glu", "conv_w", "w_br", "w_out")
REPLICATED = ("norm_g", "b_ada", "s5_a_re", "s5_a_im", "s5_log_dt", "s5_b_re", "s5_b_im", "s5_c_re", "s5_c_im",
              "s5_d", "conv_b", "na_q_g", "na_k_g", "na_rel_bias", "gqa_q_g", "gqa_k_g", "gqa_sink")
WEIGHTS = ("c_ctx", "norm_g", "w_ada", "b_ada", "w_in", "s5_a_re", "s5_a_im", "s5_log_dt", "s5_b_re", "s5_b_im",
           "s5_c_re", "s5_c_im", "s5_d", "s5_w_glu", "conv_w", "conv_b", "na_q_g", "na_k_g", "na_rel_bias",
           "gqa_q_g", "gqa_k_g", "gqa_sink", "w_br", "w_out")


def _pack(pieces, row_multiple, dtype):
    flat = jnp.concatenate([p.reshape(-1).astype(dtype) for p in pieces])
    rows = -(-flat.shape[0] // PACK_W)
    rows = -(-rows // row_multiple) * row_multiple
    return jnp.pad(flat, (0, rows * PACK_W - flat.shape[0])).reshape(rows, PACK_W)


def _unpack(buf, shapes):
    flat = buf.reshape(-1)
    out, pos = [], 0
    for shp in shapes:
        size = int(np.prod(shp))
        out.append(flat[pos:pos + size].reshape(shp))
        pos += size
    return out


def _local_step(x, ctx, target, c_vec, c_ctx, layers):
    depth = len(layers)
    n_ctx, n_lat = ctx.shape[0], x.shape[0]
    cc = jnp.zeros((8, D_MODEL), F32).at[0].set(c_ctx).at[1].set(c_vec)
    rope = _rope_tables(n_ctx, n_lat)
    xt = jnp.concatenate([ctx, x], axis=0)
    saved = []
    for l in range(depth):
        xt, sv = _layer_fwd(xt, cc, layers[l], rope, n_ctx)
        saved.append(sv)
    loss_tile, dxt = _loss_head(xt, target, n_ctx)
    grads = [None] * depth
    dc_ctx = jnp.zeros((D_MODEL,), F32)
    for l in reversed(range(depth)):
        dxt, dcc, grads[l] = _layer_bwd(dxt, saved[l], cc, layers[l], rope, n_ctx)
        dc_ctx = dc_ctx + dcc[0]
    return loss_tile[0, 0], dxt[n_ctx:][None], dc_ctx, grads


def kernel(x, c, ctx, c_ctx, norm_g, w_ada, b_ada, w_in, s5_a_re, s5_a_im, s5_log_dt, s5_b_re, s5_b_im,
           s5_c_re, s5_c_im, s5_d, s5_w_glu, conv_w, conv_b, na_q_g, na_k_g, na_rel_bias, gqa_q_g,
           gqa_k_g, gqa_sink, w_br, w_out, loss_target, m_c_ctx, m_norm_g, m_w_ada, m_b_ada, m_w_in,
           m_s5_a_re, m_s5_a_im, m_s5_log_dt, m_s5_b_re, m_s5_b_im, m_s5_c_re, m_s5_c_im, m_s5_d,
           m_s5_w_glu, m_conv_w, m_conv_b, m_na_q_g, m_na_k_g, m_na_rel_bias, m_gqa_q_g, m_gqa_k_g,
           m_gqa_sink, m_w_br, m_w_out, v_c_ctx, v_norm_g, v_w_ada, v_b_ada, v_w_in, v_s5_a_re,
           v_s5_a_im, v_s5_log_dt, v_s5_b_re, v_s5_b_im, v_s5_c_re, v_s5_c_im, v_s5_d, v_s5_w_glu,
           v_conv_w, v_conv_b, v_na_q_g, v_na_k_g, v_na_rel_bias, v_gqa_q_g, v_gqa_k_g, v_gqa_sink,
           v_w_br, v_w_out):
    a = dict(locals())
    depth = a["norm_g"].shape[0]
    x, ctx, target = a["x"][0], a["ctx"][0], a["loss_target"][0]
    n_ctx, n_lat = ctx.shape[0], x.shape[0]
    assert n_ctx % ATT_BLK == 0 and n_lat % (4 * GRID_W) == 0 and n_lat // GRID_W >= NA_ROWS

    cast = lambda n: a[n].astype(BF16)
    conv8 = jnp.pad(a["conv_w"], ((0, 0), (0, 5), (0, 0)))
    g_ada, g_in, g_glu, g_br, g_out, g_conv = _gather_weights(
        cast("w_ada"), cast("w_in"), cast("s5_w_glu"), cast("w_br"), cast("w_out"), conv8)
    zpad = jnp.zeros((D_MODEL, OFF["gqa_gate"] - OFF["pad"]), BF16)
    split = OFF["pad"] - 2 * N_SHARD_IN
    layers = []
    for l in range(depth):
        w = {n: a[n][l] for n in REPLICATED}
        w.update(w_ada=g_ada[l], s5_w_glu=g_glu[l], w_br=g_br[l], w_out=g_out[l], conv_w=g_conv[l])
        w["w_in"] = jnp.concatenate([g_in[0, l], g_in[1, l], g_in[2, l][:, :split], zpad, g_in[2, l][:, split:],
                                     g_in[3, l]], axis=1)
        layers.append(w)

    loss_local, grad_x, dc_ctx, grads = _local_step(x, ctx, target, a["c"][0], a["c_ctx"], layers)
    loss = lax.psum(loss_local, ("x", "y", "c"))

    chip = 2 * lax.axis_index("x") + lax.axis_index("y")
    take = [functools.partial(lambda win, pc: jnp.concatenate([win[:, lo:hi] for lo, hi in pc], axis=1), pc=pc)
            for pc in WIN_PIECES]

    def small_pack(values, c_ctx_value, l):
        pieces = [values[n] for n in REPLICATED]
        pieces.append(c_ctx_value if l == 0 else jnp.zeros((D_MODEL,), F32))
        return _pack(pieces, 8 * SUM_STEPS, F32)

    mine, theirs = [], []
    for l in range(depth):
        g = grads[l]
        small = small_pack(g, dc_ctx, l)
        recv = _scatter_grads(g["w_ada"], g["w_in"], g["s5_w_glu"], g["w_br"], g["w_out"], g["conv_w"], small)
        part = list(_sum_chips([r.reshape(4, -1, r.shape[-1]) for r in recv]))
        part[1] = lax.switch(chip, take, part[1])
        mine.append(part)
        theirs.append(_core_swap(part))

    families = ("w_ada", "w_in", "s5_w_glu", "w_br", "w_out", "conv_w")
    out = {}
    for k, n in enumerate(families):
        p, q = jnp.stack([m[k] for m in mine]), jnp.stack([t[k] for t in theirs])
        if n == "conv_w":
            p, q = p[:, :3], q[:, :3]
        as3d = lambda arr: arr.reshape(depth, -1, arr.shape[-1])
        res = _adamw(p, q, as3d(a[n]), as3d(a["m_" + n]), as3d(a["v_" + n]), "adamw_" + n)
        out[n] = [r.reshape(a[n].shape) for r in res]
    p, q = jnp.stack([m[6] for m in mine]), jnp.stack([t[6] for t in theirs])
    packs = [jnp.stack([small_pack({n: a[pre + n][l] for n in REPLICATED}, a[pre + "c_ctx"], l)
                        for l in range(depth)]) for pre in ("", "m_", "v_")]
    res = _adamw(p, q, *packs, "adamw_small")
    shapes = [a[n].shape[1:] for n in REPLICATED] + [a["c_ctx"].shape]
    per_layer = [[_unpack(r[l], shapes) for l in range(depth)] for r in res]
    for j, n in enumerate(REPLICATED):
        out[n] = [jnp.stack([per_layer[key][l][j] for l in range(depth)]) for key in range(4)]
    out["c_ctx"] = [per_layer[key][0][-1] for key in range(4)]
    results = [loss, grad_x]
    for key in range(4):
        results += [out[n][key] for n in WEIGHTS]
    return tuple(results)
```

```python
import functools

import numpy as np
import jax
import jax.numpy as jnp
from jax import lax
from jax.experimental import pallas as pl
from jax.experimental.pallas import tpu as pltpu

F32 = jnp.float32
BF16 = jnp.bfloat16

D_MODEL = 1024
MIX_W = 512
GRID_W = 64
HEAD_DIM = 64
N_HEADS = 8
S5_GROUPS = 32
S5_CH = 16
S5_CHUNK = 32
NA_ROWS = 8
NA_COLS = 16
WINDOW = 128
ROPE_BASE = 10000.0
ROPE_PAIRS = 16
EPS = 1e-6
NEG_INF = -1e30
ATT_BLK = 256
TOK = 256
VMEM_LIMIT = 56 * 1024 * 1024

ADAM_LR, ADAM_B1, ADAM_B2, ADAM_EPS, ADAM_WD, ADAM_STEP = 0.001, 0.9, 0.999, 1e-8, 0.01, 10

OFF = dict(s5_u=0, s5_gate=512, conv_v=1024, conv_b=1536, conv_c=2048, conv_gate=2560,
           na_q=3072, na_k=3584, na_v=4096, na_gate=4608, gqa_q=5120, gqa_k=5632, gqa_v=5760,
           pad=5888, gqa_gate=6144, merge_s5=6656, merge_conv=7680, merge_na=8704, merge_gqa=9728)
N_Z = 10752
GATE_OFFS = (OFF["s5_gate"], OFF["conv_gate"], OFF["na_gate"], OFF["gqa_gate"])
MERGE_OFFS = (OFF["merge_s5"], OFF["merge_conv"], OFF["merge_na"], OFF["merge_gqa"])


def _cparams(sem):
    return pltpu.CompilerParams(dimension_semantics=sem, vmem_limit_bytes=VMEM_LIMIT)


def _dot(a, b, ca, cb):
    return lax.dot_general(a.astype(BF16), b.astype(BF16), (((ca,), (cb,)), ((), ())),
                           preferred_element_type=F32)


def _dot_tn(a, b):
    return _dot(a.astype(F32).T, b, 1, 0)


@jax.custom_vjp
def mm(a, b):
    return _dot(a, b, 1, 0)


@jax.custom_vjp
def mm_nt(a, b):
    return _dot(a, b, 1, 1)


@jax.custom_vjp
def mm_tn(a, b):
    return _dot_tn(a, b)


mm.defvjp(lambda a, b: (mm(a, b), (a, b)), lambda r, g: (mm_nt(g, r[1]), mm_tn(r[0], g)))
mm_nt.defvjp(lambda a, b: (mm_nt(a, b), (a, b)), lambda r, g: (mm(g, r[1]), mm_tn(g, r[0])))
mm_tn.defvjp(lambda a, b: (mm_tn(a, b), (a, b)), lambda r, g: (mm_nt(r[1], g), mm(r[0], g)))


@functools.partial(jax.custom_vjp, nondiff_argnums=(1,))
def lane_roll(x, shift):
    return pltpu.roll(x, shift, 1)


lane_roll.defvjp(lambda x, shift: (lane_roll(x, shift), None),
                 lambda shift, _, g: (lane_roll(g, (g.shape[1] - shift) % g.shape[1]),))


def _silu(x):
    return x * jax.nn.sigmoid(x)


def _dsilu(x):
    s = jax.nn.sigmoid(x)
    return s * (1.0 + x * (1.0 - s))


def _pick(n, prefs):
    for p in prefs:
        if n % p == 0:
            return p
    return n


def _matmul(a, b, *, trans_b=False, out_dtype=F32, tm=None, tn=None, tk=None, name):
    squeeze = a.ndim == 2
    if squeeze:
        a, b = a[None], b[None]
    nb, m, k = a.shape
    n = b.shape[1] if trans_b else b.shape[2]
    tm = tm or _pick(m, (1280, 1024, 640, 512, 256, 128))
    tn = tn or _pick(n, (1536, 1024, 512, 256, 128))
    tk = tk or _pick(k, (1536, 1280, 1024, 768, 640, 512, 256, 128))
    nk = k // tk

    def body(a_ref, b_ref, o_ref, *scr):
        part = _dot(a_ref[...], b_ref[...], 1, 1 if trans_b else 0)
        if nk == 1:
            o_ref[...] = part.astype(out_dtype)
        else:
            acc = scr[0]
            kk = pl.program_id(3)

            @pl.when(kk == 0)
            def _():
                acc[...] = part

            @pl.when(kk > 0)
            def _():
                acc[...] += part

            @pl.when(kk == nk - 1)
            def _():
                o_ref[...] = acc[...].astype(out_dtype)

    if trans_b:
        b_spec = pl.BlockSpec((None, tn, tk), lambda bb, i, j, kk: (bb, j, kk))
    else:
        b_spec = pl.BlockSpec((None, tk, tn), lambda bb, i, j, kk: (bb, kk, j))
    out = pl.pallas_call(
        body, name=name,
        grid=(nb, m // tm, n // tn, nk),
        in_specs=[pl.BlockSpec((None, tm, tk), lambda bb, i, j, kk: (bb, i, kk)), b_spec],
        out_specs=pl.BlockSpec((None, tm, tn), lambda bb, i, j, kk: (bb, i, j)),
        out_shape=jax.ShapeDtypeStruct((nb, m, n), out_dtype),
        scratch_shapes=[] if nk == 1 else [pltpu.VMEM((tm, tn), F32)],
        compiler_params=_cparams(("parallel", "parallel", "parallel", "arbitrary")),
    )(a, b)
    return out[0] if squeeze else out


def _adaln_fn(cc, w, b):
    return mm(_silu(cc), w) + b


def _adaln_fwd(cc, w_ada, b_ada):
    def body(cc_ref, w_ref, b_ref, o_ref):
        o_ref[...] = _adaln_fn(cc_ref[...], w_ref[...], b_ref[...])

    return pl.pallas_call(
        body, name="adaln_fwd", out_shape=jax.ShapeDtypeStruct((8, 3 * D_MODEL), F32),
        compiler_params=pltpu.CompilerParams(vmem_limit_bytes=VMEM_LIMIT),
    )(cc, w_ada, b_ada)


def _adaln_bwd(cc, w_ada, dmod):
    def body(cc_ref, w_ref, g_ref, dcc_ref, dw_ref, db_ref):
        cc_v, g = cc_ref[...], g_ref[...]
        dw_ref[...] = mm_tn(_silu(cc_v), g).astype(BF16)
        db_ref[...] = jnp.sum(g, axis=0, keepdims=True)
        dcc_ref[...] = mm_nt(g, w_ref[...]) * _dsilu(cc_v)

    return pl.pallas_call(
        body, name="adaln_bwd",
        out_shape=(jax.ShapeDtypeStruct((8, D_MODEL), F32),
                   jax.ShapeDtypeStruct((D_MODEL, 3 * D_MODEL), BF16),
                   jax.ShapeDtypeStruct((1, 3 * D_MODEL), F32)),
        compiler_params=pltpu.CompilerParams(vmem_limit_bytes=VMEM_LIMIT),
    )(cc, w_ada, dmod)


def _seg_spec(which, n_ctx_tiles):
    return pl.BlockSpec((None, None, 1, D_MODEL),
                        lambda i: (jnp.where(i < n_ctx_tiles, 0, 1), which, 0, 0))


def _row_spec(width, col_block=0, tile=TOK):
    return pl.BlockSpec((tile, width), lambda i: (i, col_block))


def _const_spec(shape):
    zeros = (0,) * len(shape)
    return pl.BlockSpec(shape, lambda i: zeros)


def _modnorm_fn(x, g, shift, scale):
    y = x * lax.rsqrt(jnp.mean(x * x, axis=-1, keepdims=True) + EPS)
    return (y * g) * (1.0 + scale) + shift


def _modnorm_fwd(xt, g, mod4, n_ctx):
    t = xt.shape[0]
    nct = n_ctx // TOK

    def body(x_ref, g_ref, sh_ref, sc_ref, o_ref):
        o_ref[...] = _modnorm_fn(x_ref[...], g_ref[...], sh_ref[...], sc_ref[...]).astype(BF16)

    return pl.pallas_call(
        body, name="modnorm_fwd", grid=(t // TOK,),
        in_specs=[_row_spec(D_MODEL), _const_spec((1, D_MODEL)), _seg_spec(0, nct), _seg_spec(1, nct)],
        out_specs=_row_spec(D_MODEL),
        out_shape=jax.ShapeDtypeStruct((t, D_MODEL), BF16),
        compiler_params=_cparams(("parallel",)),
    )(xt, g, mod4, mod4)


def _modnorm_bwd(xt, g, mod4, dh, dres, n_ctx):
    t = xt.shape[0]
    nct = n_ctx // TOK

    def body(x_ref, g_ref, sh_ref, sc_ref, dh_ref, dres_ref, dx_ref, dg_ref, dsh_ref, dsc_ref):
        i = pl.program_id(0)
        _, vjp = jax.vjp(_modnorm_fn, x_ref[...], g_ref[...], sh_ref[...], sc_ref[...])
        dx, dg, dsh, dsc = vjp(dh_ref[...])
        dx_ref[...] = dx + dres_ref[...]

        @pl.when(i == 0)
        def _():
            dg_ref[...] = jnp.zeros_like(dg_ref)

        dg_ref[...] += dg
        first = jnp.logical_or(i == 0, i == nct)

        @pl.when(first)
        def _():
            dsh_ref[...] = dsh
            dsc_ref[...] = dsc

        @pl.when(jnp.logical_not(first))
        def _():
            dsh_ref[...] += dsh
            dsc_ref[...] += dsc

    seg_out = lambda which: pl.BlockSpec((None, None, 1, D_MODEL),
                                         lambda i: (jnp.where(i < nct, 0, 1), which, 0, 0))
    dx, dg, dss, dss2 = pl.pallas_call(
        body, name="modnorm_bwd", grid=(t // TOK,),
        in_specs=[_row_spec(D_MODEL), _const_spec((1, D_MODEL)), _seg_spec(0, nct), _seg_spec(1, nct),
                  _row_spec(D_MODEL), _row_spec(D_MODEL)],
        out_specs=[_row_spec(D_MODEL), _const_spec((1, D_MODEL)), seg_out(0), seg_out(0)],
        out_shape=(jax.ShapeDtypeStruct((t, D_MODEL), F32), jax.ShapeDtypeStruct((1, D_MODEL), F32),
                   jax.ShapeDtypeStruct((2, 1, 1, D_MODEL), F32), jax.ShapeDtypeStruct((2, 1, 1, D_MODEL), F32)),
        compiler_params=_cparams(("arbitrary",)),
    )(xt, g, mod4, mod4, dh, dres)
    return dx, dg, dss, dss2


def _group_mean_sq(x, gs):
    x2 = x * x
    hi = x2.astype(BF16).astype(F32)
    return mm(hi, gs) + mm(x2 - hi, gs)


def _head_norm(x, g, gs):
    return (x * lax.rsqrt(_group_mean_sq(x, gs) + EPS)) * g


def _rope(x, cos, sin_signed):
    lane = lax.broadcasted_iota(jnp.int32, (1, 128), 1)
    first_half = jnp.bitwise_and(lane, 63) < 32
    cols = []
    for c in range(x.shape[1] // 128):
        xb = x[:, 128 * c:128 * (c + 1)]
        partner = jnp.where(first_half, lane_roll(xb, 96), lane_roll(xb, 32))
        cols.append(xb * cos + partner * sin_signed)
    return cols[0] if len(cols) == 1 else jnp.concatenate(cols, axis=1)


def _prep_fn(zq_na, zk_na, zv_na, zq_g, zk_g, zv_g, g_naq, g_nak, g_gq, g_gk, cos, sin_signed, gs512, gs128, expand):
    q_na = _head_norm(zq_na, g_naq, gs512)
    k_na = _head_norm(zk_na, g_nak, gs512)
    q_g = _rope(_head_norm(zq_g, g_gq, gs512), cos, sin_signed)
    k_g = _rope(_head_norm(zk_g, g_gk, gs128), cos, sin_signed)
    return q_na, k_na, zv_na, q_g, mm(k_g, expand), mm(zv_g, expand)


def _prep_consts():
    gid = np.arange(512) // 64
    gs512 = (gid[:, None] == gid[None, :]).astype(np.float32) / 64.0
    expand = np.zeros((128, 512), np.float32)
    for h in range(N_HEADS):
        for j in range(64):
            expand[64 * (h // 4) + j, 64 * h + j] = 1.0
    return jnp.asarray(gs512), jnp.asarray(gs512[:128, :128]), jnp.asarray(expand)


def _prep_in_specs():
    blk = lambda off, w: _row_spec(w, off // w)
    return [blk(OFF["na_q"], 512), blk(OFF["na_k"], 512), blk(OFF["na_v"], 512), blk(OFF["gqa_q"], 512),
            blk(OFF["gqa_k"], 128), blk(OFF["gqa_v"], 128),
            _const_spec((1, 512)), _const_spec((1, 512)), _const_spec((1, 512)), _const_spec((1, 128)),
            _row_spec(128), _row_spec(128),
            _const_spec((512, 512)), _const_spec((128, 128)), _const_spec((128, 512))]


def _prep_fwd(z, gains, rope_tabs):
    t = z.shape[0]
    consts = _prep_consts()

    def body(*refs):
        ins, outs = refs[:15], refs[15:]
        res = _prep_fn(*[r[...] for r in ins])
        for o_ref, v in zip(outs, res):
            o_ref[...] = v.astype(BF16)

    return pl.pallas_call(
        body, name="prep_fwd", grid=(t // TOK,),
        in_specs=_prep_in_specs(),
        out_specs=[_row_spec(512)] * 6,
        out_shape=tuple(jax.ShapeDtypeStruct((t, 512), BF16) for _ in range(6)),
        compiler_params=_cparams(("parallel",)),
    )(z, z, z, z, z, z, *gains, *rope_tabs, *consts)


def _prep_bwd(z, gains, rope_tabs, cots, du_a, du_b):
    t = z.shape[0]
    consts = _prep_consts()

    def body(*refs):
        ins, cot, (dua_ref, dub_ref), outs = refs[:15], refs[15:21], refs[21:23], refs[23:]
        i = pl.program_id(0)
        vals = [r[...] for r in ins]
        _, vjp = jax.vjp(lambda *a: _prep_fn(*a, *vals[10:]), *vals[:10])
        grads = vjp(tuple(c[...] for c in cot))
        for o_ref, v in zip(outs[:6], grads[:6]):
            o_ref[...] = v.astype(BF16)
        outs[6][...] = (dua_ref[...] + dub_ref[...]).astype(BF16)

        @pl.when(i == 0)
        def _():
            for o_ref in outs[7:]:
                o_ref[...] = jnp.zeros_like(o_ref)

        for o_ref, v in zip(outs[7:], grads[6:10]):
            o_ref[...] += v

    return pl.pallas_call(
        body, name="prep_bwd", grid=(t // TOK,),
        in_specs=_prep_in_specs() + [_row_spec(512)] * 8,
        out_specs=[_row_spec(512)] * 4 + [_row_spec(128)] * 2 + [_row_spec(512)]
        + [_const_spec((1, 512))] * 3 + [_const_spec((1, 128))],
        out_shape=tuple([jax.ShapeDtypeStruct((t, 512), BF16)] * 4 + [jax.ShapeDtypeStruct((t, 128), BF16)] * 2
                        + [jax.ShapeDtypeStruct((t, 512), BF16)]
                        + [jax.ShapeDtypeStruct((1, 512), F32)] * 3 + [jax.ShapeDtypeStruct((1, 128), F32)]),
        compiler_params=_cparams(("arbitrary",)),
    )(z, z, z, z, z, z, *gains, *rope_tabs, *consts, *cots, du_a, du_b)


def _s5post_fn(ys, u, d, w_glu):
    y = jax.nn.gelu(ys + d * u)
    return y * jax.nn.sigmoid(mm(y, w_glu))


def _s5post_fwd(ys, z, d, w_glu):
    t = z.shape[0]

    def body(ys_ref, u_ref, d_ref, w_ref, o_ref):
        o_ref[...] = _s5post_fn(ys_ref[...], u_ref[...], d_ref[...], w_ref[...])

    return pl.pallas_call(
        body, name="s5post_fwd", grid=(t // TOK,),
        in_specs=[_row_spec(512), _row_spec(512, OFF["s5_u"] // 512),
                  _const_spec((1, 512)), _const_spec((512, 512))],
        out_specs=_row_spec(512), out_shape=jax.ShapeDtypeStruct((t, 512), F32),
        compiler_params=_cparams(("parallel",)),
    )(ys, z, d, w_glu)


def _s5post_bwd(ys, z, d, w_glu, dy):
    t = z.shape[0]

    def body(ys_ref, u_ref, d_ref, w_ref, dy_ref, dpre_ref, du_ref, dd_ref, dw_ref):
        i = pl.program_id(0)
        _, vjp = jax.vjp(_s5post_fn, ys_ref[...], u_ref[...], d_ref[...], w_ref[...].astype(F32))
        dys, du, dd, dw = vjp(dy_ref[...])
        dpre_ref[...] = dys.astype(BF16)
        du_ref[...] = du

        @pl.when(i == 0)
        def _():
            dd_ref[...] = jnp.zeros_like(dd_ref)
            dw_ref[...] = jnp.zeros_like(dw_ref)

        dd_ref[...] += dd
        dw_ref[...] += dw

    return pl.pallas_call(
        body, name="s5post_bwd", grid=(t // TOK,),
        in_specs=[_row_spec(512), _row_spec(512, OFF["s5_u"] // 512),
                  _const_spec((1, 512)), _const_spec((512, 512)), _row_spec(512)],
        out_specs=[_row_spec(512), _row_spec(512), _const_spec((1, 512)), _const_spec((512, 512))],
        out_shape=(jax.ShapeDtypeStruct((t, 512), BF16), jax.ShapeDtypeStruct((t, 512), F32),
                   jax.ShapeDtypeStruct((1, 512), F32), jax.ShapeDtypeStruct((512, 512), F32)),
        compiler_params=_cparams(("arbitrary",)),
    )(ys, z, d, w_glu, dy)


def _halo_specs(col_block, t):
    last = t // 8 - 1
    prev = pl.BlockSpec((8, 512), lambda i: (jnp.maximum(i * (TOK // 8) - 1, 0), col_block))
    nxt = pl.BlockSpec((8, 512), lambda i: (jnp.minimum((i + 1) * (TOK // 8), last), col_block))
    return [_row_spec(512, col_block), prev, nxt]


def _shifted(cur, prev_row, next_row, tok0, n_ctx, t_total):
    row = lax.broadcasted_iota(jnp.int32, (TOK, 1), 0)
    tpos = row + tok0
    down = jnp.where(row == 0, prev_row, pltpu.roll(cur, 1, 0))
    down = jnp.where(jnp.logical_or(tpos == 0, tpos == n_ctx), 0.0, down)
    up = jnp.where(row == TOK - 1, next_row, pltpu.roll(cur, TOK - 1, 0))
    up = jnp.where(jnp.logical_or(tpos == n_ctx - 1, tpos == t_total - 1), 0.0, up)
    return down, up


def _conv_fwd(z, conv_w, conv_b, n_ctx):
    t = z.shape[0]

    def body(v_ref, vp_ref, vn_ref, c_ref, cp_ref, cn_ref, b_ref, w_ref, cb_ref, o_ref):
        tok0 = pl.program_id(0) * TOK
        zz = v_ref[...] * c_ref[...]
        zz_m1, zz_p1 = _shifted(zz, vp_ref[7:8, :] * cp_ref[7:8, :], vn_ref[0:1, :] * cn_ref[0:1, :], tok0, n_ctx, t)
        s = cb_ref[...] + zz_m1 * w_ref[0:1, :] + zz * w_ref[1:2, :] + zz_p1 * w_ref[2:3, :]
        o_ref[...] = b_ref[...] * s

    return pl.pallas_call(
        body, name="conv_fwd", grid=(t // TOK,),
        in_specs=_halo_specs(OFF["conv_v"] // 512, t) + _halo_specs(OFF["conv_c"] // 512, t)
        + [_row_spec(512, OFF["conv_b"] // 512), _const_spec((8, 512)), _const_spec((1, 512))],
        out_specs=_row_spec(512), out_shape=jax.ShapeDtypeStruct((t, 512), F32),
        compiler_params=_cparams(("parallel",)),
    )(z, z, z, z, z, z, z, conv_w, conv_b)


def _conv_bwd(z, conv_w, conv_b, dy, n_ctx):
    t = z.shape[0]

    def body(v_ref, vp_ref, vn_ref, c_ref, cp_ref, cn_ref, b_ref, bp_ref, bn_ref, dy_ref, dyp_ref, dyn_ref,
             w_ref, cb_ref, dv_ref, db_ref, dc_ref, dw_ref, dcb_ref):
        i = pl.program_id(0)
        tok0 = i * TOK
        v, c, b, dy_v = v_ref[...], c_ref[...], b_ref[...], dy_ref[...]
        w0, w1, w2 = w_ref[0:1, :], w_ref[1:2, :], w_ref[2:3, :]
        zz = v * c
        zz_m1, zz_p1 = _shifted(zz, vp_ref[7:8, :] * cp_ref[7:8, :], vn_ref[0:1, :] * cn_ref[0:1, :], tok0, n_ctx, t)
        s = cb_ref[...] + zz_m1 * w0 + zz * w1 + zz_p1 * w2
        ds = dy_v * b
        ds_m1, ds_p1 = _shifted(ds, dyp_ref[7:8, :] * bp_ref[7:8, :], dyn_ref[0:1, :] * bn_ref[0:1, :], tok0, n_ctx, t)
        dzz = ds_p1 * w0 + ds * w1 + ds_m1 * w2
        db_ref[...] = (dy_v * s).astype(BF16)
        dv_ref[...] = (dzz * c).astype(BF16)
        dc_ref[...] = (dzz * v).astype(BF16)

        @pl.when(i == 0)
        def _():
            dw_ref[...] = jnp.zeros_like(dw_ref)
            dcb_ref[...] = jnp.zeros_like(dcb_ref)

        rsum = lambda a: jnp.sum(a, axis=0, keepdims=True)
        dw_ref[0:1, :] += rsum(ds * zz_m1)
        dw_ref[1:2, :] += rsum(ds * zz)
        dw_ref[2:3, :] += rsum(ds * zz_p1)
        dcb_ref[...] += rsum(ds)

    return pl.pallas_call(
        body, name="conv_bwd", grid=(t // TOK,),
        in_specs=_halo_specs(OFF["conv_v"] // 512, t) + _halo_specs(OFF["conv_c"] // 512, t)
        + _halo_specs(OFF["conv_b"] // 512, t) + _halo_specs(0, t) + [_const_spec((8, 512)), _const_spec((1, 512))],
        out_specs=[_row_spec(512)] * 3 + [_const_spec((8, 512)), _const_spec((1, 512))],
        out_shape=tuple([jax.ShapeDtypeStruct((t, 512), BF16)] * 3
                        + [jax.ShapeDtypeStruct((8, 512), F32), jax.ShapeDtypeStruct((1, 512), F32)]),
        compiler_params=_cparams(("arbitrary",)),
    )(z, z, z, z, z, z, z, z, z, dy, dy, dy, conv_w, conv_b)


def _merge_col_specs(tile):
    specs = []
    for off in MERGE_OFFS:
        specs.append(pl.BlockSpec((tile, 512), functools.partial(lambda i, cb: (i, cb), cb=off // 512)))
        specs.append(pl.BlockSpec((tile, 512), functools.partial(lambda i, cb: (i, cb), cb=off // 512 + 1)))
    return specs


def _merge_fwd(xt, ys, z, mod4, w_br, w_out, n_ctx):
    t = xt.shape[0]
    nct = n_ctx // TOK

    def body(x_ref, *refs):
        y_refs, gt_refs, mg_refs = refs[0:4], refs[4:8], refs[8:16]
        gate_ref, wbr_ref, wout_ref, o_ref = refs[16:20]
        acc_lo = acc_hi = None
        for k in range(4):
            gated = y_refs[k][...] * _silu(gt_refs[k][...])
            proj = mm(gated, wbr_ref[k])
            lo = jax.nn.sigmoid(mg_refs[2 * k][...]) * proj[:, :512]
            hi = jax.nn.sigmoid(mg_refs[2 * k + 1][...]) * proj[:, 512:]
            acc_lo = lo if acc_lo is None else acc_lo + lo
            acc_hi = hi if acc_hi is None else acc_hi + hi
        acc = jnp.concatenate([acc_lo, acc_hi], axis=1)
        o_ref[...] = x_ref[...] + gate_ref[...] * mm(acc, wout_ref[...])

    gate_specs = [pl.BlockSpec((TOK, 512), functools.partial(lambda i, cb: (i, cb), cb=o // 512)) for o in GATE_OFFS]
    return pl.pallas_call(
        body, name="merge_fwd", grid=(t // TOK,),
        in_specs=[_row_spec(D_MODEL)] + [_row_spec(512)] * 4 + gate_specs + _merge_col_specs(TOK)
        + [_seg_spec(2, nct), _const_spec((4, 512, 1024)), _const_spec((1024, 1024))],
        out_specs=_row_spec(D_MODEL), out_shape=jax.ShapeDtypeStruct((t, D_MODEL), F32),
        compiler_params=_cparams(("parallel",)),
    )(xt, *ys, z, z, z, z, z, z, z, z, z, z, z, z, mod4, w_br, w_out)


MERGE_BWD_TILE = 128


def _merge_bwd(g, ys, z, mod4, w_br, w_out, n_ctx):
    t = g.shape[0]
    tile = MERGE_BWD_TILE
    nct = n_ctx // tile
    nsteps = t // tile

    def body(g_ref, *refs):
        y_refs, gt_refs, mg_refs = refs[0:4], refs[4:8], refs[8:16]
        gate_ref, wbr_hbm, wout_hbm = refs[16:19]
        dy_refs, dgt_refs, dmg_refs = refs[19:23], refs[23:27], refs[27:31]
        dgate_ref, dwbr_hbm, dwout_hbm = refs[31:34]
        wbr_v, wout_v, dwbr_acc, dwout_acc = refs[34:38]
        i = pl.program_id(0)

        @pl.when(i == 0)
        def _():
            pltpu.sync_copy(wbr_hbm, wbr_v)
            pltpu.sync_copy(wout_hbm, wout_v)
            dwbr_acc[...] = jnp.zeros_like(dwbr_acc)
            dwout_acc[...] = jnp.zeros_like(dwout_acc)

        g_v, gate = g_ref[...], gate_ref[...]
        gated, proj, sig = [], [], []
        acc = None
        for k in range(4):
            gated.append(y_refs[k][...] * _silu(gt_refs[k][...]))
            proj.append(mm(gated[k], wbr_v[k]))
            sig.append(jax.nn.sigmoid(jnp.concatenate([mg_refs[2 * k][...], mg_refs[2 * k + 1][...]], axis=1)))
            contrib = sig[k] * proj[k]
            acc = contrib if acc is None else acc + contrib
        o = mm(acc, wout_v[...])
        dgate = jnp.sum(g_v * o, axis=0, keepdims=True)
        first = jnp.logical_or(i == 0, i == nct)

        @pl.when(first)
        def _():
            dgate_ref[...] = dgate

        @pl.when(jnp.logical_not(first))
        def _():
            dgate_ref[...] += dgate

        do = g_v * gate
        dwout_acc[...] += mm_tn(acc, do)
        dacc = mm_nt(do, wout_v[...])
        for k in range(4):
            dmg_refs[k][...] = (dacc * proj[k] * sig[k] * (1.0 - sig[k])).astype(BF16)
            dproj = dacc * sig[k]
            dwbr_acc[k] += mm_tn(gated[k], dproj)
            dgated = mm_nt(dproj, wbr_v[k])
            gt = gt_refs[k][...]
            dy_refs[k][...] = dgated * _silu(gt)
            dgt_refs[k][...] = (dgated * y_refs[k][...] * _dsilu(gt)).astype(BF16)

        @pl.when(i == nsteps - 1)
        def _():
            wbr_v[...] = dwbr_acc[...].astype(BF16)
            wout_v[...] = dwout_acc[...].astype(BF16)
            pltpu.sync_copy(wbr_v, dwbr_hbm)
            pltpu.sync_copy(wout_v, dwout_hbm)

    row = lambda w: _row_spec(w, 0, tile)
    gate_specs = [pl.BlockSpec((tile, 512), functools.partial(lambda i, cb: (i, cb), cb=o // 512)) for o in GATE_OFFS]
    anyspec = pl.BlockSpec(memory_space=pl.ANY)
    seg = pl.BlockSpec((None, None, 1, D_MODEL), lambda i: (jnp.where(i < nct, 0, 1), 2, 0, 0))
    seg_out = pl.BlockSpec((None, None, 1, D_MODEL), lambda i: (jnp.where(i < nct, 0, 1), 0, 0, 0))
    res = pl.pallas_call(
        body, name="merge_bwd", grid=(nsteps,),
        in_specs=[row(D_MODEL)] + [row(512)] * 4 + gate_specs + _merge_col_specs(tile) + [seg, anyspec, anyspec],
        out_specs=[row(512)] * 8 + [row(1024)] * 4 + [seg_out, anyspec, anyspec],
        out_shape=tuple([jax.ShapeDtypeStruct((t, 512), F32)] * 4 + [jax.ShapeDtypeStruct((t, 512), BF16)] * 4
                        + [jax.ShapeDtypeStruct((t, 1024), BF16)] * 4
                        + [jax.ShapeDtypeStruct((2, 1, 1, D_MODEL), F32),
                           jax.ShapeDtypeStruct((4, 512, 1024), BF16), jax.ShapeDtypeStruct((1024, 1024), BF16)]),
        scratch_shapes=[pltpu.VMEM((4, 512, 1024), BF16), pltpu.VMEM((1024, 1024), BF16),
                        pltpu.VMEM((4, 512, 1024), F32), pltpu.VMEM((1024, 1024), F32)],
        compiler_params=_cparams(("arbitrary",)),
    )(g, *ys, z, z, z, z, z, z, z, z, z, z, z, z, mod4, w_br, w_out)
    return res[0:4], res[4:8], res[8:12], res[12], res[13], res[14]


SCAN_ROWS = 16


def _scan_spec(k):
    return pl.BlockSpec((k, SCAN_ROWS, 128), lambda i: (0, i, 0))


def _scan_chunk(j, k, n_ctx_chunks, is_backward):
    backward = jnp.where(j < n_ctx_chunks, n_ctx_chunks - 1 - j, k - 1 - (j - n_ctx_chunks))
    return jnp.where(is_backward, backward, j)


def _scan_block_is_backward():
    return pl.program_id(0) >= 32 // SCAN_ROWS


def _s5_scan_fwd(s, a1, a2, k, n_ctx_chunks):
    kp = s.shape[0]

    def body(s_ref, a1_ref, a2_ref, hp_ref):
        a1_v, a2_v = a1_ref[...], a2_ref[...]
        hp_ref[...] = jnp.zeros_like(hp_ref)
        is_backward = _scan_block_is_backward()

        def step(j, h):
            c = _scan_chunk(j, k, n_ctx_chunks, is_backward)
            hp_ref[c] = h
            return a1_v * h + a2_v * pltpu.roll(h, 64, 1) + s_ref[c]

        lax.fori_loop(0, k, step, jnp.zeros((SCAN_ROWS, 128), F32))

    vec = pl.BlockSpec((SCAN_ROWS, 128), lambda i: (i, 0))
    return pl.pallas_call(
        body, name="s5_scan_fwd", grid=(64 // SCAN_ROWS,),
        in_specs=[_scan_spec(kp), vec, vec], out_specs=_scan_spec(kp),
        out_shape=jax.ShapeDtypeStruct(s.shape, F32),
        compiler_params=_cparams(("parallel",)),
    )(s, a1, a2)


def _s5_scan_bwd(dhp, hp, a1, a2, k, n_ctx_chunks):
    kp = hp.shape[0]

    def body(dhp_ref, hp_ref, a1_ref, a2_ref, ds_ref, da1_ref, da2_ref):
        a1_v, a2_v = a1_ref[...], a2_ref[...]
        ds_ref[...] = jnp.zeros_like(ds_ref)
        is_backward = _scan_block_is_backward()

        def step(j, carry):
            lam, d1, d2 = carry
            c = _scan_chunk(k - 1 - j, k, n_ctx_chunks, is_backward)
            ds_ref[c] = lam
            h = hp_ref[c]
            d1 = d1 + lam * h
            d2 = d2 + lam * pltpu.roll(h, 64, 1)
            lam = dhp_ref[c] + a1_v * lam + pltpu.roll(a2_v * lam, 64, 1)
            return lam, d1, d2

        zero = jnp.zeros((SCAN_ROWS, 128), F32)
        _, d1, d2 = lax.fori_loop(0, k, step, (zero, zero, zero))
        da1_ref[...] = d1
        da2_ref[...] = d2

    vec = pl.BlockSpec((SCAN_ROWS, 128), lambda i: (i, 0))
    return pl.pallas_call(
        body, name="s5_scan_bwd", grid=(64 // SCAN_ROWS,),
        in_specs=[_scan_spec(kp), _scan_spec(kp), vec, vec], out_specs=[_scan_spec(kp), vec, vec],
        out_shape=(jax.ShapeDtypeStruct(hp.shape, F32), jax.ShapeDtypeStruct((64, 128), F32),
                   jax.ShapeDtypeStruct((64, 128), F32)),
        compiler_params=_cparams(("parallel",)),
    )(dhp, hp, a1, a2)


@functools.lru_cache(maxsize=None)
def _window_patterns(mode, n_lat):
    nb = n_lat // ATT_BLK
    assert nb >= 3
    iq, ik = np.arange(ATT_BLK)[:, None], np.arange(3 * ATT_BLK)[None, :]

    def valid(ql):
        if mode == "na":
            r, qcol = 4 * ql + iq // GRID_W, iq % GRID_W
            kr, kcol = 4 * (ql - 1) + ik // GRID_W, ik % GRID_W
            rs = np.clip(r - NA_ROWS // 2, 0, n_lat // GRID_W - NA_ROWS)
            cs = np.clip(qcol - NA_COLS // 2, 0, GRID_W - NA_COLS)
            return (kr >= rs) & (kr < rs + NA_ROWS) & (kcol >= cs) & (kcol < cs + NA_COLS)
        tq, ts = ATT_BLK * ql + iq, ATT_BLK * (ql - 1) + ik
        return (np.abs(tq - ts) <= WINDOW) & (ts >= 0) & (ts < n_lat)

    interior = valid(1)
    assert all(np.array_equal(valid(ql), interior) for ql in range(1, nb - 1))
    return np.stack([valid(0), interior, valid(nb - 1), np.zeros_like(interior)])


def _pattern_of_block(qb, nqb):
    return jnp.where(qb == 0, 3, jnp.where(qb == 1, 0, jnp.where(qb == nqb - 1, 2, 1)))


def _attn_block(q, k3, v3, kc, vc, bias0, bias1, sink):
    lane = lax.broadcasted_iota(jnp.int32, (1, 128), 1)
    scale = HEAD_DIM ** -0.5
    outs = []
    for e, bias in enumerate((bias0, bias1)):
        in_head = (lane < 64) if e == 0 else (lane >= 64)
        qe = jnp.where(in_head, q, 0.0)
        s_lat = mm_nt(qe, k3) * scale + bias
        s_ctx = mm_nt(qe, kc) * scale
        mx = jnp.maximum(jnp.max(s_lat, axis=1, keepdims=True), jnp.max(s_ctx, axis=1, keepdims=True))
        if sink is not None:
            srow = lax.broadcasted_iota(jnp.int32, sink.shape, 0)
            sv = jnp.sum(jnp.where(srow == e, sink, 0.0), keepdims=True) * (1.0 / 128.0)
            mx = jnp.maximum(mx, sv)
        mx = lax.stop_gradient(mx)
        e_lat = jnp.exp(s_lat - mx)
        e_ctx = jnp.exp(s_ctx - mx)
        den = jnp.sum(e_lat, axis=1, keepdims=True) + jnp.sum(e_ctx, axis=1, keepdims=True)
        if sink is not None:
            den = den + jnp.exp(sv - mx)
        inv = 1.0 / den
        outs.append(mm(e_lat * inv, v3) + mm(e_ctx * inv, vc))
    return jnp.where(lane < 64, outs[0], outs[1])


def _attn_specs(n_ctx, nqb, per_head):
    def kwin(s):
        return pl.BlockSpec((ATT_BLK, 128), lambda hp, qb: (jnp.maximum(qb - 1, 0) + s, hp))

    q = pl.BlockSpec((ATT_BLK, 128), lambda hp, qb: (qb, hp))
    ctx = pl.BlockSpec((n_ctx, 128), lambda hp, qb: (0, hp))
    if per_head:
        bias = pl.BlockSpec((None, None, 2, ATT_BLK, 3 * ATT_BLK),
                            lambda hp, qb: (_pattern_of_block(qb, nqb), hp, 0, 0, 0))
    else:
        bias = pl.BlockSpec((None, ATT_BLK, 3 * ATT_BLK), lambda hp, qb: (_pattern_of_block(qb, nqb), 0, 0))
    sink = pl.BlockSpec((None, 8, 128), lambda hp, qb: (hp, 0, 0))
    return q, [kwin(0), kwin(1), kwin(2)], ctx, bias, sink


def _attn_fwd(q, kpad, vpad, kc, vc, bias, sink, *, mode, n_ctx):
    t = q.shape[0]
    per_head = bias.ndim == 5
    qs, kws, ctx, bias_s, sink_s = _attn_specs(n_ctx, t // ATT_BLK, per_head)
    has_sink = sink is not None

    def body(*refs):
        q_ref, k_refs, v_refs, kc_ref, vc_ref, b_ref = refs[0], refs[1:4], refs[4:7], refs[7], refs[8], refs[9]
        s_ref = refs[10] if has_sink else None
        o_ref = refs[-1]
        k3 = jnp.concatenate([r[...] for r in k_refs], axis=0)
        v3 = jnp.concatenate([r[...] for r in v_refs], axis=0)
        b0, b1 = (b_ref[0], b_ref[1]) if per_head else (b_ref[...], b_ref[...])
        o_ref[...] = _attn_block(q_ref[...], k3, v3, kc_ref[...], vc_ref[...], b0, b1,
                                 s_ref[...] if has_sink else None)

    in_specs = [qs] + kws + kws + [ctx, ctx, bias_s] + ([sink_s] if has_sink else [])
    args = [q, kpad, kpad, kpad, vpad, vpad, vpad, kc, vc, bias] + ([sink] if has_sink else [])
    return pl.pallas_call(
        body, name=mode + "_attn_fwd", grid=(4, t // ATT_BLK),
        in_specs=in_specs, out_specs=qs, out_shape=jax.ShapeDtypeStruct((t, 512), F32),
        compiler_params=_cparams(("parallel", "parallel")),
    )(*args)


def _attn_bwd(q, kpad, vpad, kc, vc, bias, sink, do, *, mode, n_ctx):
    t = q.shape[0]
    nqb = t // ATT_BLK
    per_head = bias.ndim == 5
    qs, kws, ctx, bias_s, sink_s = _attn_specs(n_ctx, nqb, per_head)
    has_sink = sink is not None
    n_in = 11 + has_sink

    def body(*refs):
        q_ref, k_refs, v_refs, kc_ref, vc_ref, b_ref = refs[0], refs[1:4], refs[4:7], refs[7], refs[8], refs[9]
        s_ref = refs[10] if has_sink else None
        do_ref = refs[n_in - 1]
        outs = list(refs[n_in:])
        dq_ref, dkp_ref, dvp_ref, dkc_ref, dvc_ref = outs[:5]
        outs = outs[5:]
        db_ref = outs.pop(0) if per_head else None
        ds_ref = outs.pop(0) if has_sink else None
        qb = pl.program_id(1)
        up = lambda r: r[...].astype(F32)
        k3 = jnp.concatenate([up(r) for r in k_refs], axis=0)
        v3 = jnp.concatenate([up(r) for r in v_refs], axis=0)
        prim = [up(q_ref), k3, v3, up(kc_ref), up(vc_ref)]
        if per_head:
            prim += [b_ref[0], b_ref[1]]
        if has_sink:
            prim += [s_ref[...]]

        def fn(*a):
            a = list(a)
            qv, k3v, v3v, kcv, vcv = a[:5]
            a = a[5:]
            b0 = a.pop(0) if per_head else b_ref[...]
            b1 = a.pop(0) if per_head else b0
            sk = a.pop(0) if has_sink else None
            return _attn_block(qv, k3v, v3v, kcv, vcv, b0, b1, sk)

        _, vjp = jax.vjp(fn, *prim)
        grads = list(vjp(do_ref[...]))
        dq_ref[...] = grads[0]
        for s in range(3):
            dkp_ref[s] = grads[1][ATT_BLK * s:ATT_BLK * (s + 1), :]
            dvp_ref[s] = grads[2][ATT_BLK * s:ATT_BLK * (s + 1), :]

        @pl.when(qb == 0)
        def _():
            dkc_ref[...] = jnp.zeros_like(dkc_ref)
            dvc_ref[...] = jnp.zeros_like(dvc_ref)
            if has_sink:
                ds_ref[...] = jnp.zeros_like(ds_ref)

        dkc_ref[...] += grads[3]
        dvc_ref[...] += grads[4]
        rest_g = grads[5:]
        if per_head:
            opens = (qb <= 2) | (qb == nqb - 1)
            g0, g1 = rest_g.pop(0), rest_g.pop(0)

            @pl.when(opens)
            def _():
                db_ref[0] = g0
                db_ref[1] = g1

            @pl.when(jnp.logical_not(opens))
            def _():
                db_ref[0] += g0
                db_ref[1] += g1

        if has_sink:
            ds_ref[...] += rest_g.pop(0)

    part = pl.BlockSpec((None, None, 3, ATT_BLK, 128), lambda hp, qb: (hp, qb, 0, 0, 0))
    in_specs = [qs] + kws + kws + [ctx, ctx, bias_s] + ([sink_s] if has_sink else []) + [qs]
    args = [q, kpad, kpad, kpad, vpad, vpad, vpad, kc, vc, bias] + ([sink] if has_sink else []) + [do]
    out_specs = [qs, part, part, ctx, ctx] + ([bias_s] if per_head else []) + ([sink_s] if has_sink else [])
    out_shape = [jax.ShapeDtypeStruct((t, 512), F32),
                 jax.ShapeDtypeStruct((4, nqb, 3, ATT_BLK, 128), F32),
                 jax.ShapeDtypeStruct((4, nqb, 3, ATT_BLK, 128), F32),
                 jax.ShapeDtypeStruct((n_ctx, 512), F32), jax.ShapeDtypeStruct((n_ctx, 512), F32)]
    if per_head:
        out_shape.append(jax.ShapeDtypeStruct(bias.shape, F32))
    if has_sink:
        out_shape.append(jax.ShapeDtypeStruct((4, 8, 128), F32))
    res = list(pl.pallas_call(
        body, name=mode + "_attn_bwd", grid=(4, nqb),
        in_specs=in_specs, out_specs=out_specs, out_shape=tuple(out_shape),
        compiler_params=_cparams(("parallel", "arbitrary")),
    )(*args))
    dq, dkp, dvp, dkc, dvc = res[:5]
    res = res[5:]
    dbias = res.pop(0) if per_head else None
    dsink = res.pop(0) if has_sink else None
    return dq, dkp, dvp, dkc, dvc, dbias, dsink


def _window_fold(part, dctx, n_ctx):
    nqb = part.shape[1]
    nkb = nqb - 1

    def body(p0_ref, p1_ref, p2_ref, c_ref, o_ref):
        kb = pl.program_id(1) - 1

        @pl.when(kb < 0)
        def _():
            o_ref[...] = c_ref[...]

        @pl.when(kb >= 0)
        def _():
            acc = p1_ref[...]
            acc = acc + jnp.where(kb + 2 <= nkb, p0_ref[...], 0.0)
            acc = acc + jnp.where(kb >= 1, p2_ref[...], 0.0)
            o_ref[...] = acc

    def pspec(s, dq):
        return pl.BlockSpec((None, None, None, ATT_BLK, 128),
                            lambda hp, b: (hp, jnp.clip(b + dq, 1, nqb - 1), s, 0, 0))

    return pl.pallas_call(
        body, name="window_fold", grid=(4, nqb),
        in_specs=[pspec(0, 1), pspec(1, 0), pspec(2, -1), pl.BlockSpec((n_ctx, 128), lambda hp, b: (0, hp))],
        out_specs=pl.BlockSpec((ATT_BLK, 128), lambda hp, b: (b, hp)),
        out_shape=jax.ShapeDtypeStruct((nqb * ATT_BLK, 512), F32),
        compiler_params=_cparams(("parallel", "parallel")),
    )(part, part, part, dctx)


def _loss_head(xt, target, n_ctx):
    t = xt.shape[0]
    nct = n_ctx // TOK

    def body(x_ref, t_ref, l_ref, d_ref):
        i = pl.program_id(0)

        @pl.when(i == 0)
        def _():
            l_ref[...] = jnp.zeros_like(l_ref)

        @pl.when(i < nct)
        def _():
            d_ref[...] = jnp.zeros_like(d_ref)

        @pl.when(i >= nct)
        def _():
            err = x_ref[...] - t_ref[...]
            d_ref[...] = err * (1.0 / D_MODEL)
            l_ref[...] += jnp.sum(err * err, keepdims=True) * (0.5 / D_MODEL)

    return pl.pallas_call(
        body, name="loss_head", grid=(t // TOK,),
        in_specs=[_row_spec(D_MODEL), pl.BlockSpec((TOK, D_MODEL), lambda i: (jnp.maximum(i - nct, 0), 0))],
        out_specs=[_const_spec((8, 128)), _row_spec(D_MODEL)],
        out_shape=(jax.ShapeDtypeStruct((8, 128), F32), jax.ShapeDtypeStruct((t, D_MODEL), F32)),
        compiler_params=_cparams(("arbitrary",)),
    )(xt, target)


PACK_W = 1024
SUM_STEPS = 8


def _sum_chips(recvs):
    def split(a):
        rows = a.shape[1]
        if rows % (8 * SUM_STEPS):
            return None
        return rows // SUM_STEPS

    def body(*refs):
        n = len(refs) // 2
        for r_ref, o_ref in zip(refs[:n], refs[n:]):
            up = lambda s: r_ref[s].astype(F32)
            o_ref[...] = ((up(0) + up(1)) + up(2)) + up(3)

    in_specs, out_specs = [], []
    for a in recvs:
        rb, tail = split(a), a.shape[2:]
        zeros = (0,) * len(tail)
        if rb is None:
            in_specs.append(pl.BlockSpec(a.shape, functools.partial(lambda i, z: (0, 0) + z, z=zeros)))
            out_specs.append(pl.BlockSpec(a.shape[1:], functools.partial(lambda i, z: (0,) + z, z=zeros)))
        else:
            in_specs.append(pl.BlockSpec((4, rb) + tail, functools.partial(lambda i, z: (0, i) + z, z=zeros)))
            out_specs.append(pl.BlockSpec((rb,) + tail, functools.partial(lambda i, z: (i,) + z, z=zeros)))
    return pl.pallas_call(
        body, name="sum_chips", grid=(SUM_STEPS,),
        in_specs=in_specs, out_specs=out_specs,
        out_shape=tuple(jax.ShapeDtypeStruct(a.shape[1:], F32) for a in recvs),
        compiler_params=_cparams(("arbitrary",)),
    )(*recvs)


ADAM_BLOCK_BYTES = 1 << 20


def _adamw(p_a, p_b, w, m, v, name):
    layers, rows, cols = w.shape
    tr = rows
    while tr % 16 == 0 and tr * cols * 4 > ADAM_BLOCK_BYTES:
        tr //= 2
    c1 = 1.0 / (1.0 - ADAM_B1 ** ADAM_STEP)
    c2 = 1.0 / (1.0 - ADAM_B2 ** ADAM_STEP)

    def body(a_ref, b_ref, w_ref, m_ref, v_ref, g_ref, d_ref, nm_ref, nv_ref):
        g = a_ref[...] + b_ref[...]
        nm = ADAM_B1 * m_ref[...] + (1.0 - ADAM_B1) * g
        nv = ADAM_B2 * v_ref[...] + (1.0 - ADAM_B2) * (g * g)
        g_ref[...] = g
        nm_ref[...] = nm
        nv_ref[...] = nv
        d_ref[...] = -ADAM_LR * ((nm * c1) / (jnp.sqrt(nv * c2) + ADAM_EPS) + ADAM_WD * w_ref[...])

    spec = pl.BlockSpec((None, tr, cols), lambda l, i: (l, i, 0))
    return pl.pallas_call(
        body, name=name, grid=(layers, rows // tr),
        in_specs=[spec] * 5, out_specs=[spec] * 4,
        out_shape=tuple(jax.ShapeDtypeStruct(w.shape, F32) for _ in range(4)),
        compiler_params=_cparams(("parallel", "parallel")),
    )(p_a, p_b, w, m, v)


MESH = pl.DeviceIdType.MESH
ANY_SPEC = pl.BlockSpec(memory_space=pl.ANY)


def _chip_exchange(srcs, out_shapes, src_window, dst_window, name):
    n = len(srcs)

    def body(*refs):
        src_refs, out_refs = refs[:n], refs[n:2 * n]
        send_sems, recv_sems, local_sems = refs[2 * n:]
        x, y, c = lax.axis_index("x"), lax.axis_index("y"), lax.axis_index("c")
        me = 2 * x + y
        peers = [(x, 1 - y), (1 - x, y), (1 - x, 1 - y)]

        def copy(k, j, from_chip, to_chip):
            px, py = peers[j]
            return pltpu.make_async_remote_copy(
                src_ref=src_window(k, src_refs[k], to_chip), dst_ref=dst_window(k, out_refs[k], from_chip),
                send_sem=send_sems.at[3 * k + j], recv_sem=recv_sems.at[3 * k + j],
                device_id=(px, py, c), device_id_type=MESH)

        local = [pltpu.make_async_copy(src_window(k, src_refs[k], me), dst_window(k, out_refs[k], me),
                                       local_sems.at[k]) for k in range(n)]
        for cp in local:
            cp.start()
        sends = [copy(k, j, me, 2 * px + py) for k in range(n) for j, (px, py) in enumerate(peers)]
        for cp in sends:
            cp.start()
        for k in range(n):
            for j, (px, py) in enumerate(peers):
                copy(k, j, 2 * px + py, me).wait_recv()
        for cp in sends:
            cp.wait_send()
        for cp in local:
            cp.wait()

    return pl.pallas_call(
        body, name=name, in_specs=[ANY_SPEC] * n, out_specs=[ANY_SPEC] * n,
        out_shape=tuple(out_shapes),
        scratch_shapes=[pltpu.SemaphoreType.DMA((3 * n,)), pltpu.SemaphoreType.DMA((3 * n,)),
                        pltpu.SemaphoreType.DMA((n,))],
    )(*srcs)


def _core_swap(srcs):
    n = len(srcs)

    def body(*refs):
        src_refs, out_refs, send_sems, recv_sems = refs[:n], refs[n:2 * n], refs[2 * n], refs[2 * n + 1]
        x, y, c = lax.axis_index("x"), lax.axis_index("y"), lax.axis_index("c")
        copies = [pltpu.make_async_remote_copy(
            src_ref=src_refs[k], dst_ref=out_refs[k], send_sem=send_sems.at[k], recv_sem=recv_sems.at[k],
            device_id=(x, y, 1 - c), device_id_type=MESH) for k in range(n)]
        for cp in copies:
            cp.start()
        for cp in copies:
            cp.wait()

    return pl.pallas_call(
        body, name="core_swap", in_specs=[ANY_SPEC] * n, out_specs=[ANY_SPEC] * n,
        out_shape=tuple(jax.ShapeDtypeStruct(s.shape, s.dtype) for s in srcs),
        scratch_shapes=[pltpu.SemaphoreType.DMA((n,)), pltpu.SemaphoreType.DMA((n,))],
    )(*srcs)


def _col_window(ref, start, size):
    idx = (slice(None),) * (len(ref.shape) - 1) + (pl.ds(pl.multiple_of(start, 128), size),)
    return ref.at[idx]


def _row_window(ref, start, size):
    idx = (slice(None),) * (len(ref.shape) - 2) + (pl.ds(pl.multiple_of(start, 8), size), slice(None))
    return ref.at[idx]


N_SHARD_IN = 2624
WIN_W = 2944
WIN_START = (0, 2560, 5248, 7808)
WIN_PIECES = (((0, 2624),), ((64, 2688),), ((0, 640), (896, 2880)), ((320, 2944),))


def _gather_weights(w_ada, w_in, w_glu, w_br, w_out, conv_w):
    lay = w_ada.shape[0]
    sizes = (768, None, 128, 256, 256, 128)

    def dst(k, ref, s):
        if k == 1:
            return ref.at[s]
        if k in (2, 4):
            return _row_window(ref, s * sizes[k], sizes[k])
        return _col_window(ref, s * sizes[k], sizes[k])

    shapes = (jax.ShapeDtypeStruct((lay, D_MODEL, 3 * D_MODEL), w_ada.dtype),
              jax.ShapeDtypeStruct((4,) + w_in.shape, w_in.dtype),
              jax.ShapeDtypeStruct((lay, MIX_W, MIX_W), w_glu.dtype),
              jax.ShapeDtypeStruct((lay, 4, MIX_W, D_MODEL), w_br.dtype),
              jax.ShapeDtypeStruct((lay, D_MODEL, D_MODEL), w_out.dtype),
              jax.ShapeDtypeStruct((lay, 8, MIX_W), conv_w.dtype))
    return _chip_exchange((w_ada, w_in, w_glu, w_br, w_out, conv_w), shapes,
                          lambda k, ref, t: ref, dst, "gather_weights")


def _scatter_grads(dw_ada, dw_in, dw_glu, dw_br, dw_out, dconv_w, small):
    def src(k, ref, t):
        if k == 0:
            return _col_window(ref, t * 768, 768)
        if k == 1:
            start = jnp.where(t == 0, WIN_START[0], jnp.where(t == 1, WIN_START[1],
                              jnp.where(t == 2, WIN_START[2], WIN_START[3])))
            return _col_window(ref, start, WIN_W)
        if k == 2:
            return _row_window(ref, t * 128, 128)
        if k == 3:
            return _col_window(ref, t * 256, 256)
        if k == 4:
            return _row_window(ref, t * 256, 256)
        if k == 5:
            return _col_window(ref, t * 128, 128)
        return ref

    pieces = ((D_MODEL, 768), (D_MODEL, WIN_W), (128, MIX_W), (4, MIX_W, 256), (256, D_MODEL), (8, 128), small.shape)
    srcs = (dw_ada, dw_in, dw_glu, dw_br, dw_out, dconv_w, small)
    shapes = tuple(jax.ShapeDtypeStruct((4,) + p, s.dtype) for p, s in zip(pieces, srcs))
    return _chip_exchange(srcs, shapes,
                          src, lambda k, ref, s: ref.at[s], "scatter_grads")


def _s5_tables(a_re, a_im, log_dt, b_re, b_im, c_re, c_im):
    ln = S5_CHUNK
    hi = lax.Precision.HIGHEST
    dt = jnp.exp(log_dt)[..., None]
    mag = jnp.exp(dt * a_re)
    abr = mag * jnp.cos(dt * a_im)
    abi = mag * jnp.sin(dt * a_im)
    den = a_re * a_re + a_im * a_im
    fr = ((abr - 1.0) * a_re + abi * a_im) / den
    fi = (abi * a_re - (abr - 1.0) * a_im) / den
    bbr = fr[..., None] * b_re - fi[..., None] * b_im
    bbi = fr[..., None] * b_im + fi[..., None] * b_re
    n = jnp.arange(ln + 1, dtype=F32)[:, None, None, None]
    pm = jnp.exp(n * dt * a_re)
    er = pm * jnp.cos(n * dt * a_im)
    ei = pm * jnp.sin(n * dt * a_im)
    e3 = lambda e, b, c: jnp.einsum("tdgp,dgpa,dgbp->dgabt", e, b, c, precision=hi)
    gt = e3(er[:ln], bbr, c_re) - e3(er[:ln], bbi, c_im) - e3(ei[:ln], bbr, c_im) - e3(ei[:ln], bbi, c_re)
    by_dir = lambda fwd, bwd: jnp.stack([fwd[:, 0], bwd[:, 1]], axis=1)
    erj, eij = by_dir(er[:ln][::-1], er[:ln]), by_dir(ei[:ln][::-1], ei[:ln])
    e2 = lambda e, b: jnp.einsum("jdgp,dgpa->dgajp", e, b, precision=hi)
    w = jnp.concatenate([e2(erj, bbr) - e2(eij, bbi), e2(erj, bbi) + e2(eij, bbr)], axis=-1)
    er1, ei1 = by_dir(er[1:], er[1:][::-1]), by_dir(ei[1:], ei[1:][::-1])
    ev = lambda c, e: jnp.einsum("dgbp,idgp->dgpbi", c, e, precision=hi)
    v = jnp.concatenate([ev(c_re, er1) - ev(c_im, ei1), -(ev(c_re, ei1) + ev(c_im, er1))], axis=2)
    a1 = jnp.concatenate([er[ln], er[ln]], axis=-1)
    a2 = jnp.concatenate([-ei[ln], ei[ln]], axis=-1)
    return (gt.transpose(1, 2, 3, 0, 4).reshape(S5_GROUPS, 256, 2 * ln),
            w.transpose(1, 2, 3, 0, 4).reshape(S5_GROUPS, S5_CH * ln, 256),
            v.transpose(1, 0, 2, 3, 4).reshape(S5_GROUPS, 256, S5_CH * ln),
            a1.reshape(64, 128), a2.reshape(64, 128))


def _lag_onehot():
    ln = S5_CHUNK
    j, i = np.meshgrid(np.arange(ln), np.arange(ln), indexing="ij")
    lag = np.arange(ln)[:, None, None]
    z = np.concatenate([lag == (i - j)[None], lag == (j - i)[None]], axis=0).astype(np.float32)
    return jnp.broadcast_to(jnp.asarray(z.reshape(2 * ln, ln * ln), BF16), (S5_GROUPS, 2 * ln, ln * ln))


def _toeplitz(gt):
    ln = S5_CHUNK
    flat = _matmul(gt, _lag_onehot(), out_dtype=BF16, name="s5_toeplitz")
    return (flat.reshape(S5_GROUPS, S5_CH, S5_CH, ln, ln).transpose(0, 1, 3, 2, 4)
            .reshape(S5_GROUPS, S5_CH * ln, S5_CH * ln))


def _toeplitz_fold(dk):
    ln = S5_CHUNK
    flat = dk.reshape(S5_GROUPS, S5_CH, ln, S5_CH, ln).transpose(0, 1, 3, 2, 4).reshape(S5_GROUPS, 256, ln * ln)
    return _matmul(flat, _lag_onehot(), trans_b=True, name="s5_toeplitz_fold")


def _chunk_rows(t):
    k = t // S5_CHUNK
    return k, -(-k // 128) * 128


def _to_chunks(u):
    k, kp = _chunk_rows(u.shape[0])
    v = u.reshape(k, S5_CHUNK, S5_GROUPS, S5_CH).transpose(2, 0, 3, 1).reshape(S5_GROUPS, k, S5_CH * S5_CHUNK)
    return jnp.pad(v, ((0, 0), (0, kp - k), (0, 0)))


def _from_chunks(y, t):
    k, _ = _chunk_rows(t)
    return y[:, :k].reshape(S5_GROUPS, k, S5_CH, S5_CHUNK).transpose(1, 3, 0, 2).reshape(t, MIX_W)


def _states_to_rows(s):
    kp = s.shape[1]
    return s.reshape(S5_GROUPS, kp, 2, 128).transpose(1, 2, 0, 3).reshape(kp, 64, 128)


def _rows_to_states(h):
    kp = h.shape[0]
    return h.reshape(kp, 2, S5_GROUPS, 128).transpose(2, 0, 1, 3).reshape(S5_GROUPS, kp, 256)


def _na_bias(rel_bias):
    a, m = np.meshgrid(np.arange(4), np.arange(12), indexing="ij")
    di = np.clip(m - a + 3, 0, 2 * NA_ROWS - 2).reshape(-1)
    qc, kc = np.meshgrid(np.arange(GRID_W), np.arange(GRID_W), indexing="ij")
    dj = np.clip(kc - qc + NA_COLS - 1, 0, 2 * NA_COLS - 2).reshape(-1)
    oh_i = jnp.asarray(di[:, None] == np.arange(2 * NA_ROWS - 1)[None, :], F32)
    oh_j = jnp.asarray(dj[:, None] == np.arange(2 * NA_COLS - 1)[None, :], F32)
    hi = lax.Precision.HIGHEST
    cols = jnp.einsum("hij,cj->hic", rel_bias, oh_j, precision=hi)
    full = jnp.einsum("ri,hic->hrc", oh_i, cols, precision=hi)
    full = full.reshape(N_HEADS, 4, 12, GRID_W, GRID_W).transpose(0, 1, 3, 2, 4)
    return full.reshape(4, 2, ATT_BLK, 3 * ATT_BLK)


def _rope_tables(n_ctx, n_lat):
    tok = jnp.arange(n_lat, dtype=jnp.int32)
    row = (tok // GRID_W).astype(F32)
    col = (tok % GRID_W).astype(F32)
    inv = ROPE_BASE ** (-jnp.arange(ROPE_PAIRS, dtype=F32) / ROPE_PAIRS)
    ang = jnp.concatenate([row[:, None] * inv, col[:, None] * inv], axis=-1)
    cos, sin = jnp.cos(ang), jnp.sin(ang)
    cos = jnp.tile(jnp.concatenate([cos, cos], axis=-1), (1, 2))
    sin = jnp.tile(jnp.concatenate([-sin, sin], axis=-1), (1, 2))
    return (jnp.concatenate([jnp.ones((n_ctx, 128), F32), cos], axis=0),
            jnp.concatenate([jnp.zeros((n_ctx, 128), F32), sin], axis=0))


def _pad_blocks(a, n_ctx):
    return jnp.pad(a[n_ctx:], ((ATT_BLK, ATT_BLK), (0, 0)))


def _layer_fwd(xt, cc, w, rope, n_ctx):
    t = xt.shape[0]
    sv = {}
    mod = _adaln_fwd(cc, w["w_ada"], w["b_ada"].reshape(1, -1))
    mod4 = mod[:2].reshape(2, 3, 1, D_MODEL)
    h = _modnorm_fwd(xt, w["norm_g"].reshape(1, -1), mod4, n_ctx)
    z = _matmul(h, w["w_in"], name="proj_fwd")

    s5_args = (w["s5_a_re"], w["s5_a_im"], w["s5_log_dt"], w["s5_b_re"], w["s5_b_im"], w["s5_c_re"], w["s5_c_im"])
    (gt, tw, tv, a1, a2), tab_vjp = jax.vjp(_s5_tables, *s5_args)
    ktoe = _toeplitz(gt)
    tw, tv = tw.astype(BF16), tv.astype(BF16)
    uc = _to_chunks(z[:, :MIX_W].astype(BF16))
    st = _matmul(uc, tw, name="s5_chunk_state")
    hprev = _s5_scan_fwd(_states_to_rows(st), a1, a2, t // S5_CHUNK, n_ctx // S5_CHUNK)
    uh = jnp.concatenate([uc, _rows_to_states(hprev).astype(BF16)], axis=2)
    ysum = _from_chunks(_matmul(uh, jnp.concatenate([ktoe, tv], axis=1), name="s5_chunk_out"), t)
    s5_d = w["s5_d"].reshape(1, MIX_W)
    y_s5 = _s5post_fwd(ysum, z, s5_d, w["s5_w_glu"])

    conv_w = w["conv_w"]
    y_conv = _conv_fwd(z, conv_w, w["conv_b"].reshape(1, -1), n_ctx)

    gains = (jnp.tile(w["na_q_g"], 8)[None], jnp.tile(w["na_k_g"], 8)[None],
             jnp.tile(w["gqa_q_g"], 8)[None], jnp.tile(w["gqa_k_g"], 2)[None])
    q_na, k_na, v_na, q_g, k_g, v_g = _prep_fwd(z, gains, rope)
    bias, bias_vjp = jax.vjp(_na_bias, w["na_rel_bias"])
    sink = jnp.zeros((4, 8, 128), F32).at[:, :2, :].set(
        jnp.broadcast_to(w["gqa_sink"].reshape(4, 2, 1), (4, 2, 128)))
    na_tab = jnp.where(_window_patterns("na", t - n_ctx)[:, None, None], bias[None], NEG_INF)
    gqa_tab = jnp.where(_window_patterns("gqa", t - n_ctx), 0.0, NEG_INF).astype(F32)
    na_in = (q_na, _pad_blocks(k_na, n_ctx), _pad_blocks(v_na, n_ctx), k_na[:n_ctx], v_na[:n_ctx], na_tab, None)
    gqa_in = (q_g, _pad_blocks(k_g, n_ctx), _pad_blocks(v_g, n_ctx), k_g[:n_ctx], v_g[:n_ctx], gqa_tab, sink)
    y_na = _attn_fwd(*na_in, mode="na", n_ctx=n_ctx)
    y_gqa = _attn_fwd(*gqa_in, mode="gqa", n_ctx=n_ctx)
    ys = (y_s5, y_conv, y_na, y_gqa)
    xt_new = _merge_fwd(xt, ys, z, mod4, w["w_br"], w["w_out"], n_ctx)
    sv.update(xt=xt, mod4=mod4, h=h, z=z, tab_vjp=tab_vjp, ktoe=ktoe, tw=tw, tv=tv, a1=a1, a2=a2, uc=uc,
              hprev=hprev, uh=uh, ysum=ysum, s5_d=s5_d, conv_w=conv_w, gains=gains, bias_vjp=bias_vjp,
              na_in=na_in, gqa_in=gqa_in, ys=ys)
    return xt_new, sv


def _layer_bwd(dxt_new, sv, cc, w, rope, n_ctx):
    t = dxt_new.shape[0]
    z, mod4 = sv["z"], sv["mod4"]
    dys, dgt, dmg, dgate, dw_br, dw_out = _merge_bwd(dxt_new, sv["ys"], z, mod4, w["w_br"], w["w_out"], n_ctx)

    dpre, du_skip, dd, dw_glu = _s5post_bwd(sv["ysum"], z, sv["s5_d"], w["s5_w_glu"], dys[0])
    dyc = _to_chunks(dpre)
    dhp = _matmul(dyc, sv["tv"], trans_b=True, name="s5_bwd_state")
    ds, da1, da2 = _s5_scan_bwd(_states_to_rows(dhp), sv["hprev"], sv["a1"], sv["a2"],
                                t // S5_CHUNK, n_ctx // S5_CHUNK)
    ds = _rows_to_states(ds).astype(BF16)
    duc = _matmul(jnp.concatenate([dyc, ds], axis=2), jnp.concatenate([sv["ktoe"], sv["tw"]], axis=2),
                  trans_b=True, name="s5_bwd_u")
    dkv = _matmul(sv["uh"].transpose(0, 2, 1), dyc, name="s5_bwd_kv")
    dtw = _matmul(sv["uc"].transpose(0, 2, 1), ds, name="s5_bwd_w")
    dgt_tab = _toeplitz_fold(dkv[:, :S5_CH * S5_CHUNK])
    s5_grads = sv["tab_vjp"]((dgt_tab, dtw, dkv[:, S5_CH * S5_CHUNK:], da1, da2))
    du_scan = _from_chunks(duc, t)

    dzv, dzb, dzc, dconv_w, dconv_b = _conv_bwd(z, sv["conv_w"], w["conv_b"].reshape(1, -1), dys[1], n_ctx)

    dq_na, dkp, dvp, dkc, dvc, dbias, _ = _attn_bwd(*sv["na_in"], dys[2], mode="na", n_ctx=n_ctx)
    dk_na, dv_na = _window_fold(dkp, dkc, n_ctx), _window_fold(dvp, dvc, n_ctx)
    dq_g, dkp, dvp, dkc, dvc, _, dsink = _attn_bwd(*sv["gqa_in"], dys[3], mode="gqa", n_ctx=n_ctx)
    dk_g, dv_g = _window_fold(dkp, dkc, n_ctx), _window_fold(dvp, dvc, n_ctx)
    pb = _prep_bwd(z, sv["gains"], rope, (dq_na, dk_na, dv_na, dq_g, dk_g, dv_g), du_skip, du_scan)
    dz_naq, dz_nak, dz_nav, dz_gq, dz_gk, dz_gv, dz_u, dg_naq, dg_nak, dg_gq, dg_gk = pb

    dz = jnp.concatenate([dz_u, dgt[0], dzv, dzb, dzc, dgt[1], dz_naq, dz_nak, dz_nav, dgt[2], dz_gq, dz_gk, dz_gv,
                          jnp.zeros((t, OFF["gqa_gate"] - OFF["pad"]), BF16), dgt[3], *dmg], axis=1)
    dh = _matmul(dz, w["w_in"], trans_b=True, name="proj_bwd_x")
    dw_in = _matmul(sv["h"].T, dz, out_dtype=BF16, name="proj_bwd_w")
    dxt, dnorm_g, dshift, dscale = _modnorm_bwd(sv["xt"], w["norm_g"].reshape(1, -1), mod4, dh, dxt_new, n_ctx)
    dmod = jnp.concatenate([dshift, dscale, dgate], axis=1).reshape(2, 3 * D_MODEL)
    dcc, dw_ada, db_ada = _adaln_bwd(cc, w["w_ada"], jnp.pad(dmod, ((0, 6), (0, 0))))

    (drel,) = sv["bias_vjp"](dbias.sum(0))
    grads = dict(
        norm_g=dnorm_g[0], w_ada=dw_ada, b_ada=db_ada[0], w_in=dw_in,
        s5_a_re=s5_grads[0], s5_a_im=s5_grads[1], s5_log_dt=s5_grads[2], s5_b_re=s5_grads[3], s5_b_im=s5_grads[4],
        s5_c_re=s5_grads[5], s5_c_im=s5_grads[6], s5_d=dd.reshape(S5_GROUPS, S5_CH), s5_w_glu=dw_glu,
        conv_w=dconv_w, conv_b=dconv_b[0],
        na_q_g=dg_naq.reshape(8, HEAD_DIM).sum(0), na_k_g=dg_nak.reshape(8, HEAD_DIM).sum(0), na_rel_bias=drel,
        gqa_q_g=dg_gq.reshape(8, HEAD_DIM).sum(0), gqa_k_g=dg_gk.reshape(2, HEAD_DIM).sum(0),
        gqa_sink=dsink[:, :2, :].sum(-1).reshape(8), w_br=dw_br, w_out=dw_out)
    return dxt, dcc, grads


SHARDED = ("w_ada", "w_in", "s5_w_glu", "conv_w", "w_br", "w_out")
REPLICATED = ("norm_g", "b_ada", "s5_a_re", "s5_a_im", "s5_log_dt", "s5_b_re", "s5_b_im", "s5_c_re", "s5_c_im",
              "s5_d", "conv_b", "na_q_g", "na_k_g", "na_rel_bias", "gqa_q_g", "gqa_k_g", "gqa_sink")
WEIGHTS = ("c_ctx", "norm_g", "w_ada", "b_ada", "w_in", "s5_a_re", "s5_a_im", "s5_log_dt", "s5_b_re", "s5_b_im",
           "s5_c_re", "s5_c_im", "s5_d", "s5_w_glu", "conv_w", "conv_b", "na_q_g", "na_k_g", "na_rel_bias",
           "gqa_q_g", "gqa_k_g", "gqa_sink", "w_br", "w_out")


def _pack(pieces, row_multiple, dtype):
    flat = jnp.concatenate([p.reshape(-1).astype(dtype) for p in pieces])
    rows = -(-flat.shape[0] // PACK_W)
    rows = -(-rows // row_multiple) * row_multiple
    return jnp.pad(flat, (0, rows * PACK_W - flat.shape[0])).reshape(rows, PACK_W)


def _unpack(buf, shapes):
    flat = buf.reshape(-1)
    out, pos = [], 0
    for shp in shapes:
        size = int(np.prod(shp))
        out.append(flat[pos:pos + size].reshape(shp))
        pos += size
    return out


def _local_step(x, ctx, target, c_vec, c_ctx, layers):
    depth = len(layers)
    n_ctx, n_lat = ctx.shape[0], x.shape[0]
    cc = jnp.zeros((8, D_MODEL), F32).at[0].set(c_ctx).at[1].set(c_vec)
    rope = _rope_tables(n_ctx, n_lat)
    xt = jnp.concatenate([ctx, x], axis=0)
    saved = []
    for l in range(depth):
        xt, sv = _layer_fwd(xt, cc, layers[l], rope, n_ctx)
        saved.append(sv)
    loss_tile, dxt = _loss_head(xt, target, n_ctx)
    grads = [None] * depth
    dc_ctx = jnp.zeros((D_MODEL,), F32)
    for l in reversed(range(depth)):
        dxt, dcc, grads[l] = _layer_bwd(dxt, saved[l], cc, layers[l], rope, n_ctx)
        dc_ctx = dc_ctx + dcc[0]
    return loss_tile[0, 0], dxt[n_ctx:][None], dc_ctx, grads


def kernel(x, c, ctx, c_ctx, norm_g, w_ada, b_ada, w_in, s5_a_re, s5_a_im, s5_log_dt, s5_b_re, s5_b_im,
           s5_c_re, s5_c_im, s5_d, s5_w_glu, conv_w, conv_b, na_q_g, na_k_g, na_rel_bias, gqa_q_g,
           gqa_k_g, gqa_sink, w_br, w_out, loss_target, m_c_ctx, m_norm_g, m_w_ada, m_b_ada, m_w_in,
           m_s5_a_re, m_s5_a_im, m_s5_log_dt, m_s5_b_re, m_s5_b_im, m_s5_c_re, m_s5_c_im, m_s5_d,
           m_s5_w_glu, m_conv_w, m_conv_b, m_na_q_g, m_na_k_g, m_na_rel_bias, m_gqa_q_g, m_gqa_k_g,
           m_gqa_sink, m_w_br, m_w_out, v_c_ctx, v_norm_g, v_w_ada, v_b_ada, v_w_in, v_s5_a_re,
           v_s5_a_im, v_s5_log_dt, v_s5_b_re, v_s5_b_im, v_s5_c_re, v_s5_c_im, v_s5_d, v_s5_w_glu,
           v_conv_w, v_conv_b, v_na_q_g, v_na_k_g, v_na_rel_bias, v_gqa_q_g, v_gqa_k_g, v_gqa_sink,
           v_w_br, v_w_out):
    a = dict(locals())
    depth = a["norm_g"].shape[0]
    x, ctx, target = a["x"][0], a["ctx"][0], a["loss_target"][0]
    n_ctx, n_lat = ctx.shape[0], x.shape[0]
    assert n_ctx % ATT_BLK == 0 and n_lat % (4 * GRID_W) == 0 and n_lat // GRID_W >= NA_ROWS

    cast = lambda n: a[n].astype(BF16)
    conv8 = jnp.pad(a["conv_w"], ((0, 0), (0, 5), (0, 0)))
    g_ada, g_in, g_glu, g_br, g_out, g_conv = _gather_weights(
        cast("w_ada"), cast("w_in"), cast("s5_w_glu"), cast("w_br"), cast("w_out"), conv8)
    zpad = jnp.zeros((D_MODEL, OFF["gqa_gate"] - OFF["pad"]), BF16)
    split = OFF["pad"] - 2 * N_SHARD_IN
    layers = []
    for l in range(depth):
        w = {n: a[n][l] for n in REPLICATED}
        w.update(w_ada=g_ada[l], s5_w_glu=g_glu[l], w_br=g_br[l], w_out=g_out[l], conv_w=g_conv[l])
        w["w_in"] = jnp.concatenate([g_in[0, l], g_in[1, l], g_in[2, l][:, :split], zpad, g_in[2, l][:, split:],
                                     g_in[3, l]], axis=1)
        layers.append(w)

    loss_local, grad_x, dc_ctx, grads = _local_step(x, ctx, target, a["c"][0], a["c_ctx"], layers)
    loss = lax.psum(loss_local, ("x", "y", "c"))

    chip = 2 * lax.axis_index("x") + lax.axis_index("y")
    take = [functools.partial(lambda win, pc: jnp.concatenate([win[:, lo:hi] for lo, hi in pc], axis=1), pc=pc)
            for pc in WIN_PIECES]

    def small_pack(values, c_ctx_value, l):
        pieces = [values[n] for n in REPLICATED]
        pieces.append(c_ctx_value if l == 0 else jnp.zeros((D_MODEL,), F32))
        return _pack(pieces, 8 * SUM_STEPS, F32)

    mine, theirs = [], []
    for l in range(depth):
        g = grads[l]
        small = small_pack(g, dc_ctx, l)
        recv = _scatter_grads(g["w_ada"], g["w_in"], g["s5_w_glu"], g["w_br"], g["w_out"], g["conv_w"], small)
        part = list(_sum_chips([r.reshape(4, -1, r.shape[-1]) for r in recv]))
        part[1] = lax.switch(chip, take, part[1])
        mine.append(part)
        theirs.append(_core_swap(part))

    families = ("w_ada", "w_in", "s5_w_glu", "w_br", "w_out", "conv_w")
    out = {}
    for k, n in enumerate(families):
        p, q = jnp.stack([m[k] for m in mine]), jnp.stack([t[k] for t in theirs])
        if n == "conv_w":
            p, q = p[:, :3], q[:, :3]
        as3d = lambda arr: arr.reshape(depth, -1, arr.shape[-1])
        res = _adamw(p, q, as3d(a[n]), as3d(a["m_" + n]), as3d(a["v_" + n]), "adamw_" + n)
        out[n] = [r.reshape(a[n].shape) for r in res]
    p, q = jnp.stack([m[6] for m in mine]), jnp.stack([t[6] for t in theirs])
    packs = [jnp.stack([small_pack({n: a[pre + n][l] for n in REPLICATED}, a[pre + "c_ctx"], l)
                        for l in range(depth)]) for pre in ("", "m_", "v_")]
    res = _adamw(p, q, *packs, "adamw_small")
    shapes = [a[n].shape[1:] for n in REPLICATED] + [a["c_ctx"].shape]
    per_layer = [[_unpack(r[l], shapes) for l in range(depth)] for r in res]
    for j, n in enumerate(REPLICATED):
        out[n] = [jnp.stack([per_layer[key][l][j] for l in range(depth)]) for key in range(4)]
    out["c_ctx"] = [per_layer[key][0][-1] for key in range(4)]
    results = [loss, grad_x]
    for key in range(4):
        results += [out[n][key] for n in WEIGHTS]
    return tuple(results)
```

```python
import functools

import numpy as np
import jax
import jax.numpy as jnp
from jax import lax
from jax.experimental import pallas as pl
from jax.experimental.pallas import tpu as pltpu

F32 = jnp.float32
BF16 = jnp.bfloat16

D_MODEL = 1024
MIX_W = 512
GRID_W = 64
HEAD_DIM = 64
N_HEADS = 8
S5_GROUPS = 32
S5_CH = 16
S5_CHUNK = 32
NA_ROWS = 8
NA_COLS = 16
WINDOW = 128
ROPE_BASE = 10000.0
ROPE_PAIRS = 16
EPS = 1e-6
NEG_INF = -1e30
ATT_BLK = 256
TOK = 256
VMEM_LIMIT = 56 * 1024 * 1024

ADAM_LR, ADAM_B1, ADAM_B2, ADAM_EPS, ADAM_WD, ADAM_STEP = 0.001, 0.9, 0.999, 1e-8, 0.01, 10

OFF = dict(s5_u=0, s5_gate=512, conv_v=1024, conv_b=1536, conv_c=2048, conv_gate=2560,
           na_q=3072, na_k=3584, na_v=4096, na_gate=4608, gqa_q=5120, gqa_k=5632, gqa_v=5760,
           pad=5888, gqa_gate=6144, merge_s5=6656, merge_conv=7680, merge_na=8704, merge_gqa=9728)
N_Z = 10752
GATE_OFFS = (OFF["s5_gate"], OFF["conv_gate"], OFF["na_gate"], OFF["gqa_gate"])
MERGE_OFFS = (OFF["merge_s5"], OFF["merge_conv"], OFF["merge_na"], OFF["merge_gqa"])


def _cparams(sem):
    return pltpu.CompilerParams(dimension_semantics=sem, vmem_limit_bytes=VMEM_LIMIT)


def _dot(a, b, ca, cb):
    return lax.dot_general(a.astype(BF16), b.astype(BF16), (((ca,), (cb,)), ((), ())),
                           preferred_element_type=F32)


def _dot_tn(a, b):
    return _dot(a.astype(F32).T, b, 1, 0)


@jax.custom_vjp
def mm(a, b):
    return _dot(a, b, 1, 0)


@jax.custom_vjp
def mm_nt(a, b):
    return _dot(a, b, 1, 1)


@jax.custom_vjp
def mm_tn(a, b):
    return _dot_tn(a, b)


mm.defvjp(lambda a, b: (mm(a, b), (a, b)), lambda r, g: (mm_nt(g, r[1]), mm_tn(r[0], g)))
mm_nt.defvjp(lambda a, b: (mm_nt(a, b), (a, b)), lambda r, g: (mm(g, r[1]), mm_tn(g, r[0])))
mm_tn.defvjp(lambda a, b: (mm_tn(a, b), (a, b)), lambda r, g: (mm_nt(r[1], g), mm(r[0], g)))


@functools.partial(jax.custom_vjp, nondiff_argnums=(1,))
def lane_roll(x, shift):
    return pltpu.roll(x, shift, 1)


lane_roll.defvjp(lambda x, shift: (lane_roll(x, shift), None),
                 lambda shift, _, g: (lane_roll(g, (g.shape[1] - shift) % g.shape[1]),))


def _silu(x):
    return x * jax.nn.sigmoid(x)


def _dsilu(x):
    s = jax.nn.sigmoid(x)
    return s * (1.0 + x * (1.0 - s))


def _pick(n, prefs):
    for p in prefs:
        if n % p == 0:
            return p
    return n


def _matmul(a, b, *, trans_b=False, out_dtype=F32, tm=None, tn=None, tk=None, name):
    squeeze = a.ndim == 2
    if squeeze:
        a, b = a[None], b[None]
    nb, m, k = a.shape
    n = b.shape[1] if trans_b else b.shape[2]
    tm = tm or _pick(m, (1280, 1024, 640, 512, 256, 128))
    tn = tn or _pick(n, (1536, 1024, 512, 256, 128))
    tk = tk or _pick(k, (1536, 1280, 1024, 768, 640, 512, 256, 128))
    nk = k // tk

    def body(a_ref, b_ref, o_ref, *scr):
        part = _dot(a_ref[...], b_ref[...], 1, 1 if trans_b else 0)
        if nk == 1:
            o_ref[...] = part.astype(out_dtype)
        else:
            acc = scr[0]
            kk = pl.program_id(3)

            @pl.when(kk == 0)
            def _():
                acc[...] = part

            @pl.when(kk > 0)
            def _():
                acc[...] += part

            @pl.when(kk == nk - 1)
            def _():
                o_ref[...] = acc[...].astype(out_dtype)

    if trans_b:
        b_spec = pl.BlockSpec((None, tn, tk), lambda bb, i, j, kk: (bb, j, kk))
    else:
        b_spec = pl.BlockSpec((None, tk, tn), lambda bb, i, j, kk: (bb, kk, j))
    out = pl.pallas_call(
        body, name=name,
        grid=(nb, m // tm, n // tn, nk),
        in_specs=[pl.BlockSpec((None, tm, tk), lambda bb, i, j, kk: (bb, i, kk)), b_spec],
        out_specs=pl.BlockSpec((None, tm, tn), lambda bb, i, j, kk: (bb, i, j)),
        out_shape=jax.ShapeDtypeStruct((nb, m, n), out_dtype),
        scratch_shapes=[] if nk == 1 else [pltpu.VMEM((tm, tn), F32)],
        compiler_params=_cparams(("parallel", "parallel", "parallel", "arbitrary")),
    )(a, b)
    return out[0] if squeeze else out


def _adaln_fn(cc, w, b):
    return mm(_silu(cc), w) + b


def _adaln_fwd(cc, w_ada, b_ada):
    def body(cc_ref, w_ref, b_ref, o_ref):
        o_ref[...] = _adaln_fn(cc_ref[...], w_ref[...], b_ref[...])

    return pl.pallas_call(
        body, name="adaln_fwd", out_shape=jax.ShapeDtypeStruct((8, 3 * D_MODEL), F32),
        compiler_params=pltpu.CompilerParams(vmem_limit_bytes=VMEM_LIMIT),
    )(cc, w_ada, b_ada)


def _adaln_bwd(cc, w_ada, dmod):
    def body(cc_ref, w_ref, g_ref, dcc_ref, dw_ref, db_ref):
        cc_v, g = cc_ref[...], g_ref[...]
        dw_ref[...] = mm_tn(_silu(cc_v), g).astype(BF16)
        db_ref[...] = jnp.sum(g, axis=0, keepdims=True)
        dcc_ref[...] = mm_nt(g, w_ref[...]) * _dsilu(cc_v)

    return pl.pallas_call(
        body, name="adaln_bwd",
        out_shape=(jax.ShapeDtypeStruct((8, D_MODEL), F32),
                   jax.ShapeDtypeStruct((D_MODEL, 3 * D_MODEL), BF16),
                   jax.ShapeDtypeStruct((1, 3 * D_MODEL), F32)),
        compiler_params=pltpu.CompilerParams(vmem_limit_bytes=VMEM_LIMIT),
    )(cc, w_ada, dmod)


def _seg_spec(which, n_ctx_tiles):
    return pl.BlockSpec((None, None, 1, D_MODEL),
                        lambda i: (jnp.where(i < n_ctx_tiles, 0, 1), which, 0, 0))


def _row_spec(width, col_block=0, tile=TOK):
    return pl.BlockSpec((tile, width), lambda i: (i, col_block))


def _const_spec(shape):
    zeros = (0,) * len(shape)
    return pl.BlockSpec(shape, lambda i: zeros)


def _modnorm_fn(x, g, shift, scale):
    y = x * lax.rsqrt(jnp.mean(x * x, axis=-1, keepdims=True) + EPS)
    return (y * g) * (1.0 + scale) + shift


def _modnorm_fwd(xt, g, mod4, n_ctx):
    t = xt.shape[0]
    nct = n_ctx // TOK

    def body(x_ref, g_ref, sh_ref, sc_ref, o_ref):
        o_ref[...] = _modnorm_fn(x_ref[...], g_ref[...], sh_ref[...], sc_ref[...]).astype(BF16)

    return pl.pallas_call(
        body, name="modnorm_fwd", grid=(t // TOK,),
        in_specs=[_row_spec(D_MODEL), _const_spec((1, D_MODEL)), _seg_spec(0, nct), _seg_spec(1, nct)],
        out_specs=_row_spec(D_MODEL),
        out_shape=jax.ShapeDtypeStruct((t, D_MODEL), BF16),
        compiler_params=_cparams(("parallel",)),
    )(xt, g, mod4, mod4)


def _modnorm_bwd(xt, g, mod4, dh, dres, n_ctx):
    t = xt.shape[0]
    nct = n_ctx // TOK

    def body(x_ref, g_ref, sh_ref, sc_ref, dh_ref, dres_ref, dx_ref, dg_ref, dsh_ref, dsc_ref):
        i = pl.program_id(0)
        _, vjp = jax.vjp(_modnorm_fn, x_ref[...], g_ref[...], sh_ref[...], sc_ref[...])
        dx, dg, dsh, dsc = vjp(dh_ref[...])
        dx_ref[...] = dx + dres_ref[...]

        @pl.when(i == 0)
        def _():
            dg_ref[...] = jnp.zeros_like(dg_ref)

        dg_ref[...] += dg
        first = jnp.logical_or(i == 0, i == nct)

        @pl.when(first)
        def _():
            dsh_ref[...] = dsh
            dsc_ref[...] = dsc

        @pl.when(jnp.logical_not(first))
        def _():
            dsh_ref[...] += dsh
            dsc_ref[...] += dsc

    seg_out = lambda which: pl.BlockSpec((None, None, 1, D_MODEL),
                                         lambda i: (jnp.where(i < nct, 0, 1), which, 0, 0))
    dx, dg, dss, dss2 = pl.pallas_call(
        body, name="modnorm_bwd", grid=(t // TOK,),
        in_specs=[_row_spec(D_MODEL), _const_spec((1, D_MODEL)), _seg_spec(0, nct), _seg_spec(1, nct),
                  _row_spec(D_MODEL), _row_spec(D_MODEL)],
        out_specs=[_row_spec(D_MODEL), _const_spec((1, D_MODEL)), seg_out(0), seg_out(0)],
        out_shape=(jax.ShapeDtypeStruct((t, D_MODEL), F32), jax.ShapeDtypeStruct((1, D_MODEL), F32),
                   jax.ShapeDtypeStruct((2, 1, 1, D_MODEL), F32), jax.ShapeDtypeStruct((2, 1, 1, D_MODEL), F32)),
        compiler_params=_cparams(("arbitrary",)),
    )(xt, g, mod4, mod4, dh, dres)
    return dx, dg, dss, dss2


def _group_mean_sq(x, gs):
    x2 = x * x
    hi = x2.astype(BF16).astype(F32)
    return mm(hi, gs) + mm(x2 - hi, gs)


def _head_norm(x, g, gs):
    return (x * lax.rsqrt(_group_mean_sq(x, gs) + EPS)) * g


def _rope(x, cos, sin_signed):
    lane = lax.broadcasted_iota(jnp.int32, (1, 128), 1)
    first_half = jnp.bitwise_and(lane, 63) < 32
    cols = []
    for c in range(x.shape[1] // 128):
        xb = x[:, 128 * c:128 * (c + 1)]
        partner = jnp.where(first_half, lane_roll(xb, 96), lane_roll(xb, 32))
        cols.append(xb * cos + partner * sin_signed)
    return cols[0] if len(cols) == 1 else jnp.concatenate(cols, axis=1)


def _prep_fn(zq_na, zk_na, zv_na, zq_g, zk_g, zv_g, g_naq, g_nak, g_gq, g_gk, cos, sin_signed, gs512, gs128, expand):
    q_na = _head_norm(zq_na, g_naq, gs512)
    k_na = _head_norm(zk_na, g_nak, gs512)
    q_g = _rope(_head_norm(zq_g, g_gq, gs512), cos, sin_signed)
    k_g = _rope(_head_norm(zk_g, g_gk, gs128), cos, sin_signed)
    return q_na, k_na, zv_na, q_g, mm(k_g, expand), mm(zv_g, expand)


def _prep_consts():
    gid = np.arange(512) // 64
    gs512 = (gid[:, None] == gid[None, :]).astype(np.float32) / 64.0
    expand = np.zeros((128, 512), np.float32)
    for h in range(N_HEADS):
        for j in range(64):
            expand[64 * (h // 4) + j, 64 * h + j] = 1.0
    return jnp.asarray(gs512), jnp.asarray(gs512[:128, :128]), jnp.asarray(expand)


def _prep_in_specs():
    blk = lambda off, w: _row_spec(w, off // w)
    return [blk(OFF["na_q"], 512), blk(OFF["na_k"], 512), blk(OFF["na_v"], 512), blk(OFF["gqa_q"], 512),
            blk(OFF["gqa_k"], 128), blk(OFF["gqa_v"], 128),
            _const_spec((1, 512)), _const_spec((1, 512)), _const_spec((1, 512)), _const_spec((1, 128)),
            _row_spec(128), _row_spec(128),
            _const_spec((512, 512)), _const_spec((128, 128)), _const_spec((128, 512))]


def _prep_fwd(z, gains, rope_tabs):
    t = z.shape[0]
    consts = _prep_consts()

    def body(*refs):
        ins, outs = refs[:15], refs[15:]
        res = _prep_fn(*[r[...] for r in ins])
        for o_ref, v in zip(outs, res):
            o_ref[...] = v.astype(BF16)

    return pl.pallas_call(
        body, name="prep_fwd", grid=(t // TOK,),
        in_specs=_prep_in_specs(),
        out_specs=[_row_spec(512)] * 6,
        out_shape=tuple(jax.ShapeDtypeStruct((t, 512), BF16) for _ in range(6)),
        compiler_params=_cparams(("parallel",)),
    )(z, z, z, z, z, z, *gains, *rope_tabs, *consts)


def _prep_bwd(z, gains, rope_tabs, dqs, dkv_lat, dkv_ctx, du_a, du_b, n_ctx):
    t = z.shape[0]
    nct = n_ctx // TOK
    consts = _prep_consts()

    def body(*refs):
        ins, dq_refs, lat_refs, ctx_refs = refs[:15], refs[15:17], refs[17:21], refs[21:25]
        (dua_ref, dub_ref), outs = refs[25:27], refs[27:]
        i = pl.program_id(0)
        vals = [r[...] for r in ins]
        _, vjp = jax.vjp(lambda *a: _prep_fn(*a, *vals[10:]), *vals[:10])
        kv = [jnp.where(i < nct, c_ref[...], l_ref[...]) for l_ref, c_ref in zip(lat_refs, ctx_refs)]
        grads = vjp((dq_refs[0][...], kv[0], kv[1], dq_refs[1][...], kv[2], kv[3]))
        for o_ref, v in zip(outs[:6], grads[:6]):
            o_ref[...] = v.astype(BF16)
        outs[6][...] = (dua_ref[...] + dub_ref[...]).astype(BF16)

        @pl.when(i == 0)
        def _():
            for o_ref in outs[7:]:
                o_ref[...] = jnp.zeros_like(o_ref)

        for o_ref, v in zip(outs[7:], grads[6:10]):
            o_ref[...] += v

    lat_spec = pl.BlockSpec((TOK, 512), lambda i: (jnp.maximum(i - nct + 1, 0), 0))
    ctx_spec = pl.BlockSpec((TOK, 512), lambda i: (jnp.minimum(i, nct - 1), 0))
    return pl.pallas_call(
        body, name="prep_bwd", grid=(t // TOK,),
        in_specs=_prep_in_specs() + [_row_spec(512)] * 2 + [lat_spec] * 4 + [ctx_spec] * 4 + [_row_spec(512)] * 2,
        out_specs=[_row_spec(512)] * 4 + [_row_spec(128)] * 2 + [_row_spec(512)]
        + [_const_spec((1, 512))] * 3 + [_const_spec((1, 128))],
        out_shape=tuple([jax.ShapeDtypeStruct((t, 512), BF16)] * 4 + [jax.ShapeDtypeStruct((t, 128), BF16)] * 2
                        + [jax.ShapeDtypeStruct((t, 512), BF16)]
                        + [jax.ShapeDtypeStruct((1, 512), F32)] * 3 + [jax.ShapeDtypeStruct((1, 128), F32)]),
        compiler_params=_cparams(("arbitrary",)),
    )(z, z, z, z, z, z, *gains, *rope_tabs, *consts, *dqs, *dkv_lat, *dkv_ctx, du_a, du_b)


def _s5post_fn(ys, u, d, w_glu):
    y = jax.nn.gelu(ys + d * u)
    return y * jax.nn.sigmoid(mm(y, w_glu))


def _s5post_fwd(ys, z, d, w_glu):
    t = z.shape[0]

    def body(ys_ref, u_ref, d_ref, w_ref, o_ref):
        o_ref[...] = _s5post_fn(ys_ref[...], u_ref[...], d_ref[...], w_ref[...])

    return pl.pallas_call(
        body, name="s5post_fwd", grid=(t // TOK,),
        in_specs=[_row_spec(512), _row_spec(512, OFF["s5_u"] // 512),
                  _const_spec((1, 512)), _const_spec((512, 512))],
        out_specs=_row_spec(512), out_shape=jax.ShapeDtypeStruct((t, 512), F32),
        compiler_params=_cparams(("parallel",)),
    )(ys, z, d, w_glu)


def _s5post_bwd(ys, z, d, w_glu, dy):
    t = z.shape[0]

    def body(ys_ref, u_ref, d_ref, w_ref, dy_ref, dpre_ref, du_ref, dd_ref, dw_ref):
        i = pl.program_id(0)
        _, vjp = jax.vjp(_s5post_fn, ys_ref[...], u_ref[...], d_ref[...], w_ref[...].astype(F32))
        dys, du, dd, dw = vjp(dy_ref[...])
        dpre_ref[...] = dys.astype(BF16)
        du_ref[...] = du

        @pl.when(i == 0)
        def _():
            dd_ref[...] = jnp.zeros_like(dd_ref)
            dw_ref[...] = jnp.zeros_like(dw_ref)

        dd_ref[...] += dd
        dw_ref[...] += dw

    return pl.pallas_call(
        body, name="s5post_bwd", grid=(t // TOK,),
        in_specs=[_row_spec(512), _row_spec(512, OFF["s5_u"] // 512),
                  _const_spec((1, 512)), _const_spec((512, 512)), _row_spec(512)],
        out_specs=[_row_spec(512), _row_spec(512), _const_spec((1, 512)), _const_spec((512, 512))],
        out_shape=(jax.ShapeDtypeStruct((t, 512), BF16), jax.ShapeDtypeStruct((t, 512), F32),
                   jax.ShapeDtypeStruct((1, 512), F32), jax.ShapeDtypeStruct((512, 512), F32)),
        compiler_params=_cparams(("arbitrary",)),
    )(ys, z, d, w_glu, dy)


def _halo_specs(col_block, t):
    last = t // 8 - 1
    prev = pl.BlockSpec((8, 512), lambda i: (jnp.maximum(i * (TOK // 8) - 1, 0), col_block))
    nxt = pl.BlockSpec((8, 512), lambda i: (jnp.minimum((i + 1) * (TOK // 8), last), col_block))
    return [_row_spec(512, col_block), prev, nxt]


def _shifted(cur, prev_row, next_row, tok0, n_ctx, t_total):
    row = lax.broadcasted_iota(jnp.int32, (TOK, 1), 0)
    tpos = row + tok0
    down = jnp.where(row == 0, prev_row, pltpu.roll(cur, 1, 0))
    down = jnp.where(jnp.logical_or(tpos == 0, tpos == n_ctx), 0.0, down)
    up = jnp.where(row == TOK - 1, next_row, pltpu.roll(cur, TOK - 1, 0))
    up = jnp.where(jnp.logical_or(tpos == n_ctx - 1, tpos == t_total - 1), 0.0, up)
    return down, up


def _conv_fwd(z, conv_w, conv_b, n_ctx):
    t = z.shape[0]

    def body(v_ref, vp_ref, vn_ref, c_ref, cp_ref, cn_ref, b_ref, w_ref, cb_ref, o_ref):
        tok0 = pl.program_id(0) * TOK
        zz = v_ref[...] * c_ref[...]
        zz_m1, zz_p1 = _shifted(zz, vp_ref[7:8, :] * cp_ref[7:8, :], vn_ref[0:1, :] * cn_ref[0:1, :], tok0, n_ctx, t)
        s = cb_ref[...] + zz_m1 * w_ref[0:1, :] + zz * w_ref[1:2, :] + zz_p1 * w_ref[2:3, :]
        o_ref[...] = b_ref[...] * s

    return pl.pallas_call(
        body, name="conv_fwd", grid=(t // TOK,),
        in_specs=_halo_specs(OFF["conv_v"] // 512, t) + _halo_specs(OFF["conv_c"] // 512, t)
        + [_row_spec(512, OFF["conv_b"] // 512), _const_spec((8, 512)), _const_spec((1, 512))],
        out_specs=_row_spec(512), out_shape=jax.ShapeDtypeStruct((t, 512), F32),
        compiler_params=_cparams(("parallel",)),
    )(z, z, z, z, z, z, z, conv_w, conv_b)


def _conv_bwd(z, conv_w, conv_b, dy, n_ctx):
    t = z.shape[0]

    def body(v_ref, vp_ref, vn_ref, c_ref, cp_ref, cn_ref, b_ref, bp_ref, bn_ref, dy_ref, dyp_ref, dyn_ref,
             w_ref, cb_ref, dv_ref, db_ref, dc_ref, dw_ref, dcb_ref):
        i = pl.program_id(0)
        tok0 = i * TOK
        v, c, b, dy_v = v_ref[...], c_ref[...], b_ref[...], dy_ref[...]
        w0, w1, w2 = w_ref[0:1, :], w_ref[1:2, :], w_ref[2:3, :]
        zz = v * c
        zz_m1, zz_p1 = _shifted(zz, vp_ref[7:8, :] * cp_ref[7:8, :], vn_ref[0:1, :] * cn_ref[0:1, :], tok0, n_ctx, t)
        s = cb_ref[...] + zz_m1 * w0 + zz * w1 + zz_p1 * w2
        ds = dy_v * b
        ds_m1, ds_p1 = _shifted(ds, dyp_ref[7:8, :] * bp_ref[7:8, :], dyn_ref[0:1, :] * bn_ref[0:1, :], tok0, n_ctx, t)
        dzz = ds_p1 * w0 + ds * w1 + ds_m1 * w2
        db_ref[...] = (dy_v * s).astype(BF16)
        dv_ref[...] = (dzz * c).astype(BF16)
        dc_ref[...] = (dzz * v).astype(BF16)

        @pl.when(i == 0)
        def _():
            dw_ref[...] = jnp.zeros_like(dw_ref)
            dcb_ref[...] = jnp.zeros_like(dcb_ref)

        rsum = lambda a: jnp.sum(a, axis=0, keepdims=True)
        dw_ref[0:1, :] += rsum(ds * zz_m1)
        dw_ref[1:2, :] += rsum(ds * zz)
        dw_ref[2:3, :] += rsum(ds * zz_p1)
        dcb_ref[...] += rsum(ds)

    return pl.pallas_call(
        body, name="conv_bwd", grid=(t // TOK,),
        in_specs=_halo_specs(OFF["conv_v"] // 512, t) + _halo_specs(OFF["conv_c"] // 512, t)
        + _halo_specs(OFF["conv_b"] // 512, t) + _halo_specs(0, t) + [_const_spec((8, 512)), _const_spec((1, 512))],
        out_specs=[_row_spec(512)] * 3 + [_const_spec((8, 512)), _const_spec((1, 512))],
        out_shape=tuple([jax.ShapeDtypeStruct((t, 512), BF16)] * 3
                        + [jax.ShapeDtypeStruct((8, 512), F32), jax.ShapeDtypeStruct((1, 512), F32)]),
        compiler_params=_cparams(("arbitrary",)),
    )(z, z, z, z, z, z, z, z, z, dy, dy, dy, conv_w, conv_b)


def _merge_col_specs(tile):
    specs = []
    for off in MERGE_OFFS:
        specs.append(pl.BlockSpec((tile, 512), functools.partial(lambda i, cb: (i, cb), cb=off // 512)))
        specs.append(pl.BlockSpec((tile, 512), functools.partial(lambda i, cb: (i, cb), cb=off // 512 + 1)))
    return specs


def _merge_fwd(xt, ys, z, mod4, w_br, w_out, n_ctx):
    t = xt.shape[0]
    nct = n_ctx // TOK

    def body(x_ref, *refs):
        y_refs, gt_refs, mg_refs = refs[0:4], refs[4:8], refs[8:16]
        gate_ref, wbr_ref, wout_ref, o_ref = refs[16:20]
        acc_lo = acc_hi = None
        for k in range(4):
            gated = y_refs[k][...] * _silu(gt_refs[k][...])
            proj = mm(gated, wbr_ref[k])
            lo = jax.nn.sigmoid(mg_refs[2 * k][...]) * proj[:, :512]
            hi = jax.nn.sigmoid(mg_refs[2 * k + 1][...]) * proj[:, 512:]
            acc_lo = lo if acc_lo is None else acc_lo + lo
            acc_hi = hi if acc_hi is None else acc_hi + hi
        acc = jnp.concatenate([acc_lo, acc_hi], axis=1)
        o_ref[...] = x_ref[...] + gate_ref[...] * mm(acc, wout_ref[...])

    gate_specs = [pl.BlockSpec((TOK, 512), functools.partial(lambda i, cb: (i, cb), cb=o // 512)) for o in GATE_OFFS]
    return pl.pallas_call(
        body, name="merge_fwd", grid=(t // TOK,),
        in_specs=[_row_spec(D_MODEL)] + [_row_spec(512)] * 4 + gate_specs + _merge_col_specs(TOK)
        + [_seg_spec(2, nct), _const_spec((4, 512, 1024)), _const_spec((1024, 1024))],
        out_specs=_row_spec(D_MODEL), out_shape=jax.ShapeDtypeStruct((t, D_MODEL), F32),
        compiler_params=_cparams(("parallel",)),
    )(xt, *ys, z, z, z, z, z, z, z, z, z, z, z, z, mod4, w_br, w_out)


MERGE_BWD_TILE = 128


def _merge_bwd(g, ys, z, mod4, w_br, w_out, n_ctx):
    t = g.shape[0]
    tile = MERGE_BWD_TILE
    nct = n_ctx // tile
    nsteps = t // tile

    def body(g_ref, *refs):
        y_refs, gt_refs, mg_refs = refs[0:4], refs[4:8], refs[8:16]
        gate_ref, wbr_hbm, wout_hbm = refs[16:19]
        dy_refs, dgt_refs, dmg_refs = refs[19:23], refs[23:27], refs[27:31]
        dgate_ref, dwbr_hbm, dwout_hbm = refs[31:34]
        wbr_v, wout_v, dwbr_acc, dwout_acc = refs[34:38]
        i = pl.program_id(0)

        @pl.when(i == 0)
        def _():
            pltpu.sync_copy(wbr_hbm, wbr_v)
            pltpu.sync_copy(wout_hbm, wout_v)
            dwbr_acc[...] = jnp.zeros_like(dwbr_acc)
            dwout_acc[...] = jnp.zeros_like(dwout_acc)

        g_v, gate = g_ref[...], gate_ref[...]
        gated, proj, sig = [], [], []
        acc = None
        for k in range(4):
            gated.append(y_refs[k][...] * _silu(gt_refs[k][...]))
            proj.append(mm(gated[k], wbr_v[k]))
            sig.append(jax.nn.sigmoid(jnp.concatenate([mg_refs[2 * k][...], mg_refs[2 * k + 1][...]], axis=1)))
            contrib = sig[k] * proj[k]
            acc = contrib if acc is None else acc + contrib
        o = mm(acc, wout_v[...])
        dgate = jnp.sum(g_v * o, axis=0, keepdims=True)
        first = jnp.logical_or(i == 0, i == nct)

        @pl.when(first)
        def _():
            dgate_ref[...] = dgate

        @pl.when(jnp.logical_not(first))
        def _():
            dgate_ref[...] += dgate

        do = g_v * gate
        dwout_acc[...] += mm_tn(acc, do)
        dacc = mm_nt(do, wout_v[...])
        for k in range(4):
            dmg_refs[k][...] = (dacc * proj[k] * sig[k] * (1.0 - sig[k])).astype(BF16)
            dproj = dacc * sig[k]
            dwbr_acc[k] += mm_tn(gated[k], dproj)
            dgated = mm_nt(dproj, wbr_v[k])
            gt = gt_refs[k][...]
            dy_refs[k][...] = dgated * _silu(gt)
            dgt_refs[k][...] = (dgated * y_refs[k][...] * _dsilu(gt)).astype(BF16)

        @pl.when(i == nsteps - 1)
        def _():
            wbr_v[...] = dwbr_acc[...].astype(BF16)
            wout_v[...] = dwout_acc[...].astype(BF16)
            pltpu.sync_copy(wbr_v, dwbr_hbm)
            pltpu.sync_copy(wout_v, dwout_hbm)

    row = lambda w: _row_spec(w, 0, tile)
    gate_specs = [pl.BlockSpec((tile, 512), functools.partial(lambda i, cb: (i, cb), cb=o // 512)) for o in GATE_OFFS]
    anyspec = pl.BlockSpec(memory_space=pl.ANY)
    seg = pl.BlockSpec((None, None, 1, D_MODEL), lambda i: (jnp.where(i < nct, 0, 1), 2, 0, 0))
    seg_out = pl.BlockSpec((None, None, 1, D_MODEL), lambda i: (jnp.where(i < nct, 0, 1), 0, 0, 0))
    res = pl.pallas_call(
        body, name="merge_bwd", grid=(nsteps,),
        in_specs=[row(D_MODEL)] + [row(512)] * 4 + gate_specs + _merge_col_specs(tile) + [seg, anyspec, anyspec],
        out_specs=[row(512)] * 8 + [row(1024)] * 4 + [seg_out, anyspec, anyspec],
        out_shape=tuple([jax.ShapeDtypeStruct((t, 512), F32)] * 4 + [jax.ShapeDtypeStruct((t, 512), BF16)] * 4
                        + [jax.ShapeDtypeStruct((t, 1024), BF16)] * 4
                        + [jax.ShapeDtypeStruct((2, 1, 1, D_MODEL), F32),
                           jax.ShapeDtypeStruct((4, 512, 1024), BF16), jax.ShapeDtypeStruct((1024, 1024), BF16)]),
        scratch_shapes=[pltpu.VMEM((4, 512, 1024), BF16), pltpu.VMEM((1024, 1024), BF16),
                        pltpu.VMEM((4, 512, 1024), F32), pltpu.VMEM((1024, 1024), F32)],
        compiler_params=_cparams(("arbitrary",)),
    )(g, *ys, z, z, z, z, z, z, z, z, z, z, z, z, mod4, w_br, w_out)
    return res[0:4], res[4:8], res[8:12], res[12], res[13], res[14]


FWD_ROWS = slice(0, S5_GROUPS)
BWD_ROWS = slice(S5_GROUPS, 2 * S5_GROUPS)


def _backward_chunk(j, k, n_ctx_chunks):
    return jnp.where(j < n_ctx_chunks, n_ctx_chunks - 1 - j, k - 1 - (j - n_ctx_chunks))


def _scan_call(body, name, n_hbm_in, out_shape, kp):
    hbm, vmem = pl.BlockSpec(memory_space=pl.ANY), pl.BlockSpec(memory_space=pltpu.VMEM)
    return pl.pallas_call(
        body, name=name, in_specs=[hbm] * n_hbm_in + [vmem, vmem],
        out_specs=[hbm] + [vmem] * (len(out_shape) - 1), out_shape=out_shape,
        scratch_shapes=[pltpu.VMEM((kp, 64, 128), F32), pltpu.VMEM((kp, 64, 128), F32)],
        compiler_params=pltpu.CompilerParams(vmem_limit_bytes=VMEM_LIMIT))


def _complex_step(a1, a2, h):
    return a1 * h + a2 * pltpu.roll(h, 64, 1)


def _s5_scan_fwd(s, a1, a2, k, n_ctx_chunks):
    kp = s.shape[0]

    def body(s_hbm, a1_ref, a2_ref, hp_hbm, s_v, hp_v):
        pltpu.sync_copy(s_hbm, s_v)
        if kp > k:
            hp_v[k:kp] = jnp.zeros((kp - k, 64, 128), F32)
        a1f, a2f, a1b, a2b = a1_ref[FWD_ROWS, :], a2_ref[FWD_ROWS, :], a1_ref[BWD_ROWS, :], a2_ref[BWD_ROWS, :]

        def step(j, carry):
            hf, hb = carry
            cb = _backward_chunk(j, k, n_ctx_chunks)
            hp_v[j, FWD_ROWS, :] = hf
            hp_v[cb, BWD_ROWS, :] = hb
            return (_complex_step(a1f, a2f, hf) + s_v[j, FWD_ROWS, :],
                    _complex_step(a1b, a2b, hb) + s_v[cb, BWD_ROWS, :])

        zero = jnp.zeros((S5_GROUPS, 128), F32)
        lax.fori_loop(0, k, step, (zero, zero))
        pltpu.sync_copy(hp_v, hp_hbm)

    return _scan_call(body, "s5_scan_fwd", 1, (jax.ShapeDtypeStruct(s.shape, F32),), kp)(s, a1, a2)[0]


def _s5_scan_bwd(dhp, hp, a1, a2, k, n_ctx_chunks):
    kp = hp.shape[0]

    def body(dhp_hbm, hp_hbm, a1_ref, a2_ref, ds_hbm, da1_ref, da2_ref, g_v, hp_v):
        pltpu.sync_copy(dhp_hbm, g_v)
        pltpu.sync_copy(hp_hbm, hp_v)
        if kp > k:
            g_v[k:kp] = jnp.zeros((kp - k, 64, 128), F32)
        coef_f = (a1_ref[FWD_ROWS, :], a2_ref[FWD_ROWS, :])
        coef_b = (a1_ref[BWD_ROWS, :], a2_ref[BWD_ROWS, :])

        def one(rows, c, lam, d1, d2):
            a1_v, a2_v = coef_f if rows is FWD_ROWS else coef_b
            dh_in = g_v[c, rows, :]
            g_v[c, rows, :] = lam
            h = hp_v[c, rows, :]
            return (dh_in + a1_v * lam + pltpu.roll(a2_v * lam, 64, 1),
                    d1 + lam * h, d2 + lam * pltpu.roll(h, 64, 1))

        def step(j, carry):
            f, b = carry
            jj = k - 1 - j
            return one(FWD_ROWS, jj, *f), one(BWD_ROWS, _backward_chunk(jj, k, n_ctx_chunks), *b)

        zero = jnp.zeros((S5_GROUPS, 128), F32)
        f, b = lax.fori_loop(0, k, step, ((zero, zero, zero), (zero, zero, zero)))
        da1_ref[FWD_ROWS, :], da2_ref[FWD_ROWS, :] = f[1], f[2]
        da1_ref[BWD_ROWS, :], da2_ref[BWD_ROWS, :] = b[1], b[2]
        pltpu.sync_copy(g_v, ds_hbm)

    shapes = (jax.ShapeDtypeStruct(hp.shape, F32), jax.ShapeDtypeStruct((64, 128), F32),
              jax.ShapeDtypeStruct((64, 128), F32))
    return _scan_call(body, "s5_scan_bwd", 2, shapes, kp)(dhp, hp, a1, a2)


@functools.lru_cache(maxsize=None)
def _window_patterns(mode, n_lat):
    nb = n_lat // ATT_BLK
    assert nb >= 3
    iq, ik = np.arange(ATT_BLK)[:, None], np.arange(3 * ATT_BLK)[None, :]

    def valid(ql):
        if mode == "na":
            r, qcol = 4 * ql + iq // GRID_W, iq % GRID_W
            kr, kcol = 4 * (ql - 1) + ik // GRID_W, ik % GRID_W
            rs = np.clip(r - NA_ROWS // 2, 0, n_lat // GRID_W - NA_ROWS)
            cs = np.clip(qcol - NA_COLS // 2, 0, GRID_W - NA_COLS)
            return (kr >= rs) & (kr < rs + NA_ROWS) & (kcol >= cs) & (kcol < cs + NA_COLS)
        tq, ts = ATT_BLK * ql + iq, ATT_BLK * (ql - 1) + ik
        return (np.abs(tq - ts) <= WINDOW) & (ts >= 0) & (ts < n_lat)

    interior = valid(1)
    assert all(np.array_equal(valid(ql), interior) for ql in range(1, nb - 1))
    return np.stack([valid(0), interior, valid(nb - 1), np.zeros_like(interior)])


def _pattern_of_block(qb, nqb):
    return jnp.where(qb == 0, 3, jnp.where(qb == 1, 0, jnp.where(qb == nqb - 1, 2, 1)))


def _attn_block(q, k3, v3, kc, vc, bias0, bias1, sink):
    lane = lax.broadcasted_iota(jnp.int32, (1, 128), 1)
    scale = HEAD_DIM ** -0.5
    outs = []
    for e, bias in enumerate((bias0, bias1)):
        in_head = (lane < 64) if e == 0 else (lane >= 64)
        qe = jnp.where(in_head, q, 0.0)
        s_lat = mm_nt(qe, k3) * scale + bias
        s_ctx = mm_nt(qe, kc) * scale
        mx = jnp.maximum(jnp.max(s_lat, axis=1, keepdims=True), jnp.max(s_ctx, axis=1, keepdims=True))
        if sink is not None:
            srow = lax.broadcasted_iota(jnp.int32, sink.shape, 0)
            sv = jnp.sum(jnp.where(srow == e, sink, 0.0), keepdims=True) * (1.0 / 128.0)
            mx = jnp.maximum(mx, sv)
        mx = lax.stop_gradient(mx)
        e_lat = jnp.exp(s_lat - mx)
        e_ctx = jnp.exp(s_ctx - mx)
        den = jnp.sum(e_lat, axis=1, keepdims=True) + jnp.sum(e_ctx, axis=1, keepdims=True)
        if sink is not None:
            den = den + jnp.exp(sv - mx)
        inv = 1.0 / den
        outs.append(mm(e_lat * inv, v3) + mm(e_ctx * inv, vc))
    return jnp.where(lane < 64, outs[0], outs[1])


def _attn_specs(n_ctx, nqb, per_head):
    def kwin(s):
        return pl.BlockSpec((ATT_BLK, 128), lambda hp, qb: (jnp.maximum(qb - 1, 0) + s, hp))

    q = pl.BlockSpec((ATT_BLK, 128), lambda hp, qb: (qb, hp))
    ctx = pl.BlockSpec((n_ctx, 128), lambda hp, qb: (0, hp))
    if per_head:
        bias = pl.BlockSpec((None, None, 2, ATT_BLK, 3 * ATT_BLK),
                            lambda hp, qb: (_pattern_of_block(qb, nqb), hp, 0, 0, 0))
    else:
        bias = pl.BlockSpec((None, ATT_BLK, 3 * ATT_BLK), lambda hp, qb: (_pattern_of_block(qb, nqb), 0, 0))
    sink = pl.BlockSpec((None, 8, 128), lambda hp, qb: (hp, 0, 0))
    return q, [kwin(0), kwin(1), kwin(2)], ctx, bias, sink


def _attn_fwd(q, kpad, vpad, kc, vc, bias, sink, *, mode, n_ctx):
    t = q.shape[0]
    per_head = bias.ndim == 5
    qs, kws, ctx, bias_s, sink_s = _attn_specs(n_ctx, t // ATT_BLK, per_head)
    has_sink = sink is not None

    def body(*refs):
        q_ref, k_refs, v_refs, kc_ref, vc_ref, b_ref = refs[0], refs[1:4], refs[4:7], refs[7], refs[8], refs[9]
        s_ref = refs[10] if has_sink else None
        o_ref = refs[-1]
        k3 = jnp.concatenate([r[...] for r in k_refs], axis=0)
        v3 = jnp.concatenate([r[...] for r in v_refs], axis=0)
        b0, b1 = (b_ref[0], b_ref[1]) if per_head else (b_ref[...], b_ref[...])
        o_ref[...] = _attn_block(q_ref[...], k3, v3, kc_ref[...], vc_ref[...], b0, b1,
                                 s_ref[...] if has_sink else None)

    in_specs = [qs] + kws + kws + [ctx, ctx, bias_s] + ([sink_s] if has_sink else [])
    args = [q, kpad, kpad, kpad, vpad, vpad, vpad, kc, vc, bias] + ([sink] if has_sink else [])
    return pl.pallas_call(
        body, name=mode + "_attn_fwd", grid=(4, t // ATT_BLK),
        in_specs=in_specs, out_specs=qs, out_shape=jax.ShapeDtypeStruct((t, 512), F32),
        compiler_params=_cparams(("parallel", "parallel")),
    )(*args)


def _attn_bwd(q, kpad, vpad, kc, vc, bias, sink, do, *, mode, n_ctx):
    t = q.shape[0]
    nqb = t // ATT_BLK
    per_head = bias.ndim == 5
    qs, kws, ctx, bias_s, sink_s = _attn_specs(n_ctx, nqb, per_head)
    has_sink = sink is not None
    n_in = 11 + has_sink

    def body(*refs):
        q_ref, k_refs, v_refs, kc_ref, vc_ref, b_ref = refs[0], refs[1:4], refs[4:7], refs[7], refs[8], refs[9]
        s_ref = refs[10] if has_sink else None
        do_ref = refs[n_in - 1]
        outs = list(refs[n_in:-2])
        dk_acc, dv_acc = refs[-2:]
        dq_ref, dkp_ref, dvp_ref, dkc_ref, dvc_ref = outs[:5]
        hp = pl.program_id(0)
        outs = outs[5:]
        db_ref = outs.pop(0) if per_head else None
        ds_ref = outs.pop(0) if has_sink else None
        qb = pl.program_id(1)
        up = lambda r: r[...].astype(F32)
        k3 = jnp.concatenate([up(r) for r in k_refs], axis=0)
        v3 = jnp.concatenate([up(r) for r in v_refs], axis=0)
        prim = [up(q_ref), k3, v3, up(kc_ref), up(vc_ref)]
        if per_head:
            prim += [b_ref[0], b_ref[1]]
        if has_sink:
            prim += [s_ref[...]]

        def fn(*a):
            a = list(a)
            qv, k3v, v3v, kcv, vcv = a[:5]
            a = a[5:]
            b0 = a.pop(0) if per_head else b_ref[...]
            b1 = a.pop(0) if per_head else b0
            sk = a.pop(0) if has_sink else None
            return _attn_block(qv, k3v, v3v, kcv, vcv, b0, b1, sk)

        _, vjp = jax.vjp(fn, *prim)
        grads = list(vjp(do_ref[...]))
        dq_ref[...] = grads[0]

        @pl.when(qb == 0)
        def _():
            dk_acc[...] = jnp.zeros_like(dk_acc)
            dv_acc[...] = jnp.zeros_like(dv_acc)
            dkc_ref[...] = jnp.zeros_like(dkc_ref)
            dvc_ref[...] = jnp.zeros_like(dvc_ref)
            if has_sink:
                ds_ref[...] = jnp.zeros_like(ds_ref)

        window = pl.ds(pl.multiple_of(jnp.maximum(qb - 1, 0) * ATT_BLK, ATT_BLK), 3 * ATT_BLK)
        dk_acc[window, :] += grads[1]
        dv_acc[window, :] += grads[2]

        @pl.when(qb == nqb - 1)
        def _():
            cols = pl.ds(pl.multiple_of(hp * 128, 128), 128)
            pltpu.sync_copy(dk_acc, dkp_ref.at[:, cols])
            pltpu.sync_copy(dv_acc, dvp_ref.at[:, cols])

        dkc_ref[...] += grads[3]
        dvc_ref[...] += grads[4]
        rest_g = grads[5:]
        if per_head:
            opens = (qb <= 2) | (qb == nqb - 1)
            g0, g1 = rest_g.pop(0), rest_g.pop(0)

            @pl.when(opens)
            def _():
                db_ref[0] = g0
                db_ref[1] = g1

            @pl.when(jnp.logical_not(opens))
            def _():
                db_ref[0] += g0
                db_ref[1] += g1

        if has_sink:
            ds_ref[...] += rest_g.pop(0)

    hbm = pl.BlockSpec(memory_space=pl.ANY)
    in_specs = [qs] + kws + kws + [ctx, ctx, bias_s] + ([sink_s] if has_sink else []) + [qs]
    args = [q, kpad, kpad, kpad, vpad, vpad, vpad, kc, vc, bias] + ([sink] if has_sink else []) + [do]
    out_specs = [qs, hbm, hbm, ctx, ctx] + ([bias_s] if per_head else []) + ([sink_s] if has_sink else [])
    out_shape = [jax.ShapeDtypeStruct((t, 512), F32),
                 jax.ShapeDtypeStruct(kpad.shape, F32), jax.ShapeDtypeStruct(kpad.shape, F32),
                 jax.ShapeDtypeStruct((n_ctx, 512), F32), jax.ShapeDtypeStruct((n_ctx, 512), F32)]
    if per_head:
        out_shape.append(jax.ShapeDtypeStruct(bias.shape, F32))
    if has_sink:
        out_shape.append(jax.ShapeDtypeStruct((4, 8, 128), F32))
    res = list(pl.pallas_call(
        body, name=mode + "_attn_bwd", grid=(4, nqb),
        in_specs=in_specs, out_specs=out_specs, out_shape=tuple(out_shape),
        scratch_shapes=[pltpu.VMEM((kpad.shape[0], 128), F32), pltpu.VMEM((kpad.shape[0], 128), F32)],
        compiler_params=_cparams(("arbitrary", "arbitrary")),
    )(*args))
    dq, dkp, dvp, dkc, dvc = res[:5]
    res = res[5:]
    dbias = res.pop(0) if per_head else None
    dsink = res.pop(0) if has_sink else None
    return dq, dkp, dvp, dkc, dvc, dbias, dsink


def _loss_head(xt, target, n_ctx):
    t = xt.shape[0]
    nct = n_ctx // TOK

    def body(x_ref, t_ref, l_ref, d_ref):
        i = pl.program_id(0)

        @pl.when(i == 0)
        def _():
            l_ref[...] = jnp.zeros_like(l_ref)

        @pl.when(i < nct)
        def _():
            d_ref[...] = jnp.zeros_like(d_ref)

        @pl.when(i >= nct)
        def _():
            err = x_ref[...] - t_ref[...]
            d_ref[...] = err * (1.0 / D_MODEL)
            l_ref[...] += jnp.sum(err * err, keepdims=True) * (0.5 / D_MODEL)

    return pl.pallas_call(
        body, name="loss_head", grid=(t // TOK,),
        in_specs=[_row_spec(D_MODEL), pl.BlockSpec((TOK, D_MODEL), lambda i: (jnp.maximum(i - nct, 0), 0))],
        out_specs=[_const_spec((8, 128)), _row_spec(D_MODEL)],
        out_shape=(jax.ShapeDtypeStruct((8, 128), F32), jax.ShapeDtypeStruct((t, D_MODEL), F32)),
        compiler_params=_cparams(("arbitrary",)),
    )(xt, target)


PACK_W = 1024
SUM_STEPS = 8


def _sum_chips(recvs):
    def split(a):
        rows = a.shape[1]
        if rows % (8 * SUM_STEPS):
            return None
        return rows // SUM_STEPS

    def body(*refs):
        n = len(refs) // 2
        for r_ref, o_ref in zip(refs[:n], refs[n:]):
            up = lambda s: r_ref[s].astype(F32)
            o_ref[...] = ((up(0) + up(1)) + up(2)) + up(3)

    in_specs, out_specs = [], []
    for a in recvs:
        rb, tail = split(a), a.shape[2:]
        zeros = (0,) * len(tail)
        if rb is None:
            in_specs.append(pl.BlockSpec(a.shape, functools.partial(lambda i, z: (0, 0) + z, z=zeros)))
            out_specs.append(pl.BlockSpec(a.shape[1:], functools.partial(lambda i, z: (0,) + z, z=zeros)))
        else:
            in_specs.append(pl.BlockSpec((4, rb) + tail, functools.partial(lambda i, z: (0, i) + z, z=zeros)))
            out_specs.append(pl.BlockSpec((rb,) + tail, functools.partial(lambda i, z: (i,) + z, z=zeros)))
    return pl.pallas_call(
        body, name="sum_chips", grid=(SUM_STEPS,),
        in_specs=in_specs, out_specs=out_specs,
        out_shape=tuple(jax.ShapeDtypeStruct(a.shape[1:], F32) for a in recvs),
        compiler_params=_cparams(("arbitrary",)),
    )(*recvs)


ADAM_BLOCK_BYTES = 1 << 20


def _adamw(p_a, p_b, w, m, v, name):
    layers, rows, cols = w.shape
    tr = rows
    while tr % 16 == 0 and tr * cols * 4 > ADAM_BLOCK_BYTES:
        tr //= 2
    c1 = 1.0 / (1.0 - ADAM_B1 ** ADAM_STEP)
    c2 = 1.0 / (1.0 - ADAM_B2 ** ADAM_STEP)

    def body(a_ref, b_ref, w_ref, m_ref, v_ref, g_ref, d_ref, nm_ref, nv_ref):
        g = a_ref[...] + b_ref[...]
        nm = ADAM_B1 * m_ref[...] + (1.0 - ADAM_B1) * g
        nv = ADAM_B2 * v_ref[...] + (1.0 - ADAM_B2) * (g * g)
        g_ref[...] = g
        nm_ref[...] = nm
        nv_ref[...] = nv
        d_ref[...] = -ADAM_LR * ((nm * c1) / (jnp.sqrt(nv * c2) + ADAM_EPS) + ADAM_WD * w_ref[...])

    spec = pl.BlockSpec((None, tr, cols), lambda l, i: (l, i, 0))
    return pl.pallas_call(
        body, name=name, grid=(layers, rows // tr),
        in_specs=[spec] * 5, out_specs=[spec] * 4,
        out_shape=tuple(jax.ShapeDtypeStruct(w.shape, F32) for _ in range(4)),
        compiler_params=_cparams(("parallel", "parallel")),
    )(p_a, p_b, w, m, v)


MESH = pl.DeviceIdType.MESH
ANY_SPEC = pl.BlockSpec(memory_space=pl.ANY)


def _chip_exchange(srcs, out_shapes, src_window, dst_window, name):
    n = len(srcs)

    def body(*refs):
        src_refs, out_refs = refs[:n], refs[n:2 * n]
        send_sems, recv_sems, local_sems = refs[2 * n:]
        x, y, c = lax.axis_index("x"), lax.axis_index("y"), lax.axis_index("c")
        me = 2 * x + y
        peers = [(x, 1 - y), (1 - x, y), (1 - x, 1 - y)]

        def copy(k, j, from_chip, to_chip):
            px, py = peers[j]
            return pltpu.make_async_remote_copy(
                src_ref=src_window(k, src_refs[k], to_chip), dst_ref=dst_window(k, out_refs[k], from_chip),
                send_sem=send_sems.at[3 * k + j], recv_sem=recv_sems.at[3 * k + j],
                device_id=(px, py, c), device_id_type=MESH)

        local = [pltpu.make_async_copy(src_window(k, src_refs[k], me), dst_window(k, out_refs[k], me),
                                       local_sems.at[k]) for k in range(n)]
        for cp in local:
            cp.start()
        sends = [copy(k, j, me, 2 * px + py) for k in range(n) for j, (px, py) in enumerate(peers)]
        for cp in sends:
            cp.start()
        for k in range(n):
            for j, (px, py) in enumerate(peers):
                copy(k, j, 2 * px + py, me).wait_recv()
        for cp in sends:
            cp.wait_send()
        for cp in local:
            cp.wait()

    return pl.pallas_call(
        body, name=name, in_specs=[ANY_SPEC] * n, out_specs=[ANY_SPEC] * n,
        out_shape=tuple(out_shapes),
        scratch_shapes=[pltpu.SemaphoreType.DMA((3 * n,)), pltpu.SemaphoreType.DMA((3 * n,)),
                        pltpu.SemaphoreType.DMA((n,))],
    )(*srcs)


def _core_swap(srcs):
    n = len(srcs)

    def body(*refs):
        src_refs, out_refs, send_sems, recv_sems = refs[:n], refs[n:2 * n], refs[2 * n], refs[2 * n + 1]
        x, y, c = lax.axis_index("x"), lax.axis_index("y"), lax.axis_index("c")
        copies = [pltpu.make_async_remote_copy(
            src_ref=src_refs[k], dst_ref=out_refs[k], send_sem=send_sems.at[k], recv_sem=recv_sems.at[k],
            device_id=(x, y, 1 - c), device_id_type=MESH) for k in range(n)]
        for cp in copies:
            cp.start()
        for cp in copies:
            cp.wait()

    return pl.pallas_call(
        body, name="core_swap", in_specs=[ANY_SPEC] * n, out_specs=[ANY_SPEC] * n,
        out_shape=tuple(jax.ShapeDtypeStruct(s.shape, s.dtype) for s in srcs),
        scratch_shapes=[pltpu.SemaphoreType.DMA((n,)), pltpu.SemaphoreType.DMA((n,))],
    )(*srcs)


def _col_window(ref, start, size):
    idx = (slice(None),) * (len(ref.shape) - 1) + (pl.ds(pl.multiple_of(start, 128), size),)
    return ref.at[idx]


def _row_window(ref, start, size):
    idx = (slice(None),) * (len(ref.shape) - 2) + (pl.ds(pl.multiple_of(start, 8), size), slice(None))
    return ref.at[idx]


N_SHARD_IN = 2624
WIN_W = 2944
WIN_START = (0, 2560, 5248, 7808)
WIN_PIECES = (((0, 2624),), ((64, 2688),), ((0, 640), (896, 2880)), ((320, 2944),))


def _gather_weights(w_ada, w_in, w_glu, w_br, w_out, conv_w):
    lay = w_ada.shape[0]
    sizes = (768, None, 128, 256, 256, 128)

    def dst(k, ref, s):
        if k == 1:
            return ref.at[s]
        if k in (2, 4):
            return _row_window(ref, s * sizes[k], sizes[k])
        return _col_window(ref, s * sizes[k], sizes[k])

    shapes = (jax.ShapeDtypeStruct((lay, D_MODEL, 3 * D_MODEL), w_ada.dtype),
              jax.ShapeDtypeStruct((4,) + w_in.shape, w_in.dtype),
              jax.ShapeDtypeStruct((lay, MIX_W, MIX_W), w_glu.dtype),
              jax.ShapeDtypeStruct((lay, 4, MIX_W, D_MODEL), w_br.dtype),
              jax.ShapeDtypeStruct((lay, D_MODEL, D_MODEL), w_out.dtype),
              jax.ShapeDtypeStruct((lay, 8, MIX_W), conv_w.dtype))
    return _chip_exchange((w_ada, w_in, w_glu, w_br, w_out, conv_w), shapes,
                          lambda k, ref, t: ref, dst, "gather_weights")


def _scatter_grads(dw_ada, dw_in, dw_glu, dw_br, dw_out, dconv_w, small):
    def src(k, ref, t):
        if k == 0:
            return _col_window(ref, t * 768, 768)
        if k == 1:
            start = jnp.where(t == 0, WIN_START[0], jnp.where(t == 1, WIN_START[1],
                              jnp.where(t == 2, WIN_START[2], WIN_START[3])))
            return _col_window(ref, start, WIN_W)
        if k == 2:
            return _row_window(ref, t * 128, 128)
        if k == 3:
            return _col_window(ref, t * 256, 256)
        if k == 4:
            return _row_window(ref, t * 256, 256)
        if k == 5:
            return _col_window(ref, t * 128, 128)
        return ref

    pieces = ((D_MODEL, 768), (D_MODEL, WIN_W), (128, MIX_W), (4, MIX_W, 256), (256, D_MODEL), (8, 128), small.shape)
    srcs = (dw_ada, dw_in, dw_glu, dw_br, dw_out, dconv_w, small)
    shapes = tuple(jax.ShapeDtypeStruct((4,) + p, s.dtype) for p, s in zip(pieces, srcs))
    return _chip_exchange(srcs, shapes,
                          src, lambda k, ref, s: ref.at[s], "scatter_grads")


def _s5_tables(a_re, a_im, log_dt, b_re, b_im, c_re, c_im):
    ln = S5_CHUNK
    hi = lax.Precision.HIGHEST
    dt = jnp.exp(log_dt)[..., None]
    mag = jnp.exp(dt * a_re)
    abr = mag * jnp.cos(dt * a_im)
    abi = mag * jnp.sin(dt * a_im)
    den = a_re * a_re + a_im * a_im
    fr = ((abr - 1.0) * a_re + abi * a_im) / den
    fi = (abi * a_re - (abr - 1.0) * a_im) / den
    bbr = fr[..., None] * b_re - fi[..., None] * b_im
    bbi = fr[..., None] * b_im + fi[..., None] * b_re
    n = jnp.arange(ln + 1, dtype=F32)[:, None, None, None]
    pm = jnp.exp(n * dt * a_re)
    er = pm * jnp.cos(n * dt * a_im)
    ei = pm * jnp.sin(n * dt * a_im)
    e3 = lambda e, b, c: jnp.einsum("tdgp,dgpa,dgbp->dgabt", e, b, c, precision=hi)
    gt = e3(er[:ln], bbr, c_re) - e3(er[:ln], bbi, c_im) - e3(ei[:ln], bbr, c_im) - e3(ei[:ln], bbi, c_re)
    by_dir = lambda fwd, bwd: jnp.stack([fwd[:, 0], bwd[:, 1]], axis=1)
    erj, eij = by_dir(er[:ln][::-1], er[:ln]), by_dir(ei[:ln][::-1], ei[:ln])
    e2 = lambda e, b: jnp.einsum("jdgp,dgpa->dgajp", e, b, precision=hi)
    w = jnp.concatenate([e2(erj, bbr) - e2(eij, bbi), e2(erj, bbi) + e2(eij, bbr)], axis=-1)
    er1, ei1 = by_dir(er[1:], er[1:][::-1]), by_dir(ei[1:], ei[1:][::-1])
    ev = lambda c, e: jnp.einsum("dgbp,idgp->dgpbi", c, e, precision=hi)
    v = jnp.concatenate([ev(c_re, er1) - ev(c_im, ei1), -(ev(c_re, ei1) + ev(c_im, er1))], axis=2)
    a1 = jnp.concatenate([er[ln], er[ln]], axis=-1)
    a2 = jnp.concatenate([-ei[ln], ei[ln]], axis=-1)
    return (gt.transpose(1, 2, 3, 0, 4).reshape(S5_GROUPS, 256, 2 * ln),
            w.transpose(1, 2, 3, 0, 4).reshape(S5_GROUPS, S5_CH * ln, 256),
            v.transpose(1, 0, 2, 3, 4).reshape(S5_GROUPS, 256, S5_CH * ln),
            a1.reshape(64, 128), a2.reshape(64, 128))


def _lag_onehot():
    ln = S5_CHUNK
    j, i = np.meshgrid(np.arange(ln), np.arange(ln), indexing="ij")
    lag = np.arange(ln)[:, None, None]
    z = np.concatenate([lag == (i - j)[None], lag == (j - i)[None]], axis=0).astype(np.float32)
    return jnp.broadcast_to(jnp.asarray(z.reshape(2 * ln, ln * ln), BF16), (S5_GROUPS, 2 * ln, ln * ln))


def _toeplitz(gt):
    ln = S5_CHUNK
    flat = _matmul(gt, _lag_onehot(), out_dtype=BF16, name="s5_toeplitz")
    return (flat.reshape(S5_GROUPS, S5_CH, S5_CH, ln, ln).transpose(0, 1, 3, 2, 4)
            .reshape(S5_GROUPS, S5_CH * ln, S5_CH * ln))


def _toeplitz_fold(dk):
    ln = S5_CHUNK
    flat = dk.reshape(S5_GROUPS, S5_CH, ln, S5_CH, ln).transpose(0, 1, 3, 2, 4).reshape(S5_GROUPS, 256, ln * ln)
    return _matmul(flat, _lag_onehot(), trans_b=True, name="s5_toeplitz_fold")


def _chunk_rows(t):
    k = t // S5_CHUNK
    return k, -(-k // 128) * 128


def _to_chunks(u):
    k, kp = _chunk_rows(u.shape[0])
    v = u.reshape(k, S5_CHUNK, S5_GROUPS, S5_CH).transpose(2, 0, 3, 1).reshape(S5_GROUPS, k, S5_CH * S5_CHUNK)
    return jnp.pad(v, ((0, 0), (0, kp - k), (0, 0)))


def _from_chunks(y, t):
    k, _ = _chunk_rows(t)
    return y[:, :k].reshape(S5_GROUPS, k, S5_CH, S5_CHUNK).transpose(1, 3, 0, 2).reshape(t, MIX_W)


def _states_to_rows(s):
    kp = s.shape[1]
    return s.reshape(S5_GROUPS, kp, 2, 128).transpose(1, 2, 0, 3).reshape(kp, 64, 128)


def _rows_to_states(h):
    kp = h.shape[0]
    return h.reshape(kp, 2, S5_GROUPS, 128).transpose(2, 0, 1, 3).reshape(S5_GROUPS, kp, 256)


def _na_bias(rel_bias):
    a, m = np.meshgrid(np.arange(4), np.arange(12), indexing="ij")
    di = np.clip(m - a + 3, 0, 2 * NA_ROWS - 2).reshape(-1)
    qc, kc = np.meshgrid(np.arange(GRID_W), np.arange(GRID_W), indexing="ij")
    dj = np.clip(kc - qc + NA_COLS - 1, 0, 2 * NA_COLS - 2).reshape(-1)
    oh_i = jnp.asarray(di[:, None] == np.arange(2 * NA_ROWS - 1)[None, :], F32)
    oh_j = jnp.asarray(dj[:, None] == np.arange(2 * NA_COLS - 1)[None, :], F32)
    hi = lax.Precision.HIGHEST
    cols = jnp.einsum("hij,cj->hic", rel_bias, oh_j, precision=hi)
    full = jnp.einsum("ri,hic->hrc", oh_i, cols, precision=hi)
    full = full.reshape(N_HEADS, 4, 12, GRID_W, GRID_W).transpose(0, 1, 3, 2, 4)
    return full.reshape(4, 2, ATT_BLK, 3 * ATT_BLK)


def _rope_tables(n_ctx, n_lat):
    tok = jnp.arange(n_lat, dtype=jnp.int32)
    row = (tok // GRID_W).astype(F32)
    col = (tok % GRID_W).astype(F32)
    inv = ROPE_BASE ** (-jnp.arange(ROPE_PAIRS, dtype=F32) / ROPE_PAIRS)
    ang = jnp.concatenate([row[:, None] * inv, col[:, None] * inv], axis=-1)
    cos, sin = jnp.cos(ang), jnp.sin(ang)
    cos = jnp.tile(jnp.concatenate([cos, cos], axis=-1), (1, 2))
    sin = jnp.tile(jnp.concatenate([-sin, sin], axis=-1), (1, 2))
    return (jnp.concatenate([jnp.ones((n_ctx, 128), F32), cos], axis=0),
            jnp.concatenate([jnp.zeros((n_ctx, 128), F32), sin], axis=0))


def _pad_blocks(a, n_ctx):
    return jnp.pad(a[n_ctx:], ((ATT_BLK, ATT_BLK), (0, 0)))


def _layer_fwd(xt, cc, w, rope, n_ctx):
    t = xt.shape[0]
    sv = {}
    mod = _adaln_fwd(cc, w["w_ada"], w["b_ada"].reshape(1, -1))
    mod4 = mod[:2].reshape(2, 3, 1, D_MODEL)
    h = _modnorm_fwd(xt, w["norm_g"].reshape(1, -1), mod4, n_ctx)
    z = _matmul(h, w["w_in"], name="proj_fwd")

    s5_args = (w["s5_a_re"], w["s5_a_im"], w["s5_log_dt"], w["s5_b_re"], w["s5_b_im"], w["s5_c_re"], w["s5_c_im"])
    (gt, tw, tv, a1, a2), tab_vjp = jax.vjp(_s5_tables, *s5_args)
    ktoe = _toeplitz(gt)
    tw, tv = tw.astype(BF16), tv.astype(BF16)
    uc = _to_chunks(z[:, :MIX_W].astype(BF16))
    st = _matmul(uc, tw, name="s5_chunk_state")
    hprev = _s5_scan_fwd(_states_to_rows(st), a1, a2, t // S5_CHUNK, n_ctx // S5_CHUNK)
    uh = jnp.concatenate([uc, _rows_to_states(hprev).astype(BF16)], axis=2)
    ysum = _from_chunks(_matmul(uh, jnp.concatenate([ktoe, tv], axis=1), name="s5_chunk_out"), t)
    s5_d = w["s5_d"].reshape(1, MIX_W)
    y_s5 = _s5post_fwd(ysum, z, s5_d, w["s5_w_glu"])

    conv_w = w["conv_w"]
    y_conv = _conv_fwd(z, conv_w, w["conv_b"].reshape(1, -1), n_ctx)

    gains = (jnp.tile(w["na_q_g"], 8)[None], jnp.tile(w["na_k_g"], 8)[None],
             jnp.tile(w["gqa_q_g"], 8)[None], jnp.tile(w["gqa_k_g"], 2)[None])
    q_na, k_na, v_na, q_g, k_g, v_g = _prep_fwd(z, gains, rope)
    bias, bias_vjp = jax.vjp(_na_bias, w["na_rel_bias"])
    sink = jnp.zeros((4, 8, 128), F32).at[:, :2, :].set(
        jnp.broadcast_to(w["gqa_sink"].reshape(4, 2, 1), (4, 2, 128)))
    na_tab = jnp.where(_window_patterns("na", t - n_ctx)[:, None, None], bias[None], NEG_INF)
    gqa_tab = jnp.where(_window_patterns("gqa", t - n_ctx), 0.0, NEG_INF).astype(F32)
    na_in = (q_na, _pad_blocks(k_na, n_ctx), _pad_blocks(v_na, n_ctx), k_na[:n_ctx], v_na[:n_ctx], na_tab, None)
    gqa_in = (q_g, _pad_blocks(k_g, n_ctx), _pad_blocks(v_g, n_ctx), k_g[:n_ctx], v_g[:n_ctx], gqa_tab, sink)
    y_na = _attn_fwd(*na_in, mode="na", n_ctx=n_ctx)
    y_gqa = _attn_fwd(*gqa_in, mode="gqa", n_ctx=n_ctx)
    ys = (y_s5, y_conv, y_na, y_gqa)
    xt_new = _merge_fwd(xt, ys, z, mod4, w["w_br"], w["w_out"], n_ctx)
    sv.update(xt=xt, mod4=mod4, h=h, z=z, tab_vjp=tab_vjp, ktoe=ktoe, tw=tw, tv=tv, a1=a1, a2=a2, uc=uc,
              hprev=hprev, uh=uh, ysum=ysum, s5_d=s5_d, conv_w=conv_w, gains=gains, bias_vjp=bias_vjp,
              na_in=na_in, gqa_in=gqa_in, ys=ys)
    return xt_new, sv


def _layer_bwd(dxt_new, sv, cc, w, rope, n_ctx):
    t = dxt_new.shape[0]
    z, mod4 = sv["z"], sv["mod4"]
    dys, dgt, dmg, dgate, dw_br, dw_out = _merge_bwd(dxt_new, sv["ys"], z, mod4, w["w_br"], w["w_out"], n_ctx)

    dpre, du_skip, dd, dw_glu = _s5post_bwd(sv["ysum"], z, sv["s5_d"], w["s5_w_glu"], dys[0])
    dyc = _to_chunks(dpre)
    dhp = _matmul(dyc, sv["tv"], trans_b=True, name="s5_bwd_state")
    ds, da1, da2 = _s5_scan_bwd(_states_to_rows(dhp), sv["hprev"], sv["a1"], sv["a2"],
                                t // S5_CHUNK, n_ctx // S5_CHUNK)
    ds = _rows_to_states(ds).astype(BF16)
    duc = _matmul(jnp.concatenate([dyc, ds], axis=2), jnp.concatenate([sv["ktoe"], sv["tw"]], axis=2),
                  trans_b=True, out_dtype=BF16, name="s5_bwd_u")
    dkv = _matmul(sv["uh"].transpose(0, 2, 1), dyc, name="s5_bwd_kv")
    dtw = _matmul(sv["uc"].transpose(0, 2, 1), ds, name="s5_bwd_w")
    dgt_tab = _toeplitz_fold(dkv[:, :S5_CH * S5_CHUNK])
    s5_grads = sv["tab_vjp"]((dgt_tab, dtw, dkv[:, S5_CH * S5_CHUNK:], da1, da2))
    du_scan = _from_chunks(duc, t)

    dzv, dzb, dzc, dconv_w, dconv_b = _conv_bwd(z, sv["conv_w"], w["conv_b"].reshape(1, -1), dys[1], n_ctx)

    dq_na, dk_na, dv_na, dkc_na, dvc_na, dbias, _ = _attn_bwd(*sv["na_in"], dys[2], mode="na", n_ctx=n_ctx)
    dq_g, dk_g, dv_g, dkc_g, dvc_g, _, dsink = _attn_bwd(*sv["gqa_in"], dys[3], mode="gqa", n_ctx=n_ctx)
    pb = _prep_bwd(z, sv["gains"], rope, (dq_na, dq_g), (dk_na, dv_na, dk_g, dv_g),
                   (dkc_na, dvc_na, dkc_g, dvc_g), du_skip, du_scan, n_ctx)
    dz_naq, dz_nak, dz_nav, dz_gq, dz_gk, dz_gv, dz_u, dg_naq, dg_nak, dg_gq, dg_gk = pb

    dz = jnp.concatenate([dz_u, dgt[0], dzv, dzb, dzc, dgt[1], dz_naq, dz_nak, dz_nav, dgt[2], dz_gq, dz_gk, dz_gv,
                          jnp.zeros((t, OFF["gqa_gate"] - OFF["pad"]), BF16), dgt[3], *dmg], axis=1)
    dh = _matmul(dz, w["w_in"], trans_b=True, name="proj_bwd_x")
    dw_in = _matmul(sv["h"].T, dz, out_dtype=BF16, name="proj_bwd_w")
    dxt, dnorm_g, dshift, dscale = _modnorm_bwd(sv["xt"], w["norm_g"].reshape(1, -1), mod4, dh, dxt_new, n_ctx)
    dmod = jnp.concatenate([dshift, dscale, dgate], axis=1).reshape(2, 3 * D_MODEL)
    dcc, dw_ada, db_ada = _adaln_bwd(cc, w["w_ada"], jnp.pad(dmod, ((0, 6), (0, 0))))

    (drel,) = sv["bias_vjp"](dbias.sum(0))
    grads = dict(
        norm_g=dnorm_g[0], w_ada=dw_ada, b_ada=db_ada[0], w_in=dw_in,
        s5_a_re=s5_grads[0], s5_a_im=s5_grads[1], s5_log_dt=s5_grads[2], s5_b_re=s5_grads[3], s5_b_im=s5_grads[4],
        s5_c_re=s5_grads[5], s5_c_im=s5_grads[6], s5_d=dd.reshape(S5_GROUPS, S5_CH), s5_w_glu=dw_glu,
        conv_w=dconv_w, conv_b=dconv_b[0],
        na_q_g=dg_naq.reshape(8, HEAD_DIM).sum(0), na_k_g=dg_nak.reshape(8, HEAD_DIM).sum(0), na_rel_bias=drel,
        gqa_q_g=dg_gq.reshape(8, HEAD_DIM).sum(0), gqa_k_g=dg_gk.reshape(2, HEAD_DIM).sum(0),
        gqa_sink=dsink[:, :2, :].sum(-1).reshape(8), w_br=dw_br, w_out=dw_out)
    return dxt, dcc, grads


SHARDED = ("w_ada", "w_in", "s5_w_glu", "conv_w", "w_br", "w_out")
REPLICATED = ("norm_g", "b_ada", "s5_a_re", "s5_a_im", "s5_log_dt", "s5_b_re", "s5_b_im", "s5_c_re", "s5_c_im",
              "s5_d", "conv_b", "na_q_g", "na_k_g", "na_rel_bias", "gqa_q_g", "gqa_k_g", "gqa_sink")
WEIGHTS = ("c_ctx", "norm_g", "w_ada", "b_ada", "w_in", "s5_a_re", "s5_a_im", "s5_log_dt", "s5_b_re", "s5_b_im",
           "s5_c_re", "s5_c_im", "s5_d", "s5_w_glu", "conv_w", "conv_b", "na_q_g", "na_k_g", "na_rel_bias",
           "gqa_q_g", "gqa_k_g", "gqa_sink", "w_br", "w_out")


def _pack(pieces, row_multiple, dtype):
    flat = jnp.concatenate([p.reshape(-1).astype(dtype) for p in pieces])
    rows = -(-flat.shape[0] // PACK_W)
    rows = -(-rows // row_multiple) * row_multiple
    return jnp.pad(flat, (0, rows * PACK_W - flat.shape[0])).reshape(rows, PACK_W)


def _unpack(buf, shapes):
    flat = buf.reshape(-1)
    out, pos = [], 0
    for shp in shapes:
        size = int(np.prod(shp))
        out.append(flat[pos:pos + size].reshape(shp))
        pos += size
    return out


def _local_step(x, ctx, target, c_vec, c_ctx, layers):
    depth = len(layers)
    n_ctx, n_lat = ctx.shape[0], x.shape[0]
    cc = jnp.zeros((8, D_MODEL), F32).at[0].set(c_ctx).at[1].set(c_vec)
    rope = _rope_tables(n_ctx, n_lat)
    xt = jnp.concatenate([ctx, x], axis=0)
    saved = []
    for l in range(depth):
        xt, sv = _layer_fwd(xt, cc, layers[l], rope, n_ctx)
        saved.append(sv)
    loss_tile, dxt = _loss_head(xt, target, n_ctx)
    grads = [None] * depth
    dc_ctx = jnp.zeros((D_MODEL,), F32)
    for l in reversed(range(depth)):
        dxt, dcc, grads[l] = _layer_bwd(dxt, saved[l], cc, layers[l], rope, n_ctx)
        dc_ctx = dc_ctx + dcc[0]
    return loss_tile[0, 0], dxt[n_ctx:][None], dc_ctx, grads


def kernel(x, c, ctx, c_ctx, norm_g, w_ada, b_ada, w_in, s5_a_re, s5_a_im, s5_log_dt, s5_b_re, s5_b_im,
           s5_c_re, s5_c_im, s5_d, s5_w_glu, conv_w, conv_b, na_q_g, na_k_g, na_rel_bias, gqa_q_g,
           gqa_k_g, gqa_sink, w_br, w_out, loss_target, m_c_ctx, m_norm_g, m_w_ada, m_b_ada, m_w_in,
           m_s5_a_re, m_s5_a_im, m_s5_log_dt, m_s5_b_re, m_s5_b_im, m_s5_c_re, m_s5_c_im, m_s5_d,
           m_s5_w_glu, m_conv_w, m_conv_b, m_na_q_g, m_na_k_g, m_na_rel_bias, m_gqa_q_g, m_gqa_k_g,
           m_gqa_sink, m_w_br, m_w_out, v_c_ctx, v_norm_g, v_w_ada, v_b_ada, v_w_in, v_s5_a_re,
           v_s5_a_im, v_s5_log_dt, v_s5_b_re, v_s5_b_im, v_s5_c_re, v_s5_c_im, v_s5_d, v_s5_w_glu,
           v_conv_w, v_conv_b, v_na_q_g, v_na_k_g, v_na_rel_bias, v_gqa_q_g, v_gqa_k_g, v_gqa_sink,
           v_w_br, v_w_out):
    a = dict(locals())
    depth = a["norm_g"].shape[0]
    x, ctx, target = a["x"][0], a["ctx"][0], a["loss_target"][0]
    n_ctx, n_lat = ctx.shape[0], x.shape[0]
    assert n_ctx % ATT_BLK == 0 and n_lat % (4 * GRID_W) == 0 and n_lat // GRID_W >= NA_ROWS

    cast = lambda n: a[n].astype(BF16)
    conv8 = jnp.pad(a["conv_w"], ((0, 0), (0, 5), (0, 0)))
    g_ada, g_in, g_glu, g_br, g_out, g_conv = _gather_weights(
        cast("w_ada"), cast("w_in"), cast("s5_w_glu"), cast("w_br"), cast("w_out"), conv8)
    zpad = jnp.zeros((D_MODEL, OFF["gqa_gate"] - OFF["pad"]), BF16)
    split = OFF["pad"] - 2 * N_SHARD_IN
    layers = []
    for l in range(depth):
        w = {n: a[n][l] for n in REPLICATED}
        w.update(w_ada=g_ada[l], s5_w_glu=g_glu[l], w_br=g_br[l], w_out=g_out[l], conv_w=g_conv[l])
        w["w_in"] = jnp.concatenate([g_in[0, l], g_in[1, l], g_in[2, l][:, :split], zpad, g_in[2, l][:, split:],
                                     g_in[3, l]], axis=1)
        layers.append(w)

    loss_local, grad_x, dc_ctx, grads = _local_step(x, ctx, target, a["c"][0], a["c_ctx"], layers)
    loss = lax.psum(loss_local, ("x", "y", "c"))

    chip = 2 * lax.axis_index("x") + lax.axis_index("y")
    take = [functools.partial(lambda win, pc: jnp.concatenate([win[:, lo:hi] for lo, hi in pc], axis=1), pc=pc)
            for pc in WIN_PIECES]

    def small_pack(values, c_ctx_value, l):
        pieces = [values[n] for n in REPLICATED]
        pieces.append(c_ctx_value if l == 0 else jnp.zeros((D_MODEL,), F32))
        return _pack(pieces, 8 * SUM_STEPS, F32)

    mine, theirs = [], []
    for l in range(depth):
        g = grads[l]
        small = small_pack(g, dc_ctx, l)
        recv = _scatter_grads(g["w_ada"], g["w_in"], g["s5_w_glu"], g["w_br"], g["w_out"], g["conv_w"], small)
        part = list(_sum_chips([r.reshape(4, -1, r.shape[-1]) for r in recv]))
        part[1] = lax.switch(chip, take, part[1])
        mine.append(part)
        theirs.append(_core_swap(part))

    families = ("w_ada", "w_in", "s5_w_glu", "w_br", "w_out", "conv_w")
    out = {}
    for k, n in enumerate(families):
        p, q = jnp.stack([m[k] for m in mine]), jnp.stack([t[k] for t in theirs])
        if n == "conv_w":
            p, q = p[:, :3], q[:, :3]
        as3d = lambda arr: arr.reshape(depth, -1, arr.shape[-1])
        res = _adamw(p, q, as3d(a[n]), as3d(a["m_" + n]), as3d(a["v_" + n]), "adamw_" + n)
        out[n] = [r.reshape(a[n].shape) for r in res]
    p, q = jnp.stack([m[6] for m in mine]), jnp.stack([t[6] for t in theirs])
    packs = [jnp.stack([small_pack({n: a[pre + n][l] for n in REPLICATED}, a[pre + "c_ctx"], l)
                        for l in range(depth)]) for pre in ("", "m_", "v_")]
    res = _adamw(p, q, *packs, "adamw_small")
    shapes = [a[n].shape[1:] for n in REPLICATED] + [a["c_ctx"].shape]
    per_layer = [[_unpack(r[l], shapes) for l in range(depth)] for r in res]
    for j, n in enumerate(REPLICATED):
        out[n] = [jnp.stack([per_layer[key][l][j] for l in range(depth)]) for key in range(4)]
    out["c_ctx"] = [per_layer[key][0][-1] for key in range(4)]
    results = [loss, grad_x]
    for key in range(4):
        results += [out[n][key] for n in WEIGHTS]
    return tuple(results)
```

```python
import functools

import numpy as np
import jax
import jax.numpy as jnp
from jax import lax
from jax.experimental import pallas as pl
from jax.experimental.pallas import tpu as pltpu

F32 = jnp.float32
BF16 = jnp.bfloat16

D_MODEL = 1024
MIX_W = 512
GRID_W = 64
HEAD_DIM = 64
N_HEADS = 8
S5_GROUPS = 32
S5_CH = 16
S5_CHUNK = 32
NA_ROWS = 8
NA_COLS = 16
WINDOW = 128
ROPE_BASE = 10000.0
ROPE_PAIRS = 16
EPS = 1e-6
NEG_INF = -1e30
ATT_BLK = 256
TOK = 256
VMEM_LIMIT = 56 * 1024 * 1024

ADAM_LR, ADAM_B1, ADAM_B2, ADAM_EPS, ADAM_WD, ADAM_STEP = 0.001, 0.9, 0.999, 1e-8, 0.01, 10

OFF = dict(s5_u=0, s5_gate=512, conv_v=1024, conv_b=1536, conv_c=2048, conv_gate=2560,
           na_q=3072, na_k=3584, na_v=4096, na_gate=4608, gqa_q=5120, gqa_k=5632, gqa_v=5760,
           pad=5888, gqa_gate=6144, merge_s5=6656, merge_conv=7680, merge_na=8704, merge_gqa=9728)
N_Z = 10752
GATE_OFFS = (OFF["s5_gate"], OFF["conv_gate"], OFF["na_gate"], OFF["gqa_gate"])
MERGE_OFFS = (OFF["merge_s5"], OFF["merge_conv"], OFF["merge_na"], OFF["merge_gqa"])


def _cparams(sem):
    return pltpu.CompilerParams(dimension_semantics=sem, vmem_limit_bytes=VMEM_LIMIT)


def _dot(a, b, ca, cb):
    return lax.dot_general(a.astype(BF16), b.astype(BF16), (((ca,), (cb,)), ((), ())),
                           preferred_element_type=F32)


def _dot_tn(a, b):
    return _dot(a.astype(F32).T, b, 1, 0)


@jax.custom_vjp
def mm(a, b):
    return _dot(a, b, 1, 0)


@jax.custom_vjp
def mm_nt(a, b):
    return _dot(a, b, 1, 1)


@jax.custom_vjp
def mm_tn(a, b):
    return _dot_tn(a, b)


mm.defvjp(lambda a, b: (mm(a, b), (a, b)), lambda r, g: (mm_nt(g, r[1]), mm_tn(r[0], g)))
mm_nt.defvjp(lambda a, b: (mm_nt(a, b), (a, b)), lambda r, g: (mm(g, r[1]), mm_tn(g, r[0])))
mm_tn.defvjp(lambda a, b: (mm_tn(a, b), (a, b)), lambda r, g: (mm_nt(r[1], g), mm(r[0], g)))


@functools.partial(jax.custom_vjp, nondiff_argnums=(1,))
def lane_roll(x, shift):
    return pltpu.roll(x, shift, 1)


lane_roll.defvjp(lambda x, shift: (lane_roll(x, shift), None),
                 lambda shift, _, g: (lane_roll(g, (g.shape[1] - shift) % g.shape[1]),))


def _silu(x):
    return x * jax.nn.sigmoid(x)


def _dsilu(x):
    s = jax.nn.sigmoid(x)
    return s * (1.0 + x * (1.0 - s))


def _pick(n, prefs):
    for p in prefs:
        if n % p == 0:
            return p
    return n


def _matmul(a, b, *, trans_b=False, out_dtype=F32, tm=None, tn=None, tk=None, name):
    squeeze = a.ndim == 2
    if squeeze:
        a, b = a[None], b[None]
    nb, m, k = a.shape
    n = b.shape[1] if trans_b else b.shape[2]
    tm = tm or _pick(m, (1280, 1024, 640, 512, 256, 128))
    tn = tn or _pick(n, (1536, 1024, 512, 256, 128))
    tk = tk or _pick(k, (1536, 1280, 1024, 768, 640, 512, 256, 128))
    nk = k // tk

    def body(a_ref, b_ref, o_ref, *scr):
        part = _dot(a_ref[...], b_ref[...], 1, 1 if trans_b else 0)
        if nk == 1:
            o_ref[...] = part.astype(out_dtype)
        else:
            acc = scr[0]
            kk = pl.program_id(3)

            @pl.when(kk == 0)
            def _():
                acc[...] = part

            @pl.when(kk > 0)
            def _():
                acc[...] += part

            @pl.when(kk == nk - 1)
            def _():
                o_ref[...] = acc[...].astype(out_dtype)

    if trans_b:
        b_spec = pl.BlockSpec((None, tn, tk), lambda bb, i, j, kk: (bb, j, kk))
    else:
        b_spec = pl.BlockSpec((None, tk, tn), lambda bb, i, j, kk: (bb, kk, j))
    out = pl.pallas_call(
        body, name=name,
        grid=(nb, m // tm, n // tn, nk),
        in_specs=[pl.BlockSpec((None, tm, tk), lambda bb, i, j, kk: (bb, i, kk)), b_spec],
        out_specs=pl.BlockSpec((None, tm, tn), lambda bb, i, j, kk: (bb, i, j)),
        out_shape=jax.ShapeDtypeStruct((nb, m, n), out_dtype),
        scratch_shapes=[] if nk == 1 else [pltpu.VMEM((tm, tn), F32)],
        compiler_params=_cparams(("parallel", "parallel", "parallel", "arbitrary")),
    )(a, b)
    return out[0] if squeeze else out


def _adaln_fn(cc, w, b):
    return mm(_silu(cc), w) + b


def _adaln_fwd(cc, w_ada, b_ada):
    def body(cc_ref, w_ref, b_ref, o_ref):
        o_ref[...] = _adaln_fn(cc_ref[...], w_ref[...], b_ref[...])

    return pl.pallas_call(
        body, name="adaln_fwd", out_shape=jax.ShapeDtypeStruct((8, 3 * D_MODEL), F32),
        compiler_params=pltpu.CompilerParams(vmem_limit_bytes=VMEM_LIMIT),
    )(cc, w_ada, b_ada)


def _adaln_bwd(cc, w_ada, dmod):
    def body(cc_ref, w_ref, g_ref, dcc_ref, dw_ref, db_ref):
        cc_v, g = cc_ref[...], g_ref[...]
        dw_ref[...] = mm_tn(_silu(cc_v), g).astype(BF16)
        db_ref[...] = jnp.sum(g, axis=0, keepdims=True)
        dcc_ref[...] = mm_nt(g, w_ref[...]) * _dsilu(cc_v)

    return pl.pallas_call(
        body, name="adaln_bwd",
        out_shape=(jax.ShapeDtypeStruct((8, D_MODEL), F32),
                   jax.ShapeDtypeStruct((D_MODEL, 3 * D_MODEL), BF16),
                   jax.ShapeDtypeStruct((1, 3 * D_MODEL), F32)),
        compiler_params=pltpu.CompilerParams(vmem_limit_bytes=VMEM_LIMIT),
    )(cc, w_ada, dmod)


def _seg_spec(which, n_ctx_tiles):
    return pl.BlockSpec((None, None, 1, D_MODEL),
                        lambda i: (jnp.where(i < n_ctx_tiles, 0, 1), which, 0, 0))


def _row_spec(width, col_block=0, tile=TOK):
    return pl.BlockSpec((tile, width), lambda i: (i, col_block))


def _const_spec(shape):
    zeros = (0,) * len(shape)
    return pl.BlockSpec(shape, lambda i: zeros)


def _modnorm_fn(x, g, shift, scale):
    y = x * lax.rsqrt(jnp.mean(x * x, axis=-1, keepdims=True) + EPS)
    return (y * g) * (1.0 + scale) + shift


def _modnorm_fwd(xt, g, mod4, n_ctx):
    t = xt.shape[0]
    nct = n_ctx // TOK

    def body(x_ref, g_ref, sh_ref, sc_ref, o_ref):
        o_ref[...] = _modnorm_fn(x_ref[...], g_ref[...], sh_ref[...], sc_ref[...]).astype(BF16)

    return pl.pallas_call(
        body, name="modnorm_fwd", grid=(t // TOK,),
        in_specs=[_row_spec(D_MODEL), _const_spec((1, D_MODEL)), _seg_spec(0, nct), _seg_spec(1, nct)],
        out_specs=_row_spec(D_MODEL),
        out_shape=jax.ShapeDtypeStruct((t, D_MODEL), BF16),
        compiler_params=_cparams(("parallel",)),
    )(xt, g, mod4, mod4)


def _modnorm_bwd(xt, g, mod4, dh, dres, n_ctx):
    t = xt.shape[0]
    nct = n_ctx // TOK

    def body(x_ref, g_ref, sh_ref, sc_ref, dh_ref, dres_ref, dx_ref, dg_ref, dsh_ref, dsc_ref):
        i = pl.program_id(0)
        _, vjp = jax.vjp(_modnorm_fn, x_ref[...], g_ref[...], sh_ref[...], sc_ref[...])
        dx, dg, dsh, dsc = vjp(dh_ref[...])
        dx_ref[...] = dx + dres_ref[...]

        @pl.when(i == 0)
        def _():
            dg_ref[...] = jnp.zeros_like(dg_ref)

        dg_ref[...] += dg
        first = jnp.logical_or(i == 0, i == nct)

        @pl.when(first)
        def _():
            dsh_ref[...] = dsh
            dsc_ref[...] = dsc

        @pl.when(jnp.logical_not(first))
        def _():
            dsh_ref[...] += dsh
            dsc_ref[...] += dsc

    seg_out = lambda which: pl.BlockSpec((None, None, 1, D_MODEL),
                                         lambda i: (jnp.where(i < nct, 0, 1), which, 0, 0))
    dx, dg, dss, dss2 = pl.pallas_call(
        body, name="modnorm_bwd", grid=(t // TOK,),
        in_specs=[_row_spec(D_MODEL), _const_spec((1, D_MODEL)), _seg_spec(0, nct), _seg_spec(1, nct),
                  _row_spec(D_MODEL), _row_spec(D_MODEL)],
        out_specs=[_row_spec(D_MODEL), _const_spec((1, D_MODEL)), seg_out(0), seg_out(0)],
        out_shape=(jax.ShapeDtypeStruct((t, D_MODEL), F32), jax.ShapeDtypeStruct((1, D_MODEL), F32),
                   jax.ShapeDtypeStruct((2, 1, 1, D_MODEL), F32), jax.ShapeDtypeStruct((2, 1, 1, D_MODEL), F32)),
        compiler_params=_cparams(("arbitrary",)),
    )(xt, g, mod4, mod4, dh, dres)
    return dx, dg, dss, dss2


def _group_mean_sq(x, gs):
    x2 = x * x
    hi = x2.astype(BF16).astype(F32)
    return mm(hi, gs) + mm(x2 - hi, gs)


def _head_norm(x, g, gs):
    return (x * lax.rsqrt(_group_mean_sq(x, gs) + EPS)) * g


def _rope(x, cos, sin_signed):
    lane = lax.broadcasted_iota(jnp.int32, (1, 128), 1)
    first_half = jnp.bitwise_and(lane, 63) < 32
    cols = []
    for c in range(x.shape[1] // 128):
        xb = x[:, 128 * c:128 * (c + 1)]
        partner = jnp.where(first_half, lane_roll(xb, 96), lane_roll(xb, 32))
        cols.append(xb * cos + partner * sin_signed)
    return cols[0] if len(cols) == 1 else jnp.concatenate(cols, axis=1)


def _prep_fn(zq_na, zk_na, zv_na, zq_g, zk_g, zv_g, g_naq, g_nak, g_gq, g_gk, cos, sin_signed, gs512, gs128, expand):
    q_na = _head_norm(zq_na, g_naq, gs512)
    k_na = _head_norm(zk_na, g_nak, gs512)
    q_g = _rope(_head_norm(zq_g, g_gq, gs512), cos, sin_signed)
    k_g = _rope(_head_norm(zk_g, g_gk, gs128), cos, sin_signed)
    return q_na, k_na, zv_na, q_g, mm(k_g, expand), mm(zv_g, expand)


def _prep_consts():
    gid = np.arange(512) // 64
    gs512 = (gid[:, None] == gid[None, :]).astype(np.float32) / 64.0
    expand = np.zeros((128, 512), np.float32)
    for h in range(N_HEADS):
        for j in range(64):
            expand[64 * (h // 4) + j, 64 * h + j] = 1.0
    return jnp.asarray(gs512), jnp.asarray(gs512[:128, :128]), jnp.asarray(expand)


def _prep_in_specs():
    blk = lambda off, w: _row_spec(w, off // w)
    return [blk(OFF["na_q"], 512), blk(OFF["na_k"], 512), blk(OFF["na_v"], 512), blk(OFF["gqa_q"], 512),
            blk(OFF["gqa_k"], 128), blk(OFF["gqa_v"], 128),
            _const_spec((1, 512)), _const_spec((1, 512)), _const_spec((1, 512)), _const_spec((1, 128)),
            _row_spec(128), _row_spec(128),
            _const_spec((512, 512)), _const_spec((128, 128)), _const_spec((128, 512))]


def _prep_fwd(z, gains, rope_tabs):
    t = z.shape[0]
    consts = _prep_consts()

    def body(*refs):
        ins, outs = refs[:15], refs[15:]
        res = _prep_fn(*[r[...] for r in ins])
        for o_ref, v in zip(outs, res):
            o_ref[...] = v.astype(BF16)

    return pl.pallas_call(
        body, name="prep_fwd", grid=(t // TOK,),
        in_specs=_prep_in_specs(),
        out_specs=[_row_spec(512)] * 6,
        out_shape=tuple(jax.ShapeDtypeStruct((t, 512), BF16) for _ in range(6)),
        compiler_params=_cparams(("parallel",)),
    )(z, z, z, z, z, z, *gains, *rope_tabs, *consts)


def _prep_bwd(z, gains, rope_tabs, dqs, dkv_lat, dkv_ctx, du_a, du_b, n_ctx):
    t = z.shape[0]
    nct = n_ctx // TOK
    consts = _prep_consts()

    def body(*refs):
        ins, dq_refs, lat_refs, ctx_refs = refs[:15], refs[15:17], refs[17:21], refs[21:25]
        (dua_ref, dub_ref), outs = refs[25:27], refs[27:]
        i = pl.program_id(0)
        vals = [r[...] for r in ins]
        _, vjp = jax.vjp(lambda *a: _prep_fn(*a, *vals[10:]), *vals[:10])
        kv = [jnp.where(i < nct, c_ref[...], l_ref[...]) for l_ref, c_ref in zip(lat_refs, ctx_refs)]
        grads = vjp((dq_refs[0][...], kv[0], kv[1], dq_refs[1][...], kv[2], kv[3]))
        for o_ref, v in zip(outs[:6], grads[:6]):
            o_ref[...] = v.astype(BF16)
        outs[6][...] = (dua_ref[...] + dub_ref[...]).astype(BF16)

        @pl.when(i == 0)
        def _():
            for o_ref in outs[7:]:
                o_ref[...] = jnp.zeros_like(o_ref)

        for o_ref, v in zip(outs[7:], grads[6:10]):
            o_ref[...] += v

    lat_spec = pl.BlockSpec((TOK, 512), lambda i: (jnp.maximum(i - nct + 1, 0), 0))
    ctx_spec = pl.BlockSpec((TOK, 512), lambda i: (jnp.minimum(i, nct - 1), 0))
    return pl.pallas_call(
        body, name="prep_bwd", grid=(t // TOK,),
        in_specs=_prep_in_specs() + [_row_spec(512)] * 2 + [lat_spec] * 4 + [ctx_spec] * 4 + [_row_spec(512)] * 2,
        out_specs=[_row_spec(512)] * 4 + [_row_spec(128)] * 2 + [_row_spec(512)]
        + [_const_spec((1, 512))] * 3 + [_const_spec((1, 128))],
        out_shape=tuple([jax.ShapeDtypeStruct((t, 512), BF16)] * 4 + [jax.ShapeDtypeStruct((t, 128), BF16)] * 2
                        + [jax.ShapeDtypeStruct((t, 512), BF16)]
                        + [jax.ShapeDtypeStruct((1, 512), F32)] * 3 + [jax.ShapeDtypeStruct((1, 128), F32)]),
        compiler_params=_cparams(("arbitrary",)),
    )(z, z, z, z, z, z, *gains, *rope_tabs, *consts, *dqs, *dkv_lat, *dkv_ctx, du_a, du_b)


def _s5post_fn(ys, u, d, w_glu):
    y = jax.nn.gelu(ys + d * u)
    return y * jax.nn.sigmoid(mm(y, w_glu))


def _s5post_fwd(ys, z, d, w_glu):
    t = z.shape[0]

    def body(ys_ref, u_ref, d_ref, w_ref, o_ref):
        o_ref[...] = _s5post_fn(ys_ref[...], u_ref[...], d_ref[...], w_ref[...])

    return pl.pallas_call(
        body, name="s5post_fwd", grid=(t // TOK,),
        in_specs=[_row_spec(512), _row_spec(512, OFF["s5_u"] // 512),
                  _const_spec((1, 512)), _const_spec((512, 512))],
        out_specs=_row_spec(512), out_shape=jax.ShapeDtypeStruct((t, 512), F32),
        compiler_params=_cparams(("parallel",)),
    )(ys, z, d, w_glu)


def _s5post_bwd(ys, z, d, w_glu, dy):
    t = z.shape[0]

    def body(ys_ref, u_ref, d_ref, w_ref, dy_ref, dpre_ref, du_ref, dd_ref, dw_ref):
        i = pl.program_id(0)
        _, vjp = jax.vjp(_s5post_fn, ys_ref[...], u_ref[...], d_ref[...], w_ref[...].astype(F32))
        dys, du, dd, dw = vjp(dy_ref[...])
        dpre_ref[...] = dys.astype(BF16)
        du_ref[...] = du

        @pl.when(i == 0)
        def _():
            dd_ref[...] = jnp.zeros_like(dd_ref)
            dw_ref[...] = jnp.zeros_like(dw_ref)

        dd_ref[...] += dd
        dw_ref[...] += dw

    return pl.pallas_call(
        body, name="s5post_bwd", grid=(t // TOK,),
        in_specs=[_row_spec(512), _row_spec(512, OFF["s5_u"] // 512),
                  _const_spec((1, 512)), _const_spec((512, 512)), _row_spec(512)],
        out_specs=[_row_spec(512), _row_spec(512), _const_spec((1, 512)), _const_spec((512, 512))],
        out_shape=(jax.ShapeDtypeStruct((t, 512), BF16), jax.ShapeDtypeStruct((t, 512), F32),
                   jax.ShapeDtypeStruct((1, 512), F32), jax.ShapeDtypeStruct((512, 512), F32)),
        compiler_params=_cparams(("arbitrary",)),
    )(ys, z, d, w_glu, dy)


def _halo_specs(col_block, t):
    last = t // 8 - 1
    prev = pl.BlockSpec((8, 512), lambda i: (jnp.maximum(i * (TOK // 8) - 1, 0), col_block))
    nxt = pl.BlockSpec((8, 512), lambda i: (jnp.minimum((i + 1) * (TOK // 8), last), col_block))
    return [_row_spec(512, col_block), prev, nxt]


def _shifted(cur, prev_row, next_row, tok0, n_ctx, t_total):
    row = lax.broadcasted_iota(jnp.int32, (TOK, 1), 0)
    tpos = row + tok0
    down = jnp.where(row == 0, prev_row, pltpu.roll(cur, 1, 0))
    down = jnp.where(jnp.logical_or(tpos == 0, tpos == n_ctx), 0.0, down)
    up = jnp.where(row == TOK - 1, next_row, pltpu.roll(cur, TOK - 1, 0))
    up = jnp.where(jnp.logical_or(tpos == n_ctx - 1, tpos == t_total - 1), 0.0, up)
    return down, up


def _conv_fwd(z, conv_w, conv_b, n_ctx):
    t = z.shape[0]

    def body(v_ref, vp_ref, vn_ref, c_ref, cp_ref, cn_ref, b_ref, w_ref, cb_ref, o_ref):
        tok0 = pl.program_id(0) * TOK
        zz = v_ref[...] * c_ref[...]
        zz_m1, zz_p1 = _shifted(zz, vp_ref[7:8, :] * cp_ref[7:8, :], vn_ref[0:1, :] * cn_ref[0:1, :], tok0, n_ctx, t)
        s = cb_ref[...] + zz_m1 * w_ref[0:1, :] + zz * w_ref[1:2, :] + zz_p1 * w_ref[2:3, :]
        o_ref[...] = b_ref[...] * s

    return pl.pallas_call(
        body, name="conv_fwd", grid=(t // TOK,),
        in_specs=_halo_specs(OFF["conv_v"] // 512, t) + _halo_specs(OFF["conv_c"] // 512, t)
        + [_row_spec(512, OFF["conv_b"] // 512), _const_spec((8, 512)), _const_spec((1, 512))],
        out_specs=_row_spec(512), out_shape=jax.ShapeDtypeStruct((t, 512), F32),
        compiler_params=_cparams(("parallel",)),
    )(z, z, z, z, z, z, z, conv_w, conv_b)


def _conv_bwd(z, conv_w, conv_b, dy, n_ctx):
    t = z.shape[0]

    def body(v_ref, vp_ref, vn_ref, c_ref, cp_ref, cn_ref, b_ref, bp_ref, bn_ref, dy_ref, dyp_ref, dyn_ref,
             w_ref, cb_ref, dv_ref, db_ref, dc_ref, dw_ref, dcb_ref):
        i = pl.program_id(0)
        tok0 = i * TOK
        v, c, b, dy_v = v_ref[...], c_ref[...], b_ref[...], dy_ref[...]
        w0, w1, w2 = w_ref[0:1, :], w_ref[1:2, :], w_ref[2:3, :]
        zz = v * c
        zz_m1, zz_p1 = _shifted(zz, vp_ref[7:8, :] * cp_ref[7:8, :], vn_ref[0:1, :] * cn_ref[0:1, :], tok0, n_ctx, t)
        s = cb_ref[...] + zz_m1 * w0 + zz * w1 + zz_p1 * w2
        ds = dy_v * b
        ds_m1, ds_p1 = _shifted(ds, dyp_ref[7:8, :] * bp_ref[7:8, :], dyn_ref[0:1, :] * bn_ref[0:1, :], tok0, n_ctx, t)
        dzz = ds_p1 * w0 + ds * w1 + ds_m1 * w2
        db_ref[...] = (dy_v * s).astype(BF16)
        dv_ref[...] = (dzz * c).astype(BF16)
        dc_ref[...] = (dzz * v).astype(BF16)

        @pl.when(i == 0)
        def _():
            dw_ref[...] = jnp.zeros_like(dw_ref)
            dcb_ref[...] = jnp.zeros_like(dcb_ref)

        rsum = lambda a: jnp.sum(a, axis=0, keepdims=True)
        dw_ref[0:1, :] += rsum(ds * zz_m1)
        dw_ref[1:2, :] += rsum(ds * zz)
        dw_ref[2:3, :] += rsum(ds * zz_p1)
        dcb_ref[...] += rsum(ds)

    return pl.pallas_call(
        body, name="conv_bwd", grid=(t // TOK,),
        in_specs=_halo_specs(OFF["conv_v"] // 512, t) + _halo_specs(OFF["conv_c"] // 512, t)
        + _halo_specs(OFF["conv_b"] // 512, t) + _halo_specs(0, t) + [_const_spec((8, 512)), _const_spec((1, 512))],
        out_specs=[_row_spec(512)] * 3 + [_const_spec((8, 512)), _const_spec((1, 512))],
        out_shape=tuple([jax.ShapeDtypeStruct((t, 512), BF16)] * 3
                        + [jax.ShapeDtypeStruct((8, 512), F32), jax.ShapeDtypeStruct((1, 512), F32)]),
        compiler_params=_cparams(("arbitrary",)),
    )(z, z, z, z, z, z, z, z, z, dy, dy, dy, conv_w, conv_b)


def _merge_col_specs(tile):
    specs = []
    for off in MERGE_OFFS:
        specs.append(pl.BlockSpec((tile, 512), functools.partial(lambda i, cb: (i, cb), cb=off // 512)))
        specs.append(pl.BlockSpec((tile, 512), functools.partial(lambda i, cb: (i, cb), cb=off // 512 + 1)))
    return specs


def _merge_fwd(xt, ys, z, mod4, w_br, w_out, n_ctx):
    t = xt.shape[0]
    nct = n_ctx // TOK

    def body(x_ref, *refs):
        y_refs, gt_refs, mg_refs = refs[0:4], refs[4:8], refs[8:16]
        gate_ref, wbr_ref, wout_ref, o_ref = refs[16:20]
        acc_lo = acc_hi = None
        for k in range(4):
            gated = y_refs[k][...] * _silu(gt_refs[k][...])
            proj = mm(gated, wbr_ref[k])
            lo = jax.nn.sigmoid(mg_refs[2 * k][...]) * proj[:, :512]
            hi = jax.nn.sigmoid(mg_refs[2 * k + 1][...]) * proj[:, 512:]
            acc_lo = lo if acc_lo is None else acc_lo + lo
            acc_hi = hi if acc_hi is None else acc_hi + hi
        acc = jnp.concatenate([acc_lo, acc_hi], axis=1)
        o_ref[...] = x_ref[...] + gate_ref[...] * mm(acc, wout_ref[...])

    gate_specs = [pl.BlockSpec((TOK, 512), functools.partial(lambda i, cb: (i, cb), cb=o // 512)) for o in GATE_OFFS]
    return pl.pallas_call(
        body, name="merge_fwd", grid=(t // TOK,),
        in_specs=[_row_spec(D_MODEL)] + [_row_spec(512)] * 4 + gate_specs + _merge_col_specs(TOK)
        + [_seg_spec(2, nct), _const_spec((4, 512, 1024)), _const_spec((1024, 1024))],
        out_specs=_row_spec(D_MODEL), out_shape=jax.ShapeDtypeStruct((t, D_MODEL), F32),
        compiler_params=_cparams(("parallel",)),
    )(xt, *ys, z, z, z, z, z, z, z, z, z, z, z, z, mod4, w_br, w_out)


MERGE_BWD_TILE = 128


def _merge_bwd(g, ys, z, mod4, w_br, w_out, n_ctx):
    t = g.shape[0]
    tile = MERGE_BWD_TILE
    nct = n_ctx // tile
    nsteps = t // tile

    def body(g_ref, *refs):
        y_refs, gt_refs, mg_refs = refs[0:4], refs[4:8], refs[8:16]
        gate_ref, wbr_hbm, wout_hbm = refs[16:19]
        dy_refs, dgt_refs, dmg_refs = refs[19:23], refs[23:27], refs[27:31]
        dgate_ref, dwbr_hbm, dwout_hbm = refs[31:34]
        wbr_v, wout_v, dwbr_acc, dwout_acc = refs[34:38]
        i = pl.program_id(0)

        @pl.when(i == 0)
        def _():
            pltpu.sync_copy(wbr_hbm, wbr_v)
            pltpu.sync_copy(wout_hbm, wout_v)
            dwbr_acc[...] = jnp.zeros_like(dwbr_acc)
            dwout_acc[...] = jnp.zeros_like(dwout_acc)

        g_v, gate = g_ref[...], gate_ref[...]
        gated, proj, sig = [], [], []
        acc = None
        for k in range(4):
            gated.append(y_refs[k][...] * _silu(gt_refs[k][...]))
            proj.append(mm(gated[k], wbr_v[k]))
            sig.append(jax.nn.sigmoid(jnp.concatenate([mg_refs[2 * k][...], mg_refs[2 * k + 1][...]], axis=1)))
            contrib = sig[k] * proj[k]
            acc = contrib if acc is None else acc + contrib
        o = mm(acc, wout_v[...])
        dgate = jnp.sum(g_v * o, axis=0, keepdims=True)
        first = jnp.logical_or(i == 0, i == nct)

        @pl.when(first)
        def _():
            dgate_ref[...] = dgate

        @pl.when(jnp.logical_not(first))
        def _():
            dgate_ref[...] += dgate

        do = g_v * gate
        dwout_acc[...] += mm_tn(acc, do)
        dacc = mm_nt(do, wout_v[...])
        for k in range(4):
            dmg_refs[k][...] = (dacc * proj[k] * sig[k] * (1.0 - sig[k])).astype(BF16)
            dproj = dacc * sig[k]
            dwbr_acc[k] += mm_tn(gated[k], dproj)
            dgated = mm_nt(dproj, wbr_v[k])
            gt = gt_refs[k][...]
            dy_refs[k][...] = dgated * _silu(gt)
            dgt_refs[k][...] = (dgated * y_refs[k][...] * _dsilu(gt)).astype(BF16)

        @pl.when(i == nsteps - 1)
        def _():
            wbr_v[...] = dwbr_acc[...].astype(BF16)
            wout_v[...] = dwout_acc[...].astype(BF16)
            pltpu.sync_copy(wbr_v, dwbr_hbm)
            pltpu.sync_copy(wout_v, dwout_hbm)

    row = lambda w: _row_spec(w, 0, tile)
    gate_specs = [pl.BlockSpec((tile, 512), functools.partial(lambda i, cb: (i, cb), cb=o // 512)) for o in GATE_OFFS]
    anyspec = pl.BlockSpec(memory_space=pl.ANY)
    seg = pl.BlockSpec((None, None, 1, D_MODEL), lambda i: (jnp.where(i < nct, 0, 1), 2, 0, 0))
    seg_out = pl.BlockSpec((None, None, 1, D_MODEL), lambda i: (jnp.where(i < nct, 0, 1), 0, 0, 0))
    res = pl.pallas_call(
        body, name="merge_bwd", grid=(nsteps,),
        in_specs=[row(D_MODEL)] + [row(512)] * 4 + gate_specs + _merge_col_specs(tile) + [seg, anyspec, anyspec],
        out_specs=[row(512)] * 8 + [row(1024)] * 4 + [seg_out, anyspec, anyspec],
        out_shape=tuple([jax.ShapeDtypeStruct((t, 512), F32)] * 4 + [jax.ShapeDtypeStruct((t, 512), BF16)] * 4
                        + [jax.ShapeDtypeStruct((t, 1024), BF16)] * 4
                        + [jax.ShapeDtypeStruct((2, 1, 1, D_MODEL), F32),
                           jax.ShapeDtypeStruct((4, 512, 1024), BF16), jax.ShapeDtypeStruct((1024, 1024), BF16)]),
        scratch_shapes=[pltpu.VMEM((4, 512, 1024), BF16), pltpu.VMEM((1024, 1024), BF16),
                        pltpu.VMEM((4, 512, 1024), F32), pltpu.VMEM((1024, 1024), F32)],
        compiler_params=_cparams(("arbitrary",)),
    )(g, *ys, z, z, z, z, z, z, z, z, z, z, z, z, mod4, w_br, w_out)
    return res[0:4], res[4:8], res[8:12], res[12], res[13], res[14]


FWD_ROWS = slice(0, S5_GROUPS)
BWD_ROWS = slice(S5_GROUPS, 2 * S5_GROUPS)


def _backward_chunk(j, k, n_ctx_chunks):
    return jnp.where(j < n_ctx_chunks, n_ctx_chunks - 1 - j, k - 1 - (j - n_ctx_chunks))


def _scan_call(body, name, n_hbm_in, out_shape, kp):
    hbm, vmem = pl.BlockSpec(memory_space=pl.ANY), pl.BlockSpec(memory_space=pltpu.VMEM)
    return pl.pallas_call(
        body, name=name, in_specs=[hbm] * n_hbm_in + [vmem, vmem],
        out_specs=[hbm] + [vmem] * (len(out_shape) - 1), out_shape=out_shape,
        scratch_shapes=[pltpu.VMEM((kp, 64, 128), F32), pltpu.VMEM((kp, 64, 128), F32)],
        compiler_params=pltpu.CompilerParams(vmem_limit_bytes=VMEM_LIMIT))


def _complex_step(a1, a2, h):
    return a1 * h + a2 * pltpu.roll(h, 64, 1)


def _s5_scan_fwd(s, a1, a2, k, n_ctx_chunks):
    kp = s.shape[0]

    def body(s_hbm, a1_ref, a2_ref, hp_hbm, s_v, hp_v):
        pltpu.sync_copy(s_hbm, s_v)
        if kp > k:
            hp_v[k:kp] = jnp.zeros((kp - k, 64, 128), F32)
        a1f, a2f, a1b, a2b = a1_ref[FWD_ROWS, :], a2_ref[FWD_ROWS, :], a1_ref[BWD_ROWS, :], a2_ref[BWD_ROWS, :]

        def step(j, carry):
            hf, hb = carry
            cb = _backward_chunk(j, k, n_ctx_chunks)
            hp_v[j, FWD_ROWS, :] = hf
            hp_v[cb, BWD_ROWS, :] = hb
            return (_complex_step(a1f, a2f, hf) + s_v[j, FWD_ROWS, :],
                    _complex_step(a1b, a2b, hb) + s_v[cb, BWD_ROWS, :])

        zero = jnp.zeros((S5_GROUPS, 128), F32)
        lax.fori_loop(0, k, step, (zero, zero))
        pltpu.sync_copy(hp_v, hp_hbm)

    return _scan_call(body, "s5_scan_fwd", 1, (jax.ShapeDtypeStruct(s.shape, F32),), kp)(s, a1, a2)[0]


def _s5_scan_bwd(dhp, hp, a1, a2, k, n_ctx_chunks):
    kp = hp.shape[0]

    def body(dhp_hbm, hp_hbm, a1_ref, a2_ref, ds_hbm, da1_ref, da2_ref, g_v, hp_v):
        pltpu.sync_copy(dhp_hbm, g_v)
        pltpu.sync_copy(hp_hbm, hp_v)
        if kp > k:
            g_v[k:kp] = jnp.zeros((kp - k, 64, 128), F32)
        coef_f = (a1_ref[FWD_ROWS, :], a2_ref[FWD_ROWS, :])
        coef_b = (a1_ref[BWD_ROWS, :], a2_ref[BWD_ROWS, :])

        def one(rows, c, lam, d1, d2):
            a1_v, a2_v = coef_f if rows is FWD_ROWS else coef_b
            dh_in = g_v[c, rows, :]
            g_v[c, rows, :] = lam
            h = hp_v[c, rows, :]
            return (dh_in + a1_v * lam + pltpu.roll(a2_v * lam, 64, 1),
                    d1 + lam * h, d2 + lam * pltpu.roll(h, 64, 1))

        def step(j, carry):
            f, b = carry
            jj = k - 1 - j
            return one(FWD_ROWS, jj, *f), one(BWD_ROWS, _backward_chunk(jj, k, n_ctx_chunks), *b)

        zero = jnp.zeros((S5_GROUPS, 128), F32)
        f, b = lax.fori_loop(0, k, step, ((zero, zero, zero), (zero, zero, zero)))
        da1_ref[FWD_ROWS, :], da2_ref[FWD_ROWS, :] = f[1], f[2]
        da1_ref[BWD_ROWS, :], da2_ref[BWD_ROWS, :] = b[1], b[2]
        pltpu.sync_copy(g_v, ds_hbm)

    shapes = (jax.ShapeDtypeStruct(hp.shape, F32), jax.ShapeDtypeStruct((64, 128), F32),
              jax.ShapeDtypeStruct((64, 128), F32))
    return _scan_call(body, "s5_scan_bwd", 2, shapes, kp)(dhp, hp, a1, a2)


WIN_GEOM = {"na": (ATT_BLK, 3, 0), "gqa": (128, 4, 128)}


def _win_rows(mode):
    kb, nw, _ = WIN_GEOM[mode]
    return kb * nw


@functools.lru_cache(maxsize=None)
def _window_patterns(mode, n_lat):
    nb = n_lat // ATT_BLK
    assert nb >= 3
    first_key = WIN_GEOM[mode][2] - ATT_BLK
    iq, ik = np.arange(ATT_BLK)[:, None], np.arange(_win_rows(mode))[None, :]

    def valid(ql):
        tq, ts = ATT_BLK * ql + iq, ATT_BLK * ql + first_key + ik
        if mode == "na":
            r, qcol, kr, kcol = tq // GRID_W, tq % GRID_W, ts // GRID_W, ts % GRID_W
            rs = np.clip(r - NA_ROWS // 2, 0, n_lat // GRID_W - NA_ROWS)
            cs = np.clip(qcol - NA_COLS // 2, 0, GRID_W - NA_COLS)
            return (kr >= rs) & (kr < rs + NA_ROWS) & (kcol >= cs) & (kcol < cs + NA_COLS)
        return (np.abs(tq - ts) <= WINDOW) & (ts >= 0) & (ts < n_lat)

    interior = valid(1)
    assert all(np.array_equal(valid(ql), interior) for ql in range(1, nb - 1))
    return np.stack([valid(0), interior, valid(nb - 1), np.zeros_like(interior)])


def _pattern_of_block(qb, nqb):
    return jnp.where(qb == 0, 3, jnp.where(qb == 1, 0, jnp.where(qb == nqb - 1, 2, 1)))


def _attn_block(q, k3, v3, kc, vc, bias0, bias1, sink):
    lane = lax.broadcasted_iota(jnp.int32, (1, 128), 1)
    scale = HEAD_DIM ** -0.5
    outs = []
    for e, bias in enumerate((bias0, bias1)):
        in_head = (lane < 64) if e == 0 else (lane >= 64)
        qe = jnp.where(in_head, q, 0.0)
        s_lat = mm_nt(qe, k3) * scale + bias
        s_ctx = mm_nt(qe, kc) * scale
        mx = jnp.maximum(jnp.max(s_lat, axis=1, keepdims=True), jnp.max(s_ctx, axis=1, keepdims=True))
        if sink is not None:
            srow = lax.broadcasted_iota(jnp.int32, sink.shape, 0)
            sv = jnp.sum(jnp.where(srow == e, sink, 0.0), keepdims=True) * (1.0 / 128.0)
            mx = jnp.maximum(mx, sv)
        mx = lax.stop_gradient(mx)
        e_lat = jnp.exp(s_lat - mx)
        e_ctx = jnp.exp(s_ctx - mx)
        den = jnp.sum(e_lat, axis=1, keepdims=True) + jnp.sum(e_ctx, axis=1, keepdims=True)
        if sink is not None:
            den = den + jnp.exp(sv - mx)
        inv = 1.0 / den
        outs.append(mm(e_lat * inv, v3) + mm(e_ctx * inv, vc))
    return jnp.where(lane < 64, outs[0], outs[1])


def _attn_specs(n_ctx, nqb, per_head, mode):
    kb, nw, skip = WIN_GEOM[mode]

    def kwin(s):
        return pl.BlockSpec(
            (kb, 128), lambda hp, qb: (jnp.maximum(qb - 1, 0) * (ATT_BLK // kb) + skip // kb + s, hp))

    q = pl.BlockSpec((ATT_BLK, 128), lambda hp, qb: (qb, hp))
    ctx = pl.BlockSpec((n_ctx, 128), lambda hp, qb: (0, hp))
    if per_head:
        bias = pl.BlockSpec((None, None, 2, ATT_BLK, kb * nw),
                            lambda hp, qb: (_pattern_of_block(qb, nqb), hp, 0, 0, 0))
    else:
        bias = pl.BlockSpec((None, ATT_BLK, kb * nw), lambda hp, qb: (_pattern_of_block(qb, nqb), 0, 0))
    sink = pl.BlockSpec((None, 8, 128), lambda hp, qb: (hp, 0, 0))
    return q, [kwin(s) for s in range(nw)], ctx, bias, sink


def _attn_fwd(q, kpad, vpad, kc, vc, bias, sink, *, mode, n_ctx):
    t = q.shape[0]
    per_head = bias.ndim == 5
    qs, kws, ctx, bias_s, sink_s = _attn_specs(n_ctx, t // ATT_BLK, per_head, mode)
    has_sink = sink is not None
    nw = len(kws)

    def body(*refs):
        q_ref, k_refs, v_refs = refs[0], refs[1:1 + nw], refs[1 + nw:1 + 2 * nw]
        kc_ref, vc_ref, b_ref = refs[1 + 2 * nw:4 + 2 * nw]
        s_ref = refs[4 + 2 * nw] if has_sink else None
        o_ref = refs[-1]
        k3 = jnp.concatenate([r[...] for r in k_refs], axis=0)
        v3 = jnp.concatenate([r[...] for r in v_refs], axis=0)
        b0, b1 = (b_ref[0], b_ref[1]) if per_head else (b_ref[...], b_ref[...])
        o_ref[...] = _attn_block(q_ref[...], k3, v3, kc_ref[...], vc_ref[...], b0, b1,
                                 s_ref[...] if has_sink else None)

    in_specs = [qs] + kws + kws + [ctx, ctx, bias_s] + ([sink_s] if has_sink else [])
    args = [q] + [kpad] * nw + [vpad] * nw + [kc, vc, bias] + ([sink] if has_sink else [])
    return pl.pallas_call(
        body, name=mode + "_attn_fwd", grid=(4, t // ATT_BLK),
        in_specs=in_specs, out_specs=qs, out_shape=jax.ShapeDtypeStruct((t, 512), F32),
        compiler_params=_cparams(("parallel", "parallel")),
    )(*args)


def _attn_bwd(q, kpad, vpad, kc, vc, bias, sink, do, *, mode, n_ctx):
    t = q.shape[0]
    nqb = t // ATT_BLK
    per_head = bias.ndim == 5
    qs, kws, ctx, bias_s, sink_s = _attn_specs(n_ctx, nqb, per_head, mode)
    has_sink = sink is not None
    nw = len(kws)
    n_in = 5 + 2 * nw + has_sink

    def body(*refs):
        q_ref, k_refs, v_refs = refs[0], refs[1:1 + nw], refs[1 + nw:1 + 2 * nw]
        kc_ref, vc_ref, b_ref = refs[1 + 2 * nw:4 + 2 * nw]
        s_ref = refs[4 + 2 * nw] if has_sink else None
        do_ref = refs[n_in - 1]
        outs = list(refs[n_in:-2])
        dk_acc, dv_acc = refs[-2:]
        dq_ref, dkp_ref, dvp_ref, dkc_ref, dvc_ref = outs[:5]
        hp = pl.program_id(0)
        outs = outs[5:]
        db_ref = outs.pop(0) if per_head else None
        ds_ref = outs.pop(0) if has_sink else None
        qb = pl.program_id(1)
        up = lambda r: r[...].astype(F32)
        k3 = jnp.concatenate([up(r) for r in k_refs], axis=0)
        v3 = jnp.concatenate([up(r) for r in v_refs], axis=0)
        prim = [up(q_ref), k3, v3, up(kc_ref), up(vc_ref)]
        if per_head:
            prim += [b_ref[0], b_ref[1]]
        if has_sink:
            prim += [s_ref[...]]

        def fn(*a):
            a = list(a)
            qv, k3v, v3v, kcv, vcv = a[:5]
            a = a[5:]
            b0 = a.pop(0) if per_head else b_ref[...]
            b1 = a.pop(0) if per_head else b0
            sk = a.pop(0) if has_sink else None
            return _attn_block(qv, k3v, v3v, kcv, vcv, b0, b1, sk)

        _, vjp = jax.vjp(fn, *prim)
        grads = list(vjp(do_ref[...]))
        dq_ref[...] = grads[0]

        @pl.when(qb == 0)
        def _():
            dk_acc[...] = jnp.zeros_like(dk_acc)
            dv_acc[...] = jnp.zeros_like(dv_acc)
            dkc_ref[...] = jnp.zeros_like(dkc_ref)
            dvc_ref[...] = jnp.zeros_like(dvc_ref)
            if has_sink:
                ds_ref[...] = jnp.zeros_like(ds_ref)

        window = pl.ds(pl.multiple_of(jnp.maximum(qb - 1, 0) * ATT_BLK + WIN_GEOM[mode][2], 128), _win_rows(mode))
        dk_acc[window, :] += grads[1]
        dv_acc[window, :] += grads[2]

        @pl.when(qb == nqb - 1)
        def _():
            cols = pl.ds(pl.multiple_of(hp * 128, 128), 128)
            pltpu.sync_copy(dk_acc, dkp_ref.at[:, cols])
            pltpu.sync_copy(dv_acc, dvp_ref.at[:, cols])

        dkc_ref[...] += grads[3]
        dvc_ref[...] += grads[4]
        rest_g = grads[5:]
        if per_head:
            opens = (qb <= 2) | (qb == nqb - 1)
            g0, g1 = rest_g.pop(0), rest_g.pop(0)

            @pl.when(opens)
            def _():
                db_ref[0] = g0
                db_ref[1] = g1

            @pl.when(jnp.logical_not(opens))
            def _():
                db_ref[0] += g0
                db_ref[1] += g1

        if has_sink:
            ds_ref[...] += rest_g.pop(0)

    hbm = pl.BlockSpec(memory_space=pl.ANY)
    in_specs = [qs] + kws + kws + [ctx, ctx, bias_s] + ([sink_s] if has_sink else []) + [qs]
    args = [q] + [kpad] * nw + [vpad] * nw + [kc, vc, bias] + ([sink] if has_sink else []) + [do]
    out_specs = [qs, hbm, hbm, ctx, ctx] + ([bias_s] if per_head else []) + ([sink_s] if has_sink else [])
    out_shape = [jax.ShapeDtypeStruct((t, 512), F32),
                 jax.ShapeDtypeStruct(kpad.shape, F32), jax.ShapeDtypeStruct(kpad.shape, F32),
                 jax.ShapeDtypeStruct((n_ctx, 512), F32), jax.ShapeDtypeStruct((n_ctx, 512), F32)]
    if per_head:
        out_shape.append(jax.ShapeDtypeStruct(bias.shape, F32))
    if has_sink:
        out_shape.append(jax.ShapeDtypeStruct((4, 8, 128), F32))
    res = list(pl.pallas_call(
        body, name=mode + "_attn_bwd", grid=(4, nqb),
        in_specs=in_specs, out_specs=out_specs, out_shape=tuple(out_shape),
        scratch_shapes=[pltpu.VMEM((kpad.shape[0], 128), F32), pltpu.VMEM((kpad.shape[0], 128), F32)],
        compiler_params=_cparams(("arbitrary", "arbitrary")),
    )(*args))
    dq, dkp, dvp, dkc, dvc = res[:5]
    res = res[5:]
    dbias = res.pop(0) if per_head else None
    dsink = res.pop(0) if has_sink else None
    return dq, dkp, dvp, dkc, dvc, dbias, dsink


def _loss_head(xt, target, n_ctx):
    t = xt.shape[0]
    nct = n_ctx // TOK

    def body(x_ref, t_ref, l_ref, d_ref):
        i = pl.program_id(0)

        @pl.when(i == 0)
        def _():
            l_ref[...] = jnp.zeros_like(l_ref)

        @pl.when(i < nct)
        def _():
            d_ref[...] = jnp.zeros_like(d_ref)

        @pl.when(i >= nct)
        def _():
            err = x_ref[...] - t_ref[...]
            d_ref[...] = err * (1.0 / D_MODEL)
            l_ref[...] += jnp.sum(err * err, keepdims=True) * (0.5 / D_MODEL)

    return pl.pallas_call(
        body, name="loss_head", grid=(t // TOK,),
        in_specs=[_row_spec(D_MODEL), pl.BlockSpec((TOK, D_MODEL), lambda i: (jnp.maximum(i - nct, 0), 0))],
        out_specs=[_const_spec((8, 128)), _row_spec(D_MODEL)],
        out_shape=(jax.ShapeDtypeStruct((8, 128), F32), jax.ShapeDtypeStruct((t, D_MODEL), F32)),
        compiler_params=_cparams(("arbitrary",)),
    )(xt, target)


PACK_W = 1024
SUM_STEPS = 8


def _sum_chips(recvs):
    def split(a):
        rows = a.shape[1]
        if rows % (8 * SUM_STEPS):
            return None
        return rows // SUM_STEPS

    def body(*refs):
        n = len(refs) // 2
        for r_ref, o_ref in zip(refs[:n], refs[n:]):
            up = lambda s: r_ref[s].astype(F32)
            o_ref[...] = ((up(0) + up(1)) + up(2)) + up(3)

    in_specs, out_specs = [], []
    for a in recvs:
        rb, tail = split(a), a.shape[2:]
        zeros = (0,) * len(tail)
        if rb is None:
            in_specs.append(pl.BlockSpec(a.shape, functools.partial(lambda i, z: (0, 0) + z, z=zeros)))
            out_specs.append(pl.BlockSpec(a.shape[1:], functools.partial(lambda i, z: (0,) + z, z=zeros)))
        else:
            in_specs.append(pl.BlockSpec((4, rb) + tail, functools.partial(lambda i, z: (0, i) + z, z=zeros)))
            out_specs.append(pl.BlockSpec((rb,) + tail, functools.partial(lambda i, z: (i,) + z, z=zeros)))
    return pl.pallas_call(
        body, name="sum_chips", grid=(SUM_STEPS,),
        in_specs=in_specs, out_specs=out_specs,
        out_shape=tuple(jax.ShapeDtypeStruct(a.shape[1:], F32) for a in recvs),
        compiler_params=_cparams(("arbitrary",)),
    )(*recvs)


ADAM_BLOCK_BYTES = 1 << 20


def _adamw(p_a, p_b, w, m, v, name):
    layers, rows, cols = w.shape
    tr = rows
    while tr % 16 == 0 and tr * cols * 4 > ADAM_BLOCK_BYTES:
        tr //= 2
    c1 = 1.0 / (1.0 - ADAM_B1 ** ADAM_STEP)
    c2 = 1.0 / (1.0 - ADAM_B2 ** ADAM_STEP)

    def body(a_ref, b_ref, w_ref, m_ref, v_ref, g_ref, d_ref, nm_ref, nv_ref):
        g = a_ref[...] + b_ref[...]
        nm = ADAM_B1 * m_ref[...] + (1.0 - ADAM_B1) * g
        nv = ADAM_B2 * v_ref[...] + (1.0 - ADAM_B2) * (g * g)
        g_ref[...] = g
        nm_ref[...] = nm
        nv_ref[...] = nv
        d_ref[...] = -ADAM_LR * ((nm * c1) / (jnp.sqrt(nv * c2) + ADAM_EPS) + ADAM_WD * w_ref[...])

    spec = pl.BlockSpec((None, tr, cols), lambda l, i: (l, i, 0))
    return pl.pallas_call(
        body, name=name, grid=(layers, rows // tr),
        in_specs=[spec] * 5, out_specs=[spec] * 4,
        out_shape=tuple(jax.ShapeDtypeStruct(w.shape, F32) for _ in range(4)),
        compiler_params=_cparams(("parallel", "parallel")),
    )(p_a, p_b, w, m, v)


MESH = pl.DeviceIdType.MESH
ANY_SPEC = pl.BlockSpec(memory_space=pl.ANY)


def _chip_exchange(srcs, out_shapes, src_window, dst_window, name):
    n = len(srcs)

    def body(*refs):
        src_refs, out_refs = refs[:n], refs[n:2 * n]
        send_sems, recv_sems, local_sems = refs[2 * n:]
        x, y, c = lax.axis_index("x"), lax.axis_index("y"), lax.axis_index("c")
        me = 2 * x + y
        peers = [(x, 1 - y), (1 - x, y), (1 - x, 1 - y)]

        def copy(k, j, from_chip, to_chip):
            px, py = peers[j]
            return pltpu.make_async_remote_copy(
                src_ref=src_window(k, src_refs[k], to_chip), dst_ref=dst_window(k, out_refs[k], from_chip),
                send_sem=send_sems.at[3 * k + j], recv_sem=recv_sems.at[3 * k + j],
                device_id=(px, py, c), device_id_type=MESH)

        local = [pltpu.make_async_copy(src_window(k, src_refs[k], me), dst_window(k, out_refs[k], me),
                                       local_sems.at[k]) for k in range(n)]
        for cp in local:
            cp.start()
        sends = [copy(k, j, me, 2 * px + py) for k in range(n) for j, (px, py) in enumerate(peers)]
        for cp in sends:
            cp.start()
        for k in range(n):
            for j, (px, py) in enumerate(peers):
                copy(k, j, 2 * px + py, me).wait_recv()
        for cp in sends:
            cp.wait_send()
        for cp in local:
            cp.wait()

    return pl.pallas_call(
        body, name=name, in_specs=[ANY_SPEC] * n, out_specs=[ANY_SPEC] * n,
        out_shape=tuple(out_shapes),
        scratch_shapes=[pltpu.SemaphoreType.DMA((3 * n,)), pltpu.SemaphoreType.DMA((3 * n,)),
                        pltpu.SemaphoreType.DMA((n,))],
    )(*srcs)


def _core_swap(srcs):
    n = len(srcs)

    def body(*refs):
        src_refs, out_refs, send_sems, recv_sems = refs[:n], refs[n:2 * n], refs[2 * n], refs[2 * n + 1]
        x, y, c = lax.axis_index("x"), lax.axis_index("y"), lax.axis_index("c")
        copies = [pltpu.make_async_remote_copy(
            src_ref=src_refs[k], dst_ref=out_refs[k], send_sem=send_sems.at[k], recv_sem=recv_sems.at[k],
            device_id=(x, y, 1 - c), device_id_type=MESH) for k in range(n)]
        for cp in copies:
            cp.start()
        for cp in copies:
            cp.wait()

    return pl.pallas_call(
        body, name="core_swap", in_specs=[ANY_SPEC] * n, out_specs=[ANY_SPEC] * n,
        out_shape=tuple(jax.ShapeDtypeStruct(s.shape, s.dtype) for s in srcs),
        scratch_shapes=[pltpu.SemaphoreType.DMA((n,)), pltpu.SemaphoreType.DMA((n,))],
    )(*srcs)


def _col_window(ref, start, size):
    idx = (slice(None),) * (len(ref.shape) - 1) + (pl.ds(pl.multiple_of(start, 128), size),)
    return ref.at[idx]


def _row_window(ref, start, size):
    idx = (slice(None),) * (len(ref.shape) - 2) + (pl.ds(pl.multiple_of(start, 8), size), slice(None))
    return ref.at[idx]


N_SHARD_IN = 2624
WIN_W = 2944
WIN_START = (0, 2560, 5248, 7808)
WIN_PIECES = (((0, 2624),), ((64, 2688),), ((0, 640), (896, 2880)), ((320, 2944),))


def _core_layers(ref, lay, core):
    half = lay // 2
    return ref.at[pl.ds(core * half, half)]


def _gather_weights(w_ada, w_in, w_glu, w_br, w_out, conv_w):
    lay = w_ada.shape[0]
    assert lay % 2 == 0
    sizes = (768, None, 128, 256, 256, 128)

    def dst(k, ref, s, core=None):
        core = lax.axis_index("c") if core is None else core
        if k == 1:
            return _core_layers(ref.at[s], lay, core)
        ref = _core_layers(ref, lay, core)
        if k in (2, 4):
            return _row_window(ref, s * sizes[k], sizes[k])
        return _col_window(ref, s * sizes[k], sizes[k])

    shapes = (jax.ShapeDtypeStruct((lay, D_MODEL, 3 * D_MODEL), w_ada.dtype),
              jax.ShapeDtypeStruct((4,) + w_in.shape, w_in.dtype),
              jax.ShapeDtypeStruct((lay, MIX_W, MIX_W), w_glu.dtype),
              jax.ShapeDtypeStruct((lay, 4, MIX_W, D_MODEL), w_br.dtype),
              jax.ShapeDtypeStruct((lay, D_MODEL, D_MODEL), w_out.dtype),
              jax.ShapeDtypeStruct((lay, 8, MIX_W), conv_w.dtype))
    halves = _chip_exchange((w_ada, w_in, w_glu, w_br, w_out, conv_w), shapes,
                            lambda k, ref, t: _core_layers(ref, lay, lax.axis_index("c")), dst, "gather_weights")
    return _merge_core_halves(halves, lay)


def _merge_core_halves(halves, lay):
    n = len(halves)

    def body(*refs):
        src_refs, out_refs, send_sems, recv_sems, local_sems = refs[:n], refs[n:2 * n], *refs[2 * n:]
        x, y, c = lax.axis_index("x"), lax.axis_index("y"), lax.axis_index("c")

        def part(k, ref, core):
            if k == 1:
                half = lay // 2
                return ref.at[:, pl.ds(core * half, half)]
            return _core_layers(ref, lay, core)

        keep = [pltpu.make_async_copy(part(k, src_refs[k], c), part(k, out_refs[k], c), local_sems.at[k])
                for k in range(n)]
        give = [pltpu.make_async_remote_copy(
            src_ref=part(k, src_refs[k], c), dst_ref=part(k, out_refs[k], c), send_sem=send_sems.at[k],
            recv_sem=recv_sems.at[k], device_id=(x, y, 1 - c), device_id_type=MESH) for k in range(n)]
        for cp in keep + give:
            cp.start()
        for k in range(n):
            pltpu.make_async_remote_copy(
                src_ref=part(k, src_refs[k], c), dst_ref=part(k, out_refs[k], 1 - c), send_sem=send_sems.at[k],
                recv_sem=recv_sems.at[k], device_id=(x, y, 1 - c), device_id_type=MESH).wait_recv()
        for cp in give:
            cp.wait_send()
        for cp in keep:
            cp.wait()

    return pl.pallas_call(
        body, name="gather_merge", in_specs=[ANY_SPEC] * n, out_specs=[ANY_SPEC] * n,
        out_shape=tuple(jax.ShapeDtypeStruct(h.shape, h.dtype) for h in halves),
        scratch_shapes=[pltpu.SemaphoreType.DMA((n,)), pltpu.SemaphoreType.DMA((n,)), pltpu.SemaphoreType.DMA((n,))],
    )(*halves)


def _scatter_grads(dw_ada, dw_in, dw_glu, dw_br, dw_out, dconv_w, small):
    def src(k, ref, t):
        if k == 0:
            return _col_window(ref, t * 768, 768)
        if k == 1:
            start = jnp.where(t == 0, WIN_START[0], jnp.where(t == 1, WIN_START[1],
                              jnp.where(t == 2, WIN_START[2], WIN_START[3])))
            return _col_window(ref, start, WIN_W)
        if k == 2:
            return _row_window(ref, t * 128, 128)
        if k == 3:
            return _col_window(ref, t * 256, 256)
        if k == 4:
            return _row_window(ref, t * 256, 256)
        if k == 5:
            return _col_window(ref, t * 128, 128)
        return ref

    pieces = ((D_MODEL, 768), (D_MODEL, WIN_W), (128, MIX_W), (4, MIX_W, 256), (256, D_MODEL), (8, 128), small.shape)
    srcs = (dw_ada, dw_in, dw_glu, dw_br, dw_out, dconv_w, small)
    shapes = tuple(jax.ShapeDtypeStruct((4,) + p, s.dtype) for p, s in zip(pieces, srcs))
    return _chip_exchange(srcs, shapes,
                          src, lambda k, ref, s: ref.at[s], "scatter_grads")


def _s5_tables(a_re, a_im, log_dt, b_re, b_im, c_re, c_im):
    ln = S5_CHUNK
    hi = lax.Precision.HIGHEST
    dt = jnp.exp(log_dt)[..., None]
    mag = jnp.exp(dt * a_re)
    abr = mag * jnp.cos(dt * a_im)
    abi = mag * jnp.sin(dt * a_im)
    den = a_re * a_re + a_im * a_im
    fr = ((abr - 1.0) * a_re + abi * a_im) / den
    fi = (abi * a_re - (abr - 1.0) * a_im) / den
    bbr = fr[..., None] * b_re - fi[..., None] * b_im
    bbi = fr[..., None] * b_im + fi[..., None] * b_re
    n = jnp.arange(ln + 1, dtype=F32)[:, None, None, None]
    pm = jnp.exp(n * dt * a_re)
    er = pm * jnp.cos(n * dt * a_im)
    ei = pm * jnp.sin(n * dt * a_im)
    e3 = lambda e, b, c: jnp.einsum("tdgp,dgpa,dgbp->dgabt", e, b, c, precision=hi)
    gt = e3(er[:ln], bbr, c_re) - e3(er[:ln], bbi, c_im) - e3(ei[:ln], bbr, c_im) - e3(ei[:ln], bbi, c_re)
    by_dir = lambda fwd, bwd: jnp.stack([fwd[:, 0], bwd[:, 1]], axis=1)
    erj, eij = by_dir(er[:ln][::-1], er[:ln]), by_dir(ei[:ln][::-1], ei[:ln])
    e2 = lambda e, b: jnp.einsum("jdgp,dgpa->dgajp", e, b, precision=hi)
    w = jnp.concatenate([e2(erj, bbr) - e2(eij, bbi), e2(erj, bbi) + e2(eij, bbr)], axis=-1)
    er1, ei1 = by_dir(er[1:], er[1:][::-1]), by_dir(ei[1:], ei[1:][::-1])
    ev = lambda c, e: jnp.einsum("dgbp,idgp->dgpbi", c, e, precision=hi)
    v = jnp.concatenate([ev(c_re, er1) - ev(c_im, ei1), -(ev(c_re, ei1) + ev(c_im, er1))], axis=2)
    a1 = jnp.concatenate([er[ln], er[ln]], axis=-1)
    a2 = jnp.concatenate([-ei[ln], ei[ln]], axis=-1)
    return (gt.transpose(1, 2, 3, 0, 4).reshape(S5_GROUPS, 256, 2 * ln),
            w.transpose(1, 2, 3, 0, 4).reshape(S5_GROUPS, S5_CH * ln, 256),
            v.transpose(1, 0, 2, 3, 4).reshape(S5_GROUPS, 256, S5_CH * ln),
            a1.reshape(64, 128), a2.reshape(64, 128))


def _lag_onehot():
    ln = S5_CHUNK
    j, i = np.meshgrid(np.arange(ln), np.arange(ln), indexing="ij")
    lag = np.arange(ln)[:, None, None]
    z = np.concatenate([lag == (i - j)[None], lag == (j - i)[None]], axis=0).astype(np.float32)
    return jnp.broadcast_to(jnp.asarray(z.reshape(2 * ln, ln * ln), BF16), (S5_GROUPS, 2 * ln, ln * ln))


def _toeplitz(gt):
    ln = S5_CHUNK
    flat = _matmul(gt, _lag_onehot(), out_dtype=BF16, name="s5_toeplitz")
    return (flat.reshape(S5_GROUPS, S5_CH, S5_CH, ln, ln).transpose(0, 1, 3, 2, 4)
            .reshape(S5_GROUPS, S5_CH * ln, S5_CH * ln))


def _toeplitz_fold(dk):
    ln = S5_CHUNK
    flat = dk.reshape(S5_GROUPS, S5_CH, ln, S5_CH, ln).transpose(0, 1, 3, 2, 4).reshape(S5_GROUPS, 256, ln * ln)
    return _matmul(flat, _lag_onehot(), trans_b=True, name="s5_toeplitz_fold")


def _chunk_rows(t):
    k = t // S5_CHUNK
    return k, -(-k // 128) * 128


def _to_chunks(u):
    k, kp = _chunk_rows(u.shape[0])
    v = u.reshape(k, S5_CHUNK, S5_GROUPS, S5_CH).transpose(2, 0, 3, 1).reshape(S5_GROUPS, k, S5_CH * S5_CHUNK)
    return jnp.pad(v, ((0, 0), (0, kp - k), (0, 0)))


def _from_chunks(y, t):
    k, _ = _chunk_rows(t)
    return y[:, :k].reshape(S5_GROUPS, k, S5_CH, S5_CHUNK).transpose(1, 3, 0, 2).reshape(t, MIX_W)


def _states_to_rows(s):
    kp = s.shape[1]
    return s.reshape(S5_GROUPS, kp, 2, 128).transpose(1, 2, 0, 3).reshape(kp, 64, 128)


def _rows_to_states(h):
    kp = h.shape[0]
    return h.reshape(kp, 2, S5_GROUPS, 128).transpose(2, 0, 1, 3).reshape(S5_GROUPS, kp, 256)


def _na_bias(rel_bias):
    a, m = np.meshgrid(np.arange(4), np.arange(12), indexing="ij")
    di = np.clip(m - a + 3, 0, 2 * NA_ROWS - 2).reshape(-1)
    qc, kc = np.meshgrid(np.arange(GRID_W), np.arange(GRID_W), indexing="ij")
    dj = np.clip(kc - qc + NA_COLS - 1, 0, 2 * NA_COLS - 2).reshape(-1)
    oh_i = jnp.asarray(di[:, None] == np.arange(2 * NA_ROWS - 1)[None, :], F32)
    oh_j = jnp.asarray(dj[:, None] == np.arange(2 * NA_COLS - 1)[None, :], F32)
    hi = lax.Precision.HIGHEST
    cols = jnp.einsum("hij,cj->hic", rel_bias, oh_j, precision=hi)
    full = jnp.einsum("ri,hic->hrc", oh_i, cols, precision=hi)
    full = full.reshape(N_HEADS, 4, 12, GRID_W, GRID_W).transpose(0, 1, 3, 2, 4)
    return full.reshape(4, 2, ATT_BLK, 3 * ATT_BLK)


def _rope_tables(n_ctx, n_lat):
    tok = jnp.arange(n_lat, dtype=jnp.int32)
    row = (tok // GRID_W).astype(F32)
    col = (tok % GRID_W).astype(F32)
    inv = ROPE_BASE ** (-jnp.arange(ROPE_PAIRS, dtype=F32) / ROPE_PAIRS)
    ang = jnp.concatenate([row[:, None] * inv, col[:, None] * inv], axis=-1)
    cos, sin = jnp.cos(ang), jnp.sin(ang)
    cos = jnp.tile(jnp.concatenate([cos, cos], axis=-1), (1, 2))
    sin = jnp.tile(jnp.concatenate([-sin, sin], axis=-1), (1, 2))
    return (jnp.concatenate([jnp.ones((n_ctx, 128), F32), cos], axis=0),
            jnp.concatenate([jnp.zeros((n_ctx, 128), F32), sin], axis=0))


def _pad_blocks(a, n_ctx):
    return jnp.pad(a[n_ctx:], ((ATT_BLK, ATT_BLK), (0, 0)))


def _layer_fwd(xt, cc, w, rope, n_ctx):
    t = xt.shape[0]
    sv = {}
    mod = _adaln_fwd(cc, w["w_ada"], w["b_ada"].reshape(1, -1))
    mod4 = mod[:2].reshape(2, 3, 1, D_MODEL)
    h = _modnorm_fwd(xt, w["norm_g"].reshape(1, -1), mod4, n_ctx)
    z = _matmul(h, w["w_in"], name="proj_fwd")

    s5_args = (w["s5_a_re"], w["s5_a_im"], w["s5_log_dt"], w["s5_b_re"], w["s5_b_im"], w["s5_c_re"], w["s5_c_im"])
    (gt, tw, tv, a1, a2), tab_vjp = jax.vjp(_s5_tables, *s5_args)
    ktoe = _toeplitz(gt)
    tw, tv = tw.astype(BF16), tv.astype(BF16)
    uc = _to_chunks(z[:, :MIX_W].astype(BF16))
    st = _matmul(uc, tw, name="s5_chunk_state")
    hprev = _s5_scan_fwd(_states_to_rows(st), a1, a2, t // S5_CHUNK, n_ctx // S5_CHUNK)
    uh = jnp.concatenate([uc, _rows_to_states(hprev).astype(BF16)], axis=2)
    ysum = _from_chunks(_matmul(uh, jnp.concatenate([ktoe, tv], axis=1), name="s5_chunk_out"), t)
    s5_d = w["s5_d"].reshape(1, MIX_W)
    y_s5 = _s5post_fwd(ysum, z, s5_d, w["s5_w_glu"])

    conv_w = w["conv_w"]
    y_conv = _conv_fwd(z, conv_w, w["conv_b"].reshape(1, -1), n_ctx)

    gains = (jnp.tile(w["na_q_g"], 8)[None], jnp.tile(w["na_k_g"], 8)[None],
             jnp.tile(w["gqa_q_g"], 8)[None], jnp.tile(w["gqa_k_g"], 2)[None])
    q_na, k_na, v_na, q_g, k_g, v_g = _prep_fwd(z, gains, rope)
    bias, bias_vjp = jax.vjp(_na_bias, w["na_rel_bias"])
    sink = jnp.zeros((4, 8, 128), F32).at[:, :2, :].set(
        jnp.broadcast_to(w["gqa_sink"].reshape(4, 2, 1), (4, 2, 128)))
    na_tab = jnp.where(_window_patterns("na", t - n_ctx)[:, None, None], bias[None], NEG_INF)
    gqa_tab = jnp.where(_window_patterns("gqa", t - n_ctx), 0.0, NEG_INF).astype(F32)
    na_in = (q_na, _pad_blocks(k_na, n_ctx), _pad_blocks(v_na, n_ctx), k_na[:n_ctx], v_na[:n_ctx], na_tab, None)
    gqa_in = (q_g, _pad_blocks(k_g, n_ctx), _pad_blocks(v_g, n_ctx), k_g[:n_ctx], v_g[:n_ctx], gqa_tab, sink)
    y_na = _attn_fwd(*na_in, mode="na", n_ctx=n_ctx)
    y_gqa = _attn_fwd(*gqa_in, mode="gqa", n_ctx=n_ctx)
    ys = (y_s5, y_conv, y_na, y_gqa)
    xt_new = _merge_fwd(xt, ys, z, mod4, w["w_br"], w["w_out"], n_ctx)
    sv.update(xt=xt, mod4=mod4, h=h, z=z, tab_vjp=tab_vjp, ktoe=ktoe, tw=tw, tv=tv, a1=a1, a2=a2, uc=uc,
              hprev=hprev, uh=uh, ysum=ysum, s5_d=s5_d, conv_w=conv_w, gains=gains, bias_vjp=bias_vjp,
              na_in=na_in, gqa_in=gqa_in, ys=ys)
    return xt_new, sv


def _layer_bwd(dxt_new, sv, cc, w, rope, n_ctx):
    t = dxt_new.shape[0]
    z, mod4 = sv["z"], sv["mod4"]
    dys, dgt, dmg, dgate, dw_br, dw_out = _merge_bwd(dxt_new, sv["ys"], z, mod4, w["w_br"], w["w_out"], n_ctx)

    dpre, du_skip, dd, dw_glu = _s5post_bwd(sv["ysum"], z, sv["s5_d"], w["s5_w_glu"], dys[0])
    dyc = _to_chunks(dpre)
    dhp = _matmul(dyc, sv["tv"], trans_b=True, name="s5_bwd_state")
    ds, da1, da2 = _s5_scan_bwd(_states_to_rows(dhp), sv["hprev"], sv["a1"], sv["a2"],
                                t // S5_CHUNK, n_ctx // S5_CHUNK)
    ds = _rows_to_states(ds).astype(BF16)
    duc = _matmul(jnp.concatenate([dyc, ds], axis=2), jnp.concatenate([sv["ktoe"], sv["tw"]], axis=2),
                  trans_b=True, out_dtype=BF16, name="s5_bwd_u")
    dkv = _matmul(sv["uh"].transpose(0, 2, 1), dyc, out_dtype=BF16, name="s5_bwd_kv")
    dtw = _matmul(sv["uc"].transpose(0, 2, 1), ds, name="s5_bwd_w")
    dgt_tab = _toeplitz_fold(dkv[:, :S5_CH * S5_CHUNK])
    s5_grads = sv["tab_vjp"]((dgt_tab, dtw, dkv[:, S5_CH * S5_CHUNK:].astype(F32), da1, da2))
    du_scan = _from_chunks(duc, t)

    dzv, dzb, dzc, dconv_w, dconv_b = _conv_bwd(z, sv["conv_w"], w["conv_b"].reshape(1, -1), dys[1], n_ctx)

    dq_na, dk_na, dv_na, dkc_na, dvc_na, dbias, _ = _attn_bwd(*sv["na_in"], dys[2], mode="na", n_ctx=n_ctx)
    dq_g, dk_g, dv_g, dkc_g, dvc_g, _, dsink = _attn_bwd(*sv["gqa_in"], dys[3], mode="gqa", n_ctx=n_ctx)
    pb = _prep_bwd(z, sv["gains"], rope, (dq_na, dq_g), (dk_na, dv_na, dk_g, dv_g),
                   (dkc_na, dvc_na, dkc_g, dvc_g), du_skip, du_scan, n_ctx)
    dz_naq, dz_nak, dz_nav, dz_gq, dz_gk, dz_gv, dz_u, dg_naq, dg_nak, dg_gq, dg_gk = pb

    dz = jnp.concatenate([dz_u, dgt[0], dzv, dzb, dzc, dgt[1], dz_naq, dz_nak, dz_nav, dgt[2], dz_gq, dz_gk, dz_gv,
                          jnp.zeros((t, OFF["gqa_gate"] - OFF["pad"]), BF16), dgt[3], *dmg], axis=1)
    dh = _matmul(dz, w["w_in"], trans_b=True, name="proj_bwd_x")
    dw_in = _matmul(sv["h"].T, dz, out_dtype=BF16, name="proj_bwd_w")
    dxt, dnorm_g, dshift, dscale = _modnorm_bwd(sv["xt"], w["norm_g"].reshape(1, -1), mod4, dh, dxt_new, n_ctx)
    dmod = jnp.concatenate([dshift, dscale, dgate], axis=1).reshape(2, 3 * D_MODEL)
    dcc, dw_ada, db_ada = _adaln_bwd(cc, w["w_ada"], jnp.pad(dmod, ((0, 6), (0, 0))))

    (drel,) = sv["bias_vjp"](dbias.sum(0))
    grads = dict(
        norm_g=dnorm_g[0], w_ada=dw_ada, b_ada=db_ada[0], w_in=dw_in,
        s5_a_re=s5_grads[0], s5_a_im=s5_grads[1], s5_log_dt=s5_grads[2], s5_b_re=s5_grads[3], s5_b_im=s5_grads[4],
        s5_c_re=s5_grads[5], s5_c_im=s5_grads[6], s5_d=dd.reshape(S5_GROUPS, S5_CH), s5_w_glu=dw_glu,
        conv_w=dconv_w, conv_b=dconv_b[0],
        na_q_g=dg_naq.reshape(8, HEAD_DIM).sum(0), na_k_g=dg_nak.reshape(8, HEAD_DIM).sum(0), na_rel_bias=drel,
        gqa_q_g=dg_gq.reshape(8, HEAD_DIM).sum(0), gqa_k_g=dg_gk.reshape(2, HEAD_DIM).sum(0),
        gqa_sink=dsink[:, :2, :].sum(-1).reshape(8), w_br=dw_br, w_out=dw_out)
    return dxt, dcc, grads


SHARDED = ("w_ada", "w_in", "s5_w_glu", "conv_w", "w_br", "w_out")
REPLICATED = ("norm_g", "b_ada", "s5_a_re", "s5_a_im", "s5_log_dt", "s5_b_re", "s5_b_im", "s5_c_re", "s5_c_im",
              "s5_d", "conv_b", "na_q_g", "na_k_g", "na_rel_bias", "gqa_q_g", "gqa_k_g", "gqa_sink")
WEIGHTS = ("c_ctx", "norm_g", "w_ada", "b_ada", "w_in", "s5_a_re", "s5_a_im", "s5_log_dt", "s5_b_re", "s5_b_im",
           "s5_c_re", "s5_c_im", "s5_d", "s5_w_glu", "conv_w", "conv_b", "na_q_g", "na_k_g", "na_rel_bias",
           "gqa_q_g", "gqa_k_g", "gqa_sink", "w_br", "w_out")


def _pack(pieces, row_multiple, dtype):
    flat = jnp.concatenate([p.reshape(-1).astype(dtype) for p in pieces])
    rows = -(-flat.shape[0] // PACK_W)
    rows = -(-rows // row_multiple) * row_multiple
    return jnp.pad(flat, (0, rows * PACK_W - flat.shape[0])).reshape(rows, PACK_W)


def _unpack(buf, shapes):
    flat = buf.reshape(-1)
    out, pos = [], 0
    for shp in shapes:
        size = int(np.prod(shp))
        out.append(flat[pos:pos + size].reshape(shp))
        pos += size
    return out


def _local_step(x, ctx, target, c_vec, c_ctx, layers):
    depth = len(layers)
    n_ctx, n_lat = ctx.shape[0], x.shape[0]
    cc = jnp.zeros((8, D_MODEL), F32).at[0].set(c_ctx).at[1].set(c_vec)
    rope = _rope_tables(n_ctx, n_lat)
    xt = jnp.concatenate([ctx, x], axis=0)
    saved = []
    for l in range(depth):
        xt, sv = _layer_fwd(xt, cc, layers[l], rope, n_ctx)
        saved.append(sv)
    loss_tile, dxt = _loss_head(xt, target, n_ctx)
    grads = [None] * depth
    dc_ctx = jnp.zeros((D_MODEL,), F32)
    for l in reversed(range(depth)):
        dxt, dcc, grads[l] = _layer_bwd(dxt, saved[l], cc, layers[l], rope, n_ctx)
        dc_ctx = dc_ctx + dcc[0]
    return loss_tile[0, 0], dxt[n_ctx:][None], dc_ctx, grads


def kernel(x, c, ctx, c_ctx, norm_g, w_ada, b_ada, w_in, s5_a_re, s5_a_im, s5_log_dt, s5_b_re, s5_b_im,
           s5_c_re, s5_c_im, s5_d, s5_w_glu, conv_w, conv_b, na_q_g, na_k_g, na_rel_bias, gqa_q_g,
           gqa_k_g, gqa_sink, w_br, w_out, loss_target, m_c_ctx, m_norm_g, m_w_ada, m_b_ada, m_w_in,
           m_s5_a_re, m_s5_a_im, m_s5_log_dt, m_s5_b_re, m_s5_b_im, m_s5_c_re, m_s5_c_im, m_s5_d,
           m_s5_w_glu, m_conv_w, m_conv_b, m_na_q_g, m_na_k_g, m_na_rel_bias, m_gqa_q_g, m_gqa_k_g,
           m_gqa_sink, m_w_br, m_w_out, v_c_ctx, v_norm_g, v_w_ada, v_b_ada, v_w_in, v_s5_a_re,
           v_s5_a_im, v_s5_log_dt, v_s5_b_re, v_s5_b_im, v_s5_c_re, v_s5_c_im, v_s5_d, v_s5_w_glu,
           v_conv_w, v_conv_b, v_na_q_g, v_na_k_g, v_na_rel_bias, v_gqa_q_g, v_gqa_k_g, v_gqa_sink,
           v_w_br, v_w_out):
    a = dict(locals())
    depth = a["norm_g"].shape[0]
    x, ctx, target = a["x"][0], a["ctx"][0], a["loss_target"][0]
    n_ctx, n_lat = ctx.shape[0], x.shape[0]
    assert n_ctx % ATT_BLK == 0 and n_lat % (4 * GRID_W) == 0 and n_lat // GRID_W >= NA_ROWS

    cast = lambda n: a[n].astype(BF16)
    conv8 = jnp.pad(a["conv_w"], ((0, 0), (0, 5), (0, 0)))
    g_ada, g_in, g_glu, g_br, g_out, g_conv = _gather_weights(
        cast("w_ada"), cast("w_in"), cast("s5_w_glu"), cast("w_br"), cast("w_out"), conv8)
    zpad = jnp.zeros((D_MODEL, OFF["gqa_gate"] - OFF["pad"]), BF16)
    split = OFF["pad"] - 2 * N_SHARD_IN
    layers = []
    for l in range(depth):
        w = {n: a[n][l] for n in REPLICATED}
        w.update(w_ada=g_ada[l], s5_w_glu=g_glu[l], w_br=g_br[l], w_out=g_out[l], conv_w=g_conv[l])
        w["w_in"] = jnp.concatenate([g_in[0, l], g_in[1, l], g_in[2, l][:, :split], zpad, g_in[2, l][:, split:],
                                     g_in[3, l]], axis=1)
        layers.append(w)

    loss_local, grad_x, dc_ctx, grads = _local_step(x, ctx, target, a["c"][0], a["c_ctx"], layers)
    loss = lax.psum(loss_local, ("x", "y", "c"))

    chip = 2 * lax.axis_index("x") + lax.axis_index("y")
    take = [functools.partial(lambda win, pc: jnp.concatenate([win[:, lo:hi] for lo, hi in pc], axis=1), pc=pc)
            for pc in WIN_PIECES]

    def small_pack(values, c_ctx_value, l):
        pieces = [values[n] for n in REPLICATED]
        pieces.append(c_ctx_value if l == 0 else jnp.zeros((D_MODEL,), F32))
        return _pack(pieces, 8 * SUM_STEPS, F32)

    mine, theirs = [], []
    for l in range(depth):
        g = grads[l]
        small = small_pack(g, dc_ctx, l)
        recv = _scatter_grads(g["w_ada"], g["w_in"], g["s5_w_glu"], g["w_br"], g["w_out"], g["conv_w"], small)
        part = list(_sum_chips([r.reshape(4, -1, r.shape[-1]) for r in recv]))
        part[1] = lax.switch(chip, take, part[1])
        mine.append(part)
        theirs.append(_core_swap(part))

    families = ("w_ada", "w_in", "s5_w_glu", "w_br", "w_out", "conv_w")
    out = {}
    for k, n in enumerate(families):
        p, q = jnp.stack([m[k] for m in mine]), jnp.stack([t[k] for t in theirs])
        if n == "conv_w":
            p, q = p[:, :3], q[:, :3]
        as3d = lambda arr: arr.reshape(depth, -1, arr.shape[-1])
        res = _adamw(p, q, as3d(a[n]), as3d(a["m_" + n]), as3d(a["v_" + n]), "adamw_" + n)
        out[n] = [r.reshape(a[n].shape) for r in res]
    p, q = jnp.stack([m[6] for m in mine]), jnp.stack([t[6] for t in theirs])
    packs = [jnp.stack([small_pack({n: a[pre + n][l] for n in REPLICATED}, a[pre + "c_ctx"], l)
                        for l in range(depth)]) for pre in ("", "m_", "v_")]
    res = _adamw(p, q, *packs, "adamw_small")
    shapes = [a[n].shape[1:] for n in REPLICATED] + [a["c_ctx"].shape]
    per_layer = [[_unpack(r[l], shapes) for l in range(depth)] for r in res]
    for j, n in enumerate(REPLICATED):
        out[n] = [jnp.stack([per_layer[key][l][j] for l in range(depth)]) for key in range(4)]
    out["c_ctx"] = [per_layer[key][0][-1] for key in range(4)]
    results = [loss, grad_x]
    for key in range(4):
        results += [out[n][key] for n in WEIGHTS]
    return tuple(results)
```

```python
import functools

import numpy as np
import jax
import jax.numpy as jnp
from jax import lax
from jax.experimental import pallas as pl
from jax.experimental.pallas import tpu as pltpu

F32 = jnp.float32
BF16 = jnp.bfloat16

D_MODEL = 1024
MIX_W = 512
GRID_W = 64
HEAD_DIM = 64
N_HEADS = 8
S5_GROUPS = 32
S5_CH = 16
S5_CHUNK = 32
NA_ROWS = 8
NA_COLS = 16
WINDOW = 128
ROPE_BASE = 10000.0
ROPE_PAIRS = 16
EPS = 1e-6
NEG_INF = -1e30
ATT_BLK = 256
TOK = 256
VMEM_LIMIT = 56 * 1024 * 1024

ADAM_LR, ADAM_B1, ADAM_B2, ADAM_EPS, ADAM_WD, ADAM_STEP = 0.001, 0.9, 0.999, 1e-8, 0.01, 10

OFF = dict(s5_u=0, s5_gate=512, conv_v=1024, conv_b=1536, conv_c=2048, conv_gate=2560,
           na_q=3072, na_k=3584, na_v=4096, na_gate=4608, gqa_q=5120, gqa_k=5632, gqa_v=5760,
           pad=5888, gqa_gate=6144, merge_s5=6656, merge_conv=7680, merge_na=8704, merge_gqa=9728)
N_Z = 10752
GATE_OFFS = (OFF["s5_gate"], OFF["conv_gate"], OFF["na_gate"], OFF["gqa_gate"])
MERGE_OFFS = (OFF["merge_s5"], OFF["merge_conv"], OFF["merge_na"], OFF["merge_gqa"])


def _cparams(sem):
    return pltpu.CompilerParams(dimension_semantics=sem, vmem_limit_bytes=VMEM_LIMIT)


def _dot(a, b, ca, cb):
    return lax.dot_general(a.astype(BF16), b.astype(BF16), (((ca,), (cb,)), ((), ())),
                           preferred_element_type=F32)


def _dot_tn(a, b):
    return _dot(a.astype(F32).T, b, 1, 0)


@jax.custom_vjp
def mm(a, b):
    return _dot(a, b, 1, 0)


@jax.custom_vjp
def mm_nt(a, b):
    return _dot(a, b, 1, 1)


@jax.custom_vjp
def mm_tn(a, b):
    return _dot_tn(a, b)


mm.defvjp(lambda a, b: (mm(a, b), (a, b)), lambda r, g: (mm_nt(g, r[1]), mm_tn(r[0], g)))
mm_nt.defvjp(lambda a, b: (mm_nt(a, b), (a, b)), lambda r, g: (mm(g, r[1]), mm_tn(g, r[0])))
mm_tn.defvjp(lambda a, b: (mm_tn(a, b), (a, b)), lambda r, g: (mm_nt(r[1], g), mm(r[0], g)))


@functools.partial(jax.custom_vjp, nondiff_argnums=(1,))
def lane_roll(x, shift):
    return pltpu.roll(x, shift, 1)


lane_roll.defvjp(lambda x, shift: (lane_roll(x, shift), None),
                 lambda shift, _, g: (lane_roll(g, (g.shape[1] - shift) % g.shape[1]),))


def _silu(x):
    return x * jax.nn.sigmoid(x)


def _dsilu(x):
    s = jax.nn.sigmoid(x)
    return s * (1.0 + x * (1.0 - s))


def _pick(n, prefs):
    for p in prefs:
        if n % p == 0:
            return p
    return n


def _matmul(a, b, *, trans_b=False, out_dtype=F32, tm=None, tn=None, tk=None, name):
    squeeze = a.ndim == 2
    if squeeze:
        a, b = a[None], b[None]
    nb, m, k = a.shape
    n = b.shape[1] if trans_b else b.shape[2]
    tm = tm or _pick(m, (1280, 1024, 640, 512, 256, 128))
    tn = tn or _pick(n, (1536, 1024, 512, 256, 128))
    tk = tk or _pick(k, (1536, 1280, 1024, 768, 640, 512, 256, 128))
    nk = k // tk

    def body(a_ref, b_ref, o_ref, *scr):
        part = _dot(a_ref[...], b_ref[...], 1, 1 if trans_b else 0)
        if nk == 1:
            o_ref[...] = part.astype(out_dtype)
        else:
            acc = scr[0]
            kk = pl.program_id(3)

            @pl.when(kk == 0)
            def _():
                acc[...] = part

            @pl.when(kk > 0)
            def _():
                acc[...] += part

            @pl.when(kk == nk - 1)
            def _():
                o_ref[...] = acc[...].astype(out_dtype)

    if trans_b:
        b_spec = pl.BlockSpec((None, tn, tk), lambda bb, i, j, kk: (bb, j, kk))
    else:
        b_spec = pl.BlockSpec((None, tk, tn), lambda bb, i, j, kk: (bb, kk, j))
    out = pl.pallas_call(
        body, name=name,
        grid=(nb, m // tm, n // tn, nk),
        in_specs=[pl.BlockSpec((None, tm, tk), lambda bb, i, j, kk: (bb, i, kk)), b_spec],
        out_specs=pl.BlockSpec((None, tm, tn), lambda bb, i, j, kk: (bb, i, j)),
        out_shape=jax.ShapeDtypeStruct((nb, m, n), out_dtype),
        scratch_shapes=[] if nk == 1 else [pltpu.VMEM((tm, tn), F32)],
        compiler_params=_cparams(("parallel", "parallel", "parallel", "arbitrary")),
    )(a, b)
    return out[0] if squeeze else out


def _adaln_fn(cc, w, b):
    return mm(_silu(cc), w) + b


def _adaln_fwd(cc, w_ada, b_ada):
    def body(cc_ref, w_ref, b_ref, o_ref):
        o_ref[...] = _adaln_fn(cc_ref[...], w_ref[...], b_ref[...])

    return pl.pallas_call(
        body, name="adaln_fwd", out_shape=jax.ShapeDtypeStruct((8, 3 * D_MODEL), F32),
        compiler_params=pltpu.CompilerParams(vmem_limit_bytes=VMEM_LIMIT),
    )(cc, w_ada, b_ada)


def _adaln_bwd(cc, w_ada, dmod):
    def body(cc_ref, w_ref, g_ref, dcc_ref, dw_ref, db_ref):
        cc_v, g = cc_ref[...], g_ref[...]
        dw_ref[...] = mm_tn(_silu(cc_v), g).astype(BF16)
        db_ref[...] = jnp.sum(g, axis=0, keepdims=True)
        dcc_ref[...] = mm_nt(g, w_ref[...]) * _dsilu(cc_v)

    return pl.pallas_call(
        body, name="adaln_bwd",
        out_shape=(jax.ShapeDtypeStruct((8, D_MODEL), F32),
                   jax.ShapeDtypeStruct((D_MODEL, 3 * D_MODEL), BF16),
                   jax.ShapeDtypeStruct((1, 3 * D_MODEL), F32)),
        compiler_params=pltpu.CompilerParams(vmem_limit_bytes=VMEM_LIMIT),
    )(cc, w_ada, dmod)


def _seg_spec(which, n_ctx_tiles):
    return pl.BlockSpec((None, None, 1, D_MODEL),
                        lambda i: (jnp.where(i < n_ctx_tiles, 0, 1), which, 0, 0))


def _row_spec(width, col_block=0, tile=TOK):
    return pl.BlockSpec((tile, width), lambda i: (i, col_block))


def _const_spec(shape):
    zeros = (0,) * len(shape)
    return pl.BlockSpec(shape, lambda i: zeros)


def _modnorm_fn(x, g, shift, scale):
    y = x * lax.rsqrt(jnp.mean(x * x, axis=-1, keepdims=True) + EPS)
    return (y * g) * (1.0 + scale) + shift


def _modnorm_fwd(xt, g, mod4, n_ctx):
    t = xt.shape[0]
    nct = n_ctx // TOK

    def body(x_ref, g_ref, sh_ref, sc_ref, o_ref):
        o_ref[...] = _modnorm_fn(x_ref[...], g_ref[...], sh_ref[...], sc_ref[...]).astype(BF16)

    return pl.pallas_call(
        body, name="modnorm_fwd", grid=(t // TOK,),
        in_specs=[_row_spec(D_MODEL), _const_spec((1, D_MODEL)), _seg_spec(0, nct), _seg_spec(1, nct)],
        out_specs=_row_spec(D_MODEL),
        out_shape=jax.ShapeDtypeStruct((t, D_MODEL), BF16),
        compiler_params=_cparams(("parallel",)),
    )(xt, g, mod4, mod4)


def _modnorm_bwd(xt, g, mod4, dh, dres, n_ctx):
    t = xt.shape[0]
    nct = n_ctx // TOK

    def body(x_ref, g_ref, sh_ref, sc_ref, dh_ref, dres_ref, dx_ref, dg_ref, dsh_ref, dsc_ref):
        i = pl.program_id(0)
        _, vjp = jax.vjp(_modnorm_fn, x_ref[...], g_ref[...], sh_ref[...], sc_ref[...])
        dx, dg, dsh, dsc = vjp(dh_ref[...])
        dx_ref[...] = dx + dres_ref[...]

        @pl.when(i == 0)
        def _():
            dg_ref[...] = jnp.zeros_like(dg_ref)

        dg_ref[...] += dg
        first = jnp.logical_or(i == 0, i == nct)

        @pl.when(first)
        def _():
            dsh_ref[...] = dsh
            dsc_ref[...] = dsc

        @pl.when(jnp.logical_not(first))
        def _():
            dsh_ref[...] += dsh
            dsc_ref[...] += dsc

    seg_out = lambda which: pl.BlockSpec((None, None, 1, D_MODEL),
                                         lambda i: (jnp.where(i < nct, 0, 1), which, 0, 0))
    dx, dg, dss, dss2 = pl.pallas_call(
        body, name="modnorm_bwd", grid=(t // TOK,),
        in_specs=[_row_spec(D_MODEL), _const_spec((1, D_MODEL)), _seg_spec(0, nct), _seg_spec(1, nct),
                  _row_spec(D_MODEL), _row_spec(D_MODEL)],
        out_specs=[_row_spec(D_MODEL), _const_spec((1, D_MODEL)), seg_out(0), seg_out(0)],
        out_shape=(jax.ShapeDtypeStruct((t, D_MODEL), F32), jax.ShapeDtypeStruct((1, D_MODEL), F32),
                   jax.ShapeDtypeStruct((2, 1, 1, D_MODEL), F32), jax.ShapeDtypeStruct((2, 1, 1, D_MODEL), F32)),
        compiler_params=_cparams(("arbitrary",)),
    )(xt, g, mod4, mod4, dh, dres)
    return dx, dg, dss, dss2


def _group_mean_sq(x, gs):
    x2 = x * x
    hi = x2.astype(BF16).astype(F32)
    return mm(hi, gs) + mm(x2 - hi, gs)


def _head_norm(x, g, gs):
    return (x * lax.rsqrt(_group_mean_sq(x, gs) + EPS)) * g


def _rope(x, cos, sin_signed):
    lane = lax.broadcasted_iota(jnp.int32, (1, 128), 1)
    first_half = jnp.bitwise_and(lane, 63) < 32
    cols = []
    for c in range(x.shape[1] // 128):
        xb = x[:, 128 * c:128 * (c + 1)]
        partner = jnp.where(first_half, lane_roll(xb, 96), lane_roll(xb, 32))
        cols.append(xb * cos + partner * sin_signed)
    return cols[0] if len(cols) == 1 else jnp.concatenate(cols, axis=1)


def _prep_fn(zq_na, zk_na, zv_na, zq_g, zk_g, zv_g, g_naq, g_nak, g_gq, g_gk, cos, sin_signed, gs512, gs128, expand):
    q_na = _head_norm(zq_na, g_naq, gs512)
    k_na = _head_norm(zk_na, g_nak, gs512)
    q_g = _rope(_head_norm(zq_g, g_gq, gs512), cos, sin_signed)
    k_g = _rope(_head_norm(zk_g, g_gk, gs128), cos, sin_signed)
    return q_na, k_na, zv_na, q_g, mm(k_g, expand), mm(zv_g, expand)


def _prep_consts():
    gid = np.arange(512) // 64
    gs512 = (gid[:, None] == gid[None, :]).astype(np.float32) / 64.0
    expand = np.zeros((128, 512), np.float32)
    for h in range(N_HEADS):
        for j in range(64):
            expand[64 * (h // 4) + j, 64 * h + j] = 1.0
    return jnp.asarray(gs512), jnp.asarray(gs512[:128, :128]), jnp.asarray(expand)


def _prep_in_specs():
    blk = lambda off, w: _row_spec(w, off // w)
    return [blk(OFF["na_q"], 512), blk(OFF["na_k"], 512), blk(OFF["na_v"], 512), blk(OFF["gqa_q"], 512),
            blk(OFF["gqa_k"], 128), blk(OFF["gqa_v"], 128),
            _const_spec((1, 512)), _const_spec((1, 512)), _const_spec((1, 512)), _const_spec((1, 128)),
            _row_spec(128), _row_spec(128),
            _const_spec((512, 512)), _const_spec((128, 128)), _const_spec((128, 512))]


def _prep_fwd(z, gains, rope_tabs):
    t = z.shape[0]
    consts = _prep_consts()

    def body(*refs):
        ins, outs = refs[:15], refs[15:]
        res = _prep_fn(*[r[...] for r in ins])
        for o_ref, v in zip(outs, res):
            o_ref[...] = v.astype(BF16)

    return pl.pallas_call(
        body, name="prep_fwd", grid=(t // TOK,),
        in_specs=_prep_in_specs(),
        out_specs=[_row_spec(512)] * 6,
        out_shape=tuple(jax.ShapeDtypeStruct((t, 512), BF16) for _ in range(6)),
        compiler_params=_cparams(("parallel",)),
    )(z, z, z, z, z, z, *gains, *rope_tabs, *consts)


def _prep_bwd(z, gains, rope_tabs, dqs, dkv_lat, dkv_ctx, du_a, du_b, n_ctx):
    t = z.shape[0]
    nct = n_ctx // TOK
    consts = _prep_consts()

    def body(*refs):
        ins, dq_refs, lat_refs, ctx_refs = refs[:15], refs[15:17], refs[17:21], refs[21:25]
        (dua_ref, dub_ref), outs = refs[25:27], refs[27:]
        i = pl.program_id(0)
        vals = [r[...] for r in ins]
        _, vjp = jax.vjp(lambda *a: _prep_fn(*a, *vals[10:]), *vals[:10])
        kv = [jnp.where(i < nct, c_ref[...], l_ref[...]) for l_ref, c_ref in zip(lat_refs, ctx_refs)]
        grads = vjp((dq_refs[0][...], kv[0], kv[1], dq_refs[1][...], kv[2], kv[3]))
        for o_ref, v in zip(outs[:6], grads[:6]):
            o_ref[...] = v.astype(BF16)
        outs[6][...] = (dua_ref[...] + dub_ref[...]).astype(BF16)

        @pl.when(i == 0)
        def _():
            for o_ref in outs[7:]:
                o_ref[...] = jnp.zeros_like(o_ref)

        for o_ref, v in zip(outs[7:], grads[6:10]):
            o_ref[...] += v

    lat_spec = pl.BlockSpec((TOK, 512), lambda i: (jnp.maximum(i - nct + 1, 0), 0))
    ctx_spec = pl.BlockSpec((TOK, 512), lambda i: (jnp.minimum(i, nct - 1), 0))
    return pl.pallas_call(
        body, name="prep_bwd", grid=(t // TOK,),
        in_specs=_prep_in_specs() + [_row_spec(512)] * 2 + [lat_spec] * 4 + [ctx_spec] * 4 + [_row_spec(512)] * 2,
        out_specs=[_row_spec(512)] * 4 + [_row_spec(128)] * 2 + [_row_spec(512)]
        + [_const_spec((1, 512))] * 3 + [_const_spec((1, 128))],
        out_shape=tuple([jax.ShapeDtypeStruct((t, 512), BF16)] * 4 + [jax.ShapeDtypeStruct((t, 128), BF16)] * 2
                        + [jax.ShapeDtypeStruct((t, 512), BF16)]
                        + [jax.ShapeDtypeStruct((1, 512), F32)] * 3 + [jax.ShapeDtypeStruct((1, 128), F32)]),
        compiler_params=_cparams(("arbitrary",)),
    )(z, z, z, z, z, z, *gains, *rope_tabs, *consts, *dqs, *dkv_lat, *dkv_ctx, du_a, du_b)


def _s5post_fn(ys, u, d, w_glu):
    y = jax.nn.gelu(ys + d * u)
    return y * jax.nn.sigmoid(mm(y, w_glu))


def _s5post_fwd(ys, z, d, w_glu):
    t = z.shape[0]

    def body(ys_ref, u_ref, d_ref, w_ref, o_ref):
        o_ref[...] = _s5post_fn(ys_ref[...], u_ref[...], d_ref[...], w_ref[...])

    return pl.pallas_call(
        body, name="s5post_fwd", grid=(t // TOK,),
        in_specs=[_row_spec(512), _row_spec(512, OFF["s5_u"] // 512),
                  _const_spec((1, 512)), _const_spec((512, 512))],
        out_specs=_row_spec(512), out_shape=jax.ShapeDtypeStruct((t, 512), F32),
        compiler_params=_cparams(("parallel",)),
    )(ys, z, d, w_glu)


def _s5post_bwd(ys, z, d, w_glu, dy):
    t = z.shape[0]

    def body(ys_ref, u_ref, d_ref, w_ref, dy_ref, dpre_ref, du_ref, dd_ref, dw_ref):
        i = pl.program_id(0)
        _, vjp = jax.vjp(_s5post_fn, ys_ref[...], u_ref[...], d_ref[...], w_ref[...].astype(F32))
        dys, du, dd, dw = vjp(dy_ref[...])
        dpre_ref[...] = dys.astype(BF16)
        du_ref[...] = du

        @pl.when(i == 0)
        def _():
            dd_ref[...] = jnp.zeros_like(dd_ref)
            dw_ref[...] = jnp.zeros_like(dw_ref)

        dd_ref[...] += dd
        dw_ref[...] += dw

    return pl.pallas_call(
        body, name="s5post_bwd", grid=(t // TOK,),
        in_specs=[_row_spec(512), _row_spec(512, OFF["s5_u"] // 512),
                  _const_spec((1, 512)), _const_spec((512, 512)), _row_spec(512)],
        out_specs=[_row_spec(512), _row_spec(512), _const_spec((1, 512)), _const_spec((512, 512))],
        out_shape=(jax.ShapeDtypeStruct((t, 512), BF16), jax.ShapeDtypeStruct((t, 512), F32),
                   jax.ShapeDtypeStruct((1, 512), F32), jax.ShapeDtypeStruct((512, 512), F32)),
        compiler_params=_cparams(("arbitrary",)),
    )(ys, z, d, w_glu, dy)


def _halo_specs(col_block, t):
    last = t // 8 - 1
    prev = pl.BlockSpec((8, 512), lambda i: (jnp.maximum(i * (TOK // 8) - 1, 0), col_block))
    nxt = pl.BlockSpec((8, 512), lambda i: (jnp.minimum((i + 1) * (TOK // 8), last), col_block))
    return [_row_spec(512, col_block), prev, nxt]


def _shifted(cur, prev_row, next_row, tok0, n_ctx, t_total):
    row = lax.broadcasted_iota(jnp.int32, (TOK, 1), 0)
    tpos = row + tok0
    down = jnp.where(row == 0, prev_row, pltpu.roll(cur, 1, 0))
    down = jnp.where(jnp.logical_or(tpos == 0, tpos == n_ctx), 0.0, down)
    up = jnp.where(row == TOK - 1, next_row, pltpu.roll(cur, TOK - 1, 0))
    up = jnp.where(jnp.logical_or(tpos == n_ctx - 1, tpos == t_total - 1), 0.0, up)
    return down, up


def _conv_fwd(z, conv_w, conv_b, n_ctx):
    t = z.shape[0]

    def body(v_ref, vp_ref, vn_ref, c_ref, cp_ref, cn_ref, b_ref, w_ref, cb_ref, o_ref):
        tok0 = pl.program_id(0) * TOK
        zz = v_ref[...] * c_ref[...]
        zz_m1, zz_p1 = _shifted(zz, vp_ref[7:8, :] * cp_ref[7:8, :], vn_ref[0:1, :] * cn_ref[0:1, :], tok0, n_ctx, t)
        s = cb_ref[...] + zz_m1 * w_ref[0:1, :] + zz * w_ref[1:2, :] + zz_p1 * w_ref[2:3, :]
        o_ref[...] = b_ref[...] * s

    return pl.pallas_call(
        body, name="conv_fwd", grid=(t // TOK,),
        in_specs=_halo_specs(OFF["conv_v"] // 512, t) + _halo_specs(OFF["conv_c"] // 512, t)
        + [_row_spec(512, OFF["conv_b"] // 512), _const_spec((8, 512)), _const_spec((1, 512))],
        out_specs=_row_spec(512), out_shape=jax.ShapeDtypeStruct((t, 512), F32),
        compiler_params=_cparams(("parallel",)),
    )(z, z, z, z, z, z, z, conv_w, conv_b)


def _conv_bwd(z, conv_w, conv_b, dy, n_ctx):
    t = z.shape[0]

    def body(v_ref, vp_ref, vn_ref, c_ref, cp_ref, cn_ref, b_ref, bp_ref, bn_ref, dy_ref, dyp_ref, dyn_ref,
             w_ref, cb_ref, dv_ref, db_ref, dc_ref, dw_ref, dcb_ref):
        i = pl.program_id(0)
        tok0 = i * TOK
        v, c, b, dy_v = v_ref[...], c_ref[...], b_ref[...], dy_ref[...]
        w0, w1, w2 = w_ref[0:1, :], w_ref[1:2, :], w_ref[2:3, :]
        zz = v * c
        zz_m1, zz_p1 = _shifted(zz, vp_ref[7:8, :] * cp_ref[7:8, :], vn_ref[0:1, :] * cn_ref[0:1, :], tok0, n_ctx, t)
        s = cb_ref[...] + zz_m1 * w0 + zz * w1 + zz_p1 * w2
        ds = dy_v * b
        ds_m1, ds_p1 = _shifted(ds, dyp_ref[7:8, :] * bp_ref[7:8, :], dyn_ref[0:1, :] * bn_ref[0:1, :], tok0, n_ctx, t)
        dzz = ds_p1 * w0 + ds * w1 + ds_m1 * w2
        db_ref[...] = (dy_v * s).astype(BF16)
        dv_ref[...] = (dzz * c).astype(BF16)
        dc_ref[...] = (dzz * v).astype(BF16)

        @pl.when(i == 0)
        def _():
            dw_ref[...] = jnp.zeros_like(dw_ref)
            dcb_ref[...] = jnp.zeros_like(dcb_ref)

        rsum = lambda a: jnp.sum(a, axis=0, keepdims=True)
        dw_ref[0:1, :] += rsum(ds * zz_m1)
        dw_ref[1:2, :] += rsum(ds * zz)
        dw_ref[2:3, :] += rsum(ds * zz_p1)
        dcb_ref[...] += rsum(ds)

    return pl.pallas_call(
        body, name="conv_bwd", grid=(t // TOK,),
        in_specs=_halo_specs(OFF["conv_v"] // 512, t) + _halo_specs(OFF["conv_c"] // 512, t)
        + _halo_specs(OFF["conv_b"] // 512, t) + _halo_specs(0, t) + [_const_spec((8, 512)), _const_spec((1, 512))],
        out_specs=[_row_spec(512)] * 3 + [_const_spec((8, 512)), _const_spec((1, 512))],
        out_shape=tuple([jax.ShapeDtypeStruct((t, 512), BF16)] * 3
                        + [jax.ShapeDtypeStruct((8, 512), F32), jax.ShapeDtypeStruct((1, 512), F32)]),
        compiler_params=_cparams(("arbitrary",)),
    )(z, z, z, z, z, z, z, z, z, dy, dy, dy, conv_w, conv_b)


def _merge_col_specs(tile):
    specs = []
    for off in MERGE_OFFS:
        specs.append(pl.BlockSpec((tile, 512), functools.partial(lambda i, cb: (i, cb), cb=off // 512)))
        specs.append(pl.BlockSpec((tile, 512), functools.partial(lambda i, cb: (i, cb), cb=off // 512 + 1)))
    return specs


def _merge_fwd(xt, ys, z, mod4, w_br, w_out, n_ctx):
    t = xt.shape[0]
    nct = n_ctx // TOK

    def body(x_ref, *refs):
        y_refs, gt_refs, mg_refs = refs[0:4], refs[4:8], refs[8:16]
        gate_ref, wbr_ref, wout_ref, o_ref = refs[16:20]
        acc_lo = acc_hi = None
        for k in range(4):
            gated = y_refs[k][...] * _silu(gt_refs[k][...])
            proj = mm(gated, wbr_ref[k])
            lo = jax.nn.sigmoid(mg_refs[2 * k][...]) * proj[:, :512]
            hi = jax.nn.sigmoid(mg_refs[2 * k + 1][...]) * proj[:, 512:]
            acc_lo = lo if acc_lo is None else acc_lo + lo
            acc_hi = hi if acc_hi is None else acc_hi + hi
        acc = jnp.concatenate([acc_lo, acc_hi], axis=1)
        o_ref[...] = x_ref[...] + gate_ref[...] * mm(acc, wout_ref[...])

    gate_specs = [pl.BlockSpec((TOK, 512), functools.partial(lambda i, cb: (i, cb), cb=o // 512)) for o in GATE_OFFS]
    return pl.pallas_call(
        body, name="merge_fwd", grid=(t // TOK,),
        in_specs=[_row_spec(D_MODEL)] + [_row_spec(512)] * 4 + gate_specs + _merge_col_specs(TOK)
        + [_seg_spec(2, nct), _const_spec((4, 512, 1024)), _const_spec((1024, 1024))],
        out_specs=_row_spec(D_MODEL), out_shape=jax.ShapeDtypeStruct((t, D_MODEL), F32),
        compiler_params=_cparams(("parallel",)),
    )(xt, *ys, z, z, z, z, z, z, z, z, z, z, z, z, mod4, w_br, w_out)


MERGE_BWD_TILE = 128


def _merge_bwd(g, ys, z, mod4, w_br, w_out, n_ctx):
    t = g.shape[0]
    tile = MERGE_BWD_TILE
    nct = n_ctx // tile
    nsteps = t // tile

    def body(g_ref, *refs):
        y_refs, gt_refs, mg_refs = refs[0:4], refs[4:8], refs[8:16]
        gate_ref, wbr_hbm, wout_hbm = refs[16:19]
        dy_refs, dgt_refs, dmg_refs = refs[19:23], refs[23:27], refs[27:31]
        dgate_ref, dwbr_hbm, dwout_hbm = refs[31:34]
        wbr_v, wout_v, dwbr_acc, dwout_acc = refs[34:38]
        i = pl.program_id(0)

        @pl.when(i == 0)
        def _():
            pltpu.sync_copy(wbr_hbm, wbr_v)
            pltpu.sync_copy(wout_hbm, wout_v)
            dwbr_acc[...] = jnp.zeros_like(dwbr_acc)
            dwout_acc[...] = jnp.zeros_like(dwout_acc)

        g_v, gate = g_ref[...], gate_ref[...]
        gated, proj, sig = [], [], []
        acc = None
        for k in range(4):
            gated.append(y_refs[k][...] * _silu(gt_refs[k][...]))
            proj.append(mm(gated[k], wbr_v[k]))
            sig.append(jax.nn.sigmoid(jnp.concatenate([mg_refs[2 * k][...], mg_refs[2 * k + 1][...]], axis=1)))
            contrib = sig[k] * proj[k]
            acc = contrib if acc is None else acc + contrib
        o = mm(acc, wout_v[...])
        dgate = jnp.sum(g_v * o, axis=0, keepdims=True)
        first = jnp.logical_or(i == 0, i == nct)

        @pl.when(first)
        def _():
            dgate_ref[...] = dgate

        @pl.when(jnp.logical_not(first))
        def _():
            dgate_ref[...] += dgate

        do = g_v * gate
        dwout_acc[...] += mm_tn(acc, do)
        dacc = mm_nt(do, wout_v[...])
        for k in range(4):
            dmg_refs[k][...] = (dacc * proj[k] * sig[k] * (1.0 - sig[k])).astype(BF16)
            dproj = dacc * sig[k]
            dwbr_acc[k] += mm_tn(gated[k], dproj)
            dgated = mm_nt(dproj, wbr_v[k])
            gt = gt_refs[k][...]
            dy_refs[k][...] = dgated * _silu(gt)
            dgt_refs[k][...] = (dgated * y_refs[k][...] * _dsilu(gt)).astype(BF16)

        @pl.when(i == nsteps - 1)
        def _():
            wbr_v[...] = dwbr_acc[...].astype(BF16)
            wout_v[...] = dwout_acc[...].astype(BF16)
            pltpu.sync_copy(wbr_v, dwbr_hbm)
            pltpu.sync_copy(wout_v, dwout_hbm)

    row = lambda w: _row_spec(w, 0, tile)
    gate_specs = [pl.BlockSpec((tile, 512), functools.partial(lambda i, cb: (i, cb), cb=o // 512)) for o in GATE_OFFS]
    anyspec = pl.BlockSpec(memory_space=pl.ANY)
    seg = pl.BlockSpec((None, None, 1, D_MODEL), lambda i: (jnp.where(i < nct, 0, 1), 2, 0, 0))
    seg_out = pl.BlockSpec((None, None, 1, D_MODEL), lambda i: (jnp.where(i < nct, 0, 1), 0, 0, 0))
    res = pl.pallas_call(
        body, name="merge_bwd", grid=(nsteps,),
        in_specs=[row(D_MODEL)] + [row(512)] * 4 + gate_specs + _merge_col_specs(tile) + [seg, anyspec, anyspec],
        out_specs=[row(512)] * 8 + [row(1024)] * 4 + [seg_out, anyspec, anyspec],
        out_shape=tuple([jax.ShapeDtypeStruct((t, 512), F32)] * 4 + [jax.ShapeDtypeStruct((t, 512), BF16)] * 4
                        + [jax.ShapeDtypeStruct((t, 1024), BF16)] * 4
                        + [jax.ShapeDtypeStruct((2, 1, 1, D_MODEL), F32),
                           jax.ShapeDtypeStruct((4, 512, 1024), BF16), jax.ShapeDtypeStruct((1024, 1024), BF16)]),
        scratch_shapes=[pltpu.VMEM((4, 512, 1024), BF16), pltpu.VMEM((1024, 1024), BF16),
                        pltpu.VMEM((4, 512, 1024), F32), pltpu.VMEM((1024, 1024), F32)],
        compiler_params=_cparams(("arbitrary",)),
    )(g, *ys, z, z, z, z, z, z, z, z, z, z, z, z, mod4, w_br, w_out)
    return res[0:4], res[4:8], res[8:12], res[12], res[13], res[14]


FWD_ROWS = slice(0, S5_GROUPS)
BWD_ROWS = slice(S5_GROUPS, 2 * S5_GROUPS)


def _backward_chunk(j, k, n_ctx_chunks):
    return jnp.where(j < n_ctx_chunks, n_ctx_chunks - 1 - j, k - 1 - (j - n_ctx_chunks))


def _scan_call(body, name, n_hbm_in, out_shape, kp):
    hbm, vmem = pl.BlockSpec(memory_space=pl.ANY), pl.BlockSpec(memory_space=pltpu.VMEM)
    return pl.pallas_call(
        body, name=name, in_specs=[hbm] * n_hbm_in + [vmem, vmem],
        out_specs=[hbm] + [vmem] * (len(out_shape) - 1), out_shape=out_shape,
        scratch_shapes=[pltpu.VMEM((kp, 64, 128), F32), pltpu.VMEM((kp, 64, 128), F32)],
        compiler_params=pltpu.CompilerParams(vmem_limit_bytes=VMEM_LIMIT))


def _complex_step(a1, a2, h):
    return a1 * h + a2 * pltpu.roll(h, 64, 1)


def _s5_scan_fwd(s, a1, a2, k, n_ctx_chunks):
    kp = s.shape[0]

    def body(s_hbm, a1_ref, a2_ref, hp_hbm, s_v, hp_v):
        pltpu.sync_copy(s_hbm, s_v)
        if kp > k:
            hp_v[k:kp] = jnp.zeros((kp - k, 64, 128), F32)
        a1f, a2f, a1b, a2b = a1_ref[FWD_ROWS, :], a2_ref[FWD_ROWS, :], a1_ref[BWD_ROWS, :], a2_ref[BWD_ROWS, :]

        def step(j, carry):
            hf, hb = carry
            cb = _backward_chunk(j, k, n_ctx_chunks)
            hp_v[j, FWD_ROWS, :] = hf
            hp_v[cb, BWD_ROWS, :] = hb
            return (_complex_step(a1f, a2f, hf) + s_v[j, FWD_ROWS, :],
                    _complex_step(a1b, a2b, hb) + s_v[cb, BWD_ROWS, :])

        zero = jnp.zeros((S5_GROUPS, 128), F32)
        lax.fori_loop(0, k, step, (zero, zero))
        pltpu.sync_copy(hp_v, hp_hbm)

    return _scan_call(body, "s5_scan_fwd", 1, (jax.ShapeDtypeStruct(s.shape, F32),), kp)(s, a1, a2)[0]


def _s5_scan_bwd(dhp, hp, a1, a2, k, n_ctx_chunks):
    kp = hp.shape[0]

    def body(dhp_hbm, hp_hbm, a1_ref, a2_ref, ds_hbm, da1_ref, da2_ref, g_v, hp_v):
        pltpu.sync_copy(dhp_hbm, g_v)
        pltpu.sync_copy(hp_hbm, hp_v)
        if kp > k:
            g_v[k:kp] = jnp.zeros((kp - k, 64, 128), F32)
        coef_f = (a1_ref[FWD_ROWS, :], a2_ref[FWD_ROWS, :])
        coef_b = (a1_ref[BWD_ROWS, :], a2_ref[BWD_ROWS, :])

        def one(rows, c, lam, d1, d2):
            a1_v, a2_v = coef_f if rows is FWD_ROWS else coef_b
            dh_in = g_v[c, rows, :]
            g_v[c, rows, :] = lam
            h = hp_v[c, rows, :]
            return (dh_in + a1_v * lam + pltpu.roll(a2_v * lam, 64, 1),
                    d1 + lam * h, d2 + lam * pltpu.roll(h, 64, 1))

        def step(j, carry):
            f, b = carry
            jj = k - 1 - j
            return one(FWD_ROWS, jj, *f), one(BWD_ROWS, _backward_chunk(jj, k, n_ctx_chunks), *b)

        zero = jnp.zeros((S5_GROUPS, 128), F32)
        f, b = lax.fori_loop(0, k, step, ((zero, zero, zero), (zero, zero, zero)))
        da1_ref[FWD_ROWS, :], da2_ref[FWD_ROWS, :] = f[1], f[2]
        da1_ref[BWD_ROWS, :], da2_ref[BWD_ROWS, :] = b[1], b[2]
        pltpu.sync_copy(g_v, ds_hbm)

    shapes = (jax.ShapeDtypeStruct(hp.shape, F32), jax.ShapeDtypeStruct((64, 128), F32),
              jax.ShapeDtypeStruct((64, 128), F32))
    return _scan_call(body, "s5_scan_bwd", 2, shapes, kp)(dhp, hp, a1, a2)


WIN_GEOM = {"na": (ATT_BLK, 3, 0), "gqa": (128, 4, 128)}


def _win_rows(mode):
    kb, nw, _ = WIN_GEOM[mode]
    return kb * nw


@functools.lru_cache(maxsize=None)
def _window_patterns(mode, n_lat):
    nb = n_lat // ATT_BLK
    assert nb >= 3
    first_key = WIN_GEOM[mode][2] - ATT_BLK
    iq, ik = np.arange(ATT_BLK)[:, None], np.arange(_win_rows(mode))[None, :]

    def valid(ql):
        tq, ts = ATT_BLK * ql + iq, ATT_BLK * ql + first_key + ik
        if mode == "na":
            r, qcol, kr, kcol = tq // GRID_W, tq % GRID_W, ts // GRID_W, ts % GRID_W
            rs = np.clip(r - NA_ROWS // 2, 0, n_lat // GRID_W - NA_ROWS)
            cs = np.clip(qcol - NA_COLS // 2, 0, GRID_W - NA_COLS)
            return (kr >= rs) & (kr < rs + NA_ROWS) & (kcol >= cs) & (kcol < cs + NA_COLS)
        return (np.abs(tq - ts) <= WINDOW) & (ts >= 0) & (ts < n_lat)

    interior = valid(1)
    assert all(np.array_equal(valid(ql), interior) for ql in range(1, nb - 1))
    return np.stack([valid(0), interior, valid(nb - 1), np.zeros_like(interior)])


def _pattern_of_block(qb, nqb):
    return jnp.where(qb == 0, 3, jnp.where(qb == 1, 0, jnp.where(qb == nqb - 1, 2, 1)))


def _attn_block(q, k3, v3, kc, vc, bias0, bias1, sink):
    lane = lax.broadcasted_iota(jnp.int32, (1, 128), 1)
    scale = HEAD_DIM ** -0.5
    outs = []
    for e, bias in enumerate((bias0, bias1)):
        in_head = (lane < 64) if e == 0 else (lane >= 64)
        qe = jnp.where(in_head, q, 0.0)
        s_lat = mm_nt(qe, k3) * scale + bias
        s_ctx = mm_nt(qe, kc) * scale
        mx = jnp.maximum(jnp.max(s_lat, axis=1, keepdims=True), jnp.max(s_ctx, axis=1, keepdims=True))
        if sink is not None:
            srow = lax.broadcasted_iota(jnp.int32, sink.shape, 0)
            sv = jnp.sum(jnp.where(srow == e, sink, 0.0), keepdims=True) * (1.0 / 128.0)
            mx = jnp.maximum(mx, sv)
        mx = lax.stop_gradient(mx)
        e_lat = jnp.exp(s_lat - mx)
        e_ctx = jnp.exp(s_ctx - mx)
        den = jnp.sum(e_lat, axis=1, keepdims=True) + jnp.sum(e_ctx, axis=1, keepdims=True)
        if sink is not None:
            den = den + jnp.exp(sv - mx)
        inv = 1.0 / den
        outs.append(mm(e_lat * inv, v3) + mm(e_ctx * inv, vc))
    return jnp.where(lane < 64, outs[0], outs[1])


def _attn_specs(n_ctx, nqb, per_head, mode):
    kb, nw, skip = WIN_GEOM[mode]

    def kwin(s):
        return pl.BlockSpec(
            (kb, 128), lambda hp, qb: (jnp.maximum(qb - 1, 0) * (ATT_BLK // kb) + skip // kb + s, hp))

    q = pl.BlockSpec((ATT_BLK, 128), lambda hp, qb: (qb, hp))
    ctx = pl.BlockSpec((n_ctx, 128), lambda hp, qb: (0, hp))
    if per_head:
        bias = pl.BlockSpec((None, None, 2, ATT_BLK, kb * nw),
                            lambda hp, qb: (_pattern_of_block(qb, nqb), hp, 0, 0, 0))
    else:
        bias = pl.BlockSpec((None, ATT_BLK, kb * nw), lambda hp, qb: (_pattern_of_block(qb, nqb), 0, 0))
    sink = pl.BlockSpec((None, 8, 128), lambda hp, qb: (hp, 0, 0))
    return q, [kwin(s) for s in range(nw)], ctx, bias, sink


def _attn_fwd(q, kpad, vpad, kc, vc, bias, sink, *, mode, n_ctx):
    t = q.shape[0]
    per_head = bias.ndim == 5
    qs, kws, ctx, bias_s, sink_s = _attn_specs(n_ctx, t // ATT_BLK, per_head, mode)
    has_sink = sink is not None
    nw = len(kws)

    def body(*refs):
        q_ref, k_refs, v_refs = refs[0], refs[1:1 + nw], refs[1 + nw:1 + 2 * nw]
        kc_ref, vc_ref, b_ref = refs[1 + 2 * nw:4 + 2 * nw]
        s_ref = refs[4 + 2 * nw] if has_sink else None
        o_ref = refs[-1]
        k3 = jnp.concatenate([r[...] for r in k_refs], axis=0)
        v3 = jnp.concatenate([r[...] for r in v_refs], axis=0)
        b0, b1 = (b_ref[0], b_ref[1]) if per_head else (b_ref[...], b_ref[...])
        o_ref[...] = _attn_block(q_ref[...], k3, v3, kc_ref[...], vc_ref[...], b0, b1,
                                 s_ref[...] if has_sink else None)

    in_specs = [qs] + kws + kws + [ctx, ctx, bias_s] + ([sink_s] if has_sink else [])
    args = [q] + [kpad] * nw + [vpad] * nw + [kc, vc, bias] + ([sink] if has_sink else [])
    return pl.pallas_call(
        body, name=mode + "_attn_fwd", grid=(4, t // ATT_BLK),
        in_specs=in_specs, out_specs=qs, out_shape=jax.ShapeDtypeStruct((t, 512), F32),
        compiler_params=_cparams(("parallel", "parallel")),
    )(*args)


def _attn_bwd(q, kpad, vpad, kc, vc, bias, sink, do, *, mode, n_ctx):
    t = q.shape[0]
    nqb = t // ATT_BLK
    per_head = bias.ndim == 5
    qs, kws, ctx, bias_s, sink_s = _attn_specs(n_ctx, nqb, per_head, mode)
    has_sink = sink is not None
    nw = len(kws)
    n_in = 5 + 2 * nw + has_sink

    def body(*refs):
        q_ref, k_refs, v_refs = refs[0], refs[1:1 + nw], refs[1 + nw:1 + 2 * nw]
        kc_ref, vc_ref, b_ref = refs[1 + 2 * nw:4 + 2 * nw]
        s_ref = refs[4 + 2 * nw] if has_sink else None
        do_ref = refs[n_in - 1]
        outs = list(refs[n_in:-2])
        dk_acc, dv_acc = refs[-2:]
        dq_ref, dkp_ref, dvp_ref, dkc_ref, dvc_ref = outs[:5]
        hp = pl.program_id(0)
        outs = outs[5:]
        db_ref = outs.pop(0) if per_head else None
        ds_ref = outs.pop(0) if has_sink else None
        qb = pl.program_id(1)
        up = lambda r: r[...].astype(F32)
        k3 = jnp.concatenate([up(r) for r in k_refs], axis=0)
        v3 = jnp.concatenate([up(r) for r in v_refs], axis=0)
        prim = [up(q_ref), k3, v3, up(kc_ref), up(vc_ref)]
        if per_head:
            prim += [b_ref[0], b_ref[1]]
        if has_sink:
            prim += [s_ref[...]]

        def fn(*a):
            a = list(a)
            qv, k3v, v3v, kcv, vcv = a[:5]
            a = a[5:]
            b0 = a.pop(0) if per_head else b_ref[...]
            b1 = a.pop(0) if per_head else b0
            sk = a.pop(0) if has_sink else None
            return _attn_block(qv, k3v, v3v, kcv, vcv, b0, b1, sk)

        _, vjp = jax.vjp(fn, *prim)
        grads = list(vjp(do_ref[...]))
        dq_ref[...] = grads[0]

        @pl.when(qb == 0)
        def _():
            dk_acc[...] = jnp.zeros_like(dk_acc)
            dv_acc[...] = jnp.zeros_like(dv_acc)
            dkc_ref[...] = jnp.zeros_like(dkc_ref)
            dvc_ref[...] = jnp.zeros_like(dvc_ref)
            if has_sink:
                ds_ref[...] = jnp.zeros_like(ds_ref)

        window = pl.ds(pl.multiple_of(jnp.maximum(qb - 1, 0) * ATT_BLK + WIN_GEOM[mode][2], 128), _win_rows(mode))
        dk_acc[window, :] += grads[1]
        dv_acc[window, :] += grads[2]

        @pl.when(qb == nqb - 1)
        def _():
            cols = pl.ds(pl.multiple_of(hp * 128, 128), 128)
            pltpu.sync_copy(dk_acc, dkp_ref.at[:, cols])
            pltpu.sync_copy(dv_acc, dvp_ref.at[:, cols])

        dkc_ref[...] += grads[3]
        dvc_ref[...] += grads[4]
        rest_g = grads[5:]
        if per_head:
            opens = (qb <= 2) | (qb == nqb - 1)
            g0, g1 = rest_g.pop(0), rest_g.pop(0)

            @pl.when(opens)
            def _():
                db_ref[0] = g0
                db_ref[1] = g1

            @pl.when(jnp.logical_not(opens))
            def _():
                db_ref[0] += g0
                db_ref[1] += g1

        if has_sink:
            ds_ref[...] += rest_g.pop(0)

    hbm = pl.BlockSpec(memory_space=pl.ANY)
    in_specs = [qs] + kws + kws + [ctx, ctx, bias_s] + ([sink_s] if has_sink else []) + [qs]
    args = [q] + [kpad] * nw + [vpad] * nw + [kc, vc, bias] + ([sink] if has_sink else []) + [do]
    out_specs = [qs, hbm, hbm, ctx, ctx] + ([bias_s] if per_head else []) + ([sink_s] if has_sink else [])
    out_shape = [jax.ShapeDtypeStruct((t, 512), F32),
                 jax.ShapeDtypeStruct(kpad.shape, F32), jax.ShapeDtypeStruct(kpad.shape, F32),
                 jax.ShapeDtypeStruct((n_ctx, 512), F32), jax.ShapeDtypeStruct((n_ctx, 512), F32)]
    if per_head:
        out_shape.append(jax.ShapeDtypeStruct(bias.shape, F32))
    if has_sink:
        out_shape.append(jax.ShapeDtypeStruct((4, 8, 128), F32))
    res = list(pl.pallas_call(
        body, name=mode + "_attn_bwd", grid=(4, nqb),
        in_specs=in_specs, out_specs=out_specs, out_shape=tuple(out_shape),
        scratch_shapes=[pltpu.VMEM((kpad.shape[0], 128), F32), pltpu.VMEM((kpad.shape[0], 128), F32)],
        compiler_params=_cparams(("arbitrary", "arbitrary")),
    )(*args))
    dq, dkp, dvp, dkc, dvc = res[:5]
    res = res[5:]
    dbias = res.pop(0) if per_head else None
    dsink = res.pop(0) if has_sink else None
    return dq, dkp, dvp, dkc, dvc, dbias, dsink


def _loss_head(xt, target, n_ctx):
    t = xt.shape[0]
    nct = n_ctx // TOK

    def body(x_ref, t_ref, l_ref, d_ref):
        i = pl.program_id(0)

        @pl.when(i == 0)
        def _():
            l_ref[...] = jnp.zeros_like(l_ref)

        @pl.when(i < nct)
        def _():
            d_ref[...] = jnp.zeros_like(d_ref)

        @pl.when(i >= nct)
        def _():
            err = x_ref[...] - t_ref[...]
            d_ref[...] = err * (1.0 / D_MODEL)
            l_ref[...] += jnp.sum(err * err, keepdims=True) * (0.5 / D_MODEL)

    return pl.pallas_call(
        body, name="loss_head", grid=(t // TOK,),
        in_specs=[_row_spec(D_MODEL), pl.BlockSpec((TOK, D_MODEL), lambda i: (jnp.maximum(i - nct, 0), 0))],
        out_specs=[_const_spec((8, 128)), _row_spec(D_MODEL)],
        out_shape=(jax.ShapeDtypeStruct((8, 128), F32), jax.ShapeDtypeStruct((t, D_MODEL), F32)),
        compiler_params=_cparams(("arbitrary",)),
    )(xt, target)


PACK_W = 1024
SUM_STEPS = 8


def _sum_chips(recvs):
    def split(a):
        rows = a.shape[1]
        if rows % (8 * SUM_STEPS):
            return None
        return rows // SUM_STEPS

    def body(*refs):
        n = len(refs) // 2
        for r_ref, o_ref in zip(refs[:n], refs[n:]):
            up = lambda s: r_ref[s].astype(F32)
            o_ref[...] = ((up(0) + up(1)) + up(2)) + up(3)

    in_specs, out_specs = [], []
    for a in recvs:
        rb, tail = split(a), a.shape[2:]
        zeros = (0,) * len(tail)
        if rb is None:
            in_specs.append(pl.BlockSpec(a.shape, functools.partial(lambda i, z: (0, 0) + z, z=zeros)))
            out_specs.append(pl.BlockSpec(a.shape[1:], functools.partial(lambda i, z: (0,) + z, z=zeros)))
        else:
            in_specs.append(pl.BlockSpec((4, rb) + tail, functools.partial(lambda i, z: (0, i) + z, z=zeros)))
            out_specs.append(pl.BlockSpec((rb,) + tail, functools.partial(lambda i, z: (i,) + z, z=zeros)))
    return pl.pallas_call(
        body, name="sum_chips", grid=(SUM_STEPS,),
        in_specs=in_specs, out_specs=out_specs,
        out_shape=tuple(jax.ShapeDtypeStruct(a.shape[1:], F32) for a in recvs),
        compiler_params=_cparams(("arbitrary",)),
    )(*recvs)


ADAM_BLOCK_BYTES = 1 << 20


def _adamw(p_a, p_b, w, m, v, name):
    layers, rows, cols = w.shape
    tr = rows
    while tr % 16 == 0 and tr * cols * 4 > ADAM_BLOCK_BYTES:
        tr //= 2
    c1 = 1.0 / (1.0 - ADAM_B1 ** ADAM_STEP)
    c2 = 1.0 / (1.0 - ADAM_B2 ** ADAM_STEP)

    def body(a_ref, b_ref, w_ref, m_ref, v_ref, g_ref, d_ref, nm_ref, nv_ref):
        g = a_ref[...] + b_ref[...]
        nm = ADAM_B1 * m_ref[...] + (1.0 - ADAM_B1) * g
        nv = ADAM_B2 * v_ref[...] + (1.0 - ADAM_B2) * (g * g)
        g_ref[...] = g
        nm_ref[...] = nm
        nv_ref[...] = nv
        d_ref[...] = -ADAM_LR * ((nm * c1) / (jnp.sqrt(nv * c2) + ADAM_EPS) + ADAM_WD * w_ref[...])

    spec = pl.BlockSpec((None, tr, cols), lambda l, i: (l, i, 0))
    return pl.pallas_call(
        body, name=name, grid=(layers, rows // tr),
        in_specs=[spec] * 5, out_specs=[spec] * 4,
        out_shape=tuple(jax.ShapeDtypeStruct(w.shape, F32) for _ in range(4)),
        compiler_params=_cparams(("parallel", "parallel")),
    )(p_a, p_b, w, m, v)


MESH = pl.DeviceIdType.MESH
ANY_SPEC = pl.BlockSpec(memory_space=pl.ANY)


def _chip_exchange(srcs, out_shapes, src_window, dst_window, name):
    n = len(srcs)

    def body(*refs):
        src_refs, out_refs = refs[:n], refs[n:2 * n]
        send_sems, recv_sems, local_sems = refs[2 * n:]
        x, y, c = lax.axis_index("x"), lax.axis_index("y"), lax.axis_index("c")
        me = 2 * x + y
        peers = [(x, 1 - y), (1 - x, y), (1 - x, 1 - y)]

        def copy(k, j, from_chip, to_chip):
            px, py = peers[j]
            return pltpu.make_async_remote_copy(
                src_ref=src_window(k, src_refs[k], to_chip), dst_ref=dst_window(k, out_refs[k], from_chip),
                send_sem=send_sems.at[3 * k + j], recv_sem=recv_sems.at[3 * k + j],
                device_id=(px, py, c), device_id_type=MESH)

        local = [pltpu.make_async_copy(src_window(k, src_refs[k], me), dst_window(k, out_refs[k], me),
                                       local_sems.at[k]) for k in range(n)]
        for cp in local:
            cp.start()
        sends = [copy(k, j, me, 2 * px + py) for k in range(n) for j, (px, py) in enumerate(peers)]
        for cp in sends:
            cp.start()
        for k in range(n):
            for j, (px, py) in enumerate(peers):
                copy(k, j, 2 * px + py, me).wait_recv()
        for cp in sends:
            cp.wait_send()
        for cp in local:
            cp.wait()

    return pl.pallas_call(
        body, name=name, in_specs=[ANY_SPEC] * n, out_specs=[ANY_SPEC] * n,
        out_shape=tuple(out_shapes),
        scratch_shapes=[pltpu.SemaphoreType.DMA((3 * n,)), pltpu.SemaphoreType.DMA((3 * n,)),
                        pltpu.SemaphoreType.DMA((n,))],
    )(*srcs)


def _core_swap(srcs):
    n = len(srcs)

    def body(*refs):
        src_refs, out_refs, send_sems, recv_sems = refs[:n], refs[n:2 * n], refs[2 * n], refs[2 * n + 1]
        x, y, c = lax.axis_index("x"), lax.axis_index("y"), lax.axis_index("c")
        copies = [pltpu.make_async_remote_copy(
            src_ref=src_refs[k], dst_ref=out_refs[k], send_sem=send_sems.at[k], recv_sem=recv_sems.at[k],
            device_id=(x, y, 1 - c), device_id_type=MESH) for k in range(n)]
        for cp in copies:
            cp.start()
        for cp in copies:
            cp.wait()

    return pl.pallas_call(
        body, name="core_swap", in_specs=[ANY_SPEC] * n, out_specs=[ANY_SPEC] * n,
        out_shape=tuple(jax.ShapeDtypeStruct(s.shape, s.dtype) for s in srcs),
        scratch_shapes=[pltpu.SemaphoreType.DMA((n,)), pltpu.SemaphoreType.DMA((n,))],
    )(*srcs)


def _col_window(ref, start, size):
    idx = (slice(None),) * (len(ref.shape) - 1) + (pl.ds(pl.multiple_of(start, 128), size),)
    return ref.at[idx]


def _row_window(ref, start, size):
    idx = (slice(None),) * (len(ref.shape) - 2) + (pl.ds(pl.multiple_of(start, 8), size), slice(None))
    return ref.at[idx]


N_SHARD_IN = 2624
WIN_W = 2944
WIN_START = (0, 2560, 5248, 7808)
WIN_PIECES = (((0, 2624),), ((64, 2688),), ((0, 640), (896, 2880)), ((320, 2944),))


def _core_layers(ref, lay, core):
    half = lay // 2
    return ref.at[pl.ds(core * half, half)]


def _gather_weights(w_ada, w_in, w_glu, w_br, w_out, conv_w):
    lay = w_ada.shape[0]
    assert lay % 2 == 0
    sizes = (768, None, 128, 256, 256, 128)

    def dst(k, ref, s, core=None):
        core = lax.axis_index("c") if core is None else core
        if k == 1:
            return _core_layers(ref.at[s], lay, core)
        ref = _core_layers(ref, lay, core)
        if k in (2, 4):
            return _row_window(ref, s * sizes[k], sizes[k])
        return _col_window(ref, s * sizes[k], sizes[k])

    shapes = (jax.ShapeDtypeStruct((lay, D_MODEL, 3 * D_MODEL), w_ada.dtype),
              jax.ShapeDtypeStruct((4,) + w_in.shape, w_in.dtype),
              jax.ShapeDtypeStruct((lay, MIX_W, MIX_W), w_glu.dtype),
              jax.ShapeDtypeStruct((lay, 4, MIX_W, D_MODEL), w_br.dtype),
              jax.ShapeDtypeStruct((lay, D_MODEL, D_MODEL), w_out.dtype),
              jax.ShapeDtypeStruct((lay, 8, MIX_W), conv_w.dtype))
    halves = _chip_exchange((w_ada, w_in, w_glu, w_br, w_out, conv_w), shapes,
                            lambda k, ref, t: _core_layers(ref, lay, lax.axis_index("c")), dst, "gather_weights")
    return _merge_core_halves(halves, lay)


def _merge_core_halves(halves, lay):
    n = len(halves)

    def body(*refs):
        out_refs, send_sems, recv_sems = refs[n:2 * n], refs[2 * n], refs[2 * n + 1]
        x, y, c = lax.axis_index("x"), lax.axis_index("y"), lax.axis_index("c")

        def part(k, core):
            if k == 1:
                half = lay // 2
                return out_refs[k].at[:, pl.ds(core * half, half)]
            return _core_layers(out_refs[k], lay, core)

        def copy(k, landing_core):
            return pltpu.make_async_remote_copy(
                src_ref=part(k, c), dst_ref=part(k, landing_core), send_sem=send_sems.at[k],
                recv_sem=recv_sems.at[k], device_id=(x, y, 1 - c), device_id_type=MESH)

        give = [copy(k, c) for k in range(n)]
        for cp in give:
            cp.start()
        for k in range(n):
            copy(k, 1 - c).wait_recv()
        for cp in give:
            cp.wait_send()

    return pl.pallas_call(
        body, name="gather_merge", in_specs=[ANY_SPEC] * n, out_specs=[ANY_SPEC] * n,
        out_shape=tuple(jax.ShapeDtypeStruct(h.shape, h.dtype) for h in halves),
        input_output_aliases={k: k for k in range(n)},
        scratch_shapes=[pltpu.SemaphoreType.DMA((n,)), pltpu.SemaphoreType.DMA((n,))],
    )(*halves)


def _scatter_grads(dw_ada, dw_in, dw_glu, dw_br, dw_out, dconv_w, small):
    def src(k, ref, t):
        if k == 0:
            return _col_window(ref, t * 768, 768)
        if k == 1:
            start = jnp.where(t == 0, WIN_START[0], jnp.where(t == 1, WIN_START[1],
                              jnp.where(t == 2, WIN_START[2], WIN_START[3])))
            return _col_window(ref, start, WIN_W)
        if k == 2:
            return _row_window(ref, t * 128, 128)
        if k == 3:
            return _col_window(ref, t * 256, 256)
        if k == 4:
            return _row_window(ref, t * 256, 256)
        if k == 5:
            return _col_window(ref, t * 128, 128)
        return ref

    pieces = ((D_MODEL, 768), (D_MODEL, WIN_W), (128, MIX_W), (4, MIX_W, 256), (256, D_MODEL), (8, 128), small.shape)
    srcs = (dw_ada, dw_in, dw_glu, dw_br, dw_out, dconv_w, small)
    shapes = tuple(jax.ShapeDtypeStruct((4,) + p, s.dtype) for p, s in zip(pieces, srcs))
    return _chip_exchange(srcs, shapes,
                          src, lambda k, ref, s: ref.at[s], "scatter_grads")


def _s5_tables(a_re, a_im, log_dt, b_re, b_im, c_re, c_im):
    ln = S5_CHUNK
    hi = lax.Precision.HIGHEST
    dt = jnp.exp(log_dt)[..., None]
    mag = jnp.exp(dt * a_re)
    abr = mag * jnp.cos(dt * a_im)
    abi = mag * jnp.sin(dt * a_im)
    den = a_re * a_re + a_im * a_im
    fr = ((abr - 1.0) * a_re + abi * a_im) / den
    fi = (abi * a_re - (abr - 1.0) * a_im) / den
    bbr = fr[..., None] * b_re - fi[..., None] * b_im
    bbi = fr[..., None] * b_im + fi[..., None] * b_re
    n = jnp.arange(ln + 1, dtype=F32)[:, None, None, None]
    pm = jnp.exp(n * dt * a_re)
    er = pm * jnp.cos(n * dt * a_im)
    ei = pm * jnp.sin(n * dt * a_im)
    e3 = lambda e, b, c: jnp.einsum("tdgp,dgpa,dgbp->dgabt", e, b, c, precision=hi)
    gt = e3(er[:ln], bbr, c_re) - e3(er[:ln], bbi, c_im) - e3(ei[:ln], bbr, c_im) - e3(ei[:ln], bbi, c_re)
    by_dir = lambda fwd, bwd: jnp.stack([fwd[:, 0], bwd[:, 1]], axis=1)
    erj, eij = by_dir(er[:ln][::-1], er[:ln]), by_dir(ei[:ln][::-1], ei[:ln])
    e2 = lambda e, b: jnp.einsum("jdgp,dgpa->dgajp", e, b, precision=hi)
    w = jnp.concatenate([e2(erj, bbr) - e2(eij, bbi), e2(erj, bbi) + e2(eij, bbr)], axis=-1)
    er1, ei1 = by_dir(er[1:], er[1:][::-1]), by_dir(ei[1:], ei[1:][::-1])
    ev = lambda c, e: jnp.einsum("dgbp,idgp->dgpbi", c, e, precision=hi)
    v = jnp.concatenate([ev(c_re, er1) - ev(c_im, ei1), -(ev(c_re, ei1) + ev(c_im, er1))], axis=2)
    a1 = jnp.concatenate([er[ln], er[ln]], axis=-1)
    a2 = jnp.concatenate([-ei[ln], ei[ln]], axis=-1)
    return (gt.transpose(1, 2, 3, 0, 4).reshape(S5_GROUPS, 256, 2 * ln),
            w.transpose(1, 2, 3, 0, 4).reshape(S5_GROUPS, S5_CH * ln, 256),
            v.transpose(1, 0, 2, 3, 4).reshape(S5_GROUPS, 256, S5_CH * ln),
            a1.reshape(64, 128), a2.reshape(64, 128))


def _lag_onehot():
    ln = S5_CHUNK
    j, i = np.meshgrid(np.arange(ln), np.arange(ln), indexing="ij")
    lag = np.arange(ln)[:, None, None]
    z = np.concatenate([lag == (i - j)[None], lag == (j - i)[None]], axis=0).astype(np.float32)
    return jnp.broadcast_to(jnp.asarray(z.reshape(2 * ln, ln * ln), BF16), (S5_GROUPS, 2 * ln, ln * ln))


def _toeplitz(gt):
    ln = S5_CHUNK
    flat = _matmul(gt, _lag_onehot(), out_dtype=BF16, name="s5_toeplitz")
    return (flat.reshape(S5_GROUPS, S5_CH, S5_CH, ln, ln).transpose(0, 1, 3, 2, 4)
            .reshape(S5_GROUPS, S5_CH * ln, S5_CH * ln))


def _toeplitz_fold(dk):
    ln = S5_CHUNK
    flat = dk.reshape(S5_GROUPS, S5_CH, ln, S5_CH, ln).transpose(0, 1, 3, 2, 4).reshape(S5_GROUPS, 256, ln * ln)
    return _matmul(flat, _lag_onehot(), trans_b=True, name="s5_toeplitz_fold")


def _chunk_rows(t):
    k = t // S5_CHUNK
    return k, -(-k // 128) * 128


def _to_chunks(u):
    k, kp = _chunk_rows(u.shape[0])
    v = u.reshape(k, S5_CHUNK, S5_GROUPS, S5_CH).transpose(2, 0, 3, 1).reshape(S5_GROUPS, k, S5_CH * S5_CHUNK)
    return jnp.pad(v, ((0, 0), (0, kp - k), (0, 0)))


def _from_chunks(y, t):
    k, _ = _chunk_rows(t)
    return y[:, :k].reshape(S5_GROUPS, k, S5_CH, S5_CHUNK).transpose(1, 3, 0, 2).reshape(t, MIX_W)


def _states_to_rows(s):
    kp = s.shape[1]
    return s.reshape(S5_GROUPS, kp, 2, 128).transpose(1, 2, 0, 3).reshape(kp, 64, 128)


def _rows_to_states(h):
    kp = h.shape[0]
    return h.reshape(kp, 2, S5_GROUPS, 128).transpose(2, 0, 1, 3).reshape(S5_GROUPS, kp, 256)


def _na_bias(rel_bias):
    a, m = np.meshgrid(np.arange(4), np.arange(12), indexing="ij")
    di = np.clip(m - a + 3, 0, 2 * NA_ROWS - 2).reshape(-1)
    qc, kc = np.meshgrid(np.arange(GRID_W), np.arange(GRID_W), indexing="ij")
    dj = np.clip(kc - qc + NA_COLS - 1, 0, 2 * NA_COLS - 2).reshape(-1)
    oh_i = jnp.asarray(di[:, None] == np.arange(2 * NA_ROWS - 1)[None, :], F32)
    oh_j = jnp.asarray(dj[:, None] == np.arange(2 * NA_COLS - 1)[None, :], F32)
    hi = lax.Precision.HIGHEST
    cols = jnp.einsum("hij,cj->hic", rel_bias, oh_j, precision=hi)
    full = jnp.einsum("ri,hic->hrc", oh_i, cols, precision=hi)
    full = full.reshape(N_HEADS, 4, 12, GRID_W, GRID_W).transpose(0, 1, 3, 2, 4)
    return full.reshape(4, 2, ATT_BLK, 3 * ATT_BLK)


def _rope_tables(n_ctx, n_lat):
    tok = jnp.arange(n_lat, dtype=jnp.int32)
    row = (tok // GRID_W).astype(F32)
    col = (tok % GRID_W).astype(F32)
    inv = ROPE_BASE ** (-jnp.arange(ROPE_PAIRS, dtype=F32) / ROPE_PAIRS)
    ang = jnp.concatenate([row[:, None] * inv, col[:, None] * inv], axis=-1)
    cos, sin = jnp.cos(ang), jnp.sin(ang)
    cos = jnp.tile(jnp.concatenate([cos, cos], axis=-1), (1, 2))
    sin = jnp.tile(jnp.concatenate([-sin, sin], axis=-1), (1, 2))
    return (jnp.concatenate([jnp.ones((n_ctx, 128), F32), cos], axis=0),
            jnp.concatenate([jnp.zeros((n_ctx, 128), F32), sin], axis=0))


def _pad_blocks(a, n_ctx):
    return jnp.pad(a[n_ctx:], ((ATT_BLK, ATT_BLK), (0, 0)))


def _layer_fwd(xt, cc, w, rope, n_ctx):
    t = xt.shape[0]
    sv = {}
    mod = _adaln_fwd(cc, w["w_ada"], w["b_ada"].reshape(1, -1))
    mod4 = mod[:2].reshape(2, 3, 1, D_MODEL)
    h = _modnorm_fwd(xt, w["norm_g"].reshape(1, -1), mod4, n_ctx)
    z = _matmul(h, w["w_in"], name="proj_fwd")

    s5_args = (w["s5_a_re"], w["s5_a_im"], w["s5_log_dt"], w["s5_b_re"], w["s5_b_im"], w["s5_c_re"], w["s5_c_im"])
    (gt, tw, tv, a1, a2), tab_vjp = jax.vjp(_s5_tables, *s5_args)
    ktoe = _toeplitz(gt)
    tw, tv = tw.astype(BF16), tv.astype(BF16)
    uc = _to_chunks(z[:, :MIX_W].astype(BF16))
    st = _matmul(uc, tw, name="s5_chunk_state")
    hprev = _s5_scan_fwd(_states_to_rows(st), a1, a2, t // S5_CHUNK, n_ctx // S5_CHUNK)
    uh = jnp.concatenate([uc, _rows_to_states(hprev).astype(BF16)], axis=2)
    ysum = _from_chunks(_matmul(uh, jnp.concatenate([ktoe, tv], axis=1), name="s5_chunk_out"), t)
    s5_d = w["s5_d"].reshape(1, MIX_W)
    y_s5 = _s5post_fwd(ysum, z, s5_d, w["s5_w_glu"])

    conv_w = w["conv_w"]
    y_conv = _conv_fwd(z, conv_w, w["conv_b"].reshape(1, -1), n_ctx)

    gains = (jnp.tile(w["na_q_g"], 8)[None], jnp.tile(w["na_k_g"], 8)[None],
             jnp.tile(w["gqa_q_g"], 8)[None], jnp.tile(w["gqa_k_g"], 2)[None])
    q_na, k_na, v_na, q_g, k_g, v_g = _prep_fwd(z, gains, rope)
    bias, bias_vjp = jax.vjp(_na_bias, w["na_rel_bias"])
    sink = jnp.zeros((4, 8, 128), F32).at[:, :2, :].set(
        jnp.broadcast_to(w["gqa_sink"].reshape(4, 2, 1), (4, 2, 128)))
    na_tab = jnp.where(_window_patterns("na", t - n_ctx)[:, None, None], bias[None], NEG_INF)
    gqa_tab = jnp.where(_window_patterns("gqa", t - n_ctx), 0.0, NEG_INF).astype(F32)
    na_in = (q_na, _pad_blocks(k_na, n_ctx), _pad_blocks(v_na, n_ctx), k_na[:n_ctx], v_na[:n_ctx], na_tab, None)
    gqa_in = (q_g, _pad_blocks(k_g, n_ctx), _pad_blocks(v_g, n_ctx), k_g[:n_ctx], v_g[:n_ctx], gqa_tab, sink)
    y_na = _attn_fwd(*na_in, mode="na", n_ctx=n_ctx)
    y_gqa = _attn_fwd(*gqa_in, mode="gqa", n_ctx=n_ctx)
    ys = (y_s5, y_conv, y_na, y_gqa)
    xt_new = _merge_fwd(xt, ys, z, mod4, w["w_br"], w["w_out"], n_ctx)
    sv.update(xt=xt, mod4=mod4, h=h, z=z, tab_vjp=tab_vjp, ktoe=ktoe, tw=tw, tv=tv, a1=a1, a2=a2, uc=uc,
              hprev=hprev, uh=uh, ysum=ysum, s5_d=s5_d, conv_w=conv_w, gains=gains, bias_vjp=bias_vjp,
              na_in=na_in, gqa_in=gqa_in, ys=ys)
    return xt_new, sv


def _layer_bwd(dxt_new, sv, cc, w, rope, n_ctx):
    t = dxt_new.shape[0]
    z, mod4 = sv["z"], sv["mod4"]
    dys, dgt, dmg, dgate, dw_br, dw_out = _merge_bwd(dxt_new, sv["ys"], z, mod4, w["w_br"], w["w_out"], n_ctx)

    dpre, du_skip, dd, dw_glu = _s5post_bwd(sv["ysum"], z, sv["s5_d"], w["s5_w_glu"], dys[0])
    dyc = _to_chunks(dpre)
    dhp = _matmul(dyc, sv["tv"], trans_b=True, name="s5_bwd_state")
    ds, da1, da2 = _s5_scan_bwd(_states_to_rows(dhp), sv["hprev"], sv["a1"], sv["a2"],
                                t // S5_CHUNK, n_ctx // S5_CHUNK)
    ds = _rows_to_states(ds).astype(BF16)
    duc = _matmul(jnp.concatenate([dyc, ds], axis=2), jnp.concatenate([sv["ktoe"], sv["tw"]], axis=2),
                  trans_b=True, out_dtype=BF16, name="s5_bwd_u")
    dkv = _matmul(sv["uh"].transpose(0, 2, 1), dyc, out_dtype=BF16, name="s5_bwd_kv")
    dtw = _matmul(sv["uc"].transpose(0, 2, 1), ds, name="s5_bwd_w")
    dgt_tab = _toeplitz_fold(dkv[:, :S5_CH * S5_CHUNK])
    s5_grads = sv["tab_vjp"]((dgt_tab, dtw, dkv[:, S5_CH * S5_CHUNK:].astype(F32), da1, da2))
    du_scan = _from_chunks(duc, t)

    dzv, dzb, dzc, dconv_w, dconv_b = _conv_bwd(z, sv["conv_w"], w["conv_b"].reshape(1, -1), dys[1], n_ctx)

    dq_na, dk_na, dv_na, dkc_na, dvc_na, dbias, _ = _attn_bwd(*sv["na_in"], dys[2], mode="na", n_ctx=n_ctx)
    dq_g, dk_g, dv_g, dkc_g, dvc_g, _, dsink = _attn_bwd(*sv["gqa_in"], dys[3], mode="gqa", n_ctx=n_ctx)
    pb = _prep_bwd(z, sv["gains"], rope, (dq_na, dq_g), (dk_na, dv_na, dk_g, dv_g),
                   (dkc_na, dvc_na, dkc_g, dvc_g), du_skip, du_scan, n_ctx)
    dz_naq, dz_nak, dz_nav, dz_gq, dz_gk, dz_gv, dz_u, dg_naq, dg_nak, dg_gq, dg_gk = pb

    dz = jnp.concatenate([dz_u, dgt[0], dzv, dzb, dzc, dgt[1], dz_naq, dz_nak, dz_nav, dgt[2], dz_gq, dz_gk, dz_gv,
                          jnp.zeros((t, OFF["gqa_gate"] - OFF["pad"]), BF16), dgt[3], *dmg], axis=1)
    dh = _matmul(dz, w["w_in"], trans_b=True, name="proj_bwd_x")
    dw_in = _matmul(sv["h"].T, dz, out_dtype=BF16, name="proj_bwd_w")
    dxt, dnorm_g, dshift, dscale = _modnorm_bwd(sv["xt"], w["norm_g"].reshape(1, -1), mod4, dh, dxt_new, n_ctx)
    dmod = jnp.concatenate([dshift, dscale, dgate], axis=1).reshape(2, 3 * D_MODEL)
    dcc, dw_ada, db_ada = _adaln_bwd(cc, w["w_ada"], jnp.pad(dmod, ((0, 6), (0, 0))))

    (drel,) = sv["bias_vjp"](dbias.sum(0))
    grads = dict(
        norm_g=dnorm_g[0], w_ada=dw_ada, b_ada=db_ada[0], w_in=dw_in,
        s5_a_re=s5_grads[0], s5_a_im=s5_grads[1], s5_log_dt=s5_grads[2], s5_b_re=s5_grads[3], s5_b_im=s5_grads[4],
        s5_c_re=s5_grads[5], s5_c_im=s5_grads[6], s5_d=dd.reshape(S5_GROUPS, S5_CH), s5_w_glu=dw_glu,
        conv_w=dconv_w, conv_b=dconv_b[0],
        na_q_g=dg_naq.reshape(8, HEAD_DIM).sum(0), na_k_g=dg_nak.reshape(8, HEAD_DIM).sum(0), na_rel_bias=drel,
        gqa_q_g=dg_gq.reshape(8, HEAD_DIM).sum(0), gqa_k_g=dg_gk.reshape(2, HEAD_DIM).sum(0),
        gqa_sink=dsink[:, :2, :].sum(-1).reshape(8), w_br=dw_br, w_out=dw_out)
    return dxt, dcc, grads


SHARDED = ("w_ada", "w_in", "s5_w_glu", "conv_w", "w_br", "w_out")
REPLICATED = ("norm_g", "b_ada", "s5_a_re", "s5_a_im", "s5_log_dt", "s5_b_re", "s5_b_im", "s5_c_re", "s5_c_im",
              "s5_d", "conv_b", "na_q_g", "na_k_g", "na_rel_bias", "gqa_q_g", "gqa_k_g", "gqa_sink")
WEIGHTS = ("c_ctx", "norm_g", "w_ada", "b_ada", "w_in", "s5_a_re", "s5_a_im", "s5_log_dt", "s5_b_re", "s5_b_im",
           "s5_c_re", "s5_c_im", "s5_d", "s5_w_glu", "conv_w", "conv_b", "na_q_g", "na_k_g", "na_rel_bias",
           "gqa_q_g", "gqa_k_g", "gqa_sink", "w_br", "w_out")


def _pack(pieces, row_multiple, dtype):
    flat = jnp.concatenate([p.reshape(-1).astype(dtype) for p in pieces])
    rows = -(-flat.shape[0] // PACK_W)
    rows = -(-rows // row_multiple) * row_multiple
    return jnp.pad(flat, (0, rows * PACK_W - flat.shape[0])).reshape(rows, PACK_W)


def _unpack(buf, shapes):
    flat = buf.reshape(-1)
    out, pos = [], 0
    for shp in shapes:
        size = int(np.prod(shp))
        out.append(flat[pos:pos + size].reshape(shp))
        pos += size
    return out


def _local_step(x, ctx, target, c_vec, c_ctx, layers):
    depth = len(layers)
    n_ctx, n_lat = ctx.shape[0], x.shape[0]
    cc = jnp.zeros((8, D_MODEL), F32).at[0].set(c_ctx).at[1].set(c_vec)
    rope = _rope_tables(n_ctx, n_lat)
    xt = jnp.concatenate([ctx, x], axis=0)
    saved = []
    for l in range(depth):
        xt, sv = _layer_fwd(xt, cc, layers[l], rope, n_ctx)
        saved.append(sv)
    loss_tile, dxt = _loss_head(xt, target, n_ctx)
    grads = [None] * depth
    dc_ctx = jnp.zeros((D_MODEL,), F32)
    for l in reversed(range(depth)):
        dxt, dcc, grads[l] = _layer_bwd(dxt, saved[l], cc, layers[l], rope, n_ctx)
        dc_ctx = dc_ctx + dcc[0]
    return loss_tile[0, 0], dxt[n_ctx:][None], dc_ctx, grads


def kernel(x, c, ctx, c_ctx, norm_g, w_ada, b_ada, w_in, s5_a_re, s5_a_im, s5_log_dt, s5_b_re, s5_b_im,
           s5_c_re, s5_c_im, s5_d, s5_w_glu, conv_w, conv_b, na_q_g, na_k_g, na_rel_bias, gqa_q_g,
           gqa_k_g, gqa_sink, w_br, w_out, loss_target, m_c_ctx, m_norm_g, m_w_ada, m_b_ada, m_w_in,
           m_s5_a_re, m_s5_a_im, m_s5_log_dt, m_s5_b_re, m_s5_b_im, m_s5_c_re, m_s5_c_im, m_s5_d,
           m_s5_w_glu, m_conv_w, m_conv_b, m_na_q_g, m_na_k_g, m_na_rel_bias, m_gqa_q_g, m_gqa_k_g,
           m_gqa_sink, m_w_br, m_w_out, v_c_ctx, v_norm_g, v_w_ada, v_b_ada, v_w_in, v_s5_a_re,
           v_s5_a_im, v_s5_log_dt, v_s5_b_re, v_s5_b_im, v_s5_c_re, v_s5_c_im, v_s5_d, v_s5_w_glu,
           v_conv_w, v_conv_b, v_na_q_g, v_na_k_g, v_na_rel_bias, v_gqa_q_g, v_gqa_k_g, v_gqa_sink,
           v_w_br, v_w_out):
    a = dict(locals())
    depth = a["norm_g"].shape[0]
    x, ctx, target = a["x"][0], a["ctx"][0], a["loss_target"][0]
    n_ctx, n_lat = ctx.shape[0], x.shape[0]
    assert n_ctx % ATT_BLK == 0 and n_lat % (4 * GRID_W) == 0 and n_lat // GRID_W >= NA_ROWS

    cast = lambda n: a[n].astype(BF16)
    conv8 = jnp.pad(a["conv_w"], ((0, 0), (0, 5), (0, 0)))
    g_ada, g_in, g_glu, g_br, g_out, g_conv = _gather_weights(
        cast("w_ada"), cast("w_in"), cast("s5_w_glu"), cast("w_br"), cast("w_out"), conv8)
    zpad = jnp.zeros((D_MODEL, OFF["gqa_gate"] - OFF["pad"]), BF16)
    split = OFF["pad"] - 2 * N_SHARD_IN
    layers = []
    for l in range(depth):
        w = {n: a[n][l] for n in REPLICATED}
        w.update(w_ada=g_ada[l], s5_w_glu=g_glu[l], w_br=g_br[l], w_out=g_out[l], conv_w=g_conv[l])
        w["w_in"] = jnp.concatenate([g_in[0, l], g_in[1, l], g_in[2, l][:, :split], zpad, g_in[2, l][:, split:],
                                     g_in[3, l]], axis=1)
        layers.append(w)

    loss_local, grad_x, dc_ctx, grads = _local_step(x, ctx, target, a["c"][0], a["c_ctx"], layers)
    loss = lax.psum(loss_local, ("x", "y", "c"))

    chip = 2 * lax.axis_index("x") + lax.axis_index("y")
    take = [functools.partial(lambda win, pc: jnp.concatenate([win[:, lo:hi] for lo, hi in pc], axis=1), pc=pc)
            for pc in WIN_PIECES]

    def small_pack(values, c_ctx_value, l):
        pieces = [values[n] for n in REPLICATED]
        pieces.append(c_ctx_value if l == 0 else jnp.zeros((D_MODEL,), F32))
        return _pack(pieces, 8 * SUM_STEPS, F32)

    mine, theirs = [], []
    for l in range(depth):
        g = grads[l]
        small = small_pack(g, dc_ctx, l)
        recv = _scatter_grads(g["w_ada"], g["w_in"], g["s5_w_glu"], g["w_br"], g["w_out"], g["conv_w"], small)
        part = list(_sum_chips([r.reshape(4, -1, r.shape[-1]) for r in recv]))
        part[1] = lax.switch(chip, take, part[1])
        mine.append(part)
        theirs.append(_core_swap(part))

    families = ("w_ada", "w_in", "s5_w_glu", "w_br", "w_out", "conv_w")
    out = {}
    for k, n in enumerate(families):
        p, q = jnp.stack([m[k] for m in mine]), jnp.stack([t[k] for t in theirs])
        if n == "conv_w":
            p, q = p[:, :3], q[:, :3]
        as3d = lambda arr: arr.reshape(depth, -1, arr.shape[-1])
        res = _adamw(p, q, as3d(a[n]), as3d(a["m_" + n]), as3d(a["v_" + n]), "adamw_" + n)
        out[n] = [r.reshape(a[n].shape) for r in res]
    p, q = jnp.stack([m[6] for m in mine]), jnp.stack([t[6] for t in theirs])
    packs = [jnp.stack([small_pack({n: a[pre + n][l] for n in REPLICATED}, a[pre + "c_ctx"], l)
                        for l in range(depth)]) for pre in ("", "m_", "v_")]
    res = _adamw(p, q, *packs, "adamw_small")
    shapes = [a[n].shape[1:] for n in REPLICATED] + [a["c_ctx"].shape]
    per_layer = [[_unpack(r[l], shapes) for l in range(depth)] for r in res]
    for j, n in enumerate(REPLICATED):
        out[n] = [jnp.stack([per_layer[key][l][j] for l in range(depth)]) for key in range(4)]
    out["c_ctx"] = [per_layer[key][0][-1] for key in range(4)]
    results = [loss, grad_x]
    for key in range(4):
        results += [out[n][key] for n in WEIGHTS]
    return tuple(results)
```

```python
import functools

import numpy as np
import jax
import jax.numpy as jnp
from jax import lax
from jax.experimental import pallas as pl
from jax.experimental.pallas import tpu as pltpu

F32 = jnp.float32
BF16 = jnp.bfloat16

D_MODEL = 1024
MIX_W = 512
GRID_W = 64
HEAD_DIM = 64
N_HEADS = 8
S5_GROUPS = 32
S5_CH = 16
S5_CHUNK = 32
NA_ROWS = 8
NA_COLS = 16
WINDOW = 128
ROPE_BASE = 10000.0
ROPE_PAIRS = 16
EPS = 1e-6
NEG_INF = -1e30
ATT_BLK = 256
TOK = 256
VMEM_LIMIT = 56 * 1024 * 1024

ADAM_LR, ADAM_B1, ADAM_B2, ADAM_EPS, ADAM_WD, ADAM_STEP = 0.001, 0.9, 0.999, 1e-8, 0.01, 10

OFF = dict(s5_u=0, s5_gate=512, conv_v=1024, conv_b=1536, conv_c=2048, conv_gate=2560,
           na_q=3072, na_k=3584, na_v=4096, na_gate=4608, gqa_q=5120, gqa_k=5632, gqa_v=5760,
           pad=5888, gqa_gate=6144, merge_s5=6656, merge_conv=7680, merge_na=8704, merge_gqa=9728)
N_Z = 10752
GATE_OFFS = (OFF["s5_gate"], OFF["conv_gate"], OFF["na_gate"], OFF["gqa_gate"])
MERGE_OFFS = (OFF["merge_s5"], OFF["merge_conv"], OFF["merge_na"], OFF["merge_gqa"])


def _cparams(sem):
    return pltpu.CompilerParams(dimension_semantics=sem, vmem_limit_bytes=VMEM_LIMIT)


def _dot(a, b, ca, cb):
    return lax.dot_general(a.astype(BF16), b.astype(BF16), (((ca,), (cb,)), ((), ())),
                           preferred_element_type=F32)


def _dot_tn(a, b):
    return _dot(a.astype(F32).T, b, 1, 0)


@jax.custom_vjp
def mm(a, b):
    return _dot(a, b, 1, 0)


@jax.custom_vjp
def mm_nt(a, b):
    return _dot(a, b, 1, 1)


@jax.custom_vjp
def mm_tn(a, b):
    return _dot_tn(a, b)


mm.defvjp(lambda a, b: (mm(a, b), (a, b)), lambda r, g: (mm_nt(g, r[1]), mm_tn(r[0], g)))
mm_nt.defvjp(lambda a, b: (mm_nt(a, b), (a, b)), lambda r, g: (mm(g, r[1]), mm_tn(g, r[0])))
mm_tn.defvjp(lambda a, b: (mm_tn(a, b), (a, b)), lambda r, g: (mm_nt(r[1], g), mm(r[0], g)))


@functools.partial(jax.custom_vjp, nondiff_argnums=(1,))
def lane_roll(x, shift):
    return pltpu.roll(x, shift, 1)


lane_roll.defvjp(lambda x, shift: (lane_roll(x, shift), None),
                 lambda shift, _, g: (lane_roll(g, (g.shape[1] - shift) % g.shape[1]),))


def _silu(x):
    return x * jax.nn.sigmoid(x)


def _dsilu(x):
    s = jax.nn.sigmoid(x)
    return s * (1.0 + x * (1.0 - s))


def _pick(n, prefs):
    for p in prefs:
        if n % p == 0:
            return p
    return n


def _matmul(a, b, *, trans_b=False, out_dtype=F32, tm=None, tn=None, tk=None, name):
    squeeze = a.ndim == 2
    if squeeze:
        a, b = a[None], b[None]
    nb, m, k = a.shape
    n = b.shape[1] if trans_b else b.shape[2]
    tm = tm or _pick(m, (1280, 1024, 640, 512, 256, 128))
    tn = tn or _pick(n, (1536, 1024, 512, 256, 128))
    tk = tk or _pick(k, (1536, 1280, 1024, 768, 640, 512, 256, 128))
    nk = k // tk

    def body(a_ref, b_ref, o_ref, *scr):
        part = _dot(a_ref[...], b_ref[...], 1, 1 if trans_b else 0)
        if nk == 1:
            o_ref[...] = part.astype(out_dtype)
        else:
            acc = scr[0]
            kk = pl.program_id(3)

            @pl.when(kk == 0)
            def _():
                acc[...] = part

            @pl.when(kk > 0)
            def _():
                acc[...] += part

            @pl.when(kk == nk - 1)
            def _():
                o_ref[...] = acc[...].astype(out_dtype)

    if trans_b:
        b_spec = pl.BlockSpec((None, tn, tk), lambda bb, i, j, kk: (bb, j, kk))
    else:
        b_spec = pl.BlockSpec((None, tk, tn), lambda bb, i, j, kk: (bb, kk, j))
    out = pl.pallas_call(
        body, name=name,
        grid=(nb, m // tm, n // tn, nk),
        in_specs=[pl.BlockSpec((None, tm, tk), lambda bb, i, j, kk: (bb, i, kk)), b_spec],
        out_specs=pl.BlockSpec((None, tm, tn), lambda bb, i, j, kk: (bb, i, j)),
        out_shape=jax.ShapeDtypeStruct((nb, m, n), out_dtype),
        scratch_shapes=[] if nk == 1 else [pltpu.VMEM((tm, tn), F32)],
        compiler_params=_cparams(("parallel", "parallel", "parallel", "arbitrary")),
    )(a, b)
    return out[0] if squeeze else out


def _adaln_fn(cc, w, b):
    return mm(_silu(cc), w) + b


def _adaln_fwd(cc, w_ada, b_ada):
    def body(cc_ref, w_ref, b_ref, o_ref):
        o_ref[...] = _adaln_fn(cc_ref[...], w_ref[...], b_ref[...])

    return pl.pallas_call(
        body, name="adaln_fwd", out_shape=jax.ShapeDtypeStruct((8, 3 * D_MODEL), F32),
        compiler_params=pltpu.CompilerParams(vmem_limit_bytes=VMEM_LIMIT),
    )(cc, w_ada, b_ada)


def _adaln_bwd(cc, w_ada, dmod):
    def body(cc_ref, w_ref, g_ref, dcc_ref, dw_ref, db_ref):
        cc_v, g = cc_ref[...], g_ref[...]
        dw_ref[...] = mm_tn(_silu(cc_v), g).astype(BF16)
        db_ref[...] = jnp.sum(g, axis=0, keepdims=True)
        dcc_ref[...] = mm_nt(g, w_ref[...]) * _dsilu(cc_v)

    return pl.pallas_call(
        body, name="adaln_bwd",
        out_shape=(jax.ShapeDtypeStruct((8, D_MODEL), F32),
                   jax.ShapeDtypeStruct((D_MODEL, 3 * D_MODEL), BF16),
                   jax.ShapeDtypeStruct((1, 3 * D_MODEL), F32)),
        compiler_params=pltpu.CompilerParams(vmem_limit_bytes=VMEM_LIMIT),
    )(cc, w_ada, dmod)


def _seg_spec(which, n_ctx_tiles):
    return pl.BlockSpec((None, None, 1, D_MODEL),
                        lambda i: (jnp.where(i < n_ctx_tiles, 0, 1), which, 0, 0))


def _row_spec(width, col_block=0, tile=TOK):
    return pl.BlockSpec((tile, width), lambda i: (i, col_block))


def _const_spec(shape):
    zeros = (0,) * len(shape)
    return pl.BlockSpec(shape, lambda i: zeros)


def _modnorm_fn(x, g, shift, scale):
    y = x * lax.rsqrt(jnp.mean(x * x, axis=-1, keepdims=True) + EPS)
    return (y * g) * (1.0 + scale) + shift


def _modnorm_fwd(xt, g, mod4, n_ctx):
    t = xt.shape[0]
    nct = n_ctx // TOK

    def body(x_ref, g_ref, sh_ref, sc_ref, o_ref):
        o_ref[...] = _modnorm_fn(x_ref[...], g_ref[...], sh_ref[...], sc_ref[...]).astype(BF16)

    return pl.pallas_call(
        body, name="modnorm_fwd", grid=(t // TOK,),
        in_specs=[_row_spec(D_MODEL), _const_spec((1, D_MODEL)), _seg_spec(0, nct), _seg_spec(1, nct)],
        out_specs=_row_spec(D_MODEL),
        out_shape=jax.ShapeDtypeStruct((t, D_MODEL), BF16),
        compiler_params=_cparams(("parallel",)),
    )(xt, g, mod4, mod4)


def _modnorm_bwd(xt, g, mod4, dh, dres, n_ctx):
    t = xt.shape[0]
    nct = n_ctx // TOK

    def body(x_ref, g_ref, sh_ref, sc_ref, dh_ref, dres_ref, dx_ref, dg_ref, dsh_ref, dsc_ref):
        i = pl.program_id(0)
        _, vjp = jax.vjp(_modnorm_fn, x_ref[...], g_ref[...], sh_ref[...], sc_ref[...])
        dx, dg, dsh, dsc = vjp(dh_ref[...])
        dx_ref[...] = dx + dres_ref[...]

        @pl.when(i == 0)
        def _():
            dg_ref[...] = jnp.zeros_like(dg_ref)

        dg_ref[...] += dg
        first = jnp.logical_or(i == 0, i == nct)

        @pl.when(first)
        def _():
            dsh_ref[...] = dsh
            dsc_ref[...] = dsc

        @pl.when(jnp.logical_not(first))
        def _():
            dsh_ref[...] += dsh
            dsc_ref[...] += dsc

    seg_out = lambda which: pl.BlockSpec((None, None, 1, D_MODEL),
                                         lambda i: (jnp.where(i < nct, 0, 1), which, 0, 0))
    dx, dg, dss, dss2 = pl.pallas_call(
        body, name="modnorm_bwd", grid=(t // TOK,),
        in_specs=[_row_spec(D_MODEL), _const_spec((1, D_MODEL)), _seg_spec(0, nct), _seg_spec(1, nct),
                  _row_spec(D_MODEL), _row_spec(D_MODEL)],
        out_specs=[_row_spec(D_MODEL), _const_spec((1, D_MODEL)), seg_out(0), seg_out(0)],
        out_shape=(jax.ShapeDtypeStruct((t, D_MODEL), F32), jax.ShapeDtypeStruct((1, D_MODEL), F32),
                   jax.ShapeDtypeStruct((2, 1, 1, D_MODEL), F32), jax.ShapeDtypeStruct((2, 1, 1, D_MODEL), F32)),
        compiler_params=_cparams(("arbitrary",)),
    )(xt, g, mod4, mod4, dh, dres)
    return dx, dg, dss, dss2


def _group_mean_sq(x, gs):
    x2 = x * x
    hi = x2.astype(BF16).astype(F32)
    return mm(hi, gs) + mm(x2 - hi, gs)


def _head_norm(x, g, gs):
    return (x * lax.rsqrt(_group_mean_sq(x, gs) + EPS)) * g


def _rope(x, cos, sin_signed):
    lane = lax.broadcasted_iota(jnp.int32, (1, 128), 1)
    first_half = jnp.bitwise_and(lane, 63) < 32
    cols = []
    for c in range(x.shape[1] // 128):
        xb = x[:, 128 * c:128 * (c + 1)]
        partner = jnp.where(first_half, lane_roll(xb, 96), lane_roll(xb, 32))
        cols.append(xb * cos + partner * sin_signed)
    return cols[0] if len(cols) == 1 else jnp.concatenate(cols, axis=1)


def _prep_fn(zq_na, zk_na, zv_na, zq_g, zk_g, zv_g, g_naq, g_nak, g_gq, g_gk, cos, sin_signed, gs512, gs128, expand):
    q_na = _head_norm(zq_na, g_naq, gs512)
    k_na = _head_norm(zk_na, g_nak, gs512)
    q_g = _rope(_head_norm(zq_g, g_gq, gs512), cos, sin_signed)
    k_g = _rope(_head_norm(zk_g, g_gk, gs128), cos, sin_signed)
    return q_na, k_na, zv_na, q_g, mm(k_g, expand), mm(zv_g, expand)


def _prep_consts():
    gid = np.arange(512) // 64
    gs512 = (gid[:, None] == gid[None, :]).astype(np.float32) / 64.0
    expand = np.zeros((128, 512), np.float32)
    for h in range(N_HEADS):
        for j in range(64):
            expand[64 * (h // 4) + j, 64 * h + j] = 1.0
    return jnp.asarray(gs512), jnp.asarray(gs512[:128, :128]), jnp.asarray(expand)


def _prep_in_specs():
    blk = lambda off, w: _row_spec(w, off // w)
    return [blk(OFF["na_q"], 512), blk(OFF["na_k"], 512), blk(OFF["na_v"], 512), blk(OFF["gqa_q"], 512),
            blk(OFF["gqa_k"], 128), blk(OFF["gqa_v"], 128),
            _const_spec((1, 512)), _const_spec((1, 512)), _const_spec((1, 512)), _const_spec((1, 128)),
            _row_spec(128), _row_spec(128),
            _const_spec((512, 512)), _const_spec((128, 128)), _const_spec((128, 512))]


def _prep_fwd(z, gains, rope_tabs):
    t = z.shape[0]
    consts = _prep_consts()

    def body(*refs):
        ins, outs = refs[:15], refs[15:]
        res = _prep_fn(*[r[...] for r in ins])
        for o_ref, v in zip(outs, res):
            o_ref[...] = v.astype(BF16)

    return pl.pallas_call(
        body, name="prep_fwd", grid=(t // TOK,),
        in_specs=_prep_in_specs(),
        out_specs=[_row_spec(512)] * 6,
        out_shape=tuple(jax.ShapeDtypeStruct((t, 512), BF16) for _ in range(6)),
        compiler_params=_cparams(("parallel",)),
    )(z, z, z, z, z, z, *gains, *rope_tabs, *consts)


def _prep_bwd(z, gains, rope_tabs, dqs, dkv_lat, dkv_ctx, du_a, du_b, n_ctx):
    t = z.shape[0]
    nct = n_ctx // TOK
    consts = _prep_consts()

    def body(*refs):
        ins, dq_refs, lat_refs, ctx_refs = refs[:15], refs[15:17], refs[17:21], refs[21:25]
        (dua_ref, dub_ref), outs = refs[25:27], refs[27:]
        i = pl.program_id(0)
        vals = [r[...] for r in ins]
        _, vjp = jax.vjp(lambda *a: _prep_fn(*a, *vals[10:]), *vals[:10])
        kv = [jnp.where(i < nct, c_ref[...], l_ref[...]) for l_ref, c_ref in zip(lat_refs, ctx_refs)]
        grads = vjp((dq_refs[0][...], kv[0], kv[1], dq_refs[1][...], kv[2], kv[3]))
        for o_ref, v in zip(outs[:6], grads[:6]):
            o_ref[...] = v.astype(BF16)
        outs[6][...] = (dua_ref[...] + dub_ref[...]).astype(BF16)

        @pl.when(i == 0)
        def _():
            for o_ref in outs[7:]:
                o_ref[...] = jnp.zeros_like(o_ref)

        for o_ref, v in zip(outs[7:], grads[6:10]):
            o_ref[...] += v

    lat_spec = pl.BlockSpec((TOK, 512), lambda i: (jnp.maximum(i - nct + 1, 0), 0))
    ctx_spec = pl.BlockSpec((TOK, 512), lambda i: (jnp.minimum(i, nct - 1), 0))
    return pl.pallas_call(
        body, name="prep_bwd", grid=(t // TOK,),
        in_specs=_prep_in_specs() + [_row_spec(512)] * 2 + [lat_spec] * 4 + [ctx_spec] * 4 + [_row_spec(512)] * 2,
        out_specs=[_row_spec(512)] * 4 + [_row_spec(128)] * 2 + [_row_spec(512)]
        + [_const_spec((1, 512))] * 3 + [_const_spec((1, 128))],
        out_shape=tuple([jax.ShapeDtypeStruct((t, 512), BF16)] * 4 + [jax.ShapeDtypeStruct((t, 128), BF16)] * 2
                        + [jax.ShapeDtypeStruct((t, 512), BF16)]
                        + [jax.ShapeDtypeStruct((1, 512), F32)] * 3 + [jax.ShapeDtypeStruct((1, 128), F32)]),
        compiler_params=_cparams(("arbitrary",)),
    )(z, z, z, z, z, z, *gains, *rope_tabs, *consts, *dqs, *dkv_lat, *dkv_ctx, du_a, du_b)


def _s5post_fn(ys, u, d, w_glu):
    y = jax.nn.gelu(ys + d * u)
    return y * jax.nn.sigmoid(mm(y, w_glu))


def _s5post_fwd(ys, z, d, w_glu):
    t = z.shape[0]

    def body(ys_ref, u_ref, d_ref, w_ref, o_ref):
        o_ref[...] = _s5post_fn(ys_ref[...], u_ref[...], d_ref[...], w_ref[...])

    return pl.pallas_call(
        body, name="s5post_fwd", grid=(t // TOK,),
        in_specs=[_row_spec(512), _row_spec(512, OFF["s5_u"] // 512),
                  _const_spec((1, 512)), _const_spec((512, 512))],
        out_specs=_row_spec(512), out_shape=jax.ShapeDtypeStruct((t, 512), F32),
        compiler_params=_cparams(("parallel",)),
    )(ys, z, d, w_glu)


def _s5post_bwd(ys, z, d, w_glu, dy):
    t = z.shape[0]

    def body(ys_ref, u_ref, d_ref, w_ref, dy_ref, dpre_ref, du_ref, dd_ref, dw_ref):
        i = pl.program_id(0)
        _, vjp = jax.vjp(_s5post_fn, ys_ref[...], u_ref[...], d_ref[...], w_ref[...].astype(F32))
        dys, du, dd, dw = vjp(dy_ref[...])
        dpre_ref[...] = dys.astype(BF16)
        du_ref[...] = du

        @pl.when(i == 0)
        def _():
            dd_ref[...] = jnp.zeros_like(dd_ref)
            dw_ref[...] = jnp.zeros_like(dw_ref)

        dd_ref[...] += dd
        dw_ref[...] += dw

    return pl.pallas_call(
        body, name="s5post_bwd", grid=(t // TOK,),
        in_specs=[_row_spec(512), _row_spec(512, OFF["s5_u"] // 512),
                  _const_spec((1, 512)), _const_spec((512, 512)), _row_spec(512)],
        out_specs=[_row_spec(512), _row_spec(512), _const_spec((1, 512)), _const_spec((512, 512))],
        out_shape=(jax.ShapeDtypeStruct((t, 512), BF16), jax.ShapeDtypeStruct((t, 512), F32),
                   jax.ShapeDtypeStruct((1, 512), F32), jax.ShapeDtypeStruct((512, 512), F32)),
        compiler_params=_cparams(("arbitrary",)),
    )(ys, z, d, w_glu, dy)


def _halo_specs(col_block, t):
    last = t // 8 - 1
    prev = pl.BlockSpec((8, 512), lambda i: (jnp.maximum(i * (TOK // 8) - 1, 0), col_block))
    nxt = pl.BlockSpec((8, 512), lambda i: (jnp.minimum((i + 1) * (TOK // 8), last), col_block))
    return [_row_spec(512, col_block), prev, nxt]


def _shifted(cur, prev_row, next_row, tok0, n_ctx, t_total):
    row = lax.broadcasted_iota(jnp.int32, (TOK, 1), 0)
    tpos = row + tok0
    down = jnp.where(row == 0, prev_row, pltpu.roll(cur, 1, 0))
    down = jnp.where(jnp.logical_or(tpos == 0, tpos == n_ctx), 0.0, down)
    up = jnp.where(row == TOK - 1, next_row, pltpu.roll(cur, TOK - 1, 0))
    up = jnp.where(jnp.logical_or(tpos == n_ctx - 1, tpos == t_total - 1), 0.0, up)
    return down, up


def _conv_fwd(z, conv_w, conv_b, n_ctx):
    t = z.shape[0]

    def body(v_ref, vp_ref, vn_ref, c_ref, cp_ref, cn_ref, b_ref, w_ref, cb_ref, o_ref):
        tok0 = pl.program_id(0) * TOK
        zz = v_ref[...] * c_ref[...]
        zz_m1, zz_p1 = _shifted(zz, vp_ref[7:8, :] * cp_ref[7:8, :], vn_ref[0:1, :] * cn_ref[0:1, :], tok0, n_ctx, t)
        s = cb_ref[...] + zz_m1 * w_ref[0:1, :] + zz * w_ref[1:2, :] + zz_p1 * w_ref[2:3, :]
        o_ref[...] = b_ref[...] * s

    return pl.pallas_call(
        body, name="conv_fwd", grid=(t // TOK,),
        in_specs=_halo_specs(OFF["conv_v"] // 512, t) + _halo_specs(OFF["conv_c"] // 512, t)
        + [_row_spec(512, OFF["conv_b"] // 512), _const_spec((8, 512)), _const_spec((1, 512))],
        out_specs=_row_spec(512), out_shape=jax.ShapeDtypeStruct((t, 512), F32),
        compiler_params=_cparams(("parallel",)),
    )(z, z, z, z, z, z, z, conv_w, conv_b)


def _conv_bwd(z, conv_w, conv_b, dy, n_ctx):
    t = z.shape[0]

    def body(v_ref, vp_ref, vn_ref, c_ref, cp_ref, cn_ref, b_ref, bp_ref, bn_ref, dy_ref, dyp_ref, dyn_ref,
             w_ref, cb_ref, dv_ref, db_ref, dc_ref, dw_ref, dcb_ref):
        i = pl.program_id(0)
        tok0 = i * TOK
        v, c, b, dy_v = v_ref[...], c_ref[...], b_ref[...], dy_ref[...]
        w0, w1, w2 = w_ref[0:1, :], w_ref[1:2, :], w_ref[2:3, :]
        zz = v * c
        zz_m1, zz_p1 = _shifted(zz, vp_ref[7:8, :] * cp_ref[7:8, :], vn_ref[0:1, :] * cn_ref[0:1, :], tok0, n_ctx, t)
        s = cb_ref[...] + zz_m1 * w0 + zz * w1 + zz_p1 * w2
        ds = dy_v * b
        ds_m1, ds_p1 = _shifted(ds, dyp_ref[7:8, :] * bp_ref[7:8, :], dyn_ref[0:1, :] * bn_ref[0:1, :], tok0, n_ctx, t)
        dzz = ds_p1 * w0 + ds * w1 + ds_m1 * w2
        db_ref[...] = (dy_v * s).astype(BF16)
        dv_ref[...] = (dzz * c).astype(BF16)
        dc_ref[...] = (dzz * v).astype(BF16)

        @pl.when(i == 0)
        def _():
            dw_ref[...] = jnp.zeros_like(dw_ref)
            dcb_ref[...] = jnp.zeros_like(dcb_ref)

        rsum = lambda a: jnp.sum(a, axis=0, keepdims=True)
        dw_ref[0:1, :] += rsum(ds * zz_m1)
        dw_ref[1:2, :] += rsum(ds * zz)
        dw_ref[2:3, :] += rsum(ds * zz_p1)
        dcb_ref[...] += rsum(ds)

    return pl.pallas_call(
        body, name="conv_bwd", grid=(t // TOK,),
        in_specs=_halo_specs(OFF["conv_v"] // 512, t) + _halo_specs(OFF["conv_c"] // 512, t)
        + _halo_specs(OFF["conv_b"] // 512, t) + _halo_specs(0, t) + [_const_spec((8, 512)), _const_spec((1, 512))],
        out_specs=[_row_spec(512)] * 3 + [_const_spec((8, 512)), _const_spec((1, 512))],
        out_shape=tuple([jax.ShapeDtypeStruct((t, 512), BF16)] * 3
                        + [jax.ShapeDtypeStruct((8, 512), F32), jax.ShapeDtypeStruct((1, 512), F32)]),
        compiler_params=_cparams(("arbitrary",)),
    )(z, z, z, z, z, z, z, z, z, dy, dy, dy, conv_w, conv_b)


def _merge_col_specs(tile):
    specs = []
    for off in MERGE_OFFS:
        specs.append(pl.BlockSpec((tile, 512), functools.partial(lambda i, cb: (i, cb), cb=off // 512)))
        specs.append(pl.BlockSpec((tile, 512), functools.partial(lambda i, cb: (i, cb), cb=off // 512 + 1)))
    return specs


def _merge_fwd(xt, ys, z, mod4, w_br, w_out, n_ctx):
    t = xt.shape[0]
    nct = n_ctx // TOK

    def body(x_ref, *refs):
        y_refs, gt_refs, mg_refs = refs[0:4], refs[4:8], refs[8:16]
        gate_ref, wbr_ref, wout_ref, o_ref = refs[16:20]
        acc_lo = acc_hi = None
        for k in range(4):
            gated = y_refs[k][...] * _silu(gt_refs[k][...])
            proj = mm(gated, wbr_ref[k])
            lo = jax.nn.sigmoid(mg_refs[2 * k][...]) * proj[:, :512]
            hi = jax.nn.sigmoid(mg_refs[2 * k + 1][...]) * proj[:, 512:]
            acc_lo = lo if acc_lo is None else acc_lo + lo
            acc_hi = hi if acc_hi is None else acc_hi + hi
        acc = jnp.concatenate([acc_lo, acc_hi], axis=1)
        o_ref[...] = x_ref[...] + gate_ref[...] * mm(acc, wout_ref[...])

    gate_specs = [pl.BlockSpec((TOK, 512), functools.partial(lambda i, cb: (i, cb), cb=o // 512)) for o in GATE_OFFS]
    return pl.pallas_call(
        body, name="merge_fwd", grid=(t // TOK,),
        in_specs=[_row_spec(D_MODEL)] + [_row_spec(512)] * 4 + gate_specs + _merge_col_specs(TOK)
        + [_seg_spec(2, nct), _const_spec((4, 512, 1024)), _const_spec((1024, 1024))],
        out_specs=_row_spec(D_MODEL), out_shape=jax.ShapeDtypeStruct((t, D_MODEL), F32),
        compiler_params=_cparams(("parallel",)),
    )(xt, *ys, z, z, z, z, z, z, z, z, z, z, z, z, mod4, w_br, w_out)


MERGE_BWD_TILE = 128


def _merge_bwd(g, ys, z, mod4, w_br, w_out, n_ctx):
    t = g.shape[0]
    tile = MERGE_BWD_TILE
    nct = n_ctx // tile
    nsteps = t // tile

    def body(g_ref, *refs):
        y_refs, gt_refs, mg_refs = refs[0:4], refs[4:8], refs[8:16]
        gate_ref, wbr_hbm, wout_hbm = refs[16:19]
        dy_refs, dgt_refs, dmg_refs = refs[19:23], refs[23:27], refs[27:31]
        dgate_ref, dwbr_hbm, dwout_hbm = refs[31:34]
        wbr_v, wout_v, dwbr_acc, dwout_acc = refs[34:38]
        i = pl.program_id(0)

        @pl.when(i == 0)
        def _():
            pltpu.sync_copy(wbr_hbm, wbr_v)
            pltpu.sync_copy(wout_hbm, wout_v)
            dwbr_acc[...] = jnp.zeros_like(dwbr_acc)
            dwout_acc[...] = jnp.zeros_like(dwout_acc)

        g_v, gate = g_ref[...], gate_ref[...]
        gated, proj, sig = [], [], []
        acc = None
        for k in range(4):
            gated.append(y_refs[k][...] * _silu(gt_refs[k][...]))
            proj.append(mm(gated[k], wbr_v[k]))
            sig.append(jax.nn.sigmoid(jnp.concatenate([mg_refs[2 * k][...], mg_refs[2 * k + 1][...]], axis=1)))
            contrib = sig[k] * proj[k]
            acc = contrib if acc is None else acc + contrib
        o = mm(acc, wout_v[...])
        dgate = jnp.sum(g_v * o, axis=0, keepdims=True)
        first = jnp.logical_or(i == 0, i == nct)

        @pl.when(first)
        def _():
            dgate_ref[...] = dgate

        @pl.when(jnp.logical_not(first))
        def _():
            dgate_ref[...] += dgate

        do = g_v * gate
        dwout_acc[...] += mm_tn(acc, do)
        dacc = mm_nt(do, wout_v[...])
        for k in range(4):
            dmg_refs[k][...] = (dacc * proj[k] * sig[k] * (1.0 - sig[k])).astype(BF16)
            dproj = dacc * sig[k]
            dwbr_acc[k] += mm_tn(gated[k], dproj)
            dgated = mm_nt(dproj, wbr_v[k])
            gt = gt_refs[k][...]
            dy_refs[k][...] = dgated * _silu(gt)
            dgt_refs[k][...] = (dgated * y_refs[k][...] * _dsilu(gt)).astype(BF16)

        @pl.when(i == nsteps - 1)
        def _():
            wbr_v[...] = dwbr_acc[...].astype(BF16)
            wout_v[...] = dwout_acc[...].astype(BF16)
            pltpu.sync_copy(wbr_v, dwbr_hbm)
            pltpu.sync_copy(wout_v, dwout_hbm)

    row = lambda w: _row_spec(w, 0, tile)
    gate_specs = [pl.BlockSpec((tile, 512), functools.partial(lambda i, cb: (i, cb), cb=o // 512)) for o in GATE_OFFS]
    anyspec = pl.BlockSpec(memory_space=pl.ANY)
    seg = pl.BlockSpec((None, None, 1, D_MODEL), lambda i: (jnp.where(i < nct, 0, 1), 2, 0, 0))
    seg_out = pl.BlockSpec((None, None, 1, D_MODEL), lambda i: (jnp.where(i < nct, 0, 1), 0, 0, 0))
    res = pl.pallas_call(
        body, name="merge_bwd", grid=(nsteps,),
        in_specs=[row(D_MODEL)] + [row(512)] * 4 + gate_specs + _merge_col_specs(tile) + [seg, anyspec, anyspec],
        out_specs=[row(512)] * 8 + [row(1024)] * 4 + [seg_out, anyspec, anyspec],
        out_shape=tuple([jax.ShapeDtypeStruct((t, 512), F32)] * 4 + [jax.ShapeDtypeStruct((t, 512), BF16)] * 4
                        + [jax.ShapeDtypeStruct((t, 1024), BF16)] * 4
                        + [jax.ShapeDtypeStruct((2, 1, 1, D_MODEL), F32),
                           jax.ShapeDtypeStruct((4, 512, 1024), BF16), jax.ShapeDtypeStruct((1024, 1024), BF16)]),
        scratch_shapes=[pltpu.VMEM((4, 512, 1024), BF16), pltpu.VMEM((1024, 1024), BF16),
                        pltpu.VMEM((4, 512, 1024), F32), pltpu.VMEM((1024, 1024), F32)],
        compiler_params=_cparams(("arbitrary",)),
    )(g, *ys, z, z, z, z, z, z, z, z, z, z, z, z, mod4, w_br, w_out)
    return res[0:4], res[4:8], res[8:12], res[12], res[13], res[14]


FWD_ROWS = slice(0, S5_GROUPS)
BWD_ROWS = slice(S5_GROUPS, 2 * S5_GROUPS)


def _backward_chunk(j, k, n_ctx_chunks):
    return jnp.where(j < n_ctx_chunks, n_ctx_chunks - 1 - j, k - 1 - (j - n_ctx_chunks))


def _scan_call(body, name, n_hbm_in, out_shape, kp):
    hbm, vmem = pl.BlockSpec(memory_space=pl.ANY), pl.BlockSpec(memory_space=pltpu.VMEM)
    return pl.pallas_call(
        body, name=name, in_specs=[hbm] * n_hbm_in + [vmem, vmem],
        out_specs=[hbm] + [vmem] * (len(out_shape) - 1), out_shape=out_shape,
        scratch_shapes=[pltpu.VMEM((kp, 64, 128), F32), pltpu.VMEM((kp, 64, 128), F32)],
        compiler_params=pltpu.CompilerParams(vmem_limit_bytes=VMEM_LIMIT))


def _complex_step(a1, a2, h):
    return a1 * h + a2 * pltpu.roll(h, 64, 1)


def _s5_scan_fwd(s, a1, a2, k, n_ctx_chunks):
    kp = s.shape[0]

    def body(s_hbm, a1_ref, a2_ref, hp_hbm, s_v, hp_v):
        pltpu.sync_copy(s_hbm, s_v)
        if kp > k:
            hp_v[k:kp] = jnp.zeros((kp - k, 64, 128), F32)
        a1f, a2f, a1b, a2b = a1_ref[FWD_ROWS, :], a2_ref[FWD_ROWS, :], a1_ref[BWD_ROWS, :], a2_ref[BWD_ROWS, :]

        def step(j, carry):
            hf, hb = carry
            cb = _backward_chunk(j, k, n_ctx_chunks)
            hp_v[j, FWD_ROWS, :] = hf
            hp_v[cb, BWD_ROWS, :] = hb
            return (_complex_step(a1f, a2f, hf) + s_v[j, FWD_ROWS, :],
                    _complex_step(a1b, a2b, hb) + s_v[cb, BWD_ROWS, :])

        zero = jnp.zeros((S5_GROUPS, 128), F32)
        lax.fori_loop(0, k, step, (zero, zero))
        pltpu.sync_copy(hp_v, hp_hbm)

    return _scan_call(body, "s5_scan_fwd", 1, (jax.ShapeDtypeStruct(s.shape, F32),), kp)(s, a1, a2)[0]


def _s5_scan_bwd(dhp, hp, a1, a2, k, n_ctx_chunks):
    kp = hp.shape[0]

    def body(dhp_hbm, hp_hbm, a1_ref, a2_ref, ds_hbm, da1_ref, da2_ref, g_v, hp_v):
        pltpu.sync_copy(dhp_hbm, g_v)
        pltpu.sync_copy(hp_hbm, hp_v)
        if kp > k:
            g_v[k:kp] = jnp.zeros((kp - k, 64, 128), F32)
        coef_f = (a1_ref[FWD_ROWS, :], a2_ref[FWD_ROWS, :])
        coef_b = (a1_ref[BWD_ROWS, :], a2_ref[BWD_ROWS, :])

        def one(rows, c, lam, d1, d2):
            a1_v, a2_v = coef_f if rows is FWD_ROWS else coef_b
            dh_in = g_v[c, rows, :]
            g_v[c, rows, :] = lam
            h = hp_v[c, rows, :]
            return (dh_in + a1_v * lam + pltpu.roll(a2_v * lam, 64, 1),
                    d1 + lam * h, d2 + lam * pltpu.roll(h, 64, 1))

        def step(j, carry):
            f, b = carry
            jj = k - 1 - j
            return one(FWD_ROWS, jj, *f), one(BWD_ROWS, _backward_chunk(jj, k, n_ctx_chunks), *b)

        zero = jnp.zeros((S5_GROUPS, 128), F32)
        f, b = lax.fori_loop(0, k, step, ((zero, zero, zero), (zero, zero, zero)))
        da1_ref[FWD_ROWS, :], da2_ref[FWD_ROWS, :] = f[1], f[2]
        da1_ref[BWD_ROWS, :], da2_ref[BWD_ROWS, :] = b[1], b[2]
        pltpu.sync_copy(g_v, ds_hbm)

    shapes = (jax.ShapeDtypeStruct(hp.shape, F32), jax.ShapeDtypeStruct((64, 128), F32),
              jax.ShapeDtypeStruct((64, 128), F32))
    return _scan_call(body, "s5_scan_bwd", 2, shapes, kp)(dhp, hp, a1, a2)


WIN_GEOM = {"na": (ATT_BLK, 3, 0), "gqa": (128, 4, 128)}


def _win_rows(mode):
    kb, nw, _ = WIN_GEOM[mode]
    return kb * nw


@functools.lru_cache(maxsize=None)
def _window_patterns(mode, n_lat):
    nb = n_lat // ATT_BLK
    assert nb >= 3
    first_key = WIN_GEOM[mode][2] - ATT_BLK
    iq, ik = np.arange(ATT_BLK)[:, None], np.arange(_win_rows(mode))[None, :]

    def valid(ql):
        tq, ts = ATT_BLK * ql + iq, ATT_BLK * ql + first_key + ik
        if mode == "na":
            r, qcol, kr, kcol = tq // GRID_W, tq % GRID_W, ts // GRID_W, ts % GRID_W
            rs = np.clip(r - NA_ROWS // 2, 0, n_lat // GRID_W - NA_ROWS)
            cs = np.clip(qcol - NA_COLS // 2, 0, GRID_W - NA_COLS)
            return (kr >= rs) & (kr < rs + NA_ROWS) & (kcol >= cs) & (kcol < cs + NA_COLS)
        return (np.abs(tq - ts) <= WINDOW) & (ts >= 0) & (ts < n_lat)

    interior = valid(1)
    assert all(np.array_equal(valid(ql), interior) for ql in range(1, nb - 1))
    return np.stack([valid(0), interior, valid(nb - 1), np.zeros_like(interior)])


def _pattern_of_block(qb, nqb):
    return jnp.where(qb == 0, 3, jnp.where(qb == 1, 0, jnp.where(qb == nqb - 1, 2, 1)))


def _attn_block(q, k3, v3, kc, vc, bias0, bias1, sink):
    lane = lax.broadcasted_iota(jnp.int32, (1, 128), 1)
    scale = HEAD_DIM ** -0.5
    outs = []
    for e, bias in enumerate((bias0, bias1)):
        in_head = (lane < 64) if e == 0 else (lane >= 64)
        qe = jnp.where(in_head, q, 0.0)
        s_lat = mm_nt(qe, k3) * scale + bias
        s_ctx = mm_nt(qe, kc) * scale
        mx = jnp.maximum(jnp.max(s_lat, axis=1, keepdims=True), jnp.max(s_ctx, axis=1, keepdims=True))
        if sink is not None:
            srow = lax.broadcasted_iota(jnp.int32, sink.shape, 0)
            sv = jnp.sum(jnp.where(srow == e, sink, 0.0), keepdims=True) * (1.0 / 128.0)
            mx = jnp.maximum(mx, sv)
        mx = lax.stop_gradient(mx)
        e_lat = jnp.exp(s_lat - mx)
        e_ctx = jnp.exp(s_ctx - mx)
        den = jnp.sum(e_lat, axis=1, keepdims=True) + jnp.sum(e_ctx, axis=1, keepdims=True)
        if sink is not None:
            den = den + jnp.exp(sv - mx)
        inv = 1.0 / den
        outs.append(mm(e_lat * inv, v3) + mm(e_ctx * inv, vc))
    return jnp.where(lane < 64, outs[0], outs[1])


def _attn_specs(n_ctx, nqb, per_head, mode):
    kb, nw, skip = WIN_GEOM[mode]

    def kwin(s):
        return pl.BlockSpec(
            (kb, 128), lambda hp, qb: (jnp.maximum(qb - 1, 0) * (ATT_BLK // kb) + skip // kb + s, hp))

    q = pl.BlockSpec((ATT_BLK, 128), lambda hp, qb: (qb, hp))
    ctx = pl.BlockSpec((n_ctx, 128), lambda hp, qb: (0, hp))
    if per_head:
        bias = pl.BlockSpec((None, None, 2, ATT_BLK, kb * nw),
                            lambda hp, qb: (_pattern_of_block(qb, nqb), hp, 0, 0, 0))
    else:
        bias = pl.BlockSpec((None, ATT_BLK, kb * nw), lambda hp, qb: (_pattern_of_block(qb, nqb), 0, 0))
    sink = pl.BlockSpec((None, 8, 128), lambda hp, qb: (hp, 0, 0))
    return q, [kwin(s) for s in range(nw)], ctx, bias, sink


def _attn_fwd(q, kpad, vpad, kc, vc, bias, sink, *, mode, n_ctx):
    t = q.shape[0]
    per_head = bias.ndim == 5
    qs, kws, ctx, bias_s, sink_s = _attn_specs(n_ctx, t // ATT_BLK, per_head, mode)
    has_sink = sink is not None
    nw = len(kws)

    def body(*refs):
        q_ref, k_refs, v_refs = refs[0], refs[1:1 + nw], refs[1 + nw:1 + 2 * nw]
        kc_ref, vc_ref, b_ref = refs[1 + 2 * nw:4 + 2 * nw]
        s_ref = refs[4 + 2 * nw] if has_sink else None
        o_ref = refs[-1]
        k3 = jnp.concatenate([r[...] for r in k_refs], axis=0)
        v3 = jnp.concatenate([r[...] for r in v_refs], axis=0)
        b0, b1 = (b_ref[0], b_ref[1]) if per_head else (b_ref[...], b_ref[...])
        o_ref[...] = _attn_block(q_ref[...], k3, v3, kc_ref[...], vc_ref[...], b0, b1,
                                 s_ref[...] if has_sink else None)

    in_specs = [qs] + kws + kws + [ctx, ctx, bias_s] + ([sink_s] if has_sink else [])
    args = [q] + [kpad] * nw + [vpad] * nw + [kc, vc, bias] + ([sink] if has_sink else [])
    return pl.pallas_call(
        body, name=mode + "_attn_fwd", grid=(4, t // ATT_BLK),
        in_specs=in_specs, out_specs=qs, out_shape=jax.ShapeDtypeStruct((t, 512), F32),
        compiler_params=_cparams(("parallel", "parallel")),
    )(*args)


def _attn_bwd(q, kpad, vpad, kc, vc, bias, sink, do, *, mode, n_ctx):
    t = q.shape[0]
    nqb = t // ATT_BLK
    per_head = bias.ndim == 5
    qs, kws, ctx, bias_s, sink_s = _attn_specs(n_ctx, nqb, per_head, mode)
    has_sink = sink is not None
    nw = len(kws)
    n_in = 5 + 2 * nw + has_sink

    def body(*refs):
        q_ref, k_refs, v_refs = refs[0], refs[1:1 + nw], refs[1 + nw:1 + 2 * nw]
        kc_ref, vc_ref, b_ref = refs[1 + 2 * nw:4 + 2 * nw]
        s_ref = refs[4 + 2 * nw] if has_sink else None
        do_ref = refs[n_in - 1]
        outs = list(refs[n_in:-2])
        dk_acc, dv_acc = refs[-2:]
        dq_ref, dkp_ref, dvp_ref, dkc_ref, dvc_ref = outs[:5]
        hp = pl.program_id(0)
        outs = outs[5:]
        db_ref = outs.pop(0) if per_head else None
        ds_ref = outs.pop(0) if has_sink else None
        qb = pl.program_id(1)
        up = lambda r: r[...].astype(F32)
        k3 = jnp.concatenate([up(r) for r in k_refs], axis=0)
        v3 = jnp.concatenate([up(r) for r in v_refs], axis=0)
        prim = [up(q_ref), k3, v3, up(kc_ref), up(vc_ref)]
        if per_head:
            prim += [b_ref[0], b_ref[1]]
        if has_sink:
            prim += [s_ref[...]]

        def fn(*a):
            a = list(a)
            qv, k3v, v3v, kcv, vcv = a[:5]
            a = a[5:]
            b0 = a.pop(0) if per_head else b_ref[...]
            b1 = a.pop(0) if per_head else b0
            sk = a.pop(0) if has_sink else None
            return _attn_block(qv, k3v, v3v, kcv, vcv, b0, b1, sk)

        _, vjp = jax.vjp(fn, *prim)
        grads = list(vjp(do_ref[...]))
        dq_ref[...] = grads[0]

        @pl.when(qb == 0)
        def _():
            dk_acc[...] = jnp.zeros_like(dk_acc)
            dv_acc[...] = jnp.zeros_like(dv_acc)
            dkc_ref[...] = jnp.zeros_like(dkc_ref)
            dvc_ref[...] = jnp.zeros_like(dvc_ref)
            if has_sink:
                ds_ref[...] = jnp.zeros_like(ds_ref)

        window = pl.ds(pl.multiple_of(jnp.maximum(qb - 1, 0) * ATT_BLK + WIN_GEOM[mode][2], 128), _win_rows(mode))
        dk_acc[window, :] += grads[1]
        dv_acc[window, :] += grads[2]

        @pl.when(qb == nqb - 1)
        def _():
            cols = pl.ds(pl.multiple_of(hp * 128, 128), 128)
            pltpu.sync_copy(dk_acc, dkp_ref.at[:, cols])
            pltpu.sync_copy(dv_acc, dvp_ref.at[:, cols])

        dkc_ref[...] += grads[3]
        dvc_ref[...] += grads[4]
        rest_g = grads[5:]
        if per_head:
            opens = (qb <= 2) | (qb == nqb - 1)
            g0, g1 = rest_g.pop(0), rest_g.pop(0)

            @pl.when(opens)
            def _():
                db_ref[0] = g0
                db_ref[1] = g1

            @pl.when(jnp.logical_not(opens))
            def _():
                db_ref[0] += g0
                db_ref[1] += g1

        if has_sink:
            ds_ref[...] += rest_g.pop(0)

    hbm = pl.BlockSpec(memory_space=pl.ANY)
    in_specs = [qs] + kws + kws + [ctx, ctx, bias_s] + ([sink_s] if has_sink else []) + [qs]
    args = [q] + [kpad] * nw + [vpad] * nw + [kc, vc, bias] + ([sink] if has_sink else []) + [do]
    out_specs = [qs, hbm, hbm, ctx, ctx] + ([bias_s] if per_head else []) + ([sink_s] if has_sink else [])
    out_shape = [jax.ShapeDtypeStruct((t, 512), F32),
                 jax.ShapeDtypeStruct(kpad.shape, F32), jax.ShapeDtypeStruct(kpad.shape, F32),
                 jax.ShapeDtypeStruct((n_ctx, 512), F32), jax.ShapeDtypeStruct((n_ctx, 512), F32)]
    if per_head:
        out_shape.append(jax.ShapeDtypeStruct(bias.shape, F32))
    if has_sink:
        out_shape.append(jax.ShapeDtypeStruct((4, 8, 128), F32))
    res = list(pl.pallas_call(
        body, name=mode + "_attn_bwd", grid=(4, nqb),
        in_specs=in_specs, out_specs=out_specs, out_shape=tuple(out_shape),
        scratch_shapes=[pltpu.VMEM((kpad.shape[0], 128), F32), pltpu.VMEM((kpad.shape[0], 128), F32)],
        compiler_params=_cparams(("arbitrary", "arbitrary")),
    )(*args))
    dq, dkp, dvp, dkc, dvc = res[:5]
    res = res[5:]
    dbias = res.pop(0) if per_head else None
    dsink = res.pop(0) if has_sink else None
    return dq, dkp, dvp, dkc, dvc, dbias, dsink


def _loss_head(xt, target, n_ctx):
    t = xt.shape[0]
    nct = n_ctx // TOK

    def body(x_ref, t_ref, l_ref, d_ref):
        i = pl.program_id(0)

        @pl.when(i == 0)
        def _():
            l_ref[...] = jnp.zeros_like(l_ref)

        @pl.when(i < nct)
        def _():
            d_ref[...] = jnp.zeros_like(d_ref)

        @pl.when(i >= nct)
        def _():
            err = x_ref[...] - t_ref[...]
            d_ref[...] = err * (1.0 / D_MODEL)
            l_ref[...] += jnp.sum(err * err, keepdims=True) * (0.5 / D_MODEL)

    return pl.pallas_call(
        body, name="loss_head", grid=(t // TOK,),
        in_specs=[_row_spec(D_MODEL), pl.BlockSpec((TOK, D_MODEL), lambda i: (jnp.maximum(i - nct, 0), 0))],
        out_specs=[_const_spec((8, 128)), _row_spec(D_MODEL)],
        out_shape=(jax.ShapeDtypeStruct((8, 128), F32), jax.ShapeDtypeStruct((t, D_MODEL), F32)),
        compiler_params=_cparams(("arbitrary",)),
    )(xt, target)


PACK_W = 1024
SUM_STEPS = 8


def _sum_chips(recvs):
    def split(a):
        rows = a.shape[1]
        if rows % (8 * SUM_STEPS):
            return None
        return rows // SUM_STEPS

    def body(*refs):
        n = len(refs) // 2
        for r_ref, o_ref in zip(refs[:n], refs[n:]):
            up = lambda s: r_ref[s].astype(F32)
            o_ref[...] = ((up(0) + up(1)) + up(2)) + up(3)

    in_specs, out_specs = [], []
    for a in recvs:
        rb, tail = split(a), a.shape[2:]
        zeros = (0,) * len(tail)
        if rb is None:
            in_specs.append(pl.BlockSpec(a.shape, functools.partial(lambda i, z: (0, 0) + z, z=zeros)))
            out_specs.append(pl.BlockSpec(a.shape[1:], functools.partial(lambda i, z: (0,) + z, z=zeros)))
        else:
            in_specs.append(pl.BlockSpec((4, rb) + tail, functools.partial(lambda i, z: (0, i) + z, z=zeros)))
            out_specs.append(pl.BlockSpec((rb,) + tail, functools.partial(lambda i, z: (i,) + z, z=zeros)))
    return pl.pallas_call(
        body, name="sum_chips", grid=(SUM_STEPS,),
        in_specs=in_specs, out_specs=out_specs,
        out_shape=tuple(jax.ShapeDtypeStruct(a.shape[1:], F32) for a in recvs),
        compiler_params=_cparams(("arbitrary",)),
    )(*recvs)


ADAM_BLOCK_BYTES = 1 << 20


def _adamw(p_a, p_b, w, m, v, name):
    layers, rows, cols = w.shape
    tr = rows
    while tr % 16 == 0 and tr * cols * 4 > ADAM_BLOCK_BYTES:
        tr //= 2
    c1 = 1.0 / (1.0 - ADAM_B1 ** ADAM_STEP)
    c2 = 1.0 / (1.0 - ADAM_B2 ** ADAM_STEP)

    def body(a_ref, b_ref, w_ref, m_ref, v_ref, g_ref, d_ref, nm_ref, nv_ref):
        g = a_ref[...] + b_ref[...]
        nm = ADAM_B1 * m_ref[...] + (1.0 - ADAM_B1) * g
        nv = ADAM_B2 * v_ref[...] + (1.0 - ADAM_B2) * (g * g)
        g_ref[...] = g
        nm_ref[...] = nm
        nv_ref[...] = nv
        d_ref[...] = -ADAM_LR * ((nm * c1) / (jnp.sqrt(nv * c2) + ADAM_EPS) + ADAM_WD * w_ref[...])

    spec = pl.BlockSpec((None, tr, cols), lambda l, i: (l, i, 0))
    return pl.pallas_call(
        body, name=name, grid=(layers, rows // tr),
        in_specs=[spec] * 5, out_specs=[spec] * 4,
        out_shape=tuple(jax.ShapeDtypeStruct(w.shape, F32) for _ in range(4)),
        compiler_params=_cparams(("parallel", "parallel")),
    )(p_a, p_b, w, m, v)


MESH = pl.DeviceIdType.MESH
ANY_SPEC = pl.BlockSpec(memory_space=pl.ANY)


def _chip_exchange(srcs, out_shapes, src_window, dst_window, name):
    n = len(srcs)

    def body(*refs):
        src_refs, out_refs = refs[:n], refs[n:2 * n]
        send_sems, recv_sems, local_sems = refs[2 * n:]
        x, y, c = lax.axis_index("x"), lax.axis_index("y"), lax.axis_index("c")
        me = 2 * x + y
        peers = [(x, 1 - y), (1 - x, y), (1 - x, 1 - y)]

        def copy(k, j, from_chip, to_chip):
            px, py = peers[j]
            return pltpu.make_async_remote_copy(
                src_ref=src_window(k, src_refs[k], to_chip), dst_ref=dst_window(k, out_refs[k], from_chip),
                send_sem=send_sems.at[3 * k + j], recv_sem=recv_sems.at[3 * k + j],
                device_id=(px, py, c), device_id_type=MESH)

        local = [pltpu.make_async_copy(src_window(k, src_refs[k], me), dst_window(k, out_refs[k], me),
                                       local_sems.at[k]) for k in range(n)]
        for cp in local:
            cp.start()
        sends = [copy(k, j, me, 2 * px + py) for k in range(n) for j, (px, py) in enumerate(peers)]
        for cp in sends:
            cp.start()
        for k in range(n):
            for j, (px, py) in enumerate(peers):
                copy(k, j, 2 * px + py, me).wait_recv()
        for cp in sends:
            cp.wait_send()
        for cp in local:
            cp.wait()

    return pl.pallas_call(
        body, name=name, in_specs=[ANY_SPEC] * n, out_specs=[ANY_SPEC] * n,
        out_shape=tuple(out_shapes),
        scratch_shapes=[pltpu.SemaphoreType.DMA((3 * n,)), pltpu.SemaphoreType.DMA((3 * n,)),
                        pltpu.SemaphoreType.DMA((n,))],
    )(*srcs)


def _core_swap(srcs):
    n = len(srcs)

    def body(*refs):
        src_refs, out_refs, send_sems, recv_sems = refs[:n], refs[n:2 * n], refs[2 * n], refs[2 * n + 1]
        x, y, c = lax.axis_index("x"), lax.axis_index("y"), lax.axis_index("c")
        copies = [pltpu.make_async_remote_copy(
            src_ref=src_refs[k], dst_ref=out_refs[k], send_sem=send_sems.at[k], recv_sem=recv_sems.at[k],
            device_id=(x, y, 1 - c), device_id_type=MESH) for k in range(n)]
        for cp in copies:
            cp.start()
        for cp in copies:
            cp.wait()

    return pl.pallas_call(
        body, name="core_swap", in_specs=[ANY_SPEC] * n, out_specs=[ANY_SPEC] * n,
        out_shape=tuple(jax.ShapeDtypeStruct(s.shape, s.dtype) for s in srcs),
        scratch_shapes=[pltpu.SemaphoreType.DMA((n,)), pltpu.SemaphoreType.DMA((n,))],
    )(*srcs)


def _col_window(ref, start, size):
    idx = (slice(None),) * (len(ref.shape) - 1) + (pl.ds(pl.multiple_of(start, 128), size),)
    return ref.at[idx]


def _row_window(ref, start, size):
    idx = (slice(None),) * (len(ref.shape) - 2) + (pl.ds(pl.multiple_of(start, 8), size), slice(None))
    return ref.at[idx]


N_SHARD_IN = 2624
WIN_W = 2944
WIN_START = (0, 2560, 5248, 7808)
WIN_PIECES = (((0, 2624),), ((64, 2688),), ((0, 640), (896, 2880)), ((320, 2944),))


def _core_layers(ref, lay, core):
    half = lay // 2
    return ref.at[pl.ds(core * half, half)]


def _gather_weights(w_ada, w_in, w_glu, w_br, w_out, conv_w):
    lay = w_ada.shape[0]
    assert lay % 2 == 0
    sizes = (768, None, 128, 256, 256, 128)

    def dst(k, ref, s, core=None):
        core = lax.axis_index("c") if core is None else core
        if k == 1:
            return _core_layers(ref.at[s], lay, core)
        ref = _core_layers(ref, lay, core)
        if k in (2, 4):
            return _row_window(ref, s * sizes[k], sizes[k])
        return _col_window(ref, s * sizes[k], sizes[k])

    shapes = (jax.ShapeDtypeStruct((lay, D_MODEL, 3 * D_MODEL), w_ada.dtype),
              jax.ShapeDtypeStruct((4,) + w_in.shape, w_in.dtype),
              jax.ShapeDtypeStruct((lay, MIX_W, MIX_W), w_glu.dtype),
              jax.ShapeDtypeStruct((lay, 4, MIX_W, D_MODEL), w_br.dtype),
              jax.ShapeDtypeStruct((lay, D_MODEL, D_MODEL), w_out.dtype),
              jax.ShapeDtypeStruct((lay, 8, MIX_W), conv_w.dtype))
    halves = _chip_exchange((w_ada, w_in, w_glu, w_br, w_out, conv_w), shapes,
                            lambda k, ref, t: _core_layers(ref, lay, lax.axis_index("c")), dst, "gather_weights")
    return _merge_core_halves(halves, lay)


def _merge_core_halves(halves, lay):
    n = len(halves)

    def body(*refs):
        out_refs, send_sems, recv_sems = refs[n:2 * n], refs[2 * n], refs[2 * n + 1]
        x, y, c = lax.axis_index("x"), lax.axis_index("y"), lax.axis_index("c")

        def part(k, core):
            if k == 1:
                half = lay // 2
                return out_refs[k].at[:, pl.ds(core * half, half)]
            return _core_layers(out_refs[k], lay, core)

        def copy(k, landing_core):
            return pltpu.make_async_remote_copy(
                src_ref=part(k, c), dst_ref=part(k, landing_core), send_sem=send_sems.at[k],
                recv_sem=recv_sems.at[k], device_id=(x, y, 1 - c), device_id_type=MESH)

        give = [copy(k, c) for k in range(n)]
        for cp in give:
            cp.start()
        for k in range(n):
            copy(k, 1 - c).wait_recv()
        for cp in give:
            cp.wait_send()

    return pl.pallas_call(
        body, name="gather_merge", in_specs=[ANY_SPEC] * n, out_specs=[ANY_SPEC] * n,
        out_shape=tuple(jax.ShapeDtypeStruct(h.shape, h.dtype) for h in halves),
        input_output_aliases={k: k for k in range(n)},
        scratch_shapes=[pltpu.SemaphoreType.DMA((n,)), pltpu.SemaphoreType.DMA((n,))],
    )(*halves)


def _scatter_grads(dw_ada, dw_in, dw_glu, dw_br, dw_out, dconv_w, small):
    def src(k, ref, t):
        if k == 0:
            return _col_window(ref, t * 768, 768)
        if k == 1:
            start = jnp.where(t == 0, WIN_START[0], jnp.where(t == 1, WIN_START[1],
                              jnp.where(t == 2, WIN_START[2], WIN_START[3])))
            return _col_window(ref, start, WIN_W)
        if k == 2:
            return _row_window(ref, t * 128, 128)
        if k == 3:
            return _col_window(ref, t * 256, 256)
        if k == 4:
            return _row_window(ref, t * 256, 256)
        if k == 5:
            return _col_window(ref, t * 128, 128)
        return ref

    pieces = ((D_MODEL, 768), (D_MODEL, WIN_W), (128, MIX_W), (4, MIX_W, 256), (256, D_MODEL), (8, 128), small.shape)
    srcs = (dw_ada, dw_in, dw_glu, dw_br, dw_out, dconv_w, small)
    shapes = tuple(jax.ShapeDtypeStruct((4,) + p, s.dtype) for p, s in zip(pieces, srcs))
    return _chip_exchange(srcs, shapes,
                          src, lambda k, ref, s: ref.at[s], "scatter_grads")


def _s5_tables(a_re, a_im, log_dt, b_re, b_im, c_re, c_im):
    ln = S5_CHUNK
    hi = lax.Precision.HIGHEST
    dt = jnp.exp(log_dt)[..., None]
    mag = jnp.exp(dt * a_re)
    abr = mag * jnp.cos(dt * a_im)
    abi = mag * jnp.sin(dt * a_im)
    den = a_re * a_re + a_im * a_im
    fr = ((abr - 1.0) * a_re + abi * a_im) / den
    fi = (abi * a_re - (abr - 1.0) * a_im) / den
    bbr = fr[..., None] * b_re - fi[..., None] * b_im
    bbi = fr[..., None] * b_im + fi[..., None] * b_re
    n = jnp.arange(ln + 1, dtype=F32)[:, None, None, None]
    pm = jnp.exp(n * dt * a_re)
    er = pm * jnp.cos(n * dt * a_im)
    ei = pm * jnp.sin(n * dt * a_im)
    e3 = lambda e, b, c: jnp.einsum("tdgp,dgpa,dgbp->dgabt", e, b, c, precision=hi)
    gt = e3(er[:ln], bbr, c_re) - e3(er[:ln], bbi, c_im) - e3(ei[:ln], bbr, c_im) - e3(ei[:ln], bbi, c_re)
    by_dir = lambda fwd, bwd: jnp.stack([fwd[:, 0], bwd[:, 1]], axis=1)
    erj, eij = by_dir(er[:ln][::-1], er[:ln]), by_dir(ei[:ln][::-1], ei[:ln])
    e2 = lambda e, b: jnp.einsum("jdgp,dgpa->dgajp", e, b, precision=hi)
    w = jnp.concatenate([e2(erj, bbr) - e2(eij, bbi), e2(erj, bbi) + e2(eij, bbr)], axis=-1)
    er1, ei1 = by_dir(er[1:], er[1:][::-1]), by_dir(ei[1:], ei[1:][::-1])
    ev = lambda c, e: jnp.einsum("dgbp,idgp->dgpbi", c, e, precision=hi)
    v = jnp.concatenate([ev(c_re, er1) - ev(c_im, ei1), -(ev(c_re, ei1) + ev(c_im, er1))], axis=2)
    a1 = jnp.concatenate([er[ln], er[ln]], axis=-1)
    a2 = jnp.concatenate([-ei[ln], ei[ln]], axis=-1)
    return (gt.transpose(1, 2, 3, 0, 4).reshape(S5_GROUPS, 256, 2 * ln),
            w.transpose(1, 2, 3, 0, 4).reshape(S5_GROUPS, S5_CH * ln, 256),
            v.transpose(1, 0, 2, 3, 4).reshape(S5_GROUPS, 256, S5_CH * ln),
            a1.reshape(64, 128), a2.reshape(64, 128))


def _lag_onehot():
    ln = S5_CHUNK
    j, i = np.meshgrid(np.arange(ln), np.arange(ln), indexing="ij")
    lag = np.arange(ln)[:, None, None]
    z = np.concatenate([lag == (i - j)[None], lag == (j - i)[None]], axis=0).astype(np.float32)
    return jnp.broadcast_to(jnp.asarray(z.reshape(2 * ln, ln * ln), BF16), (S5_GROUPS, 2 * ln, ln * ln))


def _toeplitz(gt):
    ln = S5_CHUNK
    flat = _matmul(gt, _lag_onehot(), out_dtype=BF16, name="s5_toeplitz")
    return (flat.reshape(S5_GROUPS, S5_CH, S5_CH, ln, ln).transpose(0, 1, 3, 2, 4)
            .reshape(S5_GROUPS, S5_CH * ln, S5_CH * ln))


def _toeplitz_fold(dk):
    ln = S5_CHUNK
    flat = dk.reshape(S5_GROUPS, S5_CH, ln, S5_CH, ln).transpose(0, 1, 3, 2, 4).reshape(S5_GROUPS, 256, ln * ln)
    return _matmul(flat, _lag_onehot(), trans_b=True, name="s5_toeplitz_fold")


def _chunk_rows(t):
    k = t // S5_CHUNK
    return k, -(-k // 128) * 128


def _to_chunks(u):
    k, kp = _chunk_rows(u.shape[0])
    v = u.reshape(k, S5_CHUNK, S5_GROUPS, S5_CH).transpose(2, 0, 3, 1).reshape(S5_GROUPS, k, S5_CH * S5_CHUNK)
    return jnp.pad(v, ((0, 0), (0, kp - k), (0, 0)))


def _from_chunks(y, t):
    k, _ = _chunk_rows(t)
    return y[:, :k].reshape(S5_GROUPS, k, S5_CH, S5_CHUNK).transpose(1, 3, 0, 2).reshape(t, MIX_W)


def _states_to_rows(s):
    kp = s.shape[1]
    return s.reshape(S5_GROUPS, kp, 2, 128).transpose(1, 2, 0, 3).reshape(kp, 64, 128)


def _rows_to_states(h):
    kp = h.shape[0]
    return h.reshape(kp, 2, S5_GROUPS, 128).transpose(2, 0, 1, 3).reshape(S5_GROUPS, kp, 256)


def _na_bias(rel_bias):
    a, m = np.meshgrid(np.arange(4), np.arange(12), indexing="ij")
    di = np.clip(m - a + 3, 0, 2 * NA_ROWS - 2).reshape(-1)
    qc, kc = np.meshgrid(np.arange(GRID_W), np.arange(GRID_W), indexing="ij")
    dj = np.clip(kc - qc + NA_COLS - 1, 0, 2 * NA_COLS - 2).reshape(-1)
    oh_i = jnp.asarray(di[:, None] == np.arange(2 * NA_ROWS - 1)[None, :], F32)
    oh_j = jnp.asarray(dj[:, None] == np.arange(2 * NA_COLS - 1)[None, :], F32)
    hi = lax.Precision.HIGHEST
    cols = jnp.einsum("hij,cj->hic", rel_bias, oh_j, precision=hi)
    full = jnp.einsum("ri,hic->hrc", oh_i, cols, precision=hi)
    full = full.reshape(N_HEADS, 4, 12, GRID_W, GRID_W).transpose(0, 1, 3, 2, 4)
    return full.reshape(4, 2, ATT_BLK, 3 * ATT_BLK)


def _rope_tables(n_ctx, n_lat):
    tok = jnp.arange(n_lat, dtype=jnp.int32)
    row = (tok // GRID_W).astype(F32)
    col = (tok % GRID_W).astype(F32)
    inv = ROPE_BASE ** (-jnp.arange(ROPE_PAIRS, dtype=F32) / ROPE_PAIRS)
    ang = jnp.concatenate([row[:, None] * inv, col[:, None] * inv], axis=-1)
    cos, sin = jnp.cos(ang), jnp.sin(ang)
    cos = jnp.tile(jnp.concatenate([cos, cos], axis=-1), (1, 2))
    sin = jnp.tile(jnp.concatenate([-sin, sin], axis=-1), (1, 2))
    return (jnp.concatenate([jnp.ones((n_ctx, 128), F32), cos], axis=0),
            jnp.concatenate([jnp.zeros((n_ctx, 128), F32), sin], axis=0))


def _pad_blocks(a, n_ctx):
    return jnp.pad(a[n_ctx:], ((ATT_BLK, ATT_BLK), (0, 0)))


def _layer_fwd(xt, cc, w, rope, n_ctx):
    t = xt.shape[0]
    sv = {}
    mod = _adaln_fwd(cc, w["w_ada"], w["b_ada"].reshape(1, -1))
    mod4 = mod[:2].reshape(2, 3, 1, D_MODEL)
    h = _modnorm_fwd(xt, w["norm_g"].reshape(1, -1), mod4, n_ctx)
    z = _matmul(h, w["w_in"], name="proj_fwd")

    gt, tw, tv, a1, a2 = w["s5_tables"]
    ktoe = _toeplitz(gt)
    tw, tv = tw.astype(BF16), tv.astype(BF16)
    uc = _to_chunks(z[:, :MIX_W].astype(BF16))
    st = _matmul(uc, tw, name="s5_chunk_state")
    hprev = _s5_scan_fwd(_states_to_rows(st), a1, a2, t // S5_CHUNK, n_ctx // S5_CHUNK)
    uh = jnp.concatenate([uc, _rows_to_states(hprev).astype(BF16)], axis=2)
    ysum = _from_chunks(_matmul(uh, jnp.concatenate([ktoe, tv], axis=1), name="s5_chunk_out"), t)
    s5_d = w["s5_d"].reshape(1, MIX_W)
    y_s5 = _s5post_fwd(ysum, z, s5_d, w["s5_w_glu"])

    conv_w = w["conv_w"]
    y_conv = _conv_fwd(z, conv_w, w["conv_b"].reshape(1, -1), n_ctx)

    gains = (jnp.tile(w["na_q_g"], 8)[None], jnp.tile(w["na_k_g"], 8)[None],
             jnp.tile(w["gqa_q_g"], 8)[None], jnp.tile(w["gqa_k_g"], 2)[None])
    q_na, k_na, v_na, q_g, k_g, v_g = _prep_fwd(z, gains, rope)
    bias = w["na_bias"]
    sink = jnp.zeros((4, 8, 128), F32).at[:, :2, :].set(
        jnp.broadcast_to(w["gqa_sink"].reshape(4, 2, 1), (4, 2, 128)))
    na_tab = jnp.where(_window_patterns("na", t - n_ctx)[:, None, None], bias[None], NEG_INF)
    gqa_tab = jnp.where(_window_patterns("gqa", t - n_ctx), 0.0, NEG_INF).astype(F32)
    na_in = (q_na, _pad_blocks(k_na, n_ctx), _pad_blocks(v_na, n_ctx), k_na[:n_ctx], v_na[:n_ctx], na_tab, None)
    gqa_in = (q_g, _pad_blocks(k_g, n_ctx), _pad_blocks(v_g, n_ctx), k_g[:n_ctx], v_g[:n_ctx], gqa_tab, sink)
    y_na = _attn_fwd(*na_in, mode="na", n_ctx=n_ctx)
    y_gqa = _attn_fwd(*gqa_in, mode="gqa", n_ctx=n_ctx)
    ys = (y_s5, y_conv, y_na, y_gqa)
    xt_new = _merge_fwd(xt, ys, z, mod4, w["w_br"], w["w_out"], n_ctx)
    sv.update(xt=xt, mod4=mod4, h=h, z=z, ktoe=ktoe, tw=tw, tv=tv, a1=a1, a2=a2, uc=uc,
              hprev=hprev, uh=uh, ysum=ysum, s5_d=s5_d, conv_w=conv_w, gains=gains,
              na_in=na_in, gqa_in=gqa_in, ys=ys)
    return xt_new, sv


def _layer_bwd(dxt_new, sv, cc, w, rope, n_ctx):
    t = dxt_new.shape[0]
    z, mod4 = sv["z"], sv["mod4"]
    dys, dgt, dmg, dgate, dw_br, dw_out = _merge_bwd(dxt_new, sv["ys"], z, mod4, w["w_br"], w["w_out"], n_ctx)

    dpre, du_skip, dd, dw_glu = _s5post_bwd(sv["ysum"], z, sv["s5_d"], w["s5_w_glu"], dys[0])
    dyc = _to_chunks(dpre)
    dhp = _matmul(dyc, sv["tv"], trans_b=True, name="s5_bwd_state")
    ds, da1, da2 = _s5_scan_bwd(_states_to_rows(dhp), sv["hprev"], sv["a1"], sv["a2"],
                                t // S5_CHUNK, n_ctx // S5_CHUNK)
    ds = _rows_to_states(ds).astype(BF16)
    duc = _matmul(jnp.concatenate([dyc, ds], axis=2), jnp.concatenate([sv["ktoe"], sv["tw"]], axis=2),
                  trans_b=True, out_dtype=BF16, name="s5_bwd_u")
    dkv = _matmul(sv["uh"].transpose(0, 2, 1), dyc, out_dtype=BF16, name="s5_bwd_kv")
    dtw = _matmul(sv["uc"].transpose(0, 2, 1), ds, name="s5_bwd_w")
    dgt_tab = _toeplitz_fold(dkv[:, :S5_CH * S5_CHUNK])
    d_tables = (dgt_tab, dtw, dkv[:, S5_CH * S5_CHUNK:].astype(F32), da1, da2)
    du_scan = _from_chunks(duc, t)

    dzv, dzb, dzc, dconv_w, dconv_b = _conv_bwd(z, sv["conv_w"], w["conv_b"].reshape(1, -1), dys[1], n_ctx)

    dq_na, dk_na, dv_na, dkc_na, dvc_na, dbias, _ = _attn_bwd(*sv["na_in"], dys[2], mode="na", n_ctx=n_ctx)
    dq_g, dk_g, dv_g, dkc_g, dvc_g, _, dsink = _attn_bwd(*sv["gqa_in"], dys[3], mode="gqa", n_ctx=n_ctx)
    pb = _prep_bwd(z, sv["gains"], rope, (dq_na, dq_g), (dk_na, dv_na, dk_g, dv_g),
                   (dkc_na, dvc_na, dkc_g, dvc_g), du_skip, du_scan, n_ctx)
    dz_naq, dz_nak, dz_nav, dz_gq, dz_gk, dz_gv, dz_u, dg_naq, dg_nak, dg_gq, dg_gk = pb

    dz = jnp.concatenate([dz_u, dgt[0], dzv, dzb, dzc, dgt[1], dz_naq, dz_nak, dz_nav, dgt[2], dz_gq, dz_gk, dz_gv,
                          jnp.zeros((t, OFF["gqa_gate"] - OFF["pad"]), BF16), dgt[3], *dmg], axis=1)
    dh = _matmul(dz, w["w_in"], trans_b=True, name="proj_bwd_x")
    dw_in = _matmul(sv["h"].T, dz, out_dtype=BF16, name="proj_bwd_w")
    dxt, dnorm_g, dshift, dscale = _modnorm_bwd(sv["xt"], w["norm_g"].reshape(1, -1), mod4, dh, dxt_new, n_ctx)
    dmod = jnp.concatenate([dshift, dscale, dgate], axis=1).reshape(2, 3 * D_MODEL)
    dcc, dw_ada, db_ada = _adaln_bwd(cc, w["w_ada"], jnp.pad(dmod, ((0, 6), (0, 0))))

    grads = dict(
        norm_g=dnorm_g[0], w_ada=dw_ada, b_ada=db_ada[0], w_in=dw_in,
        d_s5_tables=d_tables, d_na_bias=dbias.sum(0), s5_d=dd.reshape(S5_GROUPS, S5_CH), s5_w_glu=dw_glu,
        conv_w=dconv_w, conv_b=dconv_b[0],
        na_q_g=dg_naq.reshape(8, HEAD_DIM).sum(0), na_k_g=dg_nak.reshape(8, HEAD_DIM).sum(0),
        gqa_q_g=dg_gq.reshape(8, HEAD_DIM).sum(0), gqa_k_g=dg_gk.reshape(2, HEAD_DIM).sum(0),
        gqa_sink=dsink[:, :2, :].sum(-1).reshape(8), w_br=dw_br, w_out=dw_out)
    return dxt, dcc, grads


SHARDED = ("w_ada", "w_in", "s5_w_glu", "conv_w", "w_br", "w_out")
REPLICATED = ("norm_g", "b_ada", "s5_a_re", "s5_a_im", "s5_log_dt", "s5_b_re", "s5_b_im", "s5_c_re", "s5_c_im",
              "s5_d", "conv_b", "na_q_g", "na_k_g", "na_rel_bias", "gqa_q_g", "gqa_k_g", "gqa_sink")
WEIGHTS = ("c_ctx", "norm_g", "w_ada", "b_ada", "w_in", "s5_a_re", "s5_a_im", "s5_log_dt", "s5_b_re", "s5_b_im",
           "s5_c_re", "s5_c_im", "s5_d", "s5_w_glu", "conv_w", "conv_b", "na_q_g", "na_k_g", "na_rel_bias",
           "gqa_q_g", "gqa_k_g", "gqa_sink", "w_br", "w_out")


def _pack(pieces, row_multiple, dtype):
    flat = jnp.concatenate([p.reshape(-1).astype(dtype) for p in pieces])
    rows = -(-flat.shape[0] // PACK_W)
    rows = -(-rows // row_multiple) * row_multiple
    return jnp.pad(flat, (0, rows * PACK_W - flat.shape[0])).reshape(rows, PACK_W)


def _unpack(buf, shapes):
    flat = buf.reshape(-1)
    out, pos = [], 0
    for shp in shapes:
        size = int(np.prod(shp))
        out.append(flat[pos:pos + size].reshape(shp))
        pos += size
    return out


def _local_step(x, ctx, target, c_vec, c_ctx, layers):
    depth = len(layers)
    n_ctx, n_lat = ctx.shape[0], x.shape[0]
    cc = jnp.zeros((8, D_MODEL), F32).at[0].set(c_ctx).at[1].set(c_vec)
    rope = _rope_tables(n_ctx, n_lat)
    s5_names = ("s5_a_re", "s5_a_im", "s5_log_dt", "s5_b_re", "s5_b_im", "s5_c_re", "s5_c_im")
    stacked = lambda n: jnp.stack([w[n] for w in layers])
    tables, tab_vjp = jax.vjp(jax.vmap(_s5_tables), *[stacked(n) for n in s5_names])
    biases, bias_vjp = jax.vjp(jax.vmap(_na_bias), stacked("na_rel_bias"))
    layers = [dict(w, s5_tables=tuple(tb[l] for tb in tables), na_bias=biases[l]) for l, w in enumerate(layers)]

    xt = jnp.concatenate([ctx, x], axis=0)
    saved = []
    for l in range(depth):
        xt, sv = _layer_fwd(xt, cc, layers[l], rope, n_ctx)
        saved.append(sv)
    loss_tile, dxt = _loss_head(xt, target, n_ctx)
    grads = [None] * depth
    dc_ctx = jnp.zeros((D_MODEL,), F32)
    for l in reversed(range(depth)):
        dxt, dcc, grads[l] = _layer_bwd(dxt, saved[l], cc, layers[l], rope, n_ctx)
        dc_ctx = dc_ctx + dcc[0]
    d_tables = [g.pop("d_s5_tables") for g in grads]
    s5_grads = tab_vjp(tuple(jnp.stack([d[j] for d in d_tables]) for j in range(5)))
    (drel,) = bias_vjp(jnp.stack([g.pop("d_na_bias") for g in grads]))
    for l, g in enumerate(grads):
        g.update({n: s5_grads[j][l] for j, n in enumerate(s5_names)}, na_rel_bias=drel[l])
    return loss_tile[0, 0], dxt[n_ctx:][None], dc_ctx, grads


def kernel(x, c, ctx, c_ctx, norm_g, w_ada, b_ada, w_in, s5_a_re, s5_a_im, s5_log_dt, s5_b_re, s5_b_im,
           s5_c_re, s5_c_im, s5_d, s5_w_glu, conv_w, conv_b, na_q_g, na_k_g, na_rel_bias, gqa_q_g,
           gqa_k_g, gqa_sink, w_br, w_out, loss_target, m_c_ctx, m_norm_g, m_w_ada, m_b_ada, m_w_in,
           m_s5_a_re, m_s5_a_im, m_s5_log_dt, m_s5_b_re, m_s5_b_im, m_s5_c_re, m_s5_c_im, m_s5_d,
           m_s5_w_glu, m_conv_w, m_conv_b, m_na_q_g, m_na_k_g, m_na_rel_bias, m_gqa_q_g, m_gqa_k_g,
           m_gqa_sink, m_w_br, m_w_out, v_c_ctx, v_norm_g, v_w_ada, v_b_ada, v_w_in, v_s5_a_re,
           v_s5_a_im, v_s5_log_dt, v_s5_b_re, v_s5_b_im, v_s5_c_re, v_s5_c_im, v_s5_d, v_s5_w_glu,
           v_conv_w, v_conv_b, v_na_q_g, v_na_k_g, v_na_rel_bias, v_gqa_q_g, v_gqa_k_g, v_gqa_sink,
           v_w_br, v_w_out):
    a = dict(locals())
    depth = a["norm_g"].shape[0]
    x, ctx, target = a["x"][0], a["ctx"][0], a["loss_target"][0]
    n_ctx, n_lat = ctx.shape[0], x.shape[0]
    assert n_ctx % ATT_BLK == 0 and n_lat % (4 * GRID_W) == 0 and n_lat // GRID_W >= NA_ROWS

    cast = lambda n: a[n].astype(BF16)
    conv8 = jnp.pad(a["conv_w"], ((0, 0), (0, 5), (0, 0)))
    g_ada, g_in, g_glu, g_br, g_out, g_conv = _gather_weights(
        cast("w_ada"), cast("w_in"), cast("s5_w_glu"), cast("w_br"), cast("w_out"), conv8)
    zpad = jnp.zeros((D_MODEL, OFF["gqa_gate"] - OFF["pad"]), BF16)
    split = OFF["pad"] - 2 * N_SHARD_IN
    layers = []
    for l in range(depth):
        w = {n: a[n][l] for n in REPLICATED}
        w.update(w_ada=g_ada[l], s5_w_glu=g_glu[l], w_br=g_br[l], w_out=g_out[l], conv_w=g_conv[l])
        w["w_in"] = jnp.concatenate([g_in[0, l], g_in[1, l], g_in[2, l][:, :split], zpad, g_in[2, l][:, split:],
                                     g_in[3, l]], axis=1)
        layers.append(w)

    loss_local, grad_x, dc_ctx, grads = _local_step(x, ctx, target, a["c"][0], a["c_ctx"], layers)
    loss = lax.psum(loss_local, ("x", "y", "c"))

    chip = 2 * lax.axis_index("x") + lax.axis_index("y")
    take = [functools.partial(lambda win, pc: jnp.concatenate([win[:, lo:hi] for lo, hi in pc], axis=1), pc=pc)
            for pc in WIN_PIECES]

    def small_pack(values, c_ctx_value, l):
        pieces = [values[n] for n in REPLICATED]
        pieces.append(c_ctx_value if l == 0 else jnp.zeros((D_MODEL,), F32))
        return _pack(pieces, 8 * SUM_STEPS, F32)

    mine, theirs = [], []
    for l in range(depth):
        g = grads[l]
        small = small_pack(g, dc_ctx, l)
        recv = _scatter_grads(g["w_ada"], g["w_in"], g["s5_w_glu"], g["w_br"], g["w_out"], g["conv_w"], small)
        part = list(_sum_chips([r.reshape(4, -1, r.shape[-1]) for r in recv]))
        part[1] = lax.switch(chip, take, part[1])
        mine.append(part)
        theirs.append(_core_swap(part))

    families = ("w_ada", "w_in", "s5_w_glu", "w_br", "w_out", "conv_w")
    out = {}
    for k, n in enumerate(families):
        p, q = jnp.stack([m[k] for m in mine]), jnp.stack([t[k] for t in theirs])
        if n == "conv_w":
            p, q = p[:, :3], q[:, :3]
        as3d = lambda arr: arr.reshape(depth, -1, arr.shape[-1])
        res = _adamw(p, q, as3d(a[n]), as3d(a["m_" + n]), as3d(a["v_" + n]), "adamw_" + n)
        out[n] = [r.reshape(a[n].shape) for r in res]
    p, q = jnp.stack([m[6] for m in mine]), jnp.stack([t[6] for t in theirs])
    packs = [jnp.stack([small_pack({n: a[pre + n][l] for n in REPLICATED}, a[pre + "c_ctx"], l)
                        for l in range(depth)]) for pre in ("", "m_", "v_")]
    res = _adamw(p, q, *packs, "adamw_small")
    shapes = [a[n].shape[1:] for n in REPLICATED] + [a["c_ctx"].shape]
    per_layer = [[_unpack(r[l], shapes) for l in range(depth)] for r in res]
    for j, n in enumerate(REPLICATED):
        out[n] = [jnp.stack([per_layer[key][l][j] for l in range(depth)]) for key in range(4)]
    out["c_ctx"] = [per_layer[key][0][-1] for key in range(4)]
    results = [loss, grad_x]
    for key in range(4):
        results += [out[n][key] for n in WEIGHTS]
    return tuple(results)
```

```python
import functools

import numpy as np
import jax
import jax.numpy as jnp
from jax import lax
from jax.experimental import pallas as pl
from jax.experimental.pallas import tpu as pltpu

F32 = jnp.float32
BF16 = jnp.bfloat16

D_MODEL = 1024
MIX_W = 512
GRID_W = 64
HEAD_DIM = 64
N_HEADS = 8
S5_GROUPS = 32
S5_CH = 16
S5_CHUNK = 32
NA_ROWS = 8
NA_COLS = 16
WINDOW = 128
ROPE_BASE = 10000.0
ROPE_PAIRS = 16
EPS = 1e-6
NEG_INF = -1e30
ATT_BLK = 256
TOK = 256
VMEM_LIMIT = 56 * 1024 * 1024

ADAM_LR, ADAM_B1, ADAM_B2, ADAM_EPS, ADAM_WD, ADAM_STEP = 0.001, 0.9, 0.999, 1e-8, 0.01, 10

OFF = dict(s5_u=0, s5_gate=512, conv_v=1024, conv_b=1536, conv_c=2048, conv_gate=2560,
           na_q=3072, na_k=3584, na_v=4096, na_gate=4608, gqa_q=5120, gqa_k=5632, gqa_v=5760,
           pad=5888, gqa_gate=6144, merge_s5=6656, merge_conv=7680, merge_na=8704, merge_gqa=9728)
N_Z = 10752
GATE_OFFS = (OFF["s5_gate"], OFF["conv_gate"], OFF["na_gate"], OFF["gqa_gate"])
MERGE_OFFS = (OFF["merge_s5"], OFF["merge_conv"], OFF["merge_na"], OFF["merge_gqa"])


def _cparams(sem):
    return pltpu.CompilerParams(dimension_semantics=sem, vmem_limit_bytes=VMEM_LIMIT)


def _dot(a, b, ca, cb):
    return lax.dot_general(a.astype(BF16), b.astype(BF16), (((ca,), (cb,)), ((), ())),
                           preferred_element_type=F32)


def _dot_tn(a, b):
    return _dot(a.astype(F32).T, b, 1, 0)


@jax.custom_vjp
def mm(a, b):
    return _dot(a, b, 1, 0)


@jax.custom_vjp
def mm_nt(a, b):
    return _dot(a, b, 1, 1)


@jax.custom_vjp
def mm_tn(a, b):
    return _dot_tn(a, b)


mm.defvjp(lambda a, b: (mm(a, b), (a, b)), lambda r, g: (mm_nt(g, r[1]), mm_tn(r[0], g)))
mm_nt.defvjp(lambda a, b: (mm_nt(a, b), (a, b)), lambda r, g: (mm(g, r[1]), mm_tn(g, r[0])))
mm_tn.defvjp(lambda a, b: (mm_tn(a, b), (a, b)), lambda r, g: (mm_nt(r[1], g), mm(r[0], g)))


@functools.partial(jax.custom_vjp, nondiff_argnums=(1,))
def lane_roll(x, shift):
    return pltpu.roll(x, shift, 1)


lane_roll.defvjp(lambda x, shift: (lane_roll(x, shift), None),
                 lambda shift, _, g: (lane_roll(g, (g.shape[1] - shift) % g.shape[1]),))


def _silu(x):
    return x * jax.nn.sigmoid(x)


def _dsilu(x):
    s = jax.nn.sigmoid(x)
    return s * (1.0 + x * (1.0 - s))


def _pick(n, prefs):
    for p in prefs:
        if n % p == 0:
            return p
    return n


def _matmul(a, b, *, trans_b=False, out_dtype=F32, tm=None, tn=None, tk=None, name):
    squeeze = a.ndim == 2
    if squeeze:
        a, b = a[None], b[None]
    nb, m, k = a.shape
    n = b.shape[1] if trans_b else b.shape[2]
    tm = tm or _pick(m, (1280, 1024, 640, 512, 256, 128))
    tn = tn or _pick(n, (1536, 1024, 512, 256, 128))
    tk = tk or _pick(k, (3584, 3328, 1536, 1280, 1024, 768, 640, 512, 256, 128))
    nk = k // tk

    def body(a_ref, b_ref, o_ref, *scr):
        part = _dot(a_ref[...], b_ref[...], 1, 1 if trans_b else 0)
        if nk == 1:
            o_ref[...] = part.astype(out_dtype)
        else:
            acc = scr[0]
            kk = pl.program_id(3)

            @pl.when(kk == 0)
            def _():
                acc[...] = part

            @pl.when(kk > 0)
            def _():
                acc[...] += part

            @pl.when(kk == nk - 1)
            def _():
                o_ref[...] = acc[...].astype(out_dtype)

    if trans_b:
        b_spec = pl.BlockSpec((None, tn, tk), lambda bb, i, j, kk: (bb, j, kk))
    else:
        b_spec = pl.BlockSpec((None, tk, tn), lambda bb, i, j, kk: (bb, kk, j))
    out = pl.pallas_call(
        body, name=name,
        grid=(nb, m // tm, n // tn, nk),
        in_specs=[pl.BlockSpec((None, tm, tk), lambda bb, i, j, kk: (bb, i, kk)), b_spec],
        out_specs=pl.BlockSpec((None, tm, tn), lambda bb, i, j, kk: (bb, i, j)),
        out_shape=jax.ShapeDtypeStruct((nb, m, n), out_dtype),
        scratch_shapes=[] if nk == 1 else [pltpu.VMEM((tm, tn), F32)],
        compiler_params=_cparams(("parallel", "parallel", "parallel", "arbitrary")),
    )(a, b)
    return out[0] if squeeze else out


def _adaln_fn(cc, w, b):
    return mm(_silu(cc), w) + b


def _adaln_fwd(cc, w_ada, b_ada):
    def body(cc_ref, w_ref, b_ref, o_ref):
        o_ref[...] = _adaln_fn(cc_ref[...], w_ref[...], b_ref[...])

    return pl.pallas_call(
        body, name="adaln_fwd", out_shape=jax.ShapeDtypeStruct((8, 3 * D_MODEL), F32),
        compiler_params=pltpu.CompilerParams(vmem_limit_bytes=VMEM_LIMIT),
    )(cc, w_ada, b_ada)


def _adaln_bwd(cc, w_ada, dmod):
    def body(cc_ref, w_ref, g_ref, dcc_ref, dw_ref, db_ref):
        cc_v, g = cc_ref[...], g_ref[...]
        dw_ref[...] = mm_tn(_silu(cc_v), g).astype(BF16)
        db_ref[...] = jnp.sum(g, axis=0, keepdims=True)
        dcc_ref[...] = mm_nt(g, w_ref[...]) * _dsilu(cc_v)

    return pl.pallas_call(
        body, name="adaln_bwd",
        out_shape=(jax.ShapeDtypeStruct((8, D_MODEL), F32),
                   jax.ShapeDtypeStruct((D_MODEL, 3 * D_MODEL), BF16),
                   jax.ShapeDtypeStruct((1, 3 * D_MODEL), F32)),
        compiler_params=pltpu.CompilerParams(vmem_limit_bytes=VMEM_LIMIT),
    )(cc, w_ada, dmod)


def _seg_spec(which, n_ctx_tiles):
    return pl.BlockSpec((None, None, 1, D_MODEL),
                        lambda i: (jnp.where(i < n_ctx_tiles, 0, 1), which, 0, 0))


def _row_spec(width, col_block=0, tile=TOK):
    return pl.BlockSpec((tile, width), lambda i: (i, col_block))


def _const_spec(shape):
    zeros = (0,) * len(shape)
    return pl.BlockSpec(shape, lambda i: zeros)


def _modnorm_fn(x, g, shift, scale):
    y = x * lax.rsqrt(jnp.mean(x * x, axis=-1, keepdims=True) + EPS)
    return (y * g) * (1.0 + scale) + shift


def _modnorm_fwd(xt, g, mod4, n_ctx):
    t = xt.shape[0]
    nct = n_ctx // TOK

    def body(x_ref, g_ref, sh_ref, sc_ref, o_ref):
        o_ref[...] = _modnorm_fn(x_ref[...], g_ref[...], sh_ref[...], sc_ref[...]).astype(BF16)

    return pl.pallas_call(
        body, name="modnorm_fwd", grid=(t // TOK,),
        in_specs=[_row_spec(D_MODEL), _const_spec((1, D_MODEL)), _seg_spec(0, nct), _seg_spec(1, nct)],
        out_specs=_row_spec(D_MODEL),
        out_shape=jax.ShapeDtypeStruct((t, D_MODEL), BF16),
        compiler_params=_cparams(("parallel",)),
    )(xt, g, mod4, mod4)


def _modnorm_bwd(xt, g, mod4, dh, dres, n_ctx):
    t = xt.shape[0]
    nct = n_ctx // TOK

    def body(x_ref, g_ref, sh_ref, sc_ref, dh_ref, dres_ref, dx_ref, dg_ref, dsh_ref, dsc_ref):
        i = pl.program_id(0)
        _, vjp = jax.vjp(_modnorm_fn, x_ref[...], g_ref[...], sh_ref[...], sc_ref[...])
        dx, dg, dsh, dsc = vjp(dh_ref[...])
        dx_ref[...] = dx + dres_ref[...]

        @pl.when(i == 0)
        def _():
            dg_ref[...] = jnp.zeros_like(dg_ref)

        dg_ref[...] += dg
        first = jnp.logical_or(i == 0, i == nct)

        @pl.when(first)
        def _():
            dsh_ref[...] = dsh
            dsc_ref[...] = dsc

        @pl.when(jnp.logical_not(first))
        def _():
            dsh_ref[...] += dsh
            dsc_ref[...] += dsc

    seg_out = lambda which: pl.BlockSpec((None, None, 1, D_MODEL),
                                         lambda i: (jnp.where(i < nct, 0, 1), which, 0, 0))
    dx, dg, dss, dss2 = pl.pallas_call(
        body, name="modnorm_bwd", grid=(t // TOK,),
        in_specs=[_row_spec(D_MODEL), _const_spec((1, D_MODEL)), _seg_spec(0, nct), _seg_spec(1, nct),
                  _row_spec(D_MODEL), _row_spec(D_MODEL)],
        out_specs=[_row_spec(D_MODEL), _const_spec((1, D_MODEL)), seg_out(0), seg_out(0)],
        out_shape=(jax.ShapeDtypeStruct((t, D_MODEL), F32), jax.ShapeDtypeStruct((1, D_MODEL), F32),
                   jax.ShapeDtypeStruct((2, 1, 1, D_MODEL), F32), jax.ShapeDtypeStruct((2, 1, 1, D_MODEL), F32)),
        compiler_params=_cparams(("arbitrary",)),
    )(xt, g, mod4, mod4, dh, dres)
    return dx, dg, dss, dss2


def _group_mean_sq(x, gs):
    x2 = x * x
    hi = x2.astype(BF16).astype(F32)
    return mm(hi, gs) + mm(x2 - hi, gs)


def _head_norm(x, g, gs):
    return (x * lax.rsqrt(_group_mean_sq(x, gs) + EPS)) * g


def _rope(x, cos, sin_signed):
    lane = lax.broadcasted_iota(jnp.int32, (1, 128), 1)
    first_half = jnp.bitwise_and(lane, 63) < 32
    cols = []
    for c in range(x.shape[1] // 128):
        xb = x[:, 128 * c:128 * (c + 1)]
        partner = jnp.where(first_half, lane_roll(xb, 96), lane_roll(xb, 32))
        cols.append(xb * cos + partner * sin_signed)
    return cols[0] if len(cols) == 1 else jnp.concatenate(cols, axis=1)


def _prep_fn(zq_na, zk_na, zv_na, zq_g, zk_g, zv_g, g_naq, g_nak, g_gq, g_gk, cos, sin_signed, gs512, gs128, expand):
    q_na = _head_norm(zq_na, g_naq, gs512)
    k_na = _head_norm(zk_na, g_nak, gs512)
    q_g = _rope(_head_norm(zq_g, g_gq, gs512), cos, sin_signed)
    k_g = _rope(_head_norm(zk_g, g_gk, gs128), cos, sin_signed)
    return q_na, k_na, zv_na, q_g, mm(k_g, expand), mm(zv_g, expand)


def _prep_consts():
    gid = np.arange(512) // 64
    gs512 = (gid[:, None] == gid[None, :]).astype(np.float32) / 64.0
    expand = np.zeros((128, 512), np.float32)
    for h in range(N_HEADS):
        for j in range(64):
            expand[64 * (h // 4) + j, 64 * h + j] = 1.0
    return jnp.asarray(gs512), jnp.asarray(gs512[:128, :128]), jnp.asarray(expand)


def _prep_in_specs():
    blk = lambda off, w: _row_spec(w, off // w)
    return [blk(OFF["na_q"], 512), blk(OFF["na_k"], 512), blk(OFF["na_v"], 512), blk(OFF["gqa_q"], 512),
            blk(OFF["gqa_k"], 128), blk(OFF["gqa_v"], 128),
            _const_spec((1, 512)), _const_spec((1, 512)), _const_spec((1, 512)), _const_spec((1, 128)),
            _row_spec(128), _row_spec(128),
            _const_spec((512, 512)), _const_spec((128, 128)), _const_spec((128, 512))]


def _prep_fwd(z, gains, rope_tabs):
    t = z.shape[0]
    consts = _prep_consts()

    def body(*refs):
        ins, outs = refs[:15], refs[15:]
        res = _prep_fn(*[r[...] for r in ins])
        for o_ref, v in zip(outs, res):
            o_ref[...] = v.astype(BF16)

    return pl.pallas_call(
        body, name="prep_fwd", grid=(t // TOK,),
        in_specs=_prep_in_specs(),
        out_specs=[_row_spec(512)] * 6,
        out_shape=tuple(jax.ShapeDtypeStruct((t, 512), BF16) for _ in range(6)),
        compiler_params=_cparams(("parallel",)),
    )(z, z, z, z, z, z, *gains, *rope_tabs, *consts)


def _prep_bwd(z, gains, rope_tabs, dqs, dkv_lat, dkv_ctx, du_a, du_b, n_ctx):
    t = z.shape[0]
    nct = n_ctx // TOK
    consts = _prep_consts()

    def body(*refs):
        ins, dq_refs, lat_refs, ctx_refs = refs[:15], refs[15:17], refs[17:21], refs[21:25]
        (dua_ref, dub_ref), outs = refs[25:27], refs[27:]
        i = pl.program_id(0)
        vals = [r[...] for r in ins]
        _, vjp = jax.vjp(lambda *a: _prep_fn(*a, *vals[10:]), *vals[:10])
        kv = [jnp.where(i < nct, c_ref[...], l_ref[...]) for l_ref, c_ref in zip(lat_refs, ctx_refs)]
        grads = vjp((dq_refs[0][...], kv[0], kv[1], dq_refs[1][...], kv[2], kv[3]))
        for o_ref, v in zip(outs[:6], grads[:6]):
            o_ref[...] = v.astype(BF16)
        outs[6][...] = (dua_ref[...] + dub_ref[...]).astype(BF16)

        @pl.when(i == 0)
        def _():
            for o_ref in outs[7:]:
                o_ref[...] = jnp.zeros_like(o_ref)

        for o_ref, v in zip(outs[7:], grads[6:10]):
            o_ref[...] += v

    lat_spec = pl.BlockSpec((TOK, 512), lambda i: (jnp.maximum(i - nct + 1, 0), 0))
    ctx_spec = pl.BlockSpec((TOK, 512), lambda i: (jnp.minimum(i, nct - 1), 0))
    return pl.pallas_call(
        body, name="prep_bwd", grid=(t // TOK,),
        in_specs=_prep_in_specs() + [_row_spec(512)] * 2 + [lat_spec] * 4 + [ctx_spec] * 4 + [_row_spec(512)] * 2,
        out_specs=[_row_spec(512)] * 4 + [_row_spec(128)] * 2 + [_row_spec(512)]
        + [_const_spec((1, 512))] * 3 + [_const_spec((1, 128))],
        out_shape=tuple([jax.ShapeDtypeStruct((t, 512), BF16)] * 4 + [jax.ShapeDtypeStruct((t, 128), BF16)] * 2
                        + [jax.ShapeDtypeStruct((t, 512), BF16)]
                        + [jax.ShapeDtypeStruct((1, 512), F32)] * 3 + [jax.ShapeDtypeStruct((1, 128), F32)]),
        compiler_params=_cparams(("arbitrary",)),
    )(z, z, z, z, z, z, *gains, *rope_tabs, *consts, *dqs, *dkv_lat, *dkv_ctx, du_a, du_b)


def _s5post_fn(ys, u, d, w_glu):
    y = jax.nn.gelu(ys + d * u)
    return y * jax.nn.sigmoid(mm(y, w_glu))


def _s5post_fwd(ys, z, d, w_glu):
    t = z.shape[0]

    def body(ys_ref, u_ref, d_ref, w_ref, o_ref):
        o_ref[...] = _s5post_fn(ys_ref[...], u_ref[...], d_ref[...], w_ref[...])

    return pl.pallas_call(
        body, name="s5post_fwd", grid=(t // TOK,),
        in_specs=[_row_spec(512), _row_spec(512, OFF["s5_u"] // 512),
                  _const_spec((1, 512)), _const_spec((512, 512))],
        out_specs=_row_spec(512), out_shape=jax.ShapeDtypeStruct((t, 512), F32),
        compiler_params=_cparams(("parallel",)),
    )(ys, z, d, w_glu)


def _s5post_bwd(ys, z, d, w_glu, dy):
    t = z.shape[0]

    def body(ys_ref, u_ref, d_ref, w_ref, dy_ref, dpre_ref, du_ref, dd_ref, dw_ref):
        i = pl.program_id(0)
        _, vjp = jax.vjp(_s5post_fn, ys_ref[...], u_ref[...], d_ref[...], w_ref[...].astype(F32))
        dys, du, dd, dw = vjp(dy_ref[...])
        dpre_ref[...] = dys.astype(BF16)
        du_ref[...] = du

        @pl.when(i == 0)
        def _():
            dd_ref[...] = jnp.zeros_like(dd_ref)
            dw_ref[...] = jnp.zeros_like(dw_ref)

        dd_ref[...] += dd
        dw_ref[...] += dw

    return pl.pallas_call(
        body, name="s5post_bwd", grid=(t // TOK,),
        in_specs=[_row_spec(512), _row_spec(512, OFF["s5_u"] // 512),
                  _const_spec((1, 512)), _const_spec((512, 512)), _row_spec(512)],
        out_specs=[_row_spec(512), _row_spec(512), _const_spec((1, 512)), _const_spec((512, 512))],
        out_shape=(jax.ShapeDtypeStruct((t, 512), BF16), jax.ShapeDtypeStruct((t, 512), F32),
                   jax.ShapeDtypeStruct((1, 512), F32), jax.ShapeDtypeStruct((512, 512), F32)),
        compiler_params=_cparams(("arbitrary",)),
    )(ys, z, d, w_glu, dy)


def _halo_specs(col_block, t):
    last = t // 8 - 1
    prev = pl.BlockSpec((8, 512), lambda i: (jnp.maximum(i * (TOK // 8) - 1, 0), col_block))
    nxt = pl.BlockSpec((8, 512), lambda i: (jnp.minimum((i + 1) * (TOK // 8), last), col_block))
    return [_row_spec(512, col_block), prev, nxt]


def _shifted(cur, prev_row, next_row, tok0, n_ctx, t_total):
    row = lax.broadcasted_iota(jnp.int32, (TOK, 1), 0)
    tpos = row + tok0
    down = jnp.where(row == 0, prev_row, pltpu.roll(cur, 1, 0))
    down = jnp.where(jnp.logical_or(tpos == 0, tpos == n_ctx), 0.0, down)
    up = jnp.where(row == TOK - 1, next_row, pltpu.roll(cur, TOK - 1, 0))
    up = jnp.where(jnp.logical_or(tpos == n_ctx - 1, tpos == t_total - 1), 0.0, up)
    return down, up


def _conv_fwd(z, conv_w, conv_b, n_ctx):
    t = z.shape[0]

    def body(v_ref, vp_ref, vn_ref, c_ref, cp_ref, cn_ref, b_ref, w_ref, cb_ref, o_ref):
        tok0 = pl.program_id(0) * TOK
        zz = v_ref[...] * c_ref[...]
        zz_m1, zz_p1 = _shifted(zz, vp_ref[7:8, :] * cp_ref[7:8, :], vn_ref[0:1, :] * cn_ref[0:1, :], tok0, n_ctx, t)
        s = cb_ref[...] + zz_m1 * w_ref[0:1, :] + zz * w_ref[1:2, :] + zz_p1 * w_ref[2:3, :]
        o_ref[...] = b_ref[...] * s

    return pl.pallas_call(
        body, name="conv_fwd", grid=(t // TOK,),
        in_specs=_halo_specs(OFF["conv_v"] // 512, t) + _halo_specs(OFF["conv_c"] // 512, t)
        + [_row_spec(512, OFF["conv_b"] // 512), _const_spec((8, 512)), _const_spec((1, 512))],
        out_specs=_row_spec(512), out_shape=jax.ShapeDtypeStruct((t, 512), F32),
        compiler_params=_cparams(("parallel",)),
    )(z, z, z, z, z, z, z, conv_w, conv_b)


def _conv_bwd(z, conv_w, conv_b, dy, n_ctx):
    t = z.shape[0]

    def body(v_ref, vp_ref, vn_ref, c_ref, cp_ref, cn_ref, b_ref, bp_ref, bn_ref, dy_ref, dyp_ref, dyn_ref,
             w_ref, cb_ref, dv_ref, db_ref, dc_ref, dw_ref, dcb_ref):
        i = pl.program_id(0)
        tok0 = i * TOK
        v, c, b, dy_v = v_ref[...], c_ref[...], b_ref[...], dy_ref[...]
        w0, w1, w2 = w_ref[0:1, :], w_ref[1:2, :], w_ref[2:3, :]
        zz = v * c
        zz_m1, zz_p1 = _shifted(zz, vp_ref[7:8, :] * cp_ref[7:8, :], vn_ref[0:1, :] * cn_ref[0:1, :], tok0, n_ctx, t)
        s = cb_ref[...] + zz_m1 * w0 + zz * w1 + zz_p1 * w2
        ds = dy_v * b
        ds_m1, ds_p1 = _shifted(ds, dyp_ref[7:8, :] * bp_ref[7:8, :], dyn_ref[0:1, :] * bn_ref[0:1, :], tok0, n_ctx, t)
        dzz = ds_p1 * w0 + ds * w1 + ds_m1 * w2
        db_ref[...] = (dy_v * s).astype(BF16)
        dv_ref[...] = (dzz * c).astype(BF16)
        dc_ref[...] = (dzz * v).astype(BF16)

        @pl.when(i == 0)
        def _():
            dw_ref[...] = jnp.zeros_like(dw_ref)
            dcb_ref[...] = jnp.zeros_like(dcb_ref)

        rsum = lambda a: jnp.sum(a, axis=0, keepdims=True)
        dw_ref[0:1, :] += rsum(ds * zz_m1)
        dw_ref[1:2, :] += rsum(ds * zz)
        dw_ref[2:3, :] += rsum(ds * zz_p1)
        dcb_ref[...] += rsum(ds)

    return pl.pallas_call(
        body, name="conv_bwd", grid=(t // TOK,),
        in_specs=_halo_specs(OFF["conv_v"] // 512, t) + _halo_specs(OFF["conv_c"] // 512, t)
        + _halo_specs(OFF["conv_b"] // 512, t) + _halo_specs(0, t) + [_const_spec((8, 512)), _const_spec((1, 512))],
        out_specs=[_row_spec(512)] * 3 + [_const_spec((8, 512)), _const_spec((1, 512))],
        out_shape=tuple([jax.ShapeDtypeStruct((t, 512), BF16)] * 3
                        + [jax.ShapeDtypeStruct((8, 512), F32), jax.ShapeDtypeStruct((1, 512), F32)]),
        compiler_params=_cparams(("arbitrary",)),
    )(z, z, z, z, z, z, z, z, z, dy, dy, dy, conv_w, conv_b)


def _merge_col_specs(tile):
    specs = []
    for off in MERGE_OFFS:
        specs.append(pl.BlockSpec((tile, 512), functools.partial(lambda i, cb: (i, cb), cb=off // 512)))
        specs.append(pl.BlockSpec((tile, 512), functools.partial(lambda i, cb: (i, cb), cb=off // 512 + 1)))
    return specs


def _merge_fwd(xt, ys, z, mod4, w_br, w_out, n_ctx):
    t = xt.shape[0]
    nct = n_ctx // TOK

    def body(x_ref, *refs):
        y_refs, gt_refs, mg_refs = refs[0:4], refs[4:8], refs[8:16]
        gate_ref, wbr_ref, wout_ref, o_ref = refs[16:20]
        acc_lo = acc_hi = None
        for k in range(4):
            gated = y_refs[k][...] * _silu(gt_refs[k][...])
            proj = mm(gated, wbr_ref[k])
            lo = jax.nn.sigmoid(mg_refs[2 * k][...]) * proj[:, :512]
            hi = jax.nn.sigmoid(mg_refs[2 * k + 1][...]) * proj[:, 512:]
            acc_lo = lo if acc_lo is None else acc_lo + lo
            acc_hi = hi if acc_hi is None else acc_hi + hi
        acc = jnp.concatenate([acc_lo, acc_hi], axis=1)
        o_ref[...] = x_ref[...] + gate_ref[...] * mm(acc, wout_ref[...])

    gate_specs = [pl.BlockSpec((TOK, 512), functools.partial(lambda i, cb: (i, cb), cb=o // 512)) for o in GATE_OFFS]
    return pl.pallas_call(
        body, name="merge_fwd", grid=(t // TOK,),
        in_specs=[_row_spec(D_MODEL)] + [_row_spec(512)] * 4 + gate_specs + _merge_col_specs(TOK)
        + [_seg_spec(2, nct), _const_spec((4, 512, 1024)), _const_spec((1024, 1024))],
        out_specs=_row_spec(D_MODEL), out_shape=jax.ShapeDtypeStruct((t, D_MODEL), F32),
        compiler_params=_cparams(("parallel",)),
    )(xt, *ys, z, z, z, z, z, z, z, z, z, z, z, z, mod4, w_br, w_out)


MERGE_BWD_TILE = 128


def _merge_bwd(g, ys, z, mod4, w_br, w_out, n_ctx):
    t = g.shape[0]
    tile = MERGE_BWD_TILE
    nct = n_ctx // tile
    nsteps = t // tile

    def body(g_ref, *refs):
        y_refs, gt_refs, mg_refs = refs[0:4], refs[4:8], refs[8:16]
        gate_ref, wbr_hbm, wout_hbm = refs[16:19]
        dy_refs, dgt_refs, dmg_refs = refs[19:23], refs[23:27], refs[27:31]
        dgate_ref, dwbr_hbm, dwout_hbm = refs[31:34]
        wbr_v, wout_v, dwbr_acc, dwout_acc = refs[34:38]
        i = pl.program_id(0)

        @pl.when(i == 0)
        def _():
            pltpu.sync_copy(wbr_hbm, wbr_v)
            pltpu.sync_copy(wout_hbm, wout_v)
            dwbr_acc[...] = jnp.zeros_like(dwbr_acc)
            dwout_acc[...] = jnp.zeros_like(dwout_acc)

        g_v, gate = g_ref[...], gate_ref[...]
        gated, proj, sig = [], [], []
        acc = None
        for k in range(4):
            gated.append(y_refs[k][...] * _silu(gt_refs[k][...]))
            proj.append(mm(gated[k], wbr_v[k]))
            sig.append(jax.nn.sigmoid(jnp.concatenate([mg_refs[2 * k][...], mg_refs[2 * k + 1][...]], axis=1)))
            contrib = sig[k] * proj[k]
            acc = contrib if acc is None else acc + contrib
        o = mm(acc, wout_v[...])
        dgate = jnp.sum(g_v * o, axis=0, keepdims=True)
        first = jnp.logical_or(i == 0, i == nct)

        @pl.when(first)
        def _():
            dgate_ref[...] = dgate

        @pl.when(jnp.logical_not(first))
        def _():
            dgate_ref[...] += dgate

        do = g_v * gate
        dwout_acc[...] += mm_tn(acc, do)
        dacc = mm_nt(do, wout_v[...])
        for k in range(4):
            dmg_refs[k][...] = (dacc * proj[k] * sig[k] * (1.0 - sig[k])).astype(BF16)
            dproj = dacc * sig[k]
            dwbr_acc[k] += mm_tn(gated[k], dproj)
            dgated = mm_nt(dproj, wbr_v[k])
            gt = gt_refs[k][...]
            dy_refs[k][...] = dgated * _silu(gt)
            dgt_refs[k][...] = (dgated * y_refs[k][...] * _dsilu(gt)).astype(BF16)

        @pl.when(i == nsteps - 1)
        def _():
            wbr_v[...] = dwbr_acc[...].astype(BF16)
            wout_v[...] = dwout_acc[...].astype(BF16)
            pltpu.sync_copy(wbr_v, dwbr_hbm)
            pltpu.sync_copy(wout_v, dwout_hbm)

    row = lambda w: _row_spec(w, 0, tile)
    gate_specs = [pl.BlockSpec((tile, 512), functools.partial(lambda i, cb: (i, cb), cb=o // 512)) for o in GATE_OFFS]
    anyspec = pl.BlockSpec(memory_space=pl.ANY)
    seg = pl.BlockSpec((None, None, 1, D_MODEL), lambda i: (jnp.where(i < nct, 0, 1), 2, 0, 0))
    seg_out = pl.BlockSpec((None, None, 1, D_MODEL), lambda i: (jnp.where(i < nct, 0, 1), 0, 0, 0))
    res = pl.pallas_call(
        body, name="merge_bwd", grid=(nsteps,),
        in_specs=[row(D_MODEL)] + [row(512)] * 4 + gate_specs + _merge_col_specs(tile) + [seg, anyspec, anyspec],
        out_specs=[row(512)] * 8 + [row(1024)] * 4 + [seg_out, anyspec, anyspec],
        out_shape=tuple([jax.ShapeDtypeStruct((t, 512), F32)] * 4 + [jax.ShapeDtypeStruct((t, 512), BF16)] * 4
                        + [jax.ShapeDtypeStruct((t, 1024), BF16)] * 4
                        + [jax.ShapeDtypeStruct((2, 1, 1, D_MODEL), F32),
                           jax.ShapeDtypeStruct((4, 512, 1024), BF16), jax.ShapeDtypeStruct((1024, 1024), BF16)]),
        scratch_shapes=[pltpu.VMEM((4, 512, 1024), BF16), pltpu.VMEM((1024, 1024), BF16),
                        pltpu.VMEM((4, 512, 1024), F32), pltpu.VMEM((1024, 1024), F32)],
        compiler_params=_cparams(("arbitrary",)),
    )(g, *ys, z, z, z, z, z, z, z, z, z, z, z, z, mod4, w_br, w_out)
    return res[0:4], res[4:8], res[8:12], res[12], res[13], res[14]


FWD_ROWS = slice(0, S5_GROUPS)
BWD_ROWS = slice(S5_GROUPS, 2 * S5_GROUPS)


def _backward_chunk(j, k, n_ctx_chunks):
    return jnp.where(j < n_ctx_chunks, n_ctx_chunks - 1 - j, k - 1 - (j - n_ctx_chunks))


def _scan_call(body, name, n_hbm_in, out_shape, kp):
    hbm, vmem = pl.BlockSpec(memory_space=pl.ANY), pl.BlockSpec(memory_space=pltpu.VMEM)
    return pl.pallas_call(
        body, name=name, in_specs=[hbm] * n_hbm_in + [vmem, vmem],
        out_specs=[hbm] + [vmem] * (len(out_shape) - 1), out_shape=out_shape,
        scratch_shapes=[pltpu.VMEM((kp, 64, 128), F32), pltpu.VMEM((kp, 64, 128), F32)],
        compiler_params=pltpu.CompilerParams(vmem_limit_bytes=VMEM_LIMIT))


def _complex_step(a1, a2, h):
    return a1 * h + a2 * pltpu.roll(h, 64, 1)


def _s5_scan_fwd(s, a1, a2, k, n_ctx_chunks):
    kp = s.shape[0]

    def body(s_hbm, a1_ref, a2_ref, hp_hbm, s_v, hp_v):
        pltpu.sync_copy(s_hbm, s_v)
        if kp > k:
            hp_v[k:kp] = jnp.zeros((kp - k, 64, 128), F32)
        a1f, a2f, a1b, a2b = a1_ref[FWD_ROWS, :], a2_ref[FWD_ROWS, :], a1_ref[BWD_ROWS, :], a2_ref[BWD_ROWS, :]

        def step(j, carry):
            hf, hb = carry
            cb = _backward_chunk(j, k, n_ctx_chunks)
            hp_v[j, FWD_ROWS, :] = hf
            hp_v[cb, BWD_ROWS, :] = hb
            return (_complex_step(a1f, a2f, hf) + s_v[j, FWD_ROWS, :],
                    _complex_step(a1b, a2b, hb) + s_v[cb, BWD_ROWS, :])

        zero = jnp.zeros((S5_GROUPS, 128), F32)
        lax.fori_loop(0, k, step, (zero, zero))
        pltpu.sync_copy(hp_v, hp_hbm)

    return _scan_call(body, "s5_scan_fwd", 1, (jax.ShapeDtypeStruct(s.shape, F32),), kp)(s, a1, a2)[0]


def _s5_scan_bwd(dhp, hp, a1, a2, k, n_ctx_chunks):
    kp = hp.shape[0]

    def body(dhp_hbm, hp_hbm, a1_ref, a2_ref, ds_hbm, da1_ref, da2_ref, g_v, hp_v):
        pltpu.sync_copy(dhp_hbm, g_v)
        pltpu.sync_copy(hp_hbm, hp_v)
        if kp > k:
            g_v[k:kp] = jnp.zeros((kp - k, 64, 128), F32)
        coef_f = (a1_ref[FWD_ROWS, :], a2_ref[FWD_ROWS, :])
        coef_b = (a1_ref[BWD_ROWS, :], a2_ref[BWD_ROWS, :])

        def one(rows, c, lam, d1, d2):
            a1_v, a2_v = coef_f if rows is FWD_ROWS else coef_b
            dh_in = g_v[c, rows, :]
            g_v[c, rows, :] = lam
            h = hp_v[c, rows, :]
            return (dh_in + a1_v * lam + pltpu.roll(a2_v * lam, 64, 1),
                    d1 + lam * h, d2 + lam * pltpu.roll(h, 64, 1))

        def step(j, carry):
            f, b = carry
            jj = k - 1 - j
            return one(FWD_ROWS, jj, *f), one(BWD_ROWS, _backward_chunk(jj, k, n_ctx_chunks), *b)

        zero = jnp.zeros((S5_GROUPS, 128), F32)
        f, b = lax.fori_loop(0, k, step, ((zero, zero, zero), (zero, zero, zero)))
        da1_ref[FWD_ROWS, :], da2_ref[FWD_ROWS, :] = f[1], f[2]
        da1_ref[BWD_ROWS, :], da2_ref[BWD_ROWS, :] = b[1], b[2]
        pltpu.sync_copy(g_v, ds_hbm)

    shapes = (jax.ShapeDtypeStruct(hp.shape, F32), jax.ShapeDtypeStruct((64, 128), F32),
              jax.ShapeDtypeStruct((64, 128), F32))
    return _scan_call(body, "s5_scan_bwd", 2, shapes, kp)(dhp, hp, a1, a2)


WIN_GEOM = {"na": (ATT_BLK, 3, 0), "gqa": (128, 4, 128)}


def _win_rows(mode):
    kb, nw, _ = WIN_GEOM[mode]
    return kb * nw


@functools.lru_cache(maxsize=None)
def _window_patterns(mode, n_lat):
    nb = n_lat // ATT_BLK
    assert nb >= 3
    first_key = WIN_GEOM[mode][2] - ATT_BLK
    iq, ik = np.arange(ATT_BLK)[:, None], np.arange(_win_rows(mode))[None, :]

    def valid(ql):
        tq, ts = ATT_BLK * ql + iq, ATT_BLK * ql + first_key + ik
        if mode == "na":
            r, qcol, kr, kcol = tq // GRID_W, tq % GRID_W, ts // GRID_W, ts % GRID_W
            rs = np.clip(r - NA_ROWS // 2, 0, n_lat // GRID_W - NA_ROWS)
            cs = np.clip(qcol - NA_COLS // 2, 0, GRID_W - NA_COLS)
            return (kr >= rs) & (kr < rs + NA_ROWS) & (kcol >= cs) & (kcol < cs + NA_COLS)
        return (np.abs(tq - ts) <= WINDOW) & (ts >= 0) & (ts < n_lat)

    interior = valid(1)
    assert all(np.array_equal(valid(ql), interior) for ql in range(1, nb - 1))
    return np.stack([valid(0), interior, valid(nb - 1), np.zeros_like(interior)])


def _pattern_of_block(qb, nqb):
    return jnp.where(qb == 0, 3, jnp.where(qb == 1, 0, jnp.where(qb == nqb - 1, 2, 1)))


def _attn_block(q, k3, v3, kc, vc, bias0, bias1, sink):
    lane = lax.broadcasted_iota(jnp.int32, (1, 128), 1)
    scale = HEAD_DIM ** -0.5
    outs = []
    for e, bias in enumerate((bias0, bias1)):
        in_head = (lane < 64) if e == 0 else (lane >= 64)
        qe = jnp.where(in_head, q, 0.0)
        s_lat = mm_nt(qe, k3) * scale + bias
        s_ctx = mm_nt(qe, kc) * scale
        mx = jnp.maximum(jnp.max(s_lat, axis=1, keepdims=True), jnp.max(s_ctx, axis=1, keepdims=True))
        if sink is not None:
            srow = lax.broadcasted_iota(jnp.int32, sink.shape, 0)
            sv = jnp.sum(jnp.where(srow == e, sink, 0.0), keepdims=True) * (1.0 / 128.0)
            mx = jnp.maximum(mx, sv)
        mx = lax.stop_gradient(mx)
        e_lat = jnp.exp(s_lat - mx)
        e_ctx = jnp.exp(s_ctx - mx)
        den = jnp.sum(e_lat, axis=1, keepdims=True) + jnp.sum(e_ctx, axis=1, keepdims=True)
        if sink is not None:
            den = den + jnp.exp(sv - mx)
        inv = 1.0 / den
        outs.append(mm(e_lat * inv, v3) + mm(e_ctx * inv, vc))
    return jnp.where(lane < 64, outs[0], outs[1])


def _attn_specs(n_ctx, nqb, per_head, mode):
    kb, nw, skip = WIN_GEOM[mode]

    def kwin(s):
        return pl.BlockSpec(
            (kb, 128), lambda hp, qb: (jnp.maximum(qb - 1, 0) * (ATT_BLK // kb) + skip // kb + s, hp))

    q = pl.BlockSpec((ATT_BLK, 128), lambda hp, qb: (qb, hp))
    ctx = pl.BlockSpec((n_ctx, 128), lambda hp, qb: (0, hp))
    if per_head:
        bias = pl.BlockSpec((None, None, 2, ATT_BLK, kb * nw),
                            lambda hp, qb: (_pattern_of_block(qb, nqb), hp, 0, 0, 0))
    else:
        bias = pl.BlockSpec((None, ATT_BLK, kb * nw), lambda hp, qb: (_pattern_of_block(qb, nqb), 0, 0))
    sink = pl.BlockSpec((None, 8, 128), lambda hp, qb: (hp, 0, 0))
    return q, [kwin(s) for s in range(nw)], ctx, bias, sink


def _attn_fwd(q, kpad, vpad, kc, vc, bias, sink, *, mode, n_ctx):
    t = q.shape[0]
    per_head = bias.ndim == 5
    qs, kws, ctx, bias_s, sink_s = _attn_specs(n_ctx, t // ATT_BLK, per_head, mode)
    has_sink = sink is not None
    nw = len(kws)

    def body(*refs):
        q_ref, k_refs, v_refs = refs[0], refs[1:1 + nw], refs[1 + nw:1 + 2 * nw]
        kc_ref, vc_ref, b_ref = refs[1 + 2 * nw:4 + 2 * nw]
        s_ref = refs[4 + 2 * nw] if has_sink else None
        o_ref = refs[-1]
        k3 = jnp.concatenate([r[...] for r in k_refs], axis=0)
        v3 = jnp.concatenate([r[...] for r in v_refs], axis=0)
        b0, b1 = (b_ref[0], b_ref[1]) if per_head else (b_ref[...], b_ref[...])
        o_ref[...] = _attn_block(q_ref[...], k3, v3, kc_ref[...], vc_ref[...], b0, b1,
                                 s_ref[...] if has_sink else None)

    in_specs = [qs] + kws + kws + [ctx, ctx, bias_s] + ([sink_s] if has_sink else [])
    args = [q] + [kpad] * nw + [vpad] * nw + [kc, vc, bias] + ([sink] if has_sink else [])
    return pl.pallas_call(
        body, name=mode + "_attn_fwd", grid=(4, t // ATT_BLK),
        in_specs=in_specs, out_specs=qs, out_shape=jax.ShapeDtypeStruct((t, 512), F32),
        compiler_params=_cparams(("parallel", "parallel")),
    )(*args)


def _attn_bwd(q, kpad, vpad, kc, vc, bias, sink, do, *, mode, n_ctx):
    t = q.shape[0]
    nqb = t // ATT_BLK
    per_head = bias.ndim == 5
    qs, kws, ctx, bias_s, sink_s = _attn_specs(n_ctx, nqb, per_head, mode)
    has_sink = sink is not None
    nw = len(kws)
    n_in = 5 + 2 * nw + has_sink

    def body(*refs):
        q_ref, k_refs, v_refs = refs[0], refs[1:1 + nw], refs[1 + nw:1 + 2 * nw]
        kc_ref, vc_ref, b_ref = refs[1 + 2 * nw:4 + 2 * nw]
        s_ref = refs[4 + 2 * nw] if has_sink else None
        do_ref = refs[n_in - 1]
        outs = list(refs[n_in:-2])
        dk_acc, dv_acc = refs[-2:]
        dq_ref, dkp_ref, dvp_ref, dkc_ref, dvc_ref = outs[:5]
        hp = pl.program_id(0)
        outs = outs[5:]
        db_ref = outs.pop(0) if per_head else None
        ds_ref = outs.pop(0) if has_sink else None
        qb = pl.program_id(1)
        up = lambda r: r[...].astype(F32)
        k3 = jnp.concatenate([up(r) for r in k_refs], axis=0)
        v3 = jnp.concatenate([up(r) for r in v_refs], axis=0)
        prim = [up(q_ref), k3, v3, up(kc_ref), up(vc_ref)]
        if per_head:
            prim += [b_ref[0], b_ref[1]]
        if has_sink:
            prim += [s_ref[...]]

        def fn(*a):
            a = list(a)
            qv, k3v, v3v, kcv, vcv = a[:5]
            a = a[5:]
            b0 = a.pop(0) if per_head else b_ref[...]
            b1 = a.pop(0) if per_head else b0
            sk = a.pop(0) if has_sink else None
            return _attn_block(qv, k3v, v3v, kcv, vcv, b0, b1, sk)

        _, vjp = jax.vjp(fn, *prim)
        grads = list(vjp(do_ref[...]))
        dq_ref[...] = grads[0]

        @pl.when(qb == 0)
        def _():
            dk_acc[...] = jnp.zeros_like(dk_acc)
            dv_acc[...] = jnp.zeros_like(dv_acc)
            dkc_ref[...] = jnp.zeros_like(dkc_ref)
            dvc_ref[...] = jnp.zeros_like(dvc_ref)
            if has_sink:
                ds_ref[...] = jnp.zeros_like(ds_ref)

        window = pl.ds(pl.multiple_of(jnp.maximum(qb - 1, 0) * ATT_BLK + WIN_GEOM[mode][2], 128), _win_rows(mode))
        dk_acc[window, :] += grads[1]
        dv_acc[window, :] += grads[2]

        @pl.when(qb == nqb - 1)
        def _():
            cols = pl.ds(pl.multiple_of(hp * 128, 128), 128)
            pltpu.sync_copy(dk_acc, dkp_ref.at[:, cols])
            pltpu.sync_copy(dv_acc, dvp_ref.at[:, cols])

        dkc_ref[...] += grads[3]
        dvc_ref[...] += grads[4]
        rest_g = grads[5:]
        if per_head:
            opens = (qb <= 2) | (qb == nqb - 1)
            g0, g1 = rest_g.pop(0), rest_g.pop(0)

            @pl.when(opens)
            def _():
                db_ref[0] = g0
                db_ref[1] = g1

            @pl.when(jnp.logical_not(opens))
            def _():
                db_ref[0] += g0
                db_ref[1] += g1

        if has_sink:
            ds_ref[...] += rest_g.pop(0)

    hbm = pl.BlockSpec(memory_space=pl.ANY)
    in_specs = [qs] + kws + kws + [ctx, ctx, bias_s] + ([sink_s] if has_sink else []) + [qs]
    args = [q] + [kpad] * nw + [vpad] * nw + [kc, vc, bias] + ([sink] if has_sink else []) + [do]
    out_specs = [qs, hbm, hbm, ctx, ctx] + ([bias_s] if per_head else []) + ([sink_s] if has_sink else [])
    out_shape = [jax.ShapeDtypeStruct((t, 512), F32),
                 jax.ShapeDtypeStruct(kpad.shape, F32), jax.ShapeDtypeStruct(kpad.shape, F32),
                 jax.ShapeDtypeStruct((n_ctx, 512), F32), jax.ShapeDtypeStruct((n_ctx, 512), F32)]
    if per_head:
        out_shape.append(jax.ShapeDtypeStruct(bias.shape, F32))
    if has_sink:
        out_shape.append(jax.ShapeDtypeStruct((4, 8, 128), F32))
    res = list(pl.pallas_call(
        body, name=mode + "_attn_bwd", grid=(4, nqb),
        in_specs=in_specs, out_specs=out_specs, out_shape=tuple(out_shape),
        scratch_shapes=[pltpu.VMEM((kpad.shape[0], 128), F32), pltpu.VMEM((kpad.shape[0], 128), F32)],
        compiler_params=_cparams(("arbitrary", "arbitrary")),
    )(*args))
    dq, dkp, dvp, dkc, dvc = res[:5]
    res = res[5:]
    dbias = res.pop(0) if per_head else None
    dsink = res.pop(0) if has_sink else None
    return dq, dkp, dvp, dkc, dvc, dbias, dsink


def _loss_head(xt, target, n_ctx):
    t = xt.shape[0]
    nct = n_ctx // TOK

    def body(x_ref, t_ref, l_ref, d_ref):
        i = pl.program_id(0)

        @pl.when(i == 0)
        def _():
            l_ref[...] = jnp.zeros_like(l_ref)

        @pl.when(i < nct)
        def _():
            d_ref[...] = jnp.zeros_like(d_ref)

        @pl.when(i >= nct)
        def _():
            err = x_ref[...] - t_ref[...]
            d_ref[...] = err * (1.0 / D_MODEL)
            l_ref[...] += jnp.sum(err * err, keepdims=True) * (0.5 / D_MODEL)

    return pl.pallas_call(
        body, name="loss_head", grid=(t // TOK,),
        in_specs=[_row_spec(D_MODEL), pl.BlockSpec((TOK, D_MODEL), lambda i: (jnp.maximum(i - nct, 0), 0))],
        out_specs=[_const_spec((8, 128)), _row_spec(D_MODEL)],
        out_shape=(jax.ShapeDtypeStruct((8, 128), F32), jax.ShapeDtypeStruct((t, D_MODEL), F32)),
        compiler_params=_cparams(("arbitrary",)),
    )(xt, target)


PACK_W = 1024
SUM_STEPS = 8


def _sum_chips(recvs):
    def split(a):
        rows = a.shape[1]
        if rows % (8 * SUM_STEPS):
            return None
        return rows // SUM_STEPS

    def body(*refs):
        n = len(refs) // 2
        for r_ref, o_ref in zip(refs[:n], refs[n:]):
            up = lambda s: r_ref[s].astype(F32)
            o_ref[...] = ((up(0) + up(1)) + up(2)) + up(3)

    in_specs, out_specs = [], []
    for a in recvs:
        rb, tail = split(a), a.shape[2:]
        zeros = (0,) * len(tail)
        if rb is None:
            in_specs.append(pl.BlockSpec(a.shape, functools.partial(lambda i, z: (0, 0) + z, z=zeros)))
            out_specs.append(pl.BlockSpec(a.shape[1:], functools.partial(lambda i, z: (0,) + z, z=zeros)))
        else:
            in_specs.append(pl.BlockSpec((4, rb) + tail, functools.partial(lambda i, z: (0, i) + z, z=zeros)))
            out_specs.append(pl.BlockSpec((rb,) + tail, functools.partial(lambda i, z: (i,) + z, z=zeros)))
    return pl.pallas_call(
        body, name="sum_chips", grid=(SUM_STEPS,),
        in_specs=in_specs, out_specs=out_specs,
        out_shape=tuple(jax.ShapeDtypeStruct(a.shape[1:], F32) for a in recvs),
        compiler_params=_cparams(("arbitrary",)),
    )(*recvs)


ADAM_BLOCK_BYTES = 1 << 20


def _adamw(p_a, p_b, w, m, v, name):
    layers, rows, cols = w.shape
    tr = rows
    while tr % 16 == 0 and tr * cols * 4 > ADAM_BLOCK_BYTES:
        tr //= 2
    c1 = 1.0 / (1.0 - ADAM_B1 ** ADAM_STEP)
    c2 = 1.0 / (1.0 - ADAM_B2 ** ADAM_STEP)

    def body(a_ref, b_ref, w_ref, m_ref, v_ref, g_ref, d_ref, nm_ref, nv_ref):
        g = a_ref[...] + b_ref[...]
        nm = ADAM_B1 * m_ref[...] + (1.0 - ADAM_B1) * g
        nv = ADAM_B2 * v_ref[...] + (1.0 - ADAM_B2) * (g * g)
        g_ref[...] = g
        nm_ref[...] = nm
        nv_ref[...] = nv
        d_ref[...] = -ADAM_LR * ((nm * c1) / (jnp.sqrt(nv * c2) + ADAM_EPS) + ADAM_WD * w_ref[...])

    spec = pl.BlockSpec((None, tr, cols), lambda l, i: (l, i, 0))
    return pl.pallas_call(
        body, name=name, grid=(layers, rows // tr),
        in_specs=[spec] * 5, out_specs=[spec] * 4,
        out_shape=tuple(jax.ShapeDtypeStruct(w.shape, F32) for _ in range(4)),
        compiler_params=_cparams(("parallel", "parallel")),
    )(p_a, p_b, w, m, v)


MESH = pl.DeviceIdType.MESH
ANY_SPEC = pl.BlockSpec(memory_space=pl.ANY)


def _chip_exchange(srcs, out_shapes, src_window, dst_window, name):
    n = len(srcs)

    def body(*refs):
        src_refs, out_refs = refs[:n], refs[n:2 * n]
        send_sems, recv_sems, local_sems = refs[2 * n:]
        x, y, c = lax.axis_index("x"), lax.axis_index("y"), lax.axis_index("c")
        me = 2 * x + y
        peers = [(x, 1 - y), (1 - x, y), (1 - x, 1 - y)]

        def copy(k, j, from_chip, to_chip):
            px, py = peers[j]
            return pltpu.make_async_remote_copy(
                src_ref=src_window(k, src_refs[k], to_chip), dst_ref=dst_window(k, out_refs[k], from_chip),
                send_sem=send_sems.at[3 * k + j], recv_sem=recv_sems.at[3 * k + j],
                device_id=(px, py, c), device_id_type=MESH)

        local = [pltpu.make_async_copy(src_window(k, src_refs[k], me), dst_window(k, out_refs[k], me),
                                       local_sems.at[k]) for k in range(n)]
        for cp in local:
            cp.start()
        sends = [copy(k, j, me, 2 * px + py) for k in range(n) for j, (px, py) in enumerate(peers)]
        for cp in sends:
            cp.start()
        for k in range(n):
            for j, (px, py) in enumerate(peers):
                copy(k, j, 2 * px + py, me).wait_recv()
        for cp in sends:
            cp.wait_send()
        for cp in local:
            cp.wait()

    return pl.pallas_call(
        body, name=name, in_specs=[ANY_SPEC] * n, out_specs=[ANY_SPEC] * n,
        out_shape=tuple(out_shapes),
        scratch_shapes=[pltpu.SemaphoreType.DMA((3 * n,)), pltpu.SemaphoreType.DMA((3 * n,)),
                        pltpu.SemaphoreType.DMA((n,))],
    )(*srcs)


def _core_swap(srcs):
    n = len(srcs)

    def body(*refs):
        src_refs, out_refs, send_sems, recv_sems = refs[:n], refs[n:2 * n], refs[2 * n], refs[2 * n + 1]
        x, y, c = lax.axis_index("x"), lax.axis_index("y"), lax.axis_index("c")
        copies = [pltpu.make_async_remote_copy(
            src_ref=src_refs[k], dst_ref=out_refs[k], send_sem=send_sems.at[k], recv_sem=recv_sems.at[k],
            device_id=(x, y, 1 - c), device_id_type=MESH) for k in range(n)]
        for cp in copies:
            cp.start()
        for cp in copies:
            cp.wait()

    return pl.pallas_call(
        body, name="core_swap", in_specs=[ANY_SPEC] * n, out_specs=[ANY_SPEC] * n,
        out_shape=tuple(jax.ShapeDtypeStruct(s.shape, s.dtype) for s in srcs),
        scratch_shapes=[pltpu.SemaphoreType.DMA((n,)), pltpu.SemaphoreType.DMA((n,))],
    )(*srcs)


def _col_window(ref, start, size):
    idx = (slice(None),) * (len(ref.shape) - 1) + (pl.ds(pl.multiple_of(start, 128), size),)
    return ref.at[idx]


def _row_window(ref, start, size):
    idx = (slice(None),) * (len(ref.shape) - 2) + (pl.ds(pl.multiple_of(start, 8), size), slice(None))
    return ref.at[idx]


N_SHARD_IN = 2624
WIN_W = 2944
WIN_START = (0, 2560, 5248, 7808)
WIN_PIECES = (((0, 2624),), ((64, 2688),), ((0, 640), (896, 2880)), ((320, 2944),))


def _core_layers(ref, lay, core):
    half = lay // 2
    return ref.at[pl.ds(core * half, half)]


def _gather_weights(w_ada, w_in, w_glu, w_br, w_out, conv_w):
    lay = w_ada.shape[0]
    assert lay % 2 == 0
    sizes = (768, None, 128, 256, 256, 128)

    def dst(k, ref, s, core=None):
        core = lax.axis_index("c") if core is None else core
        if k == 1:
            return _core_layers(ref.at[s], lay, core)
        ref = _core_layers(ref, lay, core)
        if k in (2, 4):
            return _row_window(ref, s * sizes[k], sizes[k])
        return _col_window(ref, s * sizes[k], sizes[k])

    shapes = (jax.ShapeDtypeStruct((lay, D_MODEL, 3 * D_MODEL), w_ada.dtype),
              jax.ShapeDtypeStruct((4,) + w_in.shape, w_in.dtype),
              jax.ShapeDtypeStruct((lay, MIX_W, MIX_W), w_glu.dtype),
              jax.ShapeDtypeStruct((lay, 4, MIX_W, D_MODEL), w_br.dtype),
              jax.ShapeDtypeStruct((lay, D_MODEL, D_MODEL), w_out.dtype),
              jax.ShapeDtypeStruct((lay, 8, MIX_W), conv_w.dtype))
    halves = _chip_exchange((w_ada, w_in, w_glu, w_br, w_out, conv_w), shapes,
                            lambda k, ref, t: _core_layers(ref, lay, lax.axis_index("c")), dst, "gather_weights")
    return _merge_core_halves(halves, lay)


def _merge_core_halves(halves, lay):
    n = len(halves)

    def body(*refs):
        out_refs, send_sems, recv_sems = refs[n:2 * n], refs[2 * n], refs[2 * n + 1]
        x, y, c = lax.axis_index("x"), lax.axis_index("y"), lax.axis_index("c")

        def part(k, core):
            if k == 1:
                half = lay // 2
                return out_refs[k].at[:, pl.ds(core * half, half)]
            return _core_layers(out_refs[k], lay, core)

        def copy(k, landing_core):
            return pltpu.make_async_remote_copy(
                src_ref=part(k, c), dst_ref=part(k, landing_core), send_sem=send_sems.at[k],
                recv_sem=recv_sems.at[k], device_id=(x, y, 1 - c), device_id_type=MESH)

        give = [copy(k, c) for k in range(n)]
        for cp in give:
            cp.start()
        for k in range(n):
            copy(k, 1 - c).wait_recv()
        for cp in give:
            cp.wait_send()

    return pl.pallas_call(
        body, name="gather_merge", in_specs=[ANY_SPEC] * n, out_specs=[ANY_SPEC] * n,
        out_shape=tuple(jax.ShapeDtypeStruct(h.shape, h.dtype) for h in halves),
        input_output_aliases={k: k for k in range(n)},
        scratch_shapes=[pltpu.SemaphoreType.DMA((n,)), pltpu.SemaphoreType.DMA((n,))],
    )(*halves)


def _scatter_grads(dw_ada, dw_in, dw_glu, dw_br, dw_out, dconv_w, small):
    def src(k, ref, t):
        if k == 0:
            return _col_window(ref, t * 768, 768)
        if k == 1:
            start = jnp.where(t == 0, WIN_START[0], jnp.where(t == 1, WIN_START[1],
                              jnp.where(t == 2, WIN_START[2], WIN_START[3])))
            return _col_window(ref, start, WIN_W)
        if k == 2:
            return _row_window(ref, t * 128, 128)
        if k == 3:
            return _col_window(ref, t * 256, 256)
        if k == 4:
            return _row_window(ref, t * 256, 256)
        if k == 5:
            return _col_window(ref, t * 128, 128)
        return ref

    pieces = ((D_MODEL, 768), (D_MODEL, WIN_W), (128, MIX_W), (4, MIX_W, 256), (256, D_MODEL), (8, 128), small.shape)
    srcs = (dw_ada, dw_in, dw_glu, dw_br, dw_out, dconv_w, small)
    shapes = tuple(jax.ShapeDtypeStruct((4,) + p, s.dtype) for p, s in zip(pieces, srcs))
    return _chip_exchange(srcs, shapes,
                          src, lambda k, ref, s: ref.at[s], "scatter_grads")


def _s5_tables(a_re, a_im, log_dt, b_re, b_im, c_re, c_im):
    ln = S5_CHUNK
    hi = lax.Precision.HIGHEST
    dt = jnp.exp(log_dt)[..., None]
    mag = jnp.exp(dt * a_re)
    abr = mag * jnp.cos(dt * a_im)
    abi = mag * jnp.sin(dt * a_im)
    den = a_re * a_re + a_im * a_im
    fr = ((abr - 1.0) * a_re + abi * a_im) / den
    fi = (abi * a_re - (abr - 1.0) * a_im) / den
    bbr = fr[..., None] * b_re - fi[..., None] * b_im
    bbi = fr[..., None] * b_im + fi[..., None] * b_re
    n = jnp.arange(ln + 1, dtype=F32)[:, None, None, None]
    pm = jnp.exp(n * dt * a_re)
    er = pm * jnp.cos(n * dt * a_im)
    ei = pm * jnp.sin(n * dt * a_im)
    e3 = lambda e, b, c: jnp.einsum("tdgp,dgpa,dgbp->dgabt", e, b, c, precision=hi)
    gt = e3(er[:ln], bbr, c_re) - e3(er[:ln], bbi, c_im) - e3(ei[:ln], bbr, c_im) - e3(ei[:ln], bbi, c_re)
    by_dir = lambda fwd, bwd: jnp.stack([fwd[:, 0], bwd[:, 1]], axis=1)
    erj, eij = by_dir(er[:ln][::-1], er[:ln]), by_dir(ei[:ln][::-1], ei[:ln])
    e2 = lambda e, b: jnp.einsum("jdgp,dgpa->dgajp", e, b, precision=hi)
    w = jnp.concatenate([e2(erj, bbr) - e2(eij, bbi), e2(erj, bbi) + e2(eij, bbr)], axis=-1)
    er1, ei1 = by_dir(er[1:], er[1:][::-1]), by_dir(ei[1:], ei[1:][::-1])
    ev = lambda c, e: jnp.einsum("dgbp,idgp->dgpbi", c, e, precision=hi)
    v = jnp.concatenate([ev(c_re, er1) - ev(c_im, ei1), -(ev(c_re, ei1) + ev(c_im, er1))], axis=2)
    a1 = jnp.concatenate([er[ln], er[ln]], axis=-1)
    a2 = jnp.concatenate([-ei[ln], ei[ln]], axis=-1)
    return (gt.transpose(1, 2, 3, 0, 4).reshape(S5_GROUPS, 256, 2 * ln),
            w.transpose(1, 2, 3, 0, 4).reshape(S5_GROUPS, S5_CH * ln, 256),
            v.transpose(1, 0, 2, 3, 4).reshape(S5_GROUPS, 256, S5_CH * ln),
            a1.reshape(64, 128), a2.reshape(64, 128))


def _lag_onehot():
    ln = S5_CHUNK
    j, i = np.meshgrid(np.arange(ln), np.arange(ln), indexing="ij")
    lag = np.arange(ln)[:, None, None]
    z = np.concatenate([lag == (i - j)[None], lag == (j - i)[None]], axis=0).astype(np.float32)
    return jnp.broadcast_to(jnp.asarray(z.reshape(2 * ln, ln * ln), BF16), (S5_GROUPS, 2 * ln, ln * ln))


def _toeplitz(gt):
    ln = S5_CHUNK
    flat = _matmul(gt, _lag_onehot(), out_dtype=BF16, name="s5_toeplitz")
    return (flat.reshape(S5_GROUPS, S5_CH, S5_CH, ln, ln).transpose(0, 1, 3, 2, 4)
            .reshape(S5_GROUPS, S5_CH * ln, S5_CH * ln))


def _toeplitz_fold(dk):
    ln = S5_CHUNK
    flat = dk.reshape(S5_GROUPS, S5_CH, ln, S5_CH, ln).transpose(0, 1, 3, 2, 4).reshape(S5_GROUPS, 256, ln * ln)
    return _matmul(flat, _lag_onehot(), trans_b=True, name="s5_toeplitz_fold")


def _chunk_rows(t):
    k = t // S5_CHUNK
    return k, -(-k // 128) * 128


def _to_chunks(u):
    k, kp = _chunk_rows(u.shape[0])
    v = u.reshape(k, S5_CHUNK, S5_GROUPS, S5_CH).transpose(2, 0, 3, 1).reshape(S5_GROUPS, k, S5_CH * S5_CHUNK)
    return jnp.pad(v, ((0, 0), (0, kp - k), (0, 0)))


def _from_chunks(y, t):
    k, _ = _chunk_rows(t)
    return y[:, :k].reshape(S5_GROUPS, k, S5_CH, S5_CHUNK).transpose(1, 3, 0, 2).reshape(t, MIX_W)


def _states_to_rows(s):
    kp = s.shape[1]
    return s.reshape(S5_GROUPS, kp, 2, 128).transpose(1, 2, 0, 3).reshape(kp, 64, 128)


def _rows_to_states(h):
    kp = h.shape[0]
    return h.reshape(kp, 2, S5_GROUPS, 128).transpose(2, 0, 1, 3).reshape(S5_GROUPS, kp, 256)


def _na_bias(rel_bias):
    a, m = np.meshgrid(np.arange(4), np.arange(12), indexing="ij")
    di = np.clip(m - a + 3, 0, 2 * NA_ROWS - 2).reshape(-1)
    qc, kc = np.meshgrid(np.arange(GRID_W), np.arange(GRID_W), indexing="ij")
    dj = np.clip(kc - qc + NA_COLS - 1, 0, 2 * NA_COLS - 2).reshape(-1)
    oh_i = jnp.asarray(di[:, None] == np.arange(2 * NA_ROWS - 1)[None, :], F32)
    oh_j = jnp.asarray(dj[:, None] == np.arange(2 * NA_COLS - 1)[None, :], F32)
    hi = lax.Precision.HIGHEST
    cols = jnp.einsum("hij,cj->hic", rel_bias, oh_j, precision=hi)
    full = jnp.einsum("ri,hic->hrc", oh_i, cols, precision=hi)
    full = full.reshape(N_HEADS, 4, 12, GRID_W, GRID_W).transpose(0, 1, 3, 2, 4)
    return full.reshape(4, 2, ATT_BLK, 3 * ATT_BLK)


def _rope_tables(n_ctx, n_lat):
    tok = jnp.arange(n_lat, dtype=jnp.int32)
    row = (tok // GRID_W).astype(F32)
    col = (tok % GRID_W).astype(F32)
    inv = ROPE_BASE ** (-jnp.arange(ROPE_PAIRS, dtype=F32) / ROPE_PAIRS)
    ang = jnp.concatenate([row[:, None] * inv, col[:, None] * inv], axis=-1)
    cos, sin = jnp.cos(ang), jnp.sin(ang)
    cos = jnp.tile(jnp.concatenate([cos, cos], axis=-1), (1, 2))
    sin = jnp.tile(jnp.concatenate([-sin, sin], axis=-1), (1, 2))
    return (jnp.concatenate([jnp.ones((n_ctx, 128), F32), cos], axis=0),
            jnp.concatenate([jnp.zeros((n_ctx, 128), F32), sin], axis=0))


def _pad_blocks(a, n_ctx):
    return jnp.pad(a[n_ctx:], ((ATT_BLK, ATT_BLK), (0, 0)))


def _layer_fwd(xt, cc, w, rope, n_ctx):
    t = xt.shape[0]
    sv = {}
    mod = _adaln_fwd(cc, w["w_ada"], w["b_ada"].reshape(1, -1))
    mod4 = mod[:2].reshape(2, 3, 1, D_MODEL)
    h = _modnorm_fwd(xt, w["norm_g"].reshape(1, -1), mod4, n_ctx)
    z = _matmul(h, w["w_in"], name="proj_fwd")

    s5_args = (w["s5_a_re"], w["s5_a_im"], w["s5_log_dt"], w["s5_b_re"], w["s5_b_im"], w["s5_c_re"], w["s5_c_im"])
    (gt, tw, tv, a1, a2), tab_vjp = jax.vjp(_s5_tables, *s5_args)
    ktoe = _toeplitz(gt)
    tw, tv = tw.astype(BF16), tv.astype(BF16)
    uc = _to_chunks(z[:, :MIX_W].astype(BF16))
    st = _matmul(uc, tw, name="s5_chunk_state")
    hprev = _s5_scan_fwd(_states_to_rows(st), a1, a2, t // S5_CHUNK, n_ctx // S5_CHUNK)
    uh = jnp.concatenate([uc, _rows_to_states(hprev).astype(BF16)], axis=2)
    ysum = _from_chunks(_matmul(uh, jnp.concatenate([ktoe, tv], axis=1), name="s5_chunk_out"), t)
    s5_d = w["s5_d"].reshape(1, MIX_W)
    y_s5 = _s5post_fwd(ysum, z, s5_d, w["s5_w_glu"])

    conv_w = w["conv_w"]
    y_conv = _conv_fwd(z, conv_w, w["conv_b"].reshape(1, -1), n_ctx)

    gains = (jnp.tile(w["na_q_g"], 8)[None], jnp.tile(w["na_k_g"], 8)[None],
             jnp.tile(w["gqa_q_g"], 8)[None], jnp.tile(w["gqa_k_g"], 2)[None])
    q_na, k_na, v_na, q_g, k_g, v_g = _prep_fwd(z, gains, rope)
    bias, bias_vjp = jax.vjp(_na_bias, w["na_rel_bias"])
    sink = jnp.zeros((4, 8, 128), F32).at[:, :2, :].set(
        jnp.broadcast_to(w["gqa_sink"].reshape(4, 2, 1), (4, 2, 128)))
    na_tab = jnp.where(_window_patterns("na", t - n_ctx)[:, None, None], bias[None], NEG_INF)
    gqa_tab = jnp.where(_window_patterns("gqa", t - n_ctx), 0.0, NEG_INF).astype(F32)
    na_in = (q_na, _pad_blocks(k_na, n_ctx), _pad_blocks(v_na, n_ctx), k_na[:n_ctx], v_na[:n_ctx], na_tab, None)
    gqa_in = (q_g, _pad_blocks(k_g, n_ctx), _pad_blocks(v_g, n_ctx), k_g[:n_ctx], v_g[:n_ctx], gqa_tab, sink)
    y_na = _attn_fwd(*na_in, mode="na", n_ctx=n_ctx)
    y_gqa = _attn_fwd(*gqa_in, mode="gqa", n_ctx=n_ctx)
    ys = (y_s5, y_conv, y_na, y_gqa)
    xt_new = _merge_fwd(xt, ys, z, mod4, w["w_br"], w["w_out"], n_ctx)
    sv.update(xt=xt, mod4=mod4, h=h, z=z, tab_vjp=tab_vjp, ktoe=ktoe, tw=tw, tv=tv, a1=a1, a2=a2, uc=uc,
              hprev=hprev, uh=uh, ysum=ysum, s5_d=s5_d, conv_w=conv_w, gains=gains, bias_vjp=bias_vjp,
              na_in=na_in, gqa_in=gqa_in, ys=ys)
    return xt_new, sv


def _layer_bwd(dxt_new, sv, cc, w, rope, n_ctx):
    t = dxt_new.shape[0]
    z, mod4 = sv["z"], sv["mod4"]
    dys, dgt, dmg, dgate, dw_br, dw_out = _merge_bwd(dxt_new, sv["ys"], z, mod4, w["w_br"], w["w_out"], n_ctx)

    dpre, du_skip, dd, dw_glu = _s5post_bwd(sv["ysum"], z, sv["s5_d"], w["s5_w_glu"], dys[0])
    dyc = _to_chunks(dpre)
    dhp = _matmul(dyc, sv["tv"], trans_b=True, name="s5_bwd_state")
    ds, da1, da2 = _s5_scan_bwd(_states_to_rows(dhp), sv["hprev"], sv["a1"], sv["a2"],
                                t // S5_CHUNK, n_ctx // S5_CHUNK)
    ds = _rows_to_states(ds).astype(BF16)
    duc = _matmul(jnp.concatenate([dyc, ds], axis=2), jnp.concatenate([sv["ktoe"], sv["tw"]], axis=2),
                  trans_b=True, out_dtype=BF16, name="s5_bwd_u")
    dkv = _matmul(sv["uh"].transpose(0, 2, 1), dyc, out_dtype=BF16, name="s5_bwd_kv")
    dtw = _matmul(sv["uc"].transpose(0, 2, 1), ds, name="s5_bwd_w")
    dgt_tab = _toeplitz_fold(dkv[:, :S5_CH * S5_CHUNK])
    s5_grads = sv["tab_vjp"]((dgt_tab, dtw, dkv[:, S5_CH * S5_CHUNK:].astype(F32), da1, da2))
    du_scan = _from_chunks(duc, t)

    dzv, dzb, dzc, dconv_w, dconv_b = _conv_bwd(z, sv["conv_w"], w["conv_b"].reshape(1, -1), dys[1], n_ctx)

    dq_na, dk_na, dv_na, dkc_na, dvc_na, dbias, _ = _attn_bwd(*sv["na_in"], dys[2], mode="na", n_ctx=n_ctx)
    dq_g, dk_g, dv_g, dkc_g, dvc_g, _, dsink = _attn_bwd(*sv["gqa_in"], dys[3], mode="gqa", n_ctx=n_ctx)
    pb = _prep_bwd(z, sv["gains"], rope, (dq_na, dq_g), (dk_na, dv_na, dk_g, dv_g),
                   (dkc_na, dvc_na, dkc_g, dvc_g), du_skip, du_scan, n_ctx)
    dz_naq, dz_nak, dz_nav, dz_gq, dz_gk, dz_gv, dz_u, dg_naq, dg_nak, dg_gq, dg_gk = pb

    dz = jnp.concatenate([dz_u, dgt[0], dzv, dzb, dzc, dgt[1], dz_naq, dz_nak, dz_nav, dgt[2], dz_gq, dz_gk, dz_gv,
                          jnp.zeros((t, OFF["gqa_gate"] - OFF["pad"]), BF16), dgt[3], *dmg], axis=1)
    dh = _matmul(dz, w["w_in"], trans_b=True, name="proj_bwd_x")
    dw_in = _matmul(sv["h"].T, dz, out_dtype=BF16, name="proj_bwd_w")
    dxt, dnorm_g, dshift, dscale = _modnorm_bwd(sv["xt"], w["norm_g"].reshape(1, -1), mod4, dh, dxt_new, n_ctx)
    dmod = jnp.concatenate([dshift, dscale, dgate], axis=1).reshape(2, 3 * D_MODEL)
    dcc, dw_ada, db_ada = _adaln_bwd(cc, w["w_ada"], jnp.pad(dmod, ((0, 6), (0, 0))))

    (drel,) = sv["bias_vjp"](dbias.sum(0))
    grads = dict(
        norm_g=dnorm_g[0], w_ada=dw_ada, b_ada=db_ada[0], w_in=dw_in,
        s5_a_re=s5_grads[0], s5_a_im=s5_grads[1], s5_log_dt=s5_grads[2], s5_b_re=s5_grads[3], s5_b_im=s5_grads[4],
        s5_c_re=s5_grads[5], s5_c_im=s5_grads[6], s5_d=dd.reshape(S5_GROUPS, S5_CH), s5_w_glu=dw_glu,
        conv_w=dconv_w, conv_b=dconv_b[0],
        na_q_g=dg_naq.reshape(8, HEAD_DIM).sum(0), na_k_g=dg_nak.reshape(8, HEAD_DIM).sum(0), na_rel_bias=drel,
        gqa_q_g=dg_gq.reshape(8, HEAD_DIM).sum(0), gqa_k_g=dg_gk.reshape(2, HEAD_DIM).sum(0),
        gqa_sink=dsink[:, :2, :].sum(-1).reshape(8), w_br=dw_br, w_out=dw_out)
    return dxt, dcc, grads


SHARDED = ("w_ada", "w_in", "s5_w_glu", "conv_w", "w_br", "w_out")
REPLICATED = ("norm_g", "b_ada", "s5_a_re", "s5_a_im", "s5_log_dt", "s5_b_re", "s5_b_im", "s5_c_re", "s5_c_im",
              "s5_d", "conv_b", "na_q_g", "na_k_g", "na_rel_bias", "gqa_q_g", "gqa_k_g", "gqa_sink")
WEIGHTS = ("c_ctx", "norm_g", "w_ada", "b_ada", "w_in", "s5_a_re", "s5_a_im", "s5_log_dt", "s5_b_re", "s5_b_im",
           "s5_c_re", "s5_c_im", "s5_d", "s5_w_glu", "conv_w", "conv_b", "na_q_g", "na_k_g", "na_rel_bias",
           "gqa_q_g", "gqa_k_g", "gqa_sink", "w_br", "w_out")


def _pack(pieces, row_multiple, dtype):
    flat = jnp.concatenate([p.reshape(-1).astype(dtype) for p in pieces])
    rows = -(-flat.shape[0] // PACK_W)
    rows = -(-rows // row_multiple) * row_multiple
    return jnp.pad(flat, (0, rows * PACK_W - flat.shape[0])).reshape(rows, PACK_W)


def _unpack(buf, shapes):
    flat = buf.reshape(-1)
    out, pos = [], 0
    for shp in shapes:
        size = int(np.prod(shp))
        out.append(flat[pos:pos + size].reshape(shp))
        pos += size
    return out


def _local_step(x, ctx, target, c_vec, c_ctx, layers):
    depth = len(layers)
    n_ctx, n_lat = ctx.shape[0], x.shape[0]
    cc = jnp.zeros((8, D_MODEL), F32).at[0].set(c_ctx).at[1].set(c_vec)
    rope = _rope_tables(n_ctx, n_lat)
    xt = jnp.concatenate([ctx, x], axis=0)
    saved = []
    for l in range(depth):
        xt, sv = _layer_fwd(xt, cc, layers[l], rope, n_ctx)
        saved.append(sv)
    loss_tile, dxt = _loss_head(xt, target, n_ctx)
    grads = [None] * depth
    dc_ctx = jnp.zeros((D_MODEL,), F32)
    for l in reversed(range(depth)):
        dxt, dcc, grads[l] = _layer_bwd(dxt, saved[l], cc, layers[l], rope, n_ctx)
        dc_ctx = dc_ctx + dcc[0]
    return loss_tile[0, 0], dxt[n_ctx:][None], dc_ctx, grads


def kernel(x, c, ctx, c_ctx, norm_g, w_ada, b_ada, w_in, s5_a_re, s5_a_im, s5_log_dt, s5_b_re, s5_b_im,
           s5_c_re, s5_c_im, s5_d, s5_w_glu, conv_w, conv_b, na_q_g, na_k_g, na_rel_bias, gqa_q_g,
           gqa_k_g, gqa_sink, w_br, w_out, loss_target, m_c_ctx, m_norm_g, m_w_ada, m_b_ada, m_w_in,
           m_s5_a_re, m_s5_a_im, m_s5_log_dt, m_s5_b_re, m_s5_b_im, m_s5_c_re, m_s5_c_im, m_s5_d,
           m_s5_w_glu, m_conv_w, m_conv_b, m_na_q_g, m_na_k_g, m_na_rel_bias, m_gqa_q_g, m_gqa_k_g,
           m_gqa_sink, m_w_br, m_w_out, v_c_ctx, v_norm_g, v_w_ada, v_b_ada, v_w_in, v_s5_a_re,
           v_s5_a_im, v_s5_log_dt, v_s5_b_re, v_s5_b_im, v_s5_c_re, v_s5_c_im, v_s5_d, v_s5_w_glu,
           v_conv_w, v_conv_b, v_na_q_g, v_na_k_g, v_na_rel_bias, v_gqa_q_g, v_gqa_k_g, v_gqa_sink,
           v_w_br, v_w_out):
    a = dict(locals())
    depth = a["norm_g"].shape[0]
    x, ctx, target = a["x"][0], a["ctx"][0], a["loss_target"][0]
    n_ctx, n_lat = ctx.shape[0], x.shape[0]
    assert n_ctx % ATT_BLK == 0 and n_lat % (4 * GRID_W) == 0 and n_lat // GRID_W >= NA_ROWS

    cast = lambda n: a[n].astype(BF16)
    conv8 = jnp.pad(a["conv_w"], ((0, 0), (0, 5), (0, 0)))
    g_ada, g_in, g_glu, g_br, g_out, g_conv = _gather_weights(
        cast("w_ada"), cast("w_in"), cast("s5_w_glu"), cast("w_br"), cast("w_out"), conv8)
    zpad = jnp.zeros((D_MODEL, OFF["gqa_gate"] - OFF["pad"]), BF16)
    split = OFF["pad"] - 2 * N_SHARD_IN
    layers = []
    for l in range(depth):
        w = {n: a[n][l] for n in REPLICATED}
        w.update(w_ada=g_ada[l], s5_w_glu=g_glu[l], w_br=g_br[l], w_out=g_out[l], conv_w=g_conv[l])
        w["w_in"] = jnp.concatenate([g_in[0, l], g_in[1, l], g_in[2, l][:, :split], zpad, g_in[2, l][:, split:],
                                     g_in[3, l]], axis=1)
        layers.append(w)

    loss_local, grad_x, dc_ctx, grads = _local_step(x, ctx, target, a["c"][0], a["c_ctx"], layers)
    loss = lax.psum(loss_local, ("x", "y", "c"))

    chip = 2 * lax.axis_index("x") + lax.axis_index("y")
    take = [functools.partial(lambda win, pc: jnp.concatenate([win[:, lo:hi] for lo, hi in pc], axis=1), pc=pc)
            for pc in WIN_PIECES]

    def small_pack(values, c_ctx_value, l):
        pieces = [values[n] for n in REPLICATED]
        pieces.append(c_ctx_value if l == 0 else jnp.zeros((D_MODEL,), F32))
        return _pack(pieces, 8 * SUM_STEPS, F32)

    mine, theirs = [], []
    for l in range(depth):
        g = grads[l]
        small = small_pack(g, dc_ctx, l)
        recv = _scatter_grads(g["w_ada"], g["w_in"], g["s5_w_glu"], g["w_br"], g["w_out"], g["conv_w"], small)
        part = list(_sum_chips([r.reshape(4, -1, r.shape[-1]) for r in recv]))
        part[1] = lax.switch(chip, take, part[1])
        mine.append(part)
        theirs.append(_core_swap(part))

    families = ("w_ada", "w_in", "s5_w_glu", "w_br", "w_out", "conv_w")
    out = {}
    for k, n in enumerate(families):
        p, q = jnp.stack([m[k] for m in mine]), jnp.stack([t[k] for t in theirs])
        if n == "conv_w":
            p, q = p[:, :3], q[:, :3]
        as3d = lambda arr: arr.reshape(depth, -1, arr.shape[-1])
        res = _adamw(p, q, as3d(a[n]), as3d(a["m_" + n]), as3d(a["v_" + n]), "adamw_" + n)
        out[n] = [r.reshape(a[n].shape) for r in res]
    p, q = jnp.stack([m[6] for m in mine]), jnp.stack([t[6] for t in theirs])
    packs = [jnp.stack([small_pack({n: a[pre + n][l] for n in REPLICATED}, a[pre + "c_ctx"], l)
                        for l in range(depth)]) for pre in ("", "m_", "v_")]
    res = _adamw(p, q, *packs, "adamw_small")
    shapes = [a[n].shape[1:] for n in REPLICATED] + [a["c_ctx"].shape]
    per_layer = [[_unpack(r[l], shapes) for l in range(depth)] for r in res]
    for j, n in enumerate(REPLICATED):
        out[n] = [jnp.stack([per_layer[key][l][j] for l in range(depth)]) for key in range(4)]
    out["c_ctx"] = [per_layer[key][0][-1] for key in range(4)]
    results = [loss, grad_x]
    for key in range(4):
        results += [out[n][key] for n in WEIGHTS]
    return tuple(results)
```

```python
import functools

import numpy as np
import jax
import jax.numpy as jnp
from jax import lax
from jax.experimental import pallas as pl
from jax.experimental.pallas import tpu as pltpu

F32 = jnp.float32
BF16 = jnp.bfloat16

D_MODEL = 1024
MIX_W = 512
GRID_W = 64
HEAD_DIM = 64
N_HEADS = 8
S5_GROUPS = 32
S5_CH = 16
S5_CHUNK = 32
NA_ROWS = 8
NA_COLS = 16
WINDOW = 128
ROPE_BASE = 10000.0
ROPE_PAIRS = 16
EPS = 1e-6
NEG_INF = -1e30
ATT_BLK = 256
TOK = 256
VMEM_LIMIT = 56 * 1024 * 1024

ADAM_LR, ADAM_B1, ADAM_B2, ADAM_EPS, ADAM_WD, ADAM_STEP = 0.001, 0.9, 0.999, 1e-8, 0.01, 10

OFF = dict(s5_u=0, s5_gate=512, conv_v=1024, conv_b=1536, conv_c=2048, conv_gate=2560,
           na_q=3072, na_k=3584, na_v=4096, na_gate=4608, gqa_q=5120, gqa_k=5632, gqa_v=5760,
           pad=5888, gqa_gate=6144, merge_s5=6656, merge_conv=7680, merge_na=8704, merge_gqa=9728)
N_Z = 10752
GATE_OFFS = (OFF["s5_gate"], OFF["conv_gate"], OFF["na_gate"], OFF["gqa_gate"])
MERGE_OFFS = (OFF["merge_s5"], OFF["merge_conv"], OFF["merge_na"], OFF["merge_gqa"])


def _cparams(sem):
    return pltpu.CompilerParams(dimension_semantics=sem, vmem_limit_bytes=VMEM_LIMIT)


def _dot(a, b, ca, cb):
    return lax.dot_general(a.astype(BF16), b.astype(BF16), (((ca,), (cb,)), ((), ())),
                           preferred_element_type=F32)


def _dot_tn(a, b):
    return _dot(a.astype(F32).T, b, 1, 0)


@jax.custom_vjp
def mm(a, b):
    return _dot(a, b, 1, 0)


@jax.custom_vjp
def mm_nt(a, b):
    return _dot(a, b, 1, 1)


@jax.custom_vjp
def mm_tn(a, b):
    return _dot_tn(a, b)


mm.defvjp(lambda a, b: (mm(a, b), (a, b)), lambda r, g: (mm_nt(g, r[1]), mm_tn(r[0], g)))
mm_nt.defvjp(lambda a, b: (mm_nt(a, b), (a, b)), lambda r, g: (mm(g, r[1]), mm_tn(g, r[0])))
mm_tn.defvjp(lambda a, b: (mm_tn(a, b), (a, b)), lambda r, g: (mm_nt(r[1], g), mm(r[0], g)))


@functools.partial(jax.custom_vjp, nondiff_argnums=(1,))
def lane_roll(x, shift):
    return pltpu.roll(x, shift, 1)


lane_roll.defvjp(lambda x, shift: (lane_roll(x, shift), None),
                 lambda shift, _, g: (lane_roll(g, (g.shape[1] - shift) % g.shape[1]),))


def _silu(x):
    return x * jax.nn.sigmoid(x)


def _dsilu(x):
    s = jax.nn.sigmoid(x)
    return s * (1.0 + x * (1.0 - s))


def _pick(n, prefs):
    for p in prefs:
        if n % p == 0:
            return p
    return n


def _matmul(a, b, *, trans_b=False, out_dtype=F32, tm=None, tn=None, tk=None, name):
    squeeze = a.ndim == 2
    if squeeze:
        a, b = a[None], b[None]
    nb, m, k = a.shape
    n = b.shape[1] if trans_b else b.shape[2]
    tm = tm or _pick(m, (1280, 1024, 640, 512, 256, 128))
    tn = tn or _pick(n, (1536, 1024, 512, 256, 128))
    tk = tk or _pick(k, (3584, 3328, 1536, 1280, 1024, 768, 640, 512, 256, 128))
    nk = k // tk

    def body(a_ref, b_ref, o_ref, *scr):
        part = _dot(a_ref[...], b_ref[...], 1, 1 if trans_b else 0)
        if nk == 1:
            o_ref[...] = part.astype(out_dtype)
        else:
            acc = scr[0]
            kk = pl.program_id(3)

            @pl.when(kk == 0)
            def _():
                acc[...] = part

            @pl.when(kk > 0)
            def _():
                acc[...] += part

            @pl.when(kk == nk - 1)
            def _():
                o_ref[...] = acc[...].astype(out_dtype)

    if trans_b:
        b_spec = pl.BlockSpec((None, tn, tk), lambda bb, i, j, kk: (bb, j, kk))
    else:
        b_spec = pl.BlockSpec((None, tk, tn), lambda bb, i, j, kk: (bb, kk, j))
    out = pl.pallas_call(
        body, name=name,
        grid=(nb, m // tm, n // tn, nk),
        in_specs=[pl.BlockSpec((None, tm, tk), lambda bb, i, j, kk: (bb, i, kk)), b_spec],
        out_specs=pl.BlockSpec((None, tm, tn), lambda bb, i, j, kk: (bb, i, j)),
        out_shape=jax.ShapeDtypeStruct((nb, m, n), out_dtype),
        scratch_shapes=[] if nk == 1 else [pltpu.VMEM((tm, tn), F32)],
        compiler_params=_cparams(("parallel", "parallel", "parallel", "arbitrary")),
    )(a, b)
    return out[0] if squeeze else out


def _adaln_fn(cc, w, b):
    return mm(_silu(cc), w) + b


def _adaln_fwd(cc, w_ada, b_ada):
    def body(cc_ref, w_ref, b_ref, o_ref):
        o_ref[...] = _adaln_fn(cc_ref[...], w_ref[...], b_ref[...])

    return pl.pallas_call(
        body, name="adaln_fwd", out_shape=jax.ShapeDtypeStruct((8, 3 * D_MODEL), F32),
        compiler_params=pltpu.CompilerParams(vmem_limit_bytes=VMEM_LIMIT),
    )(cc, w_ada, b_ada)


def _adaln_bwd(cc, w_ada, dmod):
    def body(cc_ref, w_ref, g_ref, dcc_ref, dw_ref, db_ref):
        cc_v, g = cc_ref[...], g_ref[...]
        dw_ref[...] = mm_tn(_silu(cc_v), g).astype(BF16)
        db_ref[...] = jnp.sum(g, axis=0, keepdims=True)
        dcc_ref[...] = mm_nt(g, w_ref[...]) * _dsilu(cc_v)

    return pl.pallas_call(
        body, name="adaln_bwd",
        out_shape=(jax.ShapeDtypeStruct((8, D_MODEL), F32),
                   jax.ShapeDtypeStruct((D_MODEL, 3 * D_MODEL), BF16),
                   jax.ShapeDtypeStruct((1, 3 * D_MODEL), F32)),
        compiler_params=pltpu.CompilerParams(vmem_limit_bytes=VMEM_LIMIT),
    )(cc, w_ada, dmod)


def _seg_spec(which, n_ctx_tiles):
    return pl.BlockSpec((None, None, 1, D_MODEL),
                        lambda i: (jnp.where(i < n_ctx_tiles, 0, 1), which, 0, 0))


def _row_spec(width, col_block=0, tile=TOK):
    return pl.BlockSpec((tile, width), lambda i: (i, col_block))


def _const_spec(shape):
    zeros = (0,) * len(shape)
    return pl.BlockSpec(shape, lambda i: zeros)


def _modnorm_fn(x, g, shift, scale):
    y = x * lax.rsqrt(jnp.mean(x * x, axis=-1, keepdims=True) + EPS)
    return (y * g) * (1.0 + scale) + shift


def _modnorm_fwd(xt, g, mod4, n_ctx):
    t = xt.shape[0]
    nct = n_ctx // TOK

    def body(x_ref, g_ref, sh_ref, sc_ref, o_ref):
        o_ref[...] = _modnorm_fn(x_ref[...], g_ref[...], sh_ref[...], sc_ref[...]).astype(BF16)

    return pl.pallas_call(
        body, name="modnorm_fwd", grid=(t // TOK,),
        in_specs=[_row_spec(D_MODEL), _const_spec((1, D_MODEL)), _seg_spec(0, nct), _seg_spec(1, nct)],
        out_specs=_row_spec(D_MODEL),
        out_shape=jax.ShapeDtypeStruct((t, D_MODEL), BF16),
        compiler_params=_cparams(("parallel",)),
    )(xt, g, mod4, mod4)


def _modnorm_bwd(xt, g, mod4, dh, dres, n_ctx):
    t = xt.shape[0]
    nct = n_ctx // TOK

    def body(x_ref, g_ref, sh_ref, sc_ref, dh_ref, dres_ref, dx_ref, dg_ref, dsh_ref, dsc_ref):
        i = pl.program_id(0)
        _, vjp = jax.vjp(_modnorm_fn, x_ref[...], g_ref[...], sh_ref[...], sc_ref[...])
        dx, dg, dsh, dsc = vjp(dh_ref[...])
        dx_ref[...] = dx + dres_ref[...]

        @pl.when(i == 0)
        def _():
            dg_ref[...] = jnp.zeros_like(dg_ref)

        dg_ref[...] += dg
        first = jnp.logical_or(i == 0, i == nct)

        @pl.when(first)
        def _():
            dsh_ref[...] = dsh
            dsc_ref[...] = dsc

        @pl.when(jnp.logical_not(first))
        def _():
            dsh_ref[...] += dsh
            dsc_ref[...] += dsc

    seg_out = lambda which: pl.BlockSpec((None, None, 1, D_MODEL),
                                         lambda i: (jnp.where(i < nct, 0, 1), which, 0, 0))
    dx, dg, dss, dss2 = pl.pallas_call(
        body, name="modnorm_bwd", grid=(t // TOK,),
        in_specs=[_row_spec(D_MODEL), _const_spec((1, D_MODEL)), _seg_spec(0, nct), _seg_spec(1, nct),
                  _row_spec(D_MODEL), _row_spec(D_MODEL)],
        out_specs=[_row_spec(D_MODEL), _const_spec((1, D_MODEL)), seg_out(0), seg_out(0)],
        out_shape=(jax.ShapeDtypeStruct((t, D_MODEL), F32), jax.ShapeDtypeStruct((1, D_MODEL), F32),
                   jax.ShapeDtypeStruct((2, 1, 1, D_MODEL), F32), jax.ShapeDtypeStruct((2, 1, 1, D_MODEL), F32)),
        compiler_params=_cparams(("arbitrary",)),
    )(xt, g, mod4, mod4, dh, dres)
    return dx, dg, dss, dss2


def _group_mean_sq(x, gs):
    x2 = x * x
    hi = x2.astype(BF16).astype(F32)
    return mm(hi, gs) + mm(x2 - hi, gs)


def _head_norm(x, g, gs):
    return (x * lax.rsqrt(_group_mean_sq(x, gs) + EPS)) * g


def _rope(x, cos, sin_signed):
    lane = lax.broadcasted_iota(jnp.int32, (1, 128), 1)
    first_half = jnp.bitwise_and(lane, 63) < 32
    cols = []
    for c in range(x.shape[1] // 128):
        xb = x[:, 128 * c:128 * (c + 1)]
        partner = jnp.where(first_half, lane_roll(xb, 96), lane_roll(xb, 32))
        cols.append(xb * cos + partner * sin_signed)
    return cols[0] if len(cols) == 1 else jnp.concatenate(cols, axis=1)


def _prep_fn(zq_na, zk_na, zv_na, zq_g, zk_g, zv_g, g_naq, g_nak, g_gq, g_gk, cos, sin_signed, gs512, gs128, expand):
    q_na = _head_norm(zq_na, g_naq, gs512)
    k_na = _head_norm(zk_na, g_nak, gs512)
    q_g = _rope(_head_norm(zq_g, g_gq, gs512), cos, sin_signed)
    k_g = _rope(_head_norm(zk_g, g_gk, gs128), cos, sin_signed)
    return q_na, k_na, zv_na, q_g, mm(k_g, expand), mm(zv_g, expand)


def _prep_consts():
    gid = np.arange(512) // 64
    gs512 = (gid[:, None] == gid[None, :]).astype(np.float32) / 64.0
    expand = np.zeros((128, 512), np.float32)
    for h in range(N_HEADS):
        for j in range(64):
            expand[64 * (h // 4) + j, 64 * h + j] = 1.0
    return jnp.asarray(gs512), jnp.asarray(gs512[:128, :128]), jnp.asarray(expand)


def _prep_in_specs():
    blk = lambda off, w: _row_spec(w, off // w)
    return [blk(OFF["na_q"], 512), blk(OFF["na_k"], 512), blk(OFF["na_v"], 512), blk(OFF["gqa_q"], 512),
            blk(OFF["gqa_k"], 128), blk(OFF["gqa_v"], 128),
            _const_spec((1, 512)), _const_spec((1, 512)), _const_spec((1, 512)), _const_spec((1, 128)),
            _row_spec(128), _row_spec(128),
            _const_spec((512, 512)), _const_spec((128, 128)), _const_spec((128, 512))]


def _prep_fwd(z, gains, rope_tabs):
    t = z.shape[0]
    consts = _prep_consts()

    def body(*refs):
        ins, outs = refs[:15], refs[15:]
        res = _prep_fn(*[r[...] for r in ins])
        for o_ref, v in zip(outs, res):
            o_ref[...] = v.astype(BF16)

    return pl.pallas_call(
        body, name="prep_fwd", grid=(t // TOK,),
        in_specs=_prep_in_specs(),
        out_specs=[_row_spec(512)] * 6,
        out_shape=tuple(jax.ShapeDtypeStruct((t, 512), BF16) for _ in range(6)),
        compiler_params=_cparams(("parallel",)),
    )(z, z, z, z, z, z, *gains, *rope_tabs, *consts)


def _prep_bwd(z, gains, rope_tabs, dqs, dkv_lat, dkv_ctx, du_a, du_b, n_ctx):
    t = z.shape[0]
    nct = n_ctx // TOK
    consts = _prep_consts()

    def body(*refs):
        ins, dq_refs, lat_refs, ctx_refs = refs[:15], refs[15:17], refs[17:21], refs[21:25]
        (dua_ref, dub_ref), outs = refs[25:27], refs[27:]
        i = pl.program_id(0)
        vals = [r[...] for r in ins]
        _, vjp = jax.vjp(lambda *a: _prep_fn(*a, *vals[10:]), *vals[:10])
        kv = [jnp.where(i < nct, c_ref[...], l_ref[...]) for l_ref, c_ref in zip(lat_refs, ctx_refs)]
        grads = vjp((dq_refs[0][...], kv[0], kv[1], dq_refs[1][...], kv[2], kv[3]))
        for o_ref, v in zip(outs[:6], grads[:6]):
            o_ref[...] = v.astype(BF16)
        outs[6][...] = (dua_ref[...] + dub_ref[...]).astype(BF16)

        @pl.when(i == 0)
        def _():
            for o_ref in outs[7:]:
                o_ref[...] = jnp.zeros_like(o_ref)

        for o_ref, v in zip(outs[7:], grads[6:10]):
            o_ref[...] += v

    lat_spec = pl.BlockSpec((TOK, 512), lambda i: (jnp.maximum(i - nct + 1, 0), 0))
    ctx_spec = pl.BlockSpec((TOK, 512), lambda i: (jnp.minimum(i, nct - 1), 0))
    return pl.pallas_call(
        body, name="prep_bwd", grid=(t // TOK,),
        in_specs=_prep_in_specs() + [_row_spec(512)] * 2 + [lat_spec] * 4 + [ctx_spec] * 4 + [_row_spec(512)] * 2,
        out_specs=[_row_spec(512)] * 4 + [_row_spec(128)] * 2 + [_row_spec(512)]
        + [_const_spec((1, 512))] * 3 + [_const_spec((1, 128))],
        out_shape=tuple([jax.ShapeDtypeStruct((t, 512), BF16)] * 4 + [jax.ShapeDtypeStruct((t, 128), BF16)] * 2
                        + [jax.ShapeDtypeStruct((t, 512), BF16)]
                        + [jax.ShapeDtypeStruct((1, 512), F32)] * 3 + [jax.ShapeDtypeStruct((1, 128), F32)]),
        compiler_params=_cparams(("arbitrary",)),
    )(z, z, z, z, z, z, *gains, *rope_tabs, *consts, *dqs, *dkv_lat, *dkv_ctx, du_a, du_b)


def _s5post_fn(ys, u, d, w_glu):
    y = jax.nn.gelu(ys + d * u)
    return y * jax.nn.sigmoid(mm(y, w_glu))


def _s5post_fwd(ys, z, d, w_glu):
    t = z.shape[0]

    def body(ys_ref, u_ref, d_ref, w_ref, o_ref):
        o_ref[...] = _s5post_fn(ys_ref[...], u_ref[...], d_ref[...], w_ref[...])

    return pl.pallas_call(
        body, name="s5post_fwd", grid=(t // TOK,),
        in_specs=[_row_spec(512), _row_spec(512, OFF["s5_u"] // 512),
                  _const_spec((1, 512)), _const_spec((512, 512))],
        out_specs=_row_spec(512), out_shape=jax.ShapeDtypeStruct((t, 512), F32),
        compiler_params=_cparams(("parallel",)),
    )(ys, z, d, w_glu)


def _s5post_bwd(ys, z, d, w_glu, dy):
    t = z.shape[0]

    def body(ys_ref, u_ref, d_ref, w_ref, dy_ref, dpre_ref, du_ref, dd_ref, dw_ref):
        i = pl.program_id(0)
        _, vjp = jax.vjp(_s5post_fn, ys_ref[...], u_ref[...], d_ref[...], w_ref[...].astype(F32))
        dys, du, dd, dw = vjp(dy_ref[...])
        dpre_ref[...] = dys.astype(BF16)
        du_ref[...] = du

        @pl.when(i == 0)
        def _():
            dd_ref[...] = jnp.zeros_like(dd_ref)
            dw_ref[...] = jnp.zeros_like(dw_ref)

        dd_ref[...] += dd
        dw_ref[...] += dw

    return pl.pallas_call(
        body, name="s5post_bwd", grid=(t // TOK,),
        in_specs=[_row_spec(512), _row_spec(512, OFF["s5_u"] // 512),
                  _const_spec((1, 512)), _const_spec((512, 512)), _row_spec(512)],
        out_specs=[_row_spec(512), _row_spec(512), _const_spec((1, 512)), _const_spec((512, 512))],
        out_shape=(jax.ShapeDtypeStruct((t, 512), BF16), jax.ShapeDtypeStruct((t, 512), F32),
                   jax.ShapeDtypeStruct((1, 512), F32), jax.ShapeDtypeStruct((512, 512), F32)),
        compiler_params=_cparams(("arbitrary",)),
    )(ys, z, d, w_glu, dy)


def _halo_specs(col_block, t):
    last = t // 8 - 1
    prev = pl.BlockSpec((8, 512), lambda i: (jnp.maximum(i * (TOK // 8) - 1, 0), col_block))
    nxt = pl.BlockSpec((8, 512), lambda i: (jnp.minimum((i + 1) * (TOK // 8), last), col_block))
    return [_row_spec(512, col_block), prev, nxt]


def _shifted(cur, prev_row, next_row, tok0, n_ctx, t_total):
    row = lax.broadcasted_iota(jnp.int32, (TOK, 1), 0)
    tpos = row + tok0
    down = jnp.where(row == 0, prev_row, pltpu.roll(cur, 1, 0))
    down = jnp.where(jnp.logical_or(tpos == 0, tpos == n_ctx), 0.0, down)
    up = jnp.where(row == TOK - 1, next_row, pltpu.roll(cur, TOK - 1, 0))
    up = jnp.where(jnp.logical_or(tpos == n_ctx - 1, tpos == t_total - 1), 0.0, up)
    return down, up


def _conv_fwd(z, conv_w, conv_b, n_ctx):
    t = z.shape[0]

    def body(v_ref, vp_ref, vn_ref, c_ref, cp_ref, cn_ref, b_ref, w_ref, cb_ref, o_ref):
        tok0 = pl.program_id(0) * TOK
        zz = v_ref[...] * c_ref[...]
        zz_m1, zz_p1 = _shifted(zz, vp_ref[7:8, :] * cp_ref[7:8, :], vn_ref[0:1, :] * cn_ref[0:1, :], tok0, n_ctx, t)
        s = cb_ref[...] + zz_m1 * w_ref[0:1, :] + zz * w_ref[1:2, :] + zz_p1 * w_ref[2:3, :]
        o_ref[...] = b_ref[...] * s

    return pl.pallas_call(
        body, name="conv_fwd", grid=(t // TOK,),
        in_specs=_halo_specs(OFF["conv_v"] // 512, t) + _halo_specs(OFF["conv_c"] // 512, t)
        + [_row_spec(512, OFF["conv_b"] // 512), _const_spec((8, 512)), _const_spec((1, 512))],
        out_specs=_row_spec(512), out_shape=jax.ShapeDtypeStruct((t, 512), F32),
        compiler_params=_cparams(("parallel",)),
    )(z, z, z, z, z, z, z, conv_w, conv_b)


def _conv_bwd(z, conv_w, conv_b, dy, n_ctx):
    t = z.shape[0]

    def body(v_ref, vp_ref, vn_ref, c_ref, cp_ref, cn_ref, b_ref, bp_ref, bn_ref, dy_ref, dyp_ref, dyn_ref,
             w_ref, cb_ref, dv_ref, db_ref, dc_ref, dw_ref, dcb_ref):
        i = pl.program_id(0)
        tok0 = i * TOK
        v, c, b, dy_v = v_ref[...], c_ref[...], b_ref[...], dy_ref[...]
        w0, w1, w2 = w_ref[0:1, :], w_ref[1:2, :], w_ref[2:3, :]
        zz = v * c
        zz_m1, zz_p1 = _shifted(zz, vp_ref[7:8, :] * cp_ref[7:8, :], vn_ref[0:1, :] * cn_ref[0:1, :], tok0, n_ctx, t)
        s = cb_ref[...] + zz_m1 * w0 + zz * w1 + zz_p1 * w2
        ds = dy_v * b
        ds_m1, ds_p1 = _shifted(ds, dyp_ref[7:8, :] * bp_ref[7:8, :], dyn_ref[0:1, :] * bn_ref[0:1, :], tok0, n_ctx, t)
        dzz = ds_p1 * w0 + ds * w1 + ds_m1 * w2
        db_ref[...] = (dy_v * s).astype(BF16)
        dv_ref[...] = (dzz * c).astype(BF16)
        dc_ref[...] = (dzz * v).astype(BF16)

        @pl.when(i == 0)
        def _():
            dw_ref[...] = jnp.zeros_like(dw_ref)
            dcb_ref[...] = jnp.zeros_like(dcb_ref)

        rsum = lambda a: jnp.sum(a, axis=0, keepdims=True)
        dw_ref[0:1, :] += rsum(ds * zz_m1)
        dw_ref[1:2, :] += rsum(ds * zz)
        dw_ref[2:3, :] += rsum(ds * zz_p1)
        dcb_ref[...] += rsum(ds)

    return pl.pallas_call(
        body, name="conv_bwd", grid=(t // TOK,),
        in_specs=_halo_specs(OFF["conv_v"] // 512, t) + _halo_specs(OFF["conv_c"] // 512, t)
        + _halo_specs(OFF["conv_b"] // 512, t) + _halo_specs(0, t) + [_const_spec((8, 512)), _const_spec((1, 512))],
        out_specs=[_row_spec(512)] * 3 + [_const_spec((8, 512)), _const_spec((1, 512))],
        out_shape=tuple([jax.ShapeDtypeStruct((t, 512), BF16)] * 3
                        + [jax.ShapeDtypeStruct((8, 512), F32), jax.ShapeDtypeStruct((1, 512), F32)]),
        compiler_params=_cparams(("arbitrary",)),
    )(z, z, z, z, z, z, z, z, z, dy, dy, dy, conv_w, conv_b)


def _merge_col_specs(tile):
    specs = []
    for off in MERGE_OFFS:
        specs.append(pl.BlockSpec((tile, 512), functools.partial(lambda i, cb: (i, cb), cb=off // 512)))
        specs.append(pl.BlockSpec((tile, 512), functools.partial(lambda i, cb: (i, cb), cb=off // 512 + 1)))
    return specs


def _merge_fwd(xt, ys, z, mod4, w_br, w_out, n_ctx):
    t = xt.shape[0]
    nct = n_ctx // TOK

    def body(x_ref, *refs):
        y_refs, gt_refs, mg_refs = refs[0:4], refs[4:8], refs[8:16]
        gate_ref, wbr_ref, wout_ref, o_ref = refs[16:20]
        acc_lo = acc_hi = None
        for k in range(4):
            gated = y_refs[k][...] * _silu(gt_refs[k][...])
            proj = mm(gated, wbr_ref[k])
            lo = jax.nn.sigmoid(mg_refs[2 * k][...]) * proj[:, :512]
            hi = jax.nn.sigmoid(mg_refs[2 * k + 1][...]) * proj[:, 512:]
            acc_lo = lo if acc_lo is None else acc_lo + lo
            acc_hi = hi if acc_hi is None else acc_hi + hi
        acc = jnp.concatenate([acc_lo, acc_hi], axis=1)
        o_ref[...] = x_ref[...] + gate_ref[...] * mm(acc, wout_ref[...])

    gate_specs = [pl.BlockSpec((TOK, 512), functools.partial(lambda i, cb: (i, cb), cb=o // 512)) for o in GATE_OFFS]
    return pl.pallas_call(
        body, name="merge_fwd", grid=(t // TOK,),
        in_specs=[_row_spec(D_MODEL)] + [_row_spec(512)] * 4 + gate_specs + _merge_col_specs(TOK)
        + [_seg_spec(2, nct), _const_spec((4, 512, 1024)), _const_spec((1024, 1024))],
        out_specs=_row_spec(D_MODEL), out_shape=jax.ShapeDtypeStruct((t, D_MODEL), F32),
        compiler_params=_cparams(("parallel",)),
    )(xt, *ys, z, z, z, z, z, z, z, z, z, z, z, z, mod4, w_br, w_out)


MERGE_BWD_TILE = 128


def _merge_bwd(g, ys, z, mod4, w_br, w_out, n_ctx):
    t = g.shape[0]
    tile = MERGE_BWD_TILE
    nct = n_ctx // tile
    nsteps = t // tile

    def body(g_ref, *refs):
        y_refs, gt_refs, mg_refs = refs[0:4], refs[4:8], refs[8:16]
        gate_ref, wbr_hbm, wout_hbm = refs[16:19]
        dy_refs, dgt_refs, dmg_refs = refs[19:23], refs[23:27], refs[27:31]
        dgate_ref, dwbr_hbm, dwout_hbm = refs[31:34]
        wbr_v, wout_v, dwbr_acc, dwout_acc = refs[34:38]
        i = pl.program_id(0)

        @pl.when(i == 0)
        def _():
            pltpu.sync_copy(wbr_hbm, wbr_v)
            pltpu.sync_copy(wout_hbm, wout_v)
            dwbr_acc[...] = jnp.zeros_like(dwbr_acc)
            dwout_acc[...] = jnp.zeros_like(dwout_acc)

        g_v, gate = g_ref[...], gate_ref[...]
        gated, proj, sig = [], [], []
        acc = None
        for k in range(4):
            gated.append(y_refs[k][...] * _silu(gt_refs[k][...]))
            proj.append(mm(gated[k], wbr_v[k]))
            sig.append(jax.nn.sigmoid(jnp.concatenate([mg_refs[2 * k][...], mg_refs[2 * k + 1][...]], axis=1)))
            contrib = sig[k] * proj[k]
            acc = contrib if acc is None else acc + contrib
        o = mm(acc, wout_v[...])
        dgate = jnp.sum(g_v * o, axis=0, keepdims=True)
        first = jnp.logical_or(i == 0, i == nct)

        @pl.when(first)
        def _():
            dgate_ref[...] = dgate

        @pl.when(jnp.logical_not(first))
        def _():
            dgate_ref[...] += dgate

        do = g_v * gate
        dwout_acc[...] += mm_tn(acc, do)
        dacc = mm_nt(do, wout_v[...])
        for k in range(4):
            dmg_refs[k][...] = (dacc * proj[k] * sig[k] * (1.0 - sig[k])).astype(BF16)
            dproj = dacc * sig[k]
            dwbr_acc[k] += mm_tn(gated[k], dproj)
            dgated = mm_nt(dproj, wbr_v[k])
            gt = gt_refs[k][...]
            dy_refs[k][...] = dgated * _silu(gt)
            dgt_refs[k][...] = (dgated * y_refs[k][...] * _dsilu(gt)).astype(BF16)

        @pl.when(i == nsteps - 1)
        def _():
            wbr_v[...] = dwbr_acc[...].astype(BF16)
            wout_v[...] = dwout_acc[...].astype(BF16)
            pltpu.sync_copy(wbr_v, dwbr_hbm)
            pltpu.sync_copy(wout_v, dwout_hbm)

    row = lambda w: _row_spec(w, 0, tile)
    gate_specs = [pl.BlockSpec((tile, 512), functools.partial(lambda i, cb: (i, cb), cb=o // 512)) for o in GATE_OFFS]
    anyspec = pl.BlockSpec(memory_space=pl.ANY)
    seg = pl.BlockSpec((None, None, 1, D_MODEL), lambda i: (jnp.where(i < nct, 0, 1), 2, 0, 0))
    seg_out = pl.BlockSpec((None, None, 1, D_MODEL), lambda i: (jnp.where(i < nct, 0, 1), 0, 0, 0))
    res = pl.pallas_call(
        body, name="merge_bwd", grid=(nsteps,),
        in_specs=[row(D_MODEL)] + [row(512)] * 4 + gate_specs + _merge_col_specs(tile) + [seg, anyspec, anyspec],
        out_specs=[row(512)] * 8 + [row(1024)] * 4 + [seg_out, anyspec, anyspec],
        out_shape=tuple([jax.ShapeDtypeStruct((t, 512), F32)] * 4 + [jax.ShapeDtypeStruct((t, 512), BF16)] * 4
                        + [jax.ShapeDtypeStruct((t, 1024), BF16)] * 4
                        + [jax.ShapeDtypeStruct((2, 1, 1, D_MODEL), F32),
                           jax.ShapeDtypeStruct((4, 512, 1024), BF16), jax.ShapeDtypeStruct((1024, 1024), BF16)]),
        scratch_shapes=[pltpu.VMEM((4, 512, 1024), BF16), pltpu.VMEM((1024, 1024), BF16),
                        pltpu.VMEM((4, 512, 1024), F32), pltpu.VMEM((1024, 1024), F32)],
        compiler_params=_cparams(("arbitrary",)),
    )(g, *ys, z, z, z, z, z, z, z, z, z, z, z, z, mod4, w_br, w_out)
    return res[0:4], res[4:8], res[8:12], res[12], res[13], res[14]


FWD_ROWS = slice(0, S5_GROUPS)
BWD_ROWS = slice(S5_GROUPS, 2 * S5_GROUPS)


def _backward_chunk(j, k, n_ctx_chunks):
    return jnp.where(j < n_ctx_chunks, n_ctx_chunks - 1 - j, k - 1 - (j - n_ctx_chunks))


def _scan_call(body, name, n_hbm_in, out_shape, kp):
    hbm, vmem = pl.BlockSpec(memory_space=pl.ANY), pl.BlockSpec(memory_space=pltpu.VMEM)
    return pl.pallas_call(
        body, name=name, in_specs=[hbm] * n_hbm_in + [vmem, vmem],
        out_specs=[hbm] + [vmem] * (len(out_shape) - 1), out_shape=out_shape,
        scratch_shapes=[pltpu.VMEM((kp, 64, 128), F32), pltpu.VMEM((kp, 64, 128), F32)],
        compiler_params=pltpu.CompilerParams(vmem_limit_bytes=VMEM_LIMIT))


def _complex_step(a1, a2, h):
    return a1 * h + a2 * pltpu.roll(h, 64, 1)


def _s5_scan_fwd(s, a1, a2, k, n_ctx_chunks):
    kp = s.shape[0]

    def body(s_hbm, a1_ref, a2_ref, hp_hbm, s_v, hp_v):
        pltpu.sync_copy(s_hbm, s_v)
        if kp > k:
            hp_v[k:kp] = jnp.zeros((kp - k, 64, 128), F32)
        a1f, a2f, a1b, a2b = a1_ref[FWD_ROWS, :], a2_ref[FWD_ROWS, :], a1_ref[BWD_ROWS, :], a2_ref[BWD_ROWS, :]

        def step(j, carry):
            hf, hb = carry
            cb = _backward_chunk(j, k, n_ctx_chunks)
            hp_v[j, FWD_ROWS, :] = hf
            hp_v[cb, BWD_ROWS, :] = hb
            return (_complex_step(a1f, a2f, hf) + s_v[j, FWD_ROWS, :],
                    _complex_step(a1b, a2b, hb) + s_v[cb, BWD_ROWS, :])

        zero = jnp.zeros((S5_GROUPS, 128), F32)
        lax.fori_loop(0, k, step, (zero, zero))
        pltpu.sync_copy(hp_v, hp_hbm)

    return _scan_call(body, "s5_scan_fwd", 1, (jax.ShapeDtypeStruct(s.shape, F32),), kp)(s, a1, a2)[0]


def _s5_scan_bwd(dhp, hp, a1, a2, k, n_ctx_chunks):
    kp = hp.shape[0]

    def body(dhp_hbm, hp_hbm, a1_ref, a2_ref, ds_hbm, da1_ref, da2_ref, g_v, hp_v):
        pltpu.sync_copy(dhp_hbm, g_v)
        pltpu.sync_copy(hp_hbm, hp_v)
        if kp > k:
            g_v[k:kp] = jnp.zeros((kp - k, 64, 128), F32)
        coef_f = (a1_ref[FWD_ROWS, :], a2_ref[FWD_ROWS, :])
        coef_b = (a1_ref[BWD_ROWS, :], a2_ref[BWD_ROWS, :])

        def one(rows, c, lam, d1, d2):
            a1_v, a2_v = coef_f if rows is FWD_ROWS else coef_b
            dh_in = g_v[c, rows, :]
            g_v[c, rows, :] = lam
            h = hp_v[c, rows, :]
            return (dh_in + a1_v * lam + pltpu.roll(a2_v * lam, 64, 1),
                    d1 + lam * h, d2 + lam * pltpu.roll(h, 64, 1))

        def step(j, carry):
            f, b = carry
            jj = k - 1 - j
            return one(FWD_ROWS, jj, *f), one(BWD_ROWS, _backward_chunk(jj, k, n_ctx_chunks), *b)

        zero = jnp.zeros((S5_GROUPS, 128), F32)
        f, b = lax.fori_loop(0, k, step, ((zero, zero, zero), (zero, zero, zero)))
        da1_ref[FWD_ROWS, :], da2_ref[FWD_ROWS, :] = f[1], f[2]
        da1_ref[BWD_ROWS, :], da2_ref[BWD_ROWS, :] = b[1], b[2]
        pltpu.sync_copy(g_v, ds_hbm)

    shapes = (jax.ShapeDtypeStruct(hp.shape, F32), jax.ShapeDtypeStruct((64, 128), F32),
              jax.ShapeDtypeStruct((64, 128), F32))
    return _scan_call(body, "s5_scan_bwd", 2, shapes, kp)(dhp, hp, a1, a2)


WIN_GEOM = {"na": (ATT_BLK, 3, 0), "gqa": (128, 4, 128)}


def _win_rows(mode):
    kb, nw, _ = WIN_GEOM[mode]
    return kb * nw


@functools.lru_cache(maxsize=None)
def _window_patterns(mode, n_lat):
    nb = n_lat // ATT_BLK
    assert nb >= 3
    first_key = WIN_GEOM[mode][2] - ATT_BLK
    iq, ik = np.arange(ATT_BLK)[:, None], np.arange(_win_rows(mode))[None, :]

    def valid(ql):
        tq, ts = ATT_BLK * ql + iq, ATT_BLK * ql + first_key + ik
        if mode == "na":
            r, qcol, kr, kcol = tq // GRID_W, tq % GRID_W, ts // GRID_W, ts % GRID_W
            rs = np.clip(r - NA_ROWS // 2, 0, n_lat // GRID_W - NA_ROWS)
            cs = np.clip(qcol - NA_COLS // 2, 0, GRID_W - NA_COLS)
            return (kr >= rs) & (kr < rs + NA_ROWS) & (kcol >= cs) & (kcol < cs + NA_COLS)
        return (np.abs(tq - ts) <= WINDOW) & (ts >= 0) & (ts < n_lat)

    interior = valid(1)
    assert all(np.array_equal(valid(ql), interior) for ql in range(1, nb - 1))
    return np.stack([valid(0), interior, valid(nb - 1), np.zeros_like(interior)])


Q_HALVES = (slice(0, ATT_BLK // 2), slice(ATT_BLK // 2, ATT_BLK))


def _pattern_of_block(qb, nqb):
    return jnp.where(qb == 0, 3, jnp.where(qb == 1, 0, jnp.where(qb == nqb - 1, 2, 1)))


def _attn_block(q, k3, v3, kc, vc, bias0, bias1, sink):
    lane = lax.broadcasted_iota(jnp.int32, (1, 128), 1)
    scale = HEAD_DIM ** -0.5
    outs = []
    for e, bias in enumerate((bias0, bias1)):
        in_head = (lane < 64) if e == 0 else (lane >= 64)
        qe = jnp.where(in_head, q, 0.0)
        s_lat = mm_nt(qe, k3) * scale + bias
        s_ctx = mm_nt(qe, kc) * scale
        mx = jnp.maximum(jnp.max(s_lat, axis=1, keepdims=True), jnp.max(s_ctx, axis=1, keepdims=True))
        if sink is not None:
            srow = lax.broadcasted_iota(jnp.int32, sink.shape, 0)
            sv = jnp.sum(jnp.where(srow == e, sink, 0.0), keepdims=True) * (1.0 / 128.0)
            mx = jnp.maximum(mx, sv)
        mx = lax.stop_gradient(mx)
        e_lat = jnp.exp(s_lat - mx)
        e_ctx = jnp.exp(s_ctx - mx)
        den = jnp.sum(e_lat, axis=1, keepdims=True) + jnp.sum(e_ctx, axis=1, keepdims=True)
        if sink is not None:
            den = den + jnp.exp(sv - mx)
        inv = 1.0 / den
        outs.append(mm(e_lat * inv, v3) + mm(e_ctx * inv, vc))
    return jnp.where(lane < 64, outs[0], outs[1])


def _attn_specs(n_ctx, nqb, per_head, mode):
    kb, nw, skip = WIN_GEOM[mode]

    def kwin(s):
        return pl.BlockSpec(
            (kb, 128), lambda hp, qb: (jnp.maximum(qb - 1, 0) * (ATT_BLK // kb) + skip // kb + s, hp))

    q = pl.BlockSpec((ATT_BLK, 128), lambda hp, qb: (qb, hp))
    ctx = pl.BlockSpec((n_ctx, 128), lambda hp, qb: (0, hp))
    if per_head:
        bias = pl.BlockSpec((None, None, 2, ATT_BLK, kb * nw),
                            lambda hp, qb: (_pattern_of_block(qb, nqb), hp, 0, 0, 0))
    else:
        bias = pl.BlockSpec((None, ATT_BLK, kb * nw), lambda hp, qb: (_pattern_of_block(qb, nqb), 0, 0))
    sink = pl.BlockSpec((None, 8, 128), lambda hp, qb: (hp, 0, 0))
    return q, [kwin(s) for s in range(nw)], ctx, bias, sink


def _attn_fwd(q, kpad, vpad, kc, vc, bias, sink, *, mode, n_ctx):
    t = q.shape[0]
    per_head = bias.ndim == 5
    qs, kws, ctx, bias_s, sink_s = _attn_specs(n_ctx, t // ATT_BLK, per_head, mode)
    has_sink = sink is not None
    nw = len(kws)

    def body(*refs):
        q_ref, k_refs, v_refs = refs[0], refs[1:1 + nw], refs[1 + nw:1 + 2 * nw]
        kc_ref, vc_ref, b_ref = refs[1 + 2 * nw:4 + 2 * nw]
        s_ref = refs[4 + 2 * nw] if has_sink else None
        o_ref = refs[-1]
        k3 = jnp.concatenate([r[...] for r in k_refs], axis=0)
        v3 = jnp.concatenate([r[...] for r in v_refs], axis=0)
        for rows in Q_HALVES:
            b0, b1 = (b_ref[0, rows, :], b_ref[1, rows, :]) if per_head else (b_ref[rows, :], b_ref[rows, :])
            o_ref[rows, :] = _attn_block(q_ref[rows, :], k3, v3, kc_ref[...], vc_ref[...], b0, b1,
                                         s_ref[...] if has_sink else None)

    in_specs = [qs] + kws + kws + [ctx, ctx, bias_s] + ([sink_s] if has_sink else [])
    args = [q] + [kpad] * nw + [vpad] * nw + [kc, vc, bias] + ([sink] if has_sink else [])
    return pl.pallas_call(
        body, name=mode + "_attn_fwd", grid=(4, t // ATT_BLK),
        in_specs=in_specs, out_specs=qs, out_shape=jax.ShapeDtypeStruct((t, 512), F32),
        compiler_params=_cparams(("parallel", "parallel")),
    )(*args)


def _attn_bwd(q, kpad, vpad, kc, vc, bias, sink, do, *, mode, n_ctx):
    t = q.shape[0]
    nqb = t // ATT_BLK
    per_head = bias.ndim == 5
    qs, kws, ctx, bias_s, sink_s = _attn_specs(n_ctx, nqb, per_head, mode)
    has_sink = sink is not None
    nw = len(kws)
    n_in = 5 + 2 * nw + has_sink

    def body(*refs):
        q_ref, k_refs, v_refs = refs[0], refs[1:1 + nw], refs[1 + nw:1 + 2 * nw]
        kc_ref, vc_ref, b_ref = refs[1 + 2 * nw:4 + 2 * nw]
        s_ref = refs[4 + 2 * nw] if has_sink else None
        do_ref = refs[n_in - 1]
        outs = list(refs[n_in:-2])
        dk_acc, dv_acc = refs[-2:]
        dq_ref, dkp_ref, dvp_ref, dkc_ref, dvc_ref = outs[:5]
        hp = pl.program_id(0)
        outs = outs[5:]
        db_ref = outs.pop(0) if per_head else None
        ds_ref = outs.pop(0) if has_sink else None
        qb = pl.program_id(1)
        up = lambda r: r[...].astype(F32)
        k3 = jnp.concatenate([up(r) for r in k_refs], axis=0)
        v3 = jnp.concatenate([up(r) for r in v_refs], axis=0)
        kc_v, vc_v = up(kc_ref), up(vc_ref)

        @pl.when(qb == 0)
        def _():
            dk_acc[...] = jnp.zeros_like(dk_acc)
            dv_acc[...] = jnp.zeros_like(dv_acc)
            dkc_ref[...] = jnp.zeros_like(dkc_ref)
            dvc_ref[...] = jnp.zeros_like(dvc_ref)
            if has_sink:
                ds_ref[...] = jnp.zeros_like(ds_ref)

        window = pl.ds(pl.multiple_of(jnp.maximum(qb - 1, 0) * ATT_BLK + WIN_GEOM[mode][2], 128), _win_rows(mode))
        opens = (qb <= 2) | (qb == nqb - 1)
        for rows in Q_HALVES:
            prim = [q_ref[rows, :].astype(F32), k3, v3, kc_v, vc_v]
            if per_head:
                prim += [b_ref[0, rows, :], b_ref[1, rows, :]]
            if has_sink:
                prim += [s_ref[...]]

            def fn(*a, rows=rows):
                a = list(a)
                qv, k3v, v3v, kcv, vcv = a[:5]
                a = a[5:]
                b0 = a.pop(0) if per_head else b_ref[rows, :]
                b1 = a.pop(0) if per_head else b0
                sk = a.pop(0) if has_sink else None
                return _attn_block(qv, k3v, v3v, kcv, vcv, b0, b1, sk)

            _, vjp = jax.vjp(fn, *prim)
            grads = list(vjp(do_ref[rows, :]))
            dq_ref[rows, :] = grads[0]
            dk_acc[window, :] += grads[1]
            dv_acc[window, :] += grads[2]
            dkc_ref[...] += grads[3]
            dvc_ref[...] += grads[4]
            rest_g = grads[5:]
            if per_head:
                g0, g1 = rest_g.pop(0), rest_g.pop(0)

                @pl.when(opens)
                def _(g0=g0, g1=g1, rows=rows):
                    db_ref[0, rows, :] = g0
                    db_ref[1, rows, :] = g1

                @pl.when(jnp.logical_not(opens))
                def _(g0=g0, g1=g1, rows=rows):
                    db_ref[0, rows, :] += g0
                    db_ref[1, rows, :] += g1

            if has_sink:
                ds_ref[...] += rest_g.pop(0)

        @pl.when(qb == nqb - 1)
        def _():
            cols = pl.ds(pl.multiple_of(hp * 128, 128), 128)
            pltpu.sync_copy(dk_acc, dkp_ref.at[:, cols])
            pltpu.sync_copy(dv_acc, dvp_ref.at[:, cols])

    hbm = pl.BlockSpec(memory_space=pl.ANY)
    in_specs = [qs] + kws + kws + [ctx, ctx, bias_s] + ([sink_s] if has_sink else []) + [qs]
    args = [q] + [kpad] * nw + [vpad] * nw + [kc, vc, bias] + ([sink] if has_sink else []) + [do]
    out_specs = [qs, hbm, hbm, ctx, ctx] + ([bias_s] if per_head else []) + ([sink_s] if has_sink else [])
    out_shape = [jax.ShapeDtypeStruct((t, 512), F32),
                 jax.ShapeDtypeStruct(kpad.shape, F32), jax.ShapeDtypeStruct(kpad.shape, F32),
                 jax.ShapeDtypeStruct((n_ctx, 512), F32), jax.ShapeDtypeStruct((n_ctx, 512), F32)]
    if per_head:
        out_shape.append(jax.ShapeDtypeStruct(bias.shape, F32))
    if has_sink:
        out_shape.append(jax.ShapeDtypeStruct((4, 8, 128), F32))
    res = list(pl.pallas_call(
        body, name=mode + "_attn_bwd", grid=(4, nqb),
        in_specs=in_specs, out_specs=out_specs, out_shape=tuple(out_shape),
        scratch_shapes=[pltpu.VMEM((kpad.shape[0], 128), F32), pltpu.VMEM((kpad.shape[0], 128), F32)],
        compiler_params=_cparams(("arbitrary", "arbitrary")),
    )(*args))
    dq, dkp, dvp, dkc, dvc = res[:5]
    res = res[5:]
    dbias = res.pop(0) if per_head else None
    dsink = res.pop(0) if has_sink else None
    return dq, dkp, dvp, dkc, dvc, dbias, dsink


def _loss_head(xt, target, n_ctx):
    t = xt.shape[0]
    nct = n_ctx // TOK

    def body(x_ref, t_ref, l_ref, d_ref):
        i = pl.program_id(0)

        @pl.when(i == 0)
        def _():
            l_ref[...] = jnp.zeros_like(l_ref)

        @pl.when(i < nct)
        def _():
            d_ref[...] = jnp.zeros_like(d_ref)

        @pl.when(i >= nct)
        def _():
            err = x_ref[...] - t_ref[...]
            d_ref[...] = err * (1.0 / D_MODEL)
            l_ref[...] += jnp.sum(err * err, keepdims=True) * (0.5 / D_MODEL)

    return pl.pallas_call(
        body, name="loss_head", grid=(t // TOK,),
        in_specs=[_row_spec(D_MODEL), pl.BlockSpec((TOK, D_MODEL), lambda i: (jnp.maximum(i - nct, 0), 0))],
        out_specs=[_const_spec((8, 128)), _row_spec(D_MODEL)],
        out_shape=(jax.ShapeDtypeStruct((8, 128), F32), jax.ShapeDtypeStruct((t, D_MODEL), F32)),
        compiler_params=_cparams(("arbitrary",)),
    )(xt, target)


PACK_W = 1024
SUM_STEPS = 8


def _sum_chips(recvs):
    def split(a):
        rows = a.shape[1]
        if rows % (8 * SUM_STEPS):
            return None
        return rows // SUM_STEPS

    def body(*refs):
        n = len(refs) // 2
        for r_ref, o_ref in zip(refs[:n], refs[n:]):
            up = lambda s: r_ref[s].astype(F32)
            o_ref[...] = ((up(0) + up(1)) + up(2)) + up(3)

    in_specs, out_specs = [], []
    for a in recvs:
        rb, tail = split(a), a.shape[2:]
        zeros = (0,) * len(tail)
        if rb is None:
            in_specs.append(pl.BlockSpec(a.shape, functools.partial(lambda i, z: (0, 0) + z, z=zeros)))
            out_specs.append(pl.BlockSpec(a.shape[1:], functools.partial(lambda i, z: (0,) + z, z=zeros)))
        else:
            in_specs.append(pl.BlockSpec((4, rb) + tail, functools.partial(lambda i, z: (0, i) + z, z=zeros)))
            out_specs.append(pl.BlockSpec((rb,) + tail, functools.partial(lambda i, z: (i,) + z, z=zeros)))
    return pl.pallas_call(
        body, name="sum_chips", grid=(SUM_STEPS,),
        in_specs=in_specs, out_specs=out_specs,
        out_shape=tuple(jax.ShapeDtypeStruct(a.shape[1:], F32) for a in recvs),
        compiler_params=_cparams(("arbitrary",)),
    )(*recvs)


ADAM_BLOCK_BYTES = 1 << 20


def _adamw(p_a, p_b, w, m, v, name):
    layers, rows, cols = w.shape
    tr = rows
    while tr % 16 == 0 and tr * cols * 4 > ADAM_BLOCK_BYTES:
        tr //= 2
    c1 = 1.0 / (1.0 - ADAM_B1 ** ADAM_STEP)
    c2 = 1.0 / (1.0 - ADAM_B2 ** ADAM_STEP)

    def body(a_ref, b_ref, w_ref, m_ref, v_ref, g_ref, d_ref, nm_ref, nv_ref):
        g = a_ref[...] + b_ref[...]
        nm = ADAM_B1 * m_ref[...] + (1.0 - ADAM_B1) * g
        nv = ADAM_B2 * v_ref[...] + (1.0 - ADAM_B2) * (g * g)
        g_ref[...] = g
        nm_ref[...] = nm
        nv_ref[...] = nv
        d_ref[...] = -ADAM_LR * ((nm * c1) / (jnp.sqrt(nv * c2) + ADAM_EPS) + ADAM_WD * w_ref[...])

    spec = pl.BlockSpec((None, tr, cols), lambda l, i: (l, i, 0))
    return pl.pallas_call(
        body, name=name, grid=(layers, rows // tr),
        in_specs=[spec] * 5, out_specs=[spec] * 4,
        out_shape=tuple(jax.ShapeDtypeStruct(w.shape, F32) for _ in range(4)),
        compiler_params=_cparams(("parallel", "parallel")),
    )(p_a, p_b, w, m, v)


MESH = pl.DeviceIdType.MESH
ANY_SPEC = pl.BlockSpec(memory_space=pl.ANY)


def _chip_exchange(srcs, out_shapes, src_window, dst_window, name):
    n = len(srcs)

    def body(*refs):
        src_refs, out_refs = refs[:n], refs[n:2 * n]
        send_sems, recv_sems, local_sems = refs[2 * n:]
        x, y, c = lax.axis_index("x"), lax.axis_index("y"), lax.axis_index("c")
        me = 2 * x + y
        peers = [(x, 1 - y), (1 - x, y), (1 - x, 1 - y)]

        def copy(k, j, from_chip, to_chip):
            px, py = peers[j]
            return pltpu.make_async_remote_copy(
                src_ref=src_window(k, src_refs[k], to_chip), dst_ref=dst_window(k, out_refs[k], from_chip),
                send_sem=send_sems.at[3 * k + j], recv_sem=recv_sems.at[3 * k + j],
                device_id=(px, py, c), device_id_type=MESH)

        local = [pltpu.make_async_copy(src_window(k, src_refs[k], me), dst_window(k, out_refs[k], me),
                                       local_sems.at[k]) for k in range(n)]
        for cp in local:
            cp.start()
        sends = [copy(k, j, me, 2 * px + py) for k in range(n) for j, (px, py) in enumerate(peers)]
        for cp in sends:
            cp.start()
        for k in range(n):
            for j, (px, py) in enumerate(peers):
                copy(k, j, 2 * px + py, me).wait_recv()
        for cp in sends:
            cp.wait_send()
        for cp in local:
            cp.wait()

    return pl.pallas_call(
        body, name=name, in_specs=[ANY_SPEC] * n, out_specs=[ANY_SPEC] * n,
        out_shape=tuple(out_shapes),
        scratch_shapes=[pltpu.SemaphoreType.DMA((3 * n,)), pltpu.SemaphoreType.DMA((3 * n,)),
                        pltpu.SemaphoreType.DMA((n,))],
    )(*srcs)


def _core_swap(srcs):
    n = len(srcs)

    def body(*refs):
        src_refs, out_refs, send_sems, recv_sems = refs[:n], refs[n:2 * n], refs[2 * n], refs[2 * n + 1]
        x, y, c = lax.axis_index("x"), lax.axis_index("y"), lax.axis_index("c")
        copies = [pltpu.make_async_remote_copy(
            src_ref=src_refs[k], dst_ref=out_refs[k], send_sem=send_sems.at[k], recv_sem=recv_sems.at[k],
            device_id=(x, y, 1 - c), device_id_type=MESH) for k in range(n)]
        for cp in copies:
            cp.start()
        for cp in copies:
            cp.wait()

    return pl.pallas_call(
        body, name="core_swap", in_specs=[ANY_SPEC] * n, out_specs=[ANY_SPEC] * n,
        out_shape=tuple(jax.ShapeDtypeStruct(s.shape, s.dtype) for s in srcs),
        scratch_shapes=[pltpu.SemaphoreType.DMA((n,)), pltpu.SemaphoreType.DMA((n,))],
    )(*srcs)


def _col_window(ref, start, size):
    idx = (slice(None),) * (len(ref.shape) - 1) + (pl.ds(pl.multiple_of(start, 128), size),)
    return ref.at[idx]


def _row_window(ref, start, size):
    idx = (slice(None),) * (len(ref.shape) - 2) + (pl.ds(pl.multiple_of(start, 8), size), slice(None))
    return ref.at[idx]


N_SHARD_IN = 2624
WIN_W = 2944
WIN_START = (0, 2560, 5248, 7808)
WIN_PIECES = (((0, 2624),), ((64, 2688),), ((0, 640), (896, 2880)), ((320, 2944),))


def _core_layers(ref, lay, core):
    half = lay // 2
    return ref.at[pl.ds(core * half, half)]


def _gather_weights(w_ada, w_in, w_glu, w_br, w_out, conv_w):
    lay = w_ada.shape[0]
    assert lay % 2 == 0
    sizes = (768, None, 128, 256, 256, 128)

    def dst(k, ref, s, core=None):
        core = lax.axis_index("c") if core is None else core
        if k == 1:
            return _core_layers(ref.at[s], lay, core)
        ref = _core_layers(ref, lay, core)
        if k in (2, 4):
            return _row_window(ref, s * sizes[k], sizes[k])
        return _col_window(ref, s * sizes[k], sizes[k])

    shapes = (jax.ShapeDtypeStruct((lay, D_MODEL, 3 * D_MODEL), w_ada.dtype),
              jax.ShapeDtypeStruct((4,) + w_in.shape, w_in.dtype),
              jax.ShapeDtypeStruct((lay, MIX_W, MIX_W), w_glu.dtype),
              jax.ShapeDtypeStruct((lay, 4, MIX_W, D_MODEL), w_br.dtype),
              jax.ShapeDtypeStruct((lay, D_MODEL, D_MODEL), w_out.dtype),
              jax.ShapeDtypeStruct((lay, 8, MIX_W), conv_w.dtype))
    halves = _chip_exchange((w_ada, w_in, w_glu, w_br, w_out, conv_w), shapes,
                            lambda k, ref, t: _core_layers(ref, lay, lax.axis_index("c")), dst, "gather_weights")
    return _merge_core_halves(halves, lay)


def _merge_core_halves(halves, lay):
    n = len(halves)

    def body(*refs):
        out_refs, send_sems, recv_sems = refs[n:2 * n], refs[2 * n], refs[2 * n + 1]
        x, y, c = lax.axis_index("x"), lax.axis_index("y"), lax.axis_index("c")

        def part(k, core):
            if k == 1:
                half = lay // 2
                return out_refs[k].at[:, pl.ds(core * half, half)]
            return _core_layers(out_refs[k], lay, core)

        def copy(k, landing_core):
            return pltpu.make_async_remote_copy(
                src_ref=part(k, c), dst_ref=part(k, landing_core), send_sem=send_sems.at[k],
                recv_sem=recv_sems.at[k], device_id=(x, y, 1 - c), device_id_type=MESH)

        give = [copy(k, c) for k in range(n)]
        for cp in give:
            cp.start()
        for k in range(n):
            copy(k, 1 - c).wait_recv()
        for cp in give:
            cp.wait_send()

    return pl.pallas_call(
        body, name="gather_merge", in_specs=[ANY_SPEC] * n, out_specs=[ANY_SPEC] * n,
        out_shape=tuple(jax.ShapeDtypeStruct(h.shape, h.dtype) for h in halves),
        input_output_aliases={k: k for k in range(n)},
        scratch_shapes=[pltpu.SemaphoreType.DMA((n,)), pltpu.SemaphoreType.DMA((n,))],
    )(*halves)


def _scatter_grads(dw_ada, dw_in, dw_glu, dw_br, dw_out, dconv_w, small):
    def src(k, ref, t):
        if k == 0:
            return _col_window(ref, t * 768, 768)
        if k == 1:
            start = jnp.where(t == 0, WIN_START[0], jnp.where(t == 1, WIN_START[1],
                              jnp.where(t == 2, WIN_START[2], WIN_START[3])))
            return _col_window(ref, start, WIN_W)
        if k == 2:
            return _row_window(ref, t * 128, 128)
        if k == 3:
            return _col_window(ref, t * 256, 256)
        if k == 4:
            return _row_window(ref, t * 256, 256)
        if k == 5:
            return _col_window(ref, t * 128, 128)
        return ref

    pieces = ((D_MODEL, 768), (D_MODEL, WIN_W), (128, MIX_W), (4, MIX_W, 256), (256, D_MODEL), (8, 128), small.shape)
    srcs = (dw_ada, dw_in, dw_glu, dw_br, dw_out, dconv_w, small)
    shapes = tuple(jax.ShapeDtypeStruct((4,) + p, s.dtype) for p, s in zip(pieces, srcs))
    return _chip_exchange(srcs, shapes,
                          src, lambda k, ref, s: ref.at[s], "scatter_grads")


def _s5_tables(a_re, a_im, log_dt, b_re, b_im, c_re, c_im):
    ln = S5_CHUNK
    hi = lax.Precision.HIGHEST
    dt = jnp.exp(log_dt)[..., None]
    mag = jnp.exp(dt * a_re)
    abr = mag * jnp.cos(dt * a_im)
    abi = mag * jnp.sin(dt * a_im)
    den = a_re * a_re + a_im * a_im
    fr = ((abr - 1.0) * a_re + abi * a_im) / den
    fi = (abi * a_re - (abr - 1.0) * a_im) / den
    bbr = fr[..., None] * b_re - fi[..., None] * b_im
    bbi = fr[..., None] * b_im + fi[..., None] * b_re
    n = jnp.arange(ln + 1, dtype=F32)[:, None, None, None]
    pm = jnp.exp(n * dt * a_re)
    er = pm * jnp.cos(n * dt * a_im)
    ei = pm * jnp.sin(n * dt * a_im)
    e3 = lambda e, b, c: jnp.einsum("tdgp,dgpa,dgbp->dgabt", e, b, c, precision=hi)
    gt = e3(er[:ln], bbr, c_re) - e3(er[:ln], bbi, c_im) - e3(ei[:ln], bbr, c_im) - e3(ei[:ln], bbi, c_re)
    by_dir = lambda fwd, bwd: jnp.stack([fwd[:, 0], bwd[:, 1]], axis=1)
    erj, eij = by_dir(er[:ln][::-1], er[:ln]), by_dir(ei[:ln][::-1], ei[:ln])
    e2 = lambda e, b: jnp.einsum("jdgp,dgpa->dgajp", e, b, precision=hi)
    w = jnp.concatenate([e2(erj, bbr) - e2(eij, bbi), e2(erj, bbi) + e2(eij, bbr)], axis=-1)
    er1, ei1 = by_dir(er[1:], er[1:][::-1]), by_dir(ei[1:], ei[1:][::-1])
    ev = lambda c, e: jnp.einsum("dgbp,idgp->dgpbi", c, e, precision=hi)
    v = jnp.concatenate([ev(c_re, er1) - ev(c_im, ei1), -(ev(c_re, ei1) + ev(c_im, er1))], axis=2)
    a1 = jnp.concatenate([er[ln], er[ln]], axis=-1)
    a2 = jnp.concatenate([-ei[ln], ei[ln]], axis=-1)
    return (gt.transpose(1, 2, 3, 0, 4).reshape(S5_GROUPS, 256, 2 * ln),
            w.transpose(1, 2, 3, 0, 4).reshape(S5_GROUPS, S5_CH * ln, 256),
            v.transpose(1, 0, 2, 3, 4).reshape(S5_GROUPS, 256, S5_CH * ln),
            a1.reshape(64, 128), a2.reshape(64, 128))


def _lag_onehot():
    ln = S5_CHUNK
    j, i = np.meshgrid(np.arange(ln), np.arange(ln), indexing="ij")
    lag = np.arange(ln)[:, None, None]
    z = np.concatenate([lag == (i - j)[None], lag == (j - i)[None]], axis=0).astype(np.float32)
    return jnp.broadcast_to(jnp.asarray(z.reshape(2 * ln, ln * ln), BF16), (S5_GROUPS, 2 * ln, ln * ln))


def _toeplitz(gt):
    ln = S5_CHUNK
    flat = _matmul(gt, _lag_onehot(), out_dtype=BF16, name="s5_toeplitz")
    return (flat.reshape(S5_GROUPS, S5_CH, S5_CH, ln, ln).transpose(0, 1, 3, 2, 4)
            .reshape(S5_GROUPS, S5_CH * ln, S5_CH * ln))


def _toeplitz_fold(dk):
    ln = S5_CHUNK
    flat = dk.reshape(S5_GROUPS, S5_CH, ln, S5_CH, ln).transpose(0, 1, 3, 2, 4).reshape(S5_GROUPS, 256, ln * ln)
    return _matmul(flat, _lag_onehot(), trans_b=True, name="s5_toeplitz_fold")


def _chunk_rows(t):
    k = t // S5_CHUNK
    return k, -(-k // 128) * 128


def _to_chunks(u):
    k, kp = _chunk_rows(u.shape[0])
    v = u.reshape(k, S5_CHUNK, S5_GROUPS, S5_CH).transpose(2, 0, 3, 1).reshape(S5_GROUPS, k, S5_CH * S5_CHUNK)
    return jnp.pad(v, ((0, 0), (0, kp - k), (0, 0)))


def _from_chunks(y, t):
    k, _ = _chunk_rows(t)
    return y[:, :k].reshape(S5_GROUPS, k, S5_CH, S5_CHUNK).transpose(1, 3, 0, 2).reshape(t, MIX_W)


def _states_to_rows(s):
    kp = s.shape[1]
    return s.reshape(S5_GROUPS, kp, 2, 128).transpose(1, 2, 0, 3).reshape(kp, 64, 128)


def _rows_to_states(h):
    kp = h.shape[0]
    return h.reshape(kp, 2, S5_GROUPS, 128).transpose(2, 0, 1, 3).reshape(S5_GROUPS, kp, 256)


def _na_bias(rel_bias):
    a, m = np.meshgrid(np.arange(4), np.arange(12), indexing="ij")
    di = np.clip(m - a + 3, 0, 2 * NA_ROWS - 2).reshape(-1)
    qc, kc = np.meshgrid(np.arange(GRID_W), np.arange(GRID_W), indexing="ij")
    dj = np.clip(kc - qc + NA_COLS - 1, 0, 2 * NA_COLS - 2).reshape(-1)
    oh_i = jnp.asarray(di[:, None] == np.arange(2 * NA_ROWS - 1)[None, :], F32)
    oh_j = jnp.asarray(dj[:, None] == np.arange(2 * NA_COLS - 1)[None, :], F32)
    hi = lax.Precision.HIGHEST
    cols = jnp.einsum("hij,cj->hic", rel_bias, oh_j, precision=hi)
    full = jnp.einsum("ri,hic->hrc", oh_i, cols, precision=hi)
    full = full.reshape(N_HEADS, 4, 12, GRID_W, GRID_W).transpose(0, 1, 3, 2, 4)
    return full.reshape(4, 2, ATT_BLK, 3 * ATT_BLK)


def _rope_tables(n_ctx, n_lat):
    tok = jnp.arange(n_lat, dtype=jnp.int32)
    row = (tok // GRID_W).astype(F32)
    col = (tok % GRID_W).astype(F32)
    inv = ROPE_BASE ** (-jnp.arange(ROPE_PAIRS, dtype=F32) / ROPE_PAIRS)
    ang = jnp.concatenate([row[:, None] * inv, col[:, None] * inv], axis=-1)
    cos, sin = jnp.cos(ang), jnp.sin(ang)
    cos = jnp.tile(jnp.concatenate([cos, cos], axis=-1), (1, 2))
    sin = jnp.tile(jnp.concatenate([-sin, sin], axis=-1), (1, 2))
    return (jnp.concatenate([jnp.ones((n_ctx, 128), F32), cos], axis=0),
            jnp.concatenate([jnp.zeros((n_ctx, 128), F32), sin], axis=0))


def _pad_blocks(a, n_ctx):
    return jnp.pad(a[n_ctx:], ((ATT_BLK, ATT_BLK), (0, 0)))


def _layer_fwd(xt, cc, w, rope, n_ctx):
    t = xt.shape[0]
    sv = {}
    mod = _adaln_fwd(cc, w["w_ada"], w["b_ada"].reshape(1, -1))
    mod4 = mod[:2].reshape(2, 3, 1, D_MODEL)
    h = _modnorm_fwd(xt, w["norm_g"].reshape(1, -1), mod4, n_ctx)
    z = _matmul(h, w["w_in"], name="proj_fwd")

    s5_args = (w["s5_a_re"], w["s5_a_im"], w["s5_log_dt"], w["s5_b_re"], w["s5_b_im"], w["s5_c_re"], w["s5_c_im"])
    (gt, tw, tv, a1, a2), tab_vjp = jax.vjp(_s5_tables, *s5_args)
    ktoe = _toeplitz(gt)
    tw, tv = tw.astype(BF16), tv.astype(BF16)
    uc = _to_chunks(z[:, :MIX_W].astype(BF16))
    st = _matmul(uc, tw, name="s5_chunk_state")
    hprev = _s5_scan_fwd(_states_to_rows(st), a1, a2, t // S5_CHUNK, n_ctx // S5_CHUNK)
    uh = jnp.concatenate([uc, _rows_to_states(hprev).astype(BF16)], axis=2)
    ysum = _from_chunks(_matmul(uh, jnp.concatenate([ktoe, tv], axis=1), name="s5_chunk_out"), t)
    s5_d = w["s5_d"].reshape(1, MIX_W)
    y_s5 = _s5post_fwd(ysum, z, s5_d, w["s5_w_glu"])

    conv_w = w["conv_w"]
    y_conv = _conv_fwd(z, conv_w, w["conv_b"].reshape(1, -1), n_ctx)

    gains = (jnp.tile(w["na_q_g"], 8)[None], jnp.tile(w["na_k_g"], 8)[None],
             jnp.tile(w["gqa_q_g"], 8)[None], jnp.tile(w["gqa_k_g"], 2)[None])
    q_na, k_na, v_na, q_g, k_g, v_g = _prep_fwd(z, gains, rope)
    bias, bias_vjp = jax.vjp(_na_bias, w["na_rel_bias"])
    sink = jnp.zeros((4, 8, 128), F32).at[:, :2, :].set(
        jnp.broadcast_to(w["gqa_sink"].reshape(4, 2, 1), (4, 2, 128)))
    na_tab = jnp.where(_window_patterns("na", t - n_ctx)[:, None, None], bias[None], NEG_INF)
    gqa_tab = jnp.where(_window_patterns("gqa", t - n_ctx), 0.0, NEG_INF).astype(F32)
    na_in = (q_na, _pad_blocks(k_na, n_ctx), _pad_blocks(v_na, n_ctx), k_na[:n_ctx], v_na[:n_ctx], na_tab, None)
    gqa_in = (q_g, _pad_blocks(k_g, n_ctx), _pad_blocks(v_g, n_ctx), k_g[:n_ctx], v_g[:n_ctx], gqa_tab, sink)
    y_na = _attn_fwd(*na_in, mode="na", n_ctx=n_ctx)
    y_gqa = _attn_fwd(*gqa_in, mode="gqa", n_ctx=n_ctx)
    ys = (y_s5, y_conv, y_na, y_gqa)
    xt_new = _merge_fwd(xt, ys, z, mod4, w["w_br"], w["w_out"], n_ctx)
    sv.update(xt=xt, mod4=mod4, h=h, z=z, tab_vjp=tab_vjp, ktoe=ktoe, tw=tw, tv=tv, a1=a1, a2=a2, uc=uc,
              hprev=hprev, uh=uh, ysum=ysum, s5_d=s5_d, conv_w=conv_w, gains=gains, bias_vjp=bias_vjp,
              na_in=na_in, gqa_in=gqa_in, ys=ys)
    return xt_new, sv


def _layer_bwd(dxt_new, sv, cc, w, rope, n_ctx):
    t = dxt_new.shape[0]
    z, mod4 = sv["z"], sv["mod4"]
    dys, dgt, dmg, dgate, dw_br, dw_out = _merge_bwd(dxt_new, sv["ys"], z, mod4, w["w_br"], w["w_out"], n_ctx)

    dpre, du_skip, dd, dw_glu = _s5post_bwd(sv["ysum"], z, sv["s5_d"], w["s5_w_glu"], dys[0])
    dyc = _to_chunks(dpre)
    dhp = _matmul(dyc, sv["tv"], trans_b=True, name="s5_bwd_state")
    ds, da1, da2 = _s5_scan_bwd(_states_to_rows(dhp), sv["hprev"], sv["a1"], sv["a2"],
                                t // S5_CHUNK, n_ctx // S5_CHUNK)
    ds = _rows_to_states(ds).astype(BF16)
    duc = _matmul(jnp.concatenate([dyc, ds], axis=2), jnp.concatenate([sv["ktoe"], sv["tw"]], axis=2),
                  trans_b=True, out_dtype=BF16, name="s5_bwd_u")
    dkv = _matmul(sv["uh"].transpose(0, 2, 1), dyc, out_dtype=BF16, name="s5_bwd_kv")
    dtw = _matmul(sv["uc"].transpose(0, 2, 1), ds, name="s5_bwd_w")
    dgt_tab = _toeplitz_fold(dkv[:, :S5_CH * S5_CHUNK])
    s5_grads = sv["tab_vjp"]((dgt_tab, dtw, dkv[:, S5_CH * S5_CHUNK:].astype(F32), da1, da2))
    du_scan = _from_chunks(duc, t)

    dzv, dzb, dzc, dconv_w, dconv_b = _conv_bwd(z, sv["conv_w"], w["conv_b"].reshape(1, -1), dys[1], n_ctx)

    dq_na, dk_na, dv_na, dkc_na, dvc_na, dbias, _ = _attn_bwd(*sv["na_in"], dys[2], mode="na", n_ctx=n_ctx)
    dq_g, dk_g, dv_g, dkc_g, dvc_g, _, dsink = _attn_bwd(*sv["gqa_in"], dys[3], mode="gqa", n_ctx=n_ctx)
    pb = _prep_bwd(z, sv["gains"], rope, (dq_na, dq_g), (dk_na, dv_na, dk_g, dv_g),
                   (dkc_na, dvc_na, dkc_g, dvc_g), du_skip, du_scan, n_ctx)
    dz_naq, dz_nak, dz_nav, dz_gq, dz_gk, dz_gv, dz_u, dg_naq, dg_nak, dg_gq, dg_gk = pb

    dz = jnp.concatenate([dz_u, dgt[0], dzv, dzb, dzc, dgt[1], dz_naq, dz_nak, dz_nav, dgt[2], dz_gq, dz_gk, dz_gv,
                          jnp.zeros((t, OFF["gqa_gate"] - OFF["pad"]), BF16), dgt[3], *dmg], axis=1)
    dh = _matmul(dz, w["w_in"], trans_b=True, name="proj_bwd_x")
    dw_in = _matmul(sv["h"].T, dz, out_dtype=BF16, name="proj_bwd_w")
    dxt, dnorm_g, dshift, dscale = _modnorm_bwd(sv["xt"], w["norm_g"].reshape(1, -1), mod4, dh, dxt_new, n_ctx)
    dmod = jnp.concatenate([dshift, dscale, dgate], axis=1).reshape(2, 3 * D_MODEL)
    dcc, dw_ada, db_ada = _adaln_bwd(cc, w["w_ada"], jnp.pad(dmod, ((0, 6), (0, 0))))

    (drel,) = sv["bias_vjp"](dbias.sum(0))
    grads = dict(
        norm_g=dnorm_g[0], w_ada=dw_ada, b_ada=db_ada[0], w_in=dw_in,
        s5_a_re=s5_grads[0], s5_a_im=s5_grads[1], s5_log_dt=s5_grads[2], s5_b_re=s5_grads[3], s5_b_im=s5_grads[4],
        s5_c_re=s5_grads[5], s5_c_im=s5_grads[6], s5_d=dd.reshape(S5_GROUPS, S5_CH), s5_w_glu=dw_glu,
        conv_w=dconv_w, conv_b=dconv_b[0],
        na_q_g=dg_naq.reshape(8, HEAD_DIM).sum(0), na_k_g=dg_nak.reshape(8, HEAD_DIM).sum(0), na_rel_bias=drel,
        gqa_q_g=dg_gq.reshape(8, HEAD_DIM).sum(0), gqa_k_g=dg_gk.reshape(2, HEAD_DIM).sum(0),
        gqa_sink=dsink[:, :2, :].sum(-1).reshape(8), w_br=dw_br, w_out=dw_out)
    return dxt, dcc, grads


SHARDED = ("w_ada", "w_in", "s5_w_glu", "conv_w", "w_br", "w_out")
REPLICATED = ("norm_g", "b_ada", "s5_a_re", "s5_a_im", "s5_log_dt", "s5_b_re", "s5_b_im", "s5_c_re", "s5_c_im",
              "s5_d", "conv_b", "na_q_g", "na_k_g", "na_rel_bias", "gqa_q_g", "gqa_k_g", "gqa_sink")
WEIGHTS = ("c_ctx", "norm_g", "w_ada", "b_ada", "w_in", "s5_a_re", "s5_a_im", "s5_log_dt", "s5_b_re", "s5_b_im",
           "s5_c_re", "s5_c_im", "s5_d", "s5_w_glu", "conv_w", "conv_b", "na_q_g", "na_k_g", "na_rel_bias",
           "gqa_q_g", "gqa_k_g", "gqa_sink", "w_br", "w_out")


def _pack(pieces, row_multiple, dtype):
    flat = jnp.concatenate([p.reshape(-1).astype(dtype) for p in pieces])
    rows = -(-flat.shape[0] // PACK_W)
    rows = -(-rows // row_multiple) * row_multiple
    return jnp.pad(flat, (0, rows * PACK_W - flat.shape[0])).reshape(rows, PACK_W)


def _unpack(buf, shapes):
    flat = buf.reshape(-1)
    out, pos = [], 0
    for shp in shapes:
        size = int(np.prod(shp))
        out.append(flat[pos:pos + size].reshape(shp))
        pos += size
    return out


def _local_step(x, ctx, target, c_vec, c_ctx, layers):
    depth = len(layers)
    n_ctx, n_lat = ctx.shape[0], x.shape[0]
    cc = jnp.zeros((8, D_MODEL), F32).at[0].set(c_ctx).at[1].set(c_vec)
    rope = _rope_tables(n_ctx, n_lat)
    xt = jnp.concatenate([ctx, x], axis=0)
    saved = []
    for l in range(depth):
        xt, sv = _layer_fwd(xt, cc, layers[l], rope, n_ctx)
        saved.append(sv)
    loss_tile, dxt = _loss_head(xt, target, n_ctx)
    grads = [None] * depth
    dc_ctx = jnp.zeros((D_MODEL,), F32)
    for l in reversed(range(depth)):
        dxt, dcc, grads[l] = _layer_bwd(dxt, saved[l], cc, layers[l], rope, n_ctx)
        dc_ctx = dc_ctx + dcc[0]
    return loss_tile[0, 0], dxt[n_ctx:][None], dc_ctx, grads


def kernel(x, c, ctx, c_ctx, norm_g, w_ada, b_ada, w_in, s5_a_re, s5_a_im, s5_log_dt, s5_b_re, s5_b_im,
           s5_c_re, s5_c_im, s5_d, s5_w_glu, conv_w, conv_b, na_q_g, na_k_g, na_rel_bias, gqa_q_g,
           gqa_k_g, gqa_sink, w_br, w_out, loss_target, m_c_ctx, m_norm_g, m_w_ada, m_b_ada, m_w_in,
           m_s5_a_re, m_s5_a_im, m_s5_log_dt, m_s5_b_re, m_s5_b_im, m_s5_c_re, m_s5_c_im, m_s5_d,
           m_s5_w_glu, m_conv_w, m_conv_b, m_na_q_g, m_na_k_g, m_na_rel_bias, m_gqa_q_g, m_gqa_k_g,
           m_gqa_sink, m_w_br, m_w_out, v_c_ctx, v_norm_g, v_w_ada, v_b_ada, v_w_in, v_s5_a_re,
           v_s5_a_im, v_s5_log_dt, v_s5_b_re, v_s5_b_im, v_s5_c_re, v_s5_c_im, v_s5_d, v_s5_w_glu,
           v_conv_w, v_conv_b, v_na_q_g, v_na_k_g, v_na_rel_bias, v_gqa_q_g, v_gqa_k_g, v_gqa_sink,
           v_w_br, v_w_out):
    a = dict(locals())
    depth = a["norm_g"].shape[0]
    x, ctx, target = a["x"][0], a["ctx"][0], a["loss_target"][0]
    n_ctx, n_lat = ctx.shape[0], x.shape[0]
    assert n_ctx % ATT_BLK == 0 and n_lat % (4 * GRID_W) == 0 and n_lat // GRID_W >= NA_ROWS

    cast = lambda n: a[n].astype(BF16)
    conv8 = jnp.pad(a["conv_w"], ((0, 0), (0, 5), (0, 0)))
    g_ada, g_in, g_glu, g_br, g_out, g_conv = _gather_weights(
        cast("w_ada"), cast("w_in"), cast("s5_w_glu"), cast("w_br"), cast("w_out"), conv8)
    zpad = jnp.zeros((D_MODEL, OFF["gqa_gate"] - OFF["pad"]), BF16)
    split = OFF["pad"] - 2 * N_SHARD_IN
    layers = []
    for l in range(depth):
        w = {n: a[n][l] for n in REPLICATED}
        w.update(w_ada=g_ada[l], s5_w_glu=g_glu[l], w_br=g_br[l], w_out=g_out[l], conv_w=g_conv[l])
        w["w_in"] = jnp.concatenate([g_in[0, l], g_in[1, l], g_in[2, l][:, :split], zpad, g_in[2, l][:, split:],
                                     g_in[3, l]], axis=1)
        layers.append(w)

    loss_local, grad_x, dc_ctx, grads = _local_step(x, ctx, target, a["c"][0], a["c_ctx"], layers)
    loss = lax.psum(loss_local, ("x", "y", "c"))

    chip = 2 * lax.axis_index("x") + lax.axis_index("y")
    take = [functools.partial(lambda win, pc: jnp.concatenate([win[:, lo:hi] for lo, hi in pc], axis=1), pc=pc)
            for pc in WIN_PIECES]

    def small_pack(values, c_ctx_value, l):
        pieces = [values[n] for n in REPLICATED]
        pieces.append(c_ctx_value if l == 0 else jnp.zeros((D_MODEL,), F32))
        return _pack(pieces, 8 * SUM_STEPS, F32)

    mine, theirs = [], []
    for l in range(depth):
        g = grads[l]
        small = small_pack(g, dc_ctx, l)
        recv = _scatter_grads(g["w_ada"], g["w_in"], g["s5_w_glu"], g["w_br"], g["w_out"], g["conv_w"], small)
        part = list(_sum_chips([r.reshape(4, -1, r.shape[-1]) for r in recv]))
        part[1] = lax.switch(chip, take, part[1])
        mine.append(part)
        theirs.append(_core_swap(part))

    families = ("w_ada", "w_in", "s5_w_glu", "w_br", "w_out", "conv_w")
    out = {}
    for k, n in enumerate(families):
        p, q = jnp.stack([m[k] for m in mine]), jnp.stack([t[k] for t in theirs])
        if n == "conv_w":
            p, q = p[:, :3], q[:, :3]
        as3d = lambda arr: arr.reshape(depth, -1, arr.shape[-1])
        res = _adamw(p, q, as3d(a[n]), as3d(a["m_" + n]), as3d(a["v_" + n]), "adamw_" + n)
        out[n] = [r.reshape(a[n].shape) for r in res]
    p, q = jnp.stack([m[6] for m in mine]), jnp.stack([t[6] for t in theirs])
    packs = [jnp.stack([small_pack({n: a[pre + n][l] for n in REPLICATED}, a[pre + "c_ctx"], l)
                        for l in range(depth)]) for pre in ("", "m_", "v_")]
    res = _adamw(p, q, *packs, "adamw_small")
    shapes = [a[n].shape[1:] for n in REPLICATED] + [a["c_ctx"].shape]
    per_layer = [[_unpack(r[l], shapes) for l in range(depth)] for r in res]
    for j, n in enumerate(REPLICATED):
        out[n] = [jnp.stack([per_layer[key][l][j] for l in range(depth)]) for key in range(4)]
    out["c_ctx"] = [per_layer[key][0][-1] for key in range(4)]
    results = [loss, grad_x]
    for key in range(4):
        results += [out[n][key] for n in WEIGHTS]
    return tuple(results)
```
